```python
import jax
import jax.numpy as jnp
from jax import lax
import numpy as np

D_MODEL = 1024
BATCH = 32
SEQ = 2048
DEPTH = 2

GRID_W = 64
CTX_LEN = 256
MIX_W = D_MODEL
N_MIXERS = 4
GROUP_W = MIX_W // N_MIXERS
HEAD_DIM = 64
N_Q_HEADS = GROUP_W // HEAD_DIM
N_KV_HEADS = N_Q_HEADS // 2
WINDOW = 128
ATT_BLOCK = WINDOW
ROPE_BASE = 10000.0
CONV_WIDTH = 31
CHUNK = 128
GMLP_HEADS = 4
GMLP_HEAD_W = GROUP_W // GMLP_HEADS
POOL_WINDOWS = (2, 4, 8, 16)
POOL_GROUP_W = GROUP_W // len(POOL_WINDOWS)
N_EXPERTS = 32
TOP_K = 4
D_FF = D_MODEL
SWIGLU_ALPHA = 1.702
SWIGLU_LIMIT = 7.0
MOE_BLOCK = 256
EPS = 1e-6
NEG_INF = -1e30

OFF_Q = 0
OFF_K = OFF_Q + N_Q_HEADS * HEAD_DIM
OFF_V = OFF_K + N_KV_HEADS * HEAD_DIM
OFF_CONV = OFF_V + N_KV_HEADS * HEAD_DIM
OFF_GMLP = OFF_CONV + 2 * GROUP_W
OFF_POOL = OFF_GMLP + 2 * GROUP_W
IN_W = OFF_POOL + GROUP_W

kernel_name = 'hybrid_parallel_dit_block'


def rmsnorm(x, g):
    xf = x.astype(jnp.float32)
    y = xf * lax.rsqrt(jnp.mean(xf * xf, axis=-1, keepdims=True) + EPS)
    return (y * g.astype(jnp.float32)).astype(x.dtype)


def layernorm(x, g, b):
    xf = x.astype(jnp.float32)
    mu = jnp.mean(xf, axis=-1, keepdims=True)
    xc = xf - mu
    y = xc * lax.rsqrt(jnp.mean(xc * xc, axis=-1, keepdims=True) + EPS)
    return (y * g.astype(jnp.float32) + b.astype(jnp.float32)).astype(x.dtype)


def group_rmsnorm(y, g):
    shp = y.shape
    yg = y.reshape(shp[:-1] + (N_MIXERS, GROUP_W))
    return rmsnorm(yg, g.reshape(N_MIXERS, GROUP_W)).reshape(shp)


def adaln(cond, w, b):
    m = jax.nn.silu(cond) @ w + b
    return m.reshape(m.shape[:-1] + (6, D_MODEL))


def axial_rope_tables(L):
    rows = L // GRID_W
    t = jnp.arange(rows * GRID_W)
    row = (t // GRID_W).astype(jnp.float32)
    col = (t % GRID_W).astype(jnp.float32)
    half = HEAD_DIM // 2
    inv = ROPE_BASE ** (-jnp.arange(0, half, 2, dtype=jnp.float32) / half)
    ang_r = row[:, None] * inv[None, :]
    ang_c = col[:, None] * inv[None, :]
    return (jnp.cos(ang_r), jnp.sin(ang_r), jnp.cos(ang_c), jnp.sin(ang_c))


def rotate(x, cos, sin):
    cs = cos[:, None, :].astype(x.dtype)
    sn = sin[:, None, :].astype(x.dtype)
    x1, x2 = jnp.split(x, 2, axis=-1)
    return jnp.concatenate([x1 * cs - x2 * sn, x2 * cs + x1 * sn], axis=-1)


def apply_axial_rope(x, tabs):
    cos_r, sin_r, cos_c, sin_c = tabs
    half = HEAD_DIM // 2
    return jnp.concatenate([rotate(x[..., :half], cos_r, sin_r),
                            rotate(x[..., half:], cos_c, sin_c)], axis=-1)


def window_attention(q, k, v, kc, vc, sink):
    b, L = q.shape[0], q.shape[1]
    nb = L // ATT_BLOCK
    grp = N_Q_HEADS // N_KV_HEADS
    scale = HEAD_DIM ** -0.5
    qb = q.reshape(b, nb, ATT_BLOCK, N_KV_HEADS, grp, HEAD_DIM)

    def band(t):
        tp = jnp.pad(t, ((0, 0), (ATT_BLOCK, ATT_BLOCK), (0, 0), (0, 0)))
        tp = tp.reshape(b, nb + 2, ATT_BLOCK, N_KV_HEADS, HEAD_DIM)
        return jnp.concatenate([tp[:, :-2], tp[:, 1:-1], tp[:, 2:]], axis=2)

    kw = band(k)
    vw = band(v)
    s_loc = jnp.einsum('bnqkgd,bnjkd->bnkgqj', qb, kw).astype(jnp.float32) * scale
    s_ctx = jnp.einsum('bnqkgd,bckd->bnkgqc', qb, kc).astype(jnp.float32) * scale
    qi = jnp.arange(ATT_BLOCK)[:, None]
    kj = jnp.arange(3 * ATT_BLOCK)[None, :]
    in_band = (kj >= qi) & (kj <= qi + 2 * WINDOW)
    key_pos = (jnp.arange(nb)[:, None] - 1) * ATT_BLOCK + jnp.arange(3 * ATT_BLOCK)[None, :]
    in_seq = (key_pos >= 0) & (key_pos < L)
    mask = in_band[None, :, :] & in_seq[:, None, :]
    s_loc = jnp.where(mask[None, :, None, None], s_loc, NEG_INF)
    s_sink = jnp.broadcast_to(
        sink.astype(jnp.float32).reshape(N_KV_HEADS, grp)[None, None, :, :, None, None],
        s_loc.shape[:-1] + (1,))
    p = jax.nn.softmax(jnp.concatenate([s_sink, s_ctx, s_loc], axis=-1), axis=-1).astype(q.dtype)
    lc = kc.shape[1]
    o = (jnp.einsum('bnkgqc,bckd->bnqkgd', p[..., 1:1 + lc], vc)
         + jnp.einsum('bnkgqj,bnjkd->bnqkgd', p[..., 1 + lc:], vw))
    return o.reshape(b, L, N_Q_HEADS * HEAD_DIM)


def ctx_attention(qc, kc, vc, sink):
    b, lc = qc.shape[0], qc.shape[1]
    grp = N_Q_HEADS // N_KV_HEADS
    scale = HEAD_DIM ** -0.5
    qg = qc.reshape(b, lc, N_KV_HEADS, grp, HEAD_DIM)
    s = jnp.einsum('bqkgd,bckd->bkgqc', qg, kc).astype(jnp.float32) * scale
    s_sink = jnp.broadcast_to(
        sink.astype(jnp.float32).reshape(N_KV_HEADS, grp)[None, :, :, None, None],
        s.shape[:-1] + (1,))
    p = jax.nn.softmax(jnp.concatenate([s_sink, s], axis=-1), axis=-1).astype(qc.dtype)
    o = jnp.einsum('bkgqc,bckd->bqkgd', p[..., 1:], vc)
    return o.reshape(b, lc, N_Q_HEADS * HEAD_DIM)


def conformer_conv(h2, dw_w, dw_b, ln_g, ln_b, pw):
    a, gte = jnp.split(h2, 2, axis=-1)
    h = a * jax.nn.sigmoid(gte)
    h = lax.conv_general_dilated(
        h, dw_w[:, None, :], window_strides=(1,),
        padding=[(CONV_WIDTH // 2, CONV_WIDTH // 2)],
        dimension_numbers=('NWC', 'WIO', 'NWC'),
        feature_group_count=GROUP_W) + dw_b
    h = jax.nn.silu(layernorm(h, ln_g, ln_b))
    return h @ pw


def chunk_gating(h2, ln_g, ln_b, ws, bs):
    u, v = jnp.split(h2, 2, axis=-1)
    v = layernorm(v, ln_g, ln_b)
    b, L = v.shape[0], v.shape[1]
    nc = L // CHUNK
    vr = v.reshape(b, nc, CHUNK, GMLP_HEADS, GMLP_HEAD_W)
    mixed = jnp.einsum('hpq,bnqhc->bnphc', ws, vr) + bs.T[None, None, :, :, None]
    return u * mixed.reshape(b, L, GROUP_W)


def pool_mix(h, pw, pscale):
    b, L = h.shape[0], h.shape[1]
    hf = h.astype(jnp.float32)
    csum = jnp.concatenate([jnp.zeros((b, 1, GROUP_W), jnp.float32), jnp.cumsum(hf, axis=1)], axis=1)
    t = jnp.arange(L)
    outs = []
    for gi, w in enumerate(POOL_WINDOWS):
        lo = jnp.clip(t - w // 2, 0, L)
        hi = jnp.clip(t + w // 2, 0, L)
        sl = slice(gi * POOL_GROUP_W, (gi + 1) * POOL_GROUP_W)
        cs = csum[..., sl]
        mean = (cs[:, hi] - cs[:, lo]) / (hi - lo).astype(jnp.float32)[:, None]
        y = (mean - hf[..., sl]).astype(h.dtype)
        outs.append(y @ pw[gi])
    return jnp.concatenate(outs, axis=-1) * pscale


def local_mixers(p, dw_w, dw_b, cln_g, cln_b, cpw, gln_g, gln_b, gws, gbs, pw, ps):
    conv = conformer_conv(p[..., OFF_CONV:OFF_GMLP], dw_w, dw_b, cln_g, cln_b, cpw)
    gm = chunk_gating(p[..., OFF_GMLP:OFF_POOL], gln_g, gln_b, gws, gbs)
    pl = pool_mix(p[..., OFF_POOL:IN_W], pw, ps)
    return jnp.concatenate([conv, gm, pl], axis=-1)


def merge_heads(att, local, gn_g, wo):
    return group_rmsnorm(jnp.concatenate([att, local], axis=-1), gn_g) @ wo


def clamped_swiglu(h):
    glu, lin = jnp.split(h, 2, axis=-1)
    glu = jnp.minimum(glu, SWIGLU_LIMIT)
    lin = jnp.clip(lin, -SWIGLU_LIMIT, SWIGLU_LIMIT)
    return glu * jax.nn.sigmoid(SWIGLU_ALPHA * glu) * (lin + 1.0)


def moe(h, rw, rb, w1, b1, w2, b2):
    n_tok, d = h.shape
    logits = (h @ rw + rb).astype(jnp.float32)
    top_val, top_idx = lax.top_k(logits, TOP_K)
    gates = jax.nn.softmax(top_val, axis=-1)
    n_asg = n_tok * TOP_K
    flat_e = top_idx.reshape(-1)
    order = jnp.argsort(flat_e)
    sorted_e = flat_e[order]
    src_tok = order // TOP_K
    counts = jnp.bincount(flat_e, length=N_EXPERTS)
    padded = (counts + MOE_BLOCK - 1) // MOE_BLOCK * MOE_BLOCK
    pad_end = jnp.cumsum(padded)
    pad_start = pad_end - padded
    start = jnp.cumsum(counts) - counts
    dest = pad_start[sorted_e] + jnp.arange(n_asg) - start[sorted_e]
    n_rows = -(-n_asg // MOE_BLOCK) * MOE_BLOCK + N_EXPERTS * MOE_BLOCK
    n_blk = n_rows // MOE_BLOCK
    row_tok = jnp.zeros((n_rows,), jnp.int32).at[dest].set(src_tok.astype(jnp.int32))
    row_w = jnp.zeros((n_rows,), h.dtype).at[dest].set(gates.reshape(-1)[order].astype(h.dtype))
    blk_e = jnp.minimum(jnp.searchsorted(pad_end, jnp.arange(n_blk) * MOE_BLOCK, side='right'),
                        N_EXPERTS - 1)
    xp = h[row_tok].reshape(n_blk, MOE_BLOCK, d)

    def run_block(args):
        xb, e = args
        return clamped_swiglu(xb @ w1[e] + b1[e]) @ w2[e] + b2[e]

    yp = lax.map(run_block, (xp, blk_e)).reshape(n_rows, d)
    return jax.ops.segment_sum(yp * row_w[:, None], row_tok, num_segments=n_tok)


def setup_inputs(seed: int = 0) -> dict:
    key = jax.random.key(seed)
    ks = jax.random.split(key, 32)

    def nrm(k, shape, scale):
        return jax.random.normal(k, shape, jnp.float32) * scale

    D = D_MODEL
    return {
        'x': nrm(ks[0], (BATCH, SEQ, D), 1.0),
        'c': nrm(ks[1], (BATCH, D), 1.0),
        'ctx': nrm(ks[2], (BATCH, CTX_LEN, D), 1.0),
        'c_ctx': nrm(ks[3], (D,), 1.0),
        'ada_w': nrm(ks[4], (DEPTH, D, 6 * D), 0.5 * D ** -0.5),
        'ada_b': nrm(ks[5], (DEPTH, 6 * D), 0.02),
        'norm1_g': 1.0 + nrm(ks[6], (DEPTH, D), 0.05),
        'norm2_g': 1.0 + nrm(ks[7], (DEPTH, D), 0.05),
        'w_in': nrm(ks[8], (DEPTH, D, IN_W), D ** -0.5),
        'attn_sink': nrm(ks[9], (DEPTH, N_Q_HEADS), 1.0),
        'conv_dw_w': nrm(ks[10], (DEPTH, CONV_WIDTH, GROUP_W), CONV_WIDTH ** -0.5),
        'conv_dw_b': nrm(ks[11], (DEPTH, GROUP_W), 0.02),
        'conv_ln_g': 1.0 + nrm(ks[12], (DEPTH, GROUP_W), 0.05),
        'conv_ln_b': nrm(ks[13], (DEPTH, GROUP_W), 0.02),
        'conv_pw_w': nrm(ks[14], (DEPTH, GROUP_W, GROUP_W), GROUP_W ** -0.5),
        'gmlp_ln_g': 1.0 + nrm(ks[15], (DEPTH, GROUP_W), 0.05),
        'gmlp_ln_b': nrm(ks[16], (DEPTH, GROUP_W), 0.02),
        'gmlp_ws': nrm(ks[17], (DEPTH, GMLP_HEADS, CHUNK, CHUNK), CHUNK ** -0.5),
        'gmlp_bs': 1.0 + nrm(ks[18], (DEPTH, GMLP_HEADS, CHUNK), 0.02),
        'pool_w': nrm(ks[19], (DEPTH, len(POOL_WINDOWS), POOL_GROUP_W, POOL_GROUP_W), POOL_GROUP_W ** -0.5),
        'pool_scale': 1.0 + nrm(ks[20], (DEPTH, GROUP_W), 0.05),
        'group_norm_g': 1.0 + nrm(ks[21], (DEPTH, MIX_W), 0.05),
        'w_out': nrm(ks[22], (DEPTH, MIX_W, D), MIX_W ** -0.5),
        'router_w': nrm(ks[23], (DEPTH, D, N_EXPERTS), D ** -0.5),
        'router_b': nrm(ks[24], (DEPTH, N_EXPERTS), 0.01),
        'exp_w1': nrm(ks[25], (DEPTH, N_EXPERTS, D, 2 * D_FF), D ** -0.5),
        'exp_b1': nrm(ks[26], (DEPTH, N_EXPERTS, 2 * D_FF), 0.02),
        'exp_w2': nrm(ks[27], (DEPTH, N_EXPERTS, D_FF, D), D_FF ** -0.5),
        'exp_b2': nrm(ks[28], (DEPTH, N_EXPERTS, D), 0.02),
        'final_norm_g': 1.0 + nrm(ks[29], (D,), 0.05),
    }


def reference(x, c, ctx, c_ctx, ada_w, ada_b, norm1_g, norm2_g, w_in, attn_sink,
              conv_dw_w, conv_dw_b, conv_ln_g, conv_ln_b, conv_pw_w,
              gmlp_ln_g, gmlp_ln_b, gmlp_ws, gmlp_bs, pool_w, pool_scale,
              group_norm_g, w_out, router_w, router_b, exp_w1, exp_b1, exp_w2, exp_b2,
              final_norm_g):
    b, L, d = x.shape
    lc = ctx.shape[1]
    kv_w = N_KV_HEADS * HEAD_DIM
    tabs = axial_rope_tables(L)
    for l in range(DEPTH):
        last = l == DEPTH - 1
        mx = adaln(c, ada_w[l], ada_b[l])[:, :, None, :]
        sh1, sc1, g1, sh2, sc2, g2 = [mx[:, i] for i in range(6)]
        mc = adaln(c_ctx, ada_w[l], ada_b[l])
        sh1c, sc1c, g1c, sh2c, sc2c, g2c = [mc[i] for i in range(6)]
        lp = (conv_dw_w[l], conv_dw_b[l], conv_ln_g[l], conv_ln_b[l], conv_pw_w[l],
              gmlp_ln_g[l], gmlp_ln_b[l], gmlp_ws[l], gmlp_bs[l], pool_w[l], pool_scale[l])

        hx = rmsnorm(x, norm1_g[l]) * (1.0 + sc1) + sh1
        hc = rmsnorm(ctx, norm1_g[l]) * (1.0 + sc1c) + sh1c
        px = hx @ w_in[l]
        if last:
            kvc = hc @ w_in[l][:, OFF_K:OFF_CONV]
        else:
            pc = hc @ w_in[l]
            kvc = pc[..., OFF_K:OFF_CONV]
        kc = kvc[..., :kv_w].reshape(b, lc, N_KV_HEADS, HEAD_DIM)
        vc = kvc[..., kv_w:].reshape(b, lc, N_KV_HEADS, HEAD_DIM)

        q = apply_axial_rope(px[..., OFF_Q:OFF_K].reshape(b, L, N_Q_HEADS, HEAD_DIM), tabs)
        k = apply_axial_rope(px[..., OFF_K:OFF_V].reshape(b, L, N_KV_HEADS, HEAD_DIM), tabs)
        v = px[..., OFF_V:OFF_CONV].reshape(b, L, N_KV_HEADS, HEAD_DIM)
        att_x = window_attention(q, k, v, kc, vc, attn_sink[l])
        x = x + g1 * merge_heads(att_x, local_mixers(px, *lp), group_norm_g[l], w_out[l])
        if not last:
            qc = pc[..., OFF_Q:OFF_K].reshape(b, lc, N_Q_HEADS, HEAD_DIM)
            att_c = ctx_attention(qc, kc, vc, attn_sink[l])
            ctx = ctx + g1c * merge_heads(att_c, local_mixers(pc, *lp), group_norm_g[l], w_out[l])

        h2x = (rmsnorm(x, norm2_g[l]) * (1.0 + sc2) + sh2).reshape(b * L, d)
        if last:
            ff = moe(h2x, router_w[l], router_b[l], exp_w1[l], exp_b1[l], exp_w2[l], exp_b2[l])
            x = x + g2 * ff.reshape(b, L, d)
        else:
            h2c = (rmsnorm(ctx, norm2_g[l]) * (1.0 + sc2c) + sh2c).reshape(b * lc, d)
            ff = moe(jnp.concatenate([h2x, h2c], axis=0), router_w[l], router_b[l],
                     exp_w1[l], exp_b1[l], exp_w2[l], exp_b2[l])
            x = x + g2 * ff[:b * L].reshape(b, L, d)
            ctx = ctx + g2c * ff[b * L:].reshape(b, lc, d)
    return rmsnorm(x, final_norm_g)
```

```python
import functools

import jax
import jax.numpy as jnp
import numpy as np
from jax import lax
from jax.experimental import pallas as pl
from jax.experimental.pallas import tpu as pltpu

F32 = jnp.float32
BF16 = jnp.bfloat16

D_MODEL = 1024
GRID_W = 64
GROUP_W = 256
HEAD_DIM = 64
N_Q_HEADS = 4
N_KV_HEADS = 2
WINDOW = 128
ATT_BLOCK = 128
ROPE_BASE = 10000.0
CONV_WIDTH = 31
CHUNK = 128
GMLP_HEADS = 4
POOL_WINDOWS = (2, 4, 8, 16)
N_EXPERTS = 32
TOP_K = 4
D_FF = 1024
SWIGLU_ALPHA = 1.702
SWIGLU_LIMIT = 7.0
EPS = 1e-6
NEG_INF = -1e30

OFF_K = 256
OFF_V = 384
OFF_CONV = 512
IN_W = 1792
MIX_IN_W = IN_W - OFF_CONV
LOCAL_W = 3 * GROUP_W

LANES = 128
HALO = 16
ROUTE_W = LANES
MOE_TM = 512
FF_CHUNK = 512
VMEM_LIMIT = 56 * 1024 * 1024


def _cparams(sem):
    return pltpu.CompilerParams(dimension_semantics=sem, vmem_limit_bytes=VMEM_LIMIT)


def _ada_kernel(c_ref, w_ref, b_ref, o_ref):
    c = c_ref[...]
    s = c * jax.nn.sigmoid(c)
    o_ref[0] = jnp.dot(s.astype(BF16), w_ref[0].astype(BF16), preferred_element_type=F32) + b_ref[0]


def _adaln(cond, ada_w, ada_b):
    depth, d, n = ada_w.shape
    r = cond.shape[0]
    tn = 1536
    return pl.pallas_call(
        _ada_kernel,
        grid=(depth, n // tn),
        in_specs=[pl.BlockSpec((r, d), lambda l, j: (0, 0)),
                  pl.BlockSpec((1, d, tn), lambda l, j: (l, 0, j)),
                  pl.BlockSpec((1, 1, tn), lambda l, j: (l, 0, j))],
        out_specs=pl.BlockSpec((1, r, tn), lambda l, j: (l, 0, j)),
        out_shape=jax.ShapeDtypeStruct((depth, r, n), F32),
        compiler_params=_cparams(("arbitrary", "arbitrary")),
    )(cond, ada_w, ada_b.reshape(depth, 1, n))


def _modulated_rmsnorm(x, g, sc, sh):
    ms = jnp.mean(x * x, axis=-1, keepdims=True)
    return x * lax.rsqrt(ms + EPS) * (g * (1.0 + sc)) + sh


def _inproj_kernel(*refs, rope, full):
    if rope:
        x_ref, sc_ref, sh_ref, g_ref, w_ref, cos_ref, s1_ref, s2_ref = refs[:8]
        outs = refs[8:]
    else:
        x_ref, sc_ref, sh_ref, g_ref, w_ref = refs[:5]
        outs = refs[5:]
    h = _modulated_rmsnorm(x_ref[0], g_ref[...], sc_ref[0], sh_ref[0])
    p = jnp.dot(h.astype(BF16), w_ref[...], preferred_element_type=F32)

    def roped(xs):
        if not rope:
            return xs
        return (xs * cos_ref[...] + pltpu.roll(xs, LANES - 16, 1) * s1_ref[...]
                + pltpu.roll(xs, 16, 1) * s2_ref[...])

    if full:
        q_ref, kv_ref, mix_ref = outs
        q_ref[0] = jnp.concatenate([roped(p[:, 0:128]), roped(p[:, 128:256])], axis=1).astype(BF16)
        kv_ref[0] = jnp.concatenate([roped(p[:, 256:384]), p[:, 384:512]], axis=1).astype(BF16)
        mix_ref[0] = p[:, OFF_CONV:]
    else:
        (kv_ref,) = outs
        kv_ref[0] = p.astype(BF16)


def _inproj(x, sc, sh, g, w, tabs, *, full):
    b, L, d = x.shape
    n = w.shape[1]
    tm = min(512, L)
    rope = tabs is not None
    bm = sc.shape[0]
    mod_map = (lambda i, bb: (bb, 0, 0)) if bm > 1 else (lambda i, bb: (0, 0, 0))
    in_specs = [pl.BlockSpec((1, tm, d), lambda i, bb: (bb, i, 0)),
                pl.BlockSpec((1, 1, d), mod_map),
                pl.BlockSpec((1, 1, d), mod_map),
                pl.BlockSpec((1, d), lambda i, bb: (0, 0)),
                pl.BlockSpec((d, n), lambda i, bb: (0, 0))]
    args = [x, sc, sh, g.reshape(1, d), w]
    if rope:
        in_specs += [pl.BlockSpec((tm, LANES), lambda i, bb: (i, 0))] * 3
        args += list(tabs)
    if full:
        out_shape = (jax.ShapeDtypeStruct((b, L, 256), BF16),
                     jax.ShapeDtypeStruct((b, L, 256), BF16),
                     jax.ShapeDtypeStruct((b, L, MIX_IN_W), F32))
        out_specs = (pl.BlockSpec((1, tm, 256), lambda i, bb: (bb, i, 0)),
                     pl.BlockSpec((1, tm, 256), lambda i, bb: (bb, i, 0)),
                     pl.BlockSpec((1, tm, MIX_IN_W), lambda i, bb: (bb, i, 0)))
    else:
        out_shape = (jax.ShapeDtypeStruct((b, L, 256), BF16),)
        out_specs = (pl.BlockSpec((1, tm, 256), lambda i, bb: (bb, i, 0)),)
    return pl.pallas_call(
        functools.partial(_inproj_kernel, rope=rope, full=full),
        grid=(L // tm, b),
        in_specs=in_specs, out_specs=out_specs, out_shape=out_shape,
        compiler_params=_cparams(("arbitrary", "arbitrary")),
    )(*args)


def _attn_kernel(sink_ref, q_ref, kv_ref, kvc_ref, g_ref, o_ref, *, local, seq_len):
    n = pl.program_id(1)
    q = q_ref[0]
    lane = lax.broadcasted_iota(jnp.int32, (1, LANES), 1)
    row = lax.broadcasted_iota(jnp.int32, (2 * ATT_BLOCK, 1), 0)
    kvc = kvc_ref[0]
    kc, vc = kvc[:, :LANES], kvc[:, LANES:]
    nt = (((1,), (1,)), ((), ()))
    if local:
        start = jnp.clip((n - 1) * ATT_BLOCK, 0, seq_len - 3 * ATT_BLOCK)
        start = pl.multiple_of(start, ATT_BLOCK)
        kw = kv_ref[0, pl.ds(start, 3 * ATT_BLOCK), :]
        kl, vl = kw[:, :LANES], kw[:, LANES:]
        qpos = n * ATT_BLOCK + lax.rem(
            lax.broadcasted_iota(jnp.int32, (2 * ATT_BLOCK, 3 * ATT_BLOCK), 0), ATT_BLOCK)
        kpos = start + lax.broadcasted_iota(jnp.int32, (2 * ATT_BLOCK, 3 * ATT_BLOCK), 1)
        band = jnp.abs(kpos - qpos) <= WINDOW
    zero = jnp.zeros((), BF16)
    outs = []
    for kh in range(N_KV_HEADS):
        lm = (lane < HEAD_DIM) if kh == 0 else (lane >= HEAD_DIM)
        lhs = jnp.concatenate([jnp.where(lm, q[:, :LANES], zero), jnp.where(lm, q[:, LANES:], zero)], axis=0)
        sk = jnp.where(row < ATT_BLOCK, sink_ref[2 * kh], sink_ref[2 * kh + 1])
        s_c = lax.dot_general(lhs, kc, nt, preferred_element_type=F32)
        m = jnp.maximum(sk, jnp.max(s_c, axis=-1, keepdims=True))
        if local:
            s_l = lax.dot_general(lhs, kl, nt, preferred_element_type=F32)
            s_l = jnp.where(band, s_l, NEG_INF)
            m = jnp.maximum(m, jnp.max(s_l, axis=-1, keepdims=True))
        e_c = jnp.exp(s_c - m)
        den = jnp.exp(sk - m) + jnp.sum(e_c, axis=-1, keepdims=True)
        o = jnp.dot(e_c.astype(BF16), vc, preferred_element_type=F32)
        if local:
            e_l = jnp.exp(s_l - m)
            den = den + jnp.sum(e_l, axis=-1, keepdims=True)
            o = o + jnp.dot(e_l.astype(BF16), vl, preferred_element_type=F32)
        outs.append(o / den)
    o01, o23 = outs
    lm0 = lane < HEAD_DIM
    att = jnp.concatenate([jnp.where(lm0, o01[:ATT_BLOCK], o23[:ATT_BLOCK]),
                           jnp.where(lm0, o01[ATT_BLOCK:], o23[ATT_BLOCK:])], axis=1)
    ms = jnp.mean(att * att, axis=-1, keepdims=True)
    o_ref[0] = (att * lax.rsqrt(ms + EPS) * g_ref[...]).astype(BF16)


def _attention(q, kv, kvc, sink, g_att, *, local):
    b, L, _ = q.shape
    lc = kvc.shape[1]
    nb = L // ATT_BLOCK
    return pl.pallas_call(
        functools.partial(_attn_kernel, local=local, seq_len=L),
        grid=(b, nb),
        in_specs=[pl.BlockSpec(memory_space=pltpu.SMEM),
                  pl.BlockSpec((1, ATT_BLOCK, 256), lambda bb, i: (bb, i, 0)),
                  pl.BlockSpec((1, kv.shape[1], 256), lambda bb, i: (bb, 0, 0)),
                  pl.BlockSpec((1, lc, 256), lambda bb, i: (bb, 0, 0)),
                  pl.BlockSpec((1, 256), lambda bb, i: (0, 0))],
        out_specs=pl.BlockSpec((1, ATT_BLOCK, 256), lambda bb, i: (bb, i, 0)),
        out_shape=jax.ShapeDtypeStruct((b, L, 256), BF16),
        compiler_params=_cparams(("arbitrary", "arbitrary")),
    )(sink, q, kv, kvc, g_att.reshape(1, 256))


def _layernorm(x, g, b):
    mu = jnp.mean(x, axis=-1, keepdims=True)
    xc = x - mu
    return xc * lax.rsqrt(jnp.mean(xc * xc, axis=-1, keepdims=True) + EPS) * g + b


def _group_norm_store(y, g):
    ms = jnp.mean(y * y, axis=-1, keepdims=True)
    return (y * lax.rsqrt(ms + EPS) * g).astype(BF16)


def _mix_kernel(cur_ref, prev_ref, next_ref, dww_ref, dwb_ref, clg_ref, clb_ref, cpw_ref,
                glg_ref, glb_ref, gws_ref, gbs_ref, pw_ref, ps_ref, gn_ref, o_ref,
                hc_ref, hp_ref, *, ts, seq_len):
    i = pl.program_id(1)
    nt = pl.num_programs(1)
    has_prev = (i > 0).astype(F32)
    has_next = (i < nt - 1).astype(F32)

    def glu(blk):
        return blk[:, 0:256] * jax.nn.sigmoid(blk[:, 256:512])

    pv = prev_ref[0]
    nx = next_ref[0]
    hc_ref[0:HALO, :] = glu(pv) * has_prev
    hc_ref[HALO + ts:2 * HALO + ts, :] = glu(nx) * has_next
    hp_ref[0:HALO, :] = pv[:, 1024:1280] * has_prev
    hp_ref[HALO + ts:2 * HALO + ts, :] = nx[:, 1024:1280] * has_next
    hc_ref[HALO:HALO + ts, :] = glu(cur_ref[0, :, 0:512])
    hp_ref[HALO:HALO + ts, :] = cur_ref[0, :, 1024:1280]

    lane = lax.broadcasted_iota(jnp.int32, (1, 256), 1)
    rc = 64
    for c in range(ts // rc):
        r0 = c * rc
        acc = jnp.zeros((rc, 256), F32)
        for j in range(CONV_WIDTH):
            acc = acc + hc_ref[r0 + 1 + j:r0 + 1 + j + rc, :] * dww_ref[j:j + 1, :]
        hcv = _layernorm(acc + dwb_ref[...], clg_ref[...], clb_ref[...])
        hcv = hcv * jax.nn.sigmoid(hcv)
        conv = jnp.dot(hcv.astype(BF16), cpw_ref[...], preferred_element_type=F32)
        o_ref[0, r0:r0 + rc, 0:256] = _group_norm_store(conv, gn_ref[:, 0:256])

        def tap(off):
            return hp_ref[r0 + HALO + off:r0 + HALO + off + rc, :]
        hcur = tap(0)
        s2 = tap(-1) + hcur
        s4 = s2 + tap(-2) + tap(1)
        s8 = s4 + tap(-4) + tap(-3) + tap(2) + tap(3)
        s16 = s8 + tap(-8) + tap(-7) + tap(-6) + tap(-5) + tap(4) + tap(5) + tap(6) + tap(7)
        wsum = jnp.where(lane < 64, s2, jnp.where(lane < 128, s4, jnp.where(lane < 192, s8, s16)))
        half = jnp.where(lane < 64, 1, jnp.where(lane < 128, 2, jnp.where(lane < 192, 4, 8)))
        t = i * ts + r0 + lax.broadcasted_iota(jnp.int32, (rc, 1), 0)
        cnt = jnp.clip(t + half, 0, seq_len) - jnp.clip(t - half, 0, seq_len)
        y = wsum / cnt.astype(F32) - hcur
        pool = jnp.dot(y.astype(BF16), pw_ref[...], preferred_element_type=F32) * ps_ref[...]
        o_ref[0, r0:r0 + rc, 512:768] = _group_norm_store(pool, gn_ref[:, 512:768])

    for c in range(ts // CHUNK):
        r0 = c * CHUNK
        u = cur_ref[0, r0:r0 + CHUNK, 512:768]
        v = _layernorm(cur_ref[0, r0:r0 + CHUNK, 768:1024], glg_ref[...], glb_ref[...])
        r = jnp.dot(gws_ref[...], v.astype(BF16), preferred_element_type=F32)
        mixed = jnp.where(lane < 64, r[0:CHUNK],
                          jnp.where(lane < 128, r[CHUNK:2 * CHUNK],
                                    jnp.where(lane < 192, r[2 * CHUNK:3 * CHUNK], r[3 * CHUNK:])))
        gm = u * (mixed + gbs_ref[...])
        o_ref[0, r0:r0 + CHUNK, 256:512] = _group_norm_store(gm, gn_ref[:, 256:512])


def _local_mixers(mix, lp):
    b, L, w = mix.shape
    ts = min(512, L)
    nt = L // ts
    hb = ts // HALO
    last_h = L // HALO - 1
    full = lambda shape: pl.BlockSpec(shape, lambda bb, i: (0,) * len(shape))
    params = [lp["dw_w"], lp["dw_b"], lp["cln_g"], lp["cln_b"], lp["cpw"], lp["gln_g"], lp["gln_b"],
              lp["gws"], lp["gbs"], lp["pw"], lp["ps"], lp["gn_local"]]
    return pl.pallas_call(
        functools.partial(_mix_kernel, ts=ts, seq_len=L),
        grid=(b, nt),
        in_specs=[pl.BlockSpec((1, ts, w), lambda bb, i: (bb, i, 0)),
                  pl.BlockSpec((1, HALO, w), lambda bb, i: (bb, jnp.maximum(i * hb - 1, 0), 0)),
                  pl.BlockSpec((1, HALO, w), lambda bb, i: (bb, jnp.minimum((i + 1) * hb, last_h), 0))]
                 + [full(p.shape) for p in params],
        out_specs=pl.BlockSpec((1, ts, LOCAL_W), lambda bb, i: (bb, i, 0)),
        out_shape=jax.ShapeDtypeStruct((b, L, LOCAL_W), BF16),
        scratch_shapes=[pltpu.VMEM((ts + 2 * HALO, 256), F32), pltpu.VMEM((ts + 2 * HALO, 256), F32)],
        compiler_params=_cparams(("arbitrary", "arbitrary")),
    )(mix, mix, mix, *params)


def _merge_kernel(att_ref, loc_ref, x_ref, g1_ref, sc_ref, sh_ref, n2g_ref, woa_ref, wol_ref,
                  rw_ref, rb_ref, xo_ref, h2_ref, gate_ref, idx_ref):
    y = (jnp.dot(att_ref[0], woa_ref[...], preferred_element_type=F32)
         + jnp.dot(loc_ref[0], wol_ref[...], preferred_element_type=F32))
    xn = x_ref[0] + g1_ref[0] * y
    xo_ref[0] = xn
    h2 = _modulated_rmsnorm(xn, n2g_ref[...], sc_ref[0], sh_ref[0]).astype(BF16)
    h2_ref[0] = h2
    logits = jnp.dot(h2, rw_ref[...], preferred_element_type=F32) + rb_ref[...]
    lane = lax.broadcasted_iota(jnp.int32, logits.shape, 1)
    vals = jnp.zeros(logits.shape, F32)
    idxs = jnp.zeros(logits.shape, jnp.int32)
    top = None
    for k in range(TOP_K):
        m = jnp.max(logits, axis=-1, keepdims=True)
        sel = jnp.min(jnp.where(logits == m, lane, ROUTE_W), axis=-1, keepdims=True)
        if top is None:
            top = m
        vals = jnp.where(lane == k, jnp.exp(m - top), vals)
        idxs = jnp.where(lane == k, sel, idxs)
        logits = jnp.where(lane == sel, NEG_INF * 2.0, logits)
    gate_ref[0] = vals / jnp.sum(vals, axis=-1, keepdims=True)
    idx_ref[0] = idxs


def _merge(att, loc, x, g1, sc2, sh2, n2g, woa, wol, rw, rb):
    b, L, d = x.shape
    tm = min(512, L)
    bm = g1.shape[0]
    mod_map = (lambda bb, i: (bb, 0, 0)) if bm > 1 else (lambda bb, i: (0, 0, 0))
    tile = lambda w: pl.BlockSpec((1, tm, w), lambda bb, i: (bb, i, 0))
    full = lambda shape: pl.BlockSpec(shape, lambda bb, i: (0,) * len(shape))
    return pl.pallas_call(
        _merge_kernel,
        grid=(b, L // tm),
        in_specs=[tile(256), tile(LOCAL_W), tile(d),
                  pl.BlockSpec((1, 1, d), mod_map), pl.BlockSpec((1, 1, d), mod_map),
                  pl.BlockSpec((1, 1, d), mod_map),
                  full((1, d)), full(woa.shape), full(wol.shape), full(rw.shape), full(rb.shape)],
        out_specs=(tile(d), tile(d), tile(ROUTE_W), tile(ROUTE_W)),
        out_shape=(jax.ShapeDtypeStruct((b, L, d), F32),
                   jax.ShapeDtypeStruct((b, L, d), BF16),
                   jax.ShapeDtypeStruct((b, L, ROUTE_W), F32),
                   jax.ShapeDtypeStruct((b, L, ROUTE_W), jnp.int32)),
        compiler_params=_cparams(("arbitrary", "arbitrary")),
    )(att, loc, x, g1, sc2, sh2, n2g.reshape(1, d), woa, wol, rw, rb)


def _moe_kernel(blk_e_ref, n_used_ref, x_ref, w1_ref, b1_ref, w2_ref, b2_ref, o_ref):
    i = pl.program_id(0)

    @pl.when(i < n_used_ref[0])
    def _():
        x = x_ref[...]
        acc = jnp.zeros((x.shape[0], D_MODEL), F32)
        for c in range(D_FF // FF_CHUNK):
            lo = c * FF_CHUNK
            glu = jnp.dot(x, w1_ref[0, :, lo:lo + FF_CHUNK], preferred_element_type=F32) \
                + b1_ref[0, :, lo:lo + FF_CHUNK]
            lin = jnp.dot(x, w1_ref[0, :, D_FF + lo:D_FF + lo + FF_CHUNK], preferred_element_type=F32) \
                + b1_ref[0, :, D_FF + lo:D_FF + lo + FF_CHUNK]
            glu = jnp.minimum(glu, SWIGLU_LIMIT)
            lin = jnp.clip(lin, -SWIGLU_LIMIT, SWIGLU_LIMIT)
            act = glu * jax.nn.sigmoid(SWIGLU_ALPHA * glu) * (lin + 1.0)
            acc = acc + jnp.dot(act.astype(BF16), w2_ref[0, lo:lo + FF_CHUNK, :], preferred_element_type=F32)
        o_ref[...] = (acc + b2_ref[0]).astype(o_ref.dtype)

    @pl.when(i >= n_used_ref[0])
    def _():
        o_ref[...] = jnp.zeros(o_ref.shape, o_ref.dtype)


def _moe_ffn(xp, blk_e, n_used, w1, b1, w2, b2):
    n_rows, d = xp.shape
    n_blk = n_rows // MOE_TM
    e = w1.shape[0]

    def row_map(i, be, nu):
        return (jnp.minimum(i, nu[0] - 1), 0)

    def w_map(i, be, nu):
        return (be[i], 0, 0)

    return pl.pallas_call(
        _moe_kernel,
        grid_spec=pltpu.PrefetchScalarGridSpec(
            num_scalar_prefetch=2,
            grid=(n_blk,),
            in_specs=[pl.BlockSpec((MOE_TM, d), row_map),
                      pl.BlockSpec((1, d, 2 * D_FF), w_map),
                      pl.BlockSpec((1, 1, 2 * D_FF), w_map),
                      pl.BlockSpec((1, D_FF, d), w_map),
                      pl.BlockSpec((1, 1, d), w_map)],
            out_specs=pl.BlockSpec((MOE_TM, d), lambda i, be, nu: (i, 0))),
        out_shape=jax.ShapeDtypeStruct((n_rows, d), BF16),
        compiler_params=_cparams(("arbitrary",)),
    )(blk_e, n_used, xp, w1, b1.reshape(e, 1, 2 * D_FF), w2, b2.reshape(e, 1, d))


def _combine_kernel(yg_ref, gate_ref, x_ref, g2_ref, fg_ref, o_ref, *, final):
    gates = gate_ref[0]
    ff = jnp.zeros(x_ref.shape[1:], F32)
    for k in range(TOP_K):
        ff = ff + gates[:, k:k + 1] * yg_ref[k, 0].astype(F32)
    xo = x_ref[0] + g2_ref[0] * ff
    if final:
        ms = jnp.mean(xo * xo, axis=-1, keepdims=True)
        xo = xo * lax.rsqrt(ms + EPS) * fg_ref[...]
    o_ref[0] = xo


def _combine(yg, gates, x, g2, fg, *, final):
    b, L, d = x.shape
    tm = min(256, L)
    bm = g2.shape[0]
    mod_map = (lambda bb, i: (bb, 0, 0)) if bm > 1 else (lambda bb, i: (0, 0, 0))
    return pl.pallas_call(
        functools.partial(_combine_kernel, final=final),
        grid=(b, L // tm),
        in_specs=[pl.BlockSpec((TOP_K, 1, tm, d), lambda bb, i: (0, bb, i, 0)),
                  pl.BlockSpec((1, tm, ROUTE_W), lambda bb, i: (bb, i, 0)),
                  pl.BlockSpec((1, tm, d), lambda bb, i: (bb, i, 0)),
                  pl.BlockSpec((1, 1, d), mod_map),
                  pl.BlockSpec((1, d), lambda bb, i: (0, 0))],
        out_specs=pl.BlockSpec((1, tm, d), lambda bb, i: (bb, i, 0)),
        out_shape=jax.ShapeDtypeStruct((b, L, d), F32),
        compiler_params=_cparams(("arbitrary", "arbitrary")),
    )(yg, gates, x, g2, fg.reshape(1, d))


def _dispatch_plan(idx):
    n_tok = idx.shape[0]
    n_asg = n_tok * TOP_K
    flat_e = idx.reshape(-1)
    order = jnp.argsort(flat_e, stable=True)
    sorted_e = flat_e[order]
    counts = jnp.bincount(flat_e, length=N_EXPERTS)
    padded = (counts + MOE_TM - 1) // MOE_TM * MOE_TM
    pad_end = jnp.cumsum(padded)
    pad_start = pad_end - padded
    start = jnp.cumsum(counts) - counts
    dest = (pad_start[sorted_e] + jnp.arange(n_asg) - start[sorted_e]).astype(jnp.int32)
    n_rows = -(-n_asg // MOE_TM) * MOE_TM + N_EXPERTS * MOE_TM
    n_blk = n_rows // MOE_TM
    row_tok = jnp.zeros((n_rows,), jnp.int32).at[dest].set((order // TOP_K).astype(jnp.int32))
    pos = jnp.zeros((n_asg,), jnp.int32).at[order].set(dest)
    blk_e = jnp.minimum(jnp.searchsorted(pad_end, jnp.arange(n_blk) * MOE_TM, side='right'),
                        N_EXPERTS - 1).astype(jnp.int32)
    n_used = (pad_end[-1] // MOE_TM).astype(jnp.int32).reshape(1)
    return row_tok, pos.reshape(n_tok, TOP_K), blk_e, n_used


_Q_PERM = np.concatenate([np.arange(0, 64), np.arange(128, 192), np.arange(64, 128), np.arange(192, 256)])


def _rope_tables(L):
    t = jnp.arange(L)
    row = (t // GRID_W).astype(F32)
    col = (t % GRID_W).astype(F32)
    half = HEAD_DIM // 2
    inv = ROPE_BASE ** (-jnp.arange(0, half, 2, dtype=F32) / half)
    ar, ac = row[:, None] * inv[None, :], col[:, None] * inv[None, :]
    z = jnp.zeros_like(ar)
    cos = jnp.concatenate([jnp.cos(ar), jnp.cos(ar), jnp.cos(ac), jnp.cos(ac)], axis=1)
    s1 = jnp.concatenate([-jnp.sin(ar), z, -jnp.sin(ac), z], axis=1)
    s2 = jnp.concatenate([z, jnp.sin(ar), z, jnp.sin(ac)], axis=1)
    rep = lambda a: jnp.concatenate([a, a], axis=1)
    return rep(cos), rep(s1), rep(s2)


def _block_diag(pw):
    g, n, _ = pw.shape
    out = jnp.zeros((g * n, g * n), pw.dtype)
    for gi in range(g):
        out = out.at[gi * n:(gi + 1) * n, gi * n:(gi + 1) * n].set(pw[gi])
    return out


def kernel(x, c, ctx, c_ctx, ada_w, ada_b, norm1_g, norm2_g, w_in, attn_sink, conv_dw_w, conv_dw_b, conv_ln_g, conv_ln_b, conv_pw_w, gmlp_ln_g, gmlp_ln_b, gmlp_ws, gmlp_bs, pool_w, pool_scale, group_norm_g, w_out, router_w, router_b, exp_w1, exp_b1, exp_w2, exp_b2, final_norm_g):
    b, L, d = x.shape
    lc = ctx.shape[1]
    depth = ada_w.shape[0]
    tabs = _rope_tables(L)

    r = -(-(b + 1) // 8) * 8
    cond = jnp.zeros((r, d), F32).at[:b].set(c).at[b].set(c_ctx)
    mods = _adaln(cond, ada_w, ada_b).reshape(depth, r, 6, d)

    for l in range(depth):
        last = l == depth - 1
        mx = [mods[l, :b, i].reshape(b, 1, d) for i in range(6)]
        mc = [mods[l, b:b + 1, i].reshape(1, 1, d) for i in range(6)]
        sh1, sc1, g1, sh2, sc2, g2 = mx
        sh1c, sc1c, g1c, sh2c, sc2c, g2c = mc

        wl = w_in[l]
        wq = wl[:, :OFF_K][:, _Q_PERM] * (HEAD_DIM ** -0.5)
        w_full = jnp.concatenate([wq, wl[:, OFF_K:]], axis=1).astype(BF16)
        gn = group_norm_g[l]
        g_att = gn[:256][_Q_PERM]
        wo = w_out[l]
        woa = wo[:256][_Q_PERM].astype(BF16)
        wol = wo[256:].astype(BF16)
        lp = dict(
            dw_w=conv_dw_w[l], dw_b=conv_dw_b[l].reshape(1, -1),
            cln_g=conv_ln_g[l].reshape(1, -1), cln_b=conv_ln_b[l].reshape(1, -1),
            cpw=conv_pw_w[l].astype(BF16),
            gln_g=gmlp_ln_g[l].reshape(1, -1), gln_b=gmlp_ln_b[l].reshape(1, -1),
            gws=gmlp_ws[l].reshape(GMLP_HEADS * CHUNK, CHUNK).astype(BF16),
            gbs=jnp.repeat(gmlp_bs[l].T, GROUP_W // GMLP_HEADS, axis=1),
            pw=_block_diag(pool_w[l]).astype(BF16), ps=pool_scale[l].reshape(1, -1),
            gn_local=gn[256:].reshape(1, -1))
        rw = jnp.zeros((d, ROUTE_W), F32).at[:, :N_EXPERTS].set(router_w[l]).astype(BF16)
        rb = jnp.full((1, ROUTE_W), NEG_INF, F32).at[0, :N_EXPERTS].set(router_b[l])
        w1 = exp_w1[l].astype(BF16)
        w2 = exp_w2[l].astype(BF16)

        q, kv, mix = _inproj(x, sc1, sh1, norm1_g[l], w_full, tabs, full=True)
        if last:
            (kvc,) = _inproj(ctx, sc1c, sh1c, norm1_g[l], wl[:, OFF_K:OFF_CONV].astype(BF16), None, full=False)
        else:
            qc, kvc, mixc = _inproj(ctx, sc1c, sh1c, norm1_g[l], w_full, None, full=True)
        att = _attention(q, kv, kvc, attn_sink[l], g_att, local=True)
        loc = _local_mixers(mix, lp)
        x, h2x, gx, ix = _merge(att, loc, x, g1, sc2, sh2, norm2_g[l], woa, wol, rw, rb)
        if not last:
            attc = _attention(qc, kvc, kvc, attn_sink[l], g_att, local=False)
            locc = _local_mixers(mixc, lp)
            ctx, h2c, gc, ic = _merge(attc, locc, ctx, g1c, sc2c, sh2c, norm2_g[l], woa, wol, rw, rb)

        if last:
            h2 = h2x.reshape(b * L, d)
            idx = ix.reshape(b * L, ROUTE_W)[:, :TOP_K]
        else:
            h2 = jnp.concatenate([h2x.reshape(b * L, d), h2c.reshape(b * lc, d)], axis=0)
            idx = jnp.concatenate([ix.reshape(b * L, ROUTE_W), ic.reshape(b * lc, ROUTE_W)], axis=0)[:, :TOP_K]
        row_tok, pos, blk_e, n_used = _dispatch_plan(idx)
        xp = h2[row_tok]
        yp = _moe_ffn(xp, blk_e, n_used, w1, exp_b1[l], w2, exp_b2[l])
        yg = yp[pos.T]
        ygx = yg[:, :b * L].reshape(TOP_K, b, L, d)
        x = _combine(ygx, gx, x, g2, final_norm_g, final=last)
        if not last:
            ygc = yg[:, b * L:].reshape(TOP_K, b, lc, d)
            ctx = _combine(ygc, gc, ctx, g2c, final_norm_g, final=False)
    return x
```

```python
import functools

import jax
import jax.numpy as jnp
import numpy as np
from jax import lax
from jax.experimental import pallas as pl
from jax.experimental.pallas import tpu as pltpu

F32 = jnp.float32
BF16 = jnp.bfloat16

D_MODEL = 1024
GRID_W = 64
GROUP_W = 256
HEAD_DIM = 64
N_Q_HEADS = 4
N_KV_HEADS = 2
WINDOW = 128
ATT_BLOCK = 128
ROPE_BASE = 10000.0
CONV_WIDTH = 31
CHUNK = 128
GMLP_HEADS = 4
POOL_WINDOWS = (2, 4, 8, 16)
N_EXPERTS = 32
TOP_K = 4
D_FF = 1024
SWIGLU_ALPHA = 1.702
SWIGLU_LIMIT = 7.0
EPS = 1e-6
NEG_INF = -1e30

OFF_K = 256
OFF_V = 384
OFF_CONV = 512
IN_W = 1792
MIX_IN_W = IN_W - OFF_CONV
LOCAL_W = 3 * GROUP_W

LANES = 128
HALO = 16
ROUTE_W = LANES
MOE_TM = 512
FF_CHUNK = 512
VMEM_LIMIT = 56 * 1024 * 1024


def _cparams(sem):
    return pltpu.CompilerParams(dimension_semantics=sem, vmem_limit_bytes=VMEM_LIMIT)


def _ada_kernel(c_ref, w_ref, b_ref, o_ref):
    c = c_ref[...]
    s = c * jax.nn.sigmoid(c)
    o_ref[0] = jnp.dot(s.astype(BF16), w_ref[0].astype(BF16), preferred_element_type=F32) + b_ref[0]


def _adaln(cond, ada_w, ada_b):
    depth, d, n = ada_w.shape
    r = cond.shape[0]
    tn = 1536
    return pl.pallas_call(
        _ada_kernel,
        grid=(depth, n // tn),
        in_specs=[pl.BlockSpec((r, d), lambda l, j: (0, 0)),
                  pl.BlockSpec((1, d, tn), lambda l, j: (l, 0, j)),
                  pl.BlockSpec((1, 1, tn), lambda l, j: (l, 0, j))],
        out_specs=pl.BlockSpec((1, r, tn), lambda l, j: (l, 0, j)),
        out_shape=jax.ShapeDtypeStruct((depth, r, n), F32),
        compiler_params=_cparams(("arbitrary", "arbitrary")),
        name="adaln",
    )(cond, ada_w, ada_b.reshape(depth, 1, n))


def _modulated_rmsnorm(x, g, sc, sh):
    ms = jnp.mean(x * x, axis=-1, keepdims=True)
    return x * lax.rsqrt(ms + EPS) * (g * (1.0 + sc)) + sh


def _inproj_kernel(*refs, rope, full):
    if rope:
        x_ref, sc_ref, sh_ref, g_ref, w_ref, cos_ref, s1_ref, s2_ref = refs[:8]
        outs = refs[8:]
    else:
        x_ref, sc_ref, sh_ref, g_ref, w_ref = refs[:5]
        outs = refs[5:]
    h = _modulated_rmsnorm(x_ref[0], g_ref[...], sc_ref[0], sh_ref[0])
    p = jnp.dot(h.astype(BF16), w_ref[...], preferred_element_type=F32)

    def roped(xs):
        if not rope:
            return xs
        return (xs * cos_ref[...] + pltpu.roll(xs, LANES - 16, 1) * s1_ref[...]
                + pltpu.roll(xs, 16, 1) * s2_ref[...])

    if full:
        q_ref, kv_ref, mix_ref = outs
        q_ref[0] = jnp.concatenate([roped(p[:, 0:128]), roped(p[:, 128:256])], axis=1).astype(BF16)
        kv_ref[0] = jnp.concatenate([roped(p[:, 256:384]), p[:, 384:512]], axis=1).astype(BF16)
        mix_ref[0] = p[:, OFF_CONV:]
    else:
        (kv_ref,) = outs
        kv_ref[0] = p.astype(BF16)


def _inproj(x, sc, sh, g, w, tabs, *, full):
    b, L, d = x.shape
    n = w.shape[1]
    tm = min(512, L)
    rope = tabs is not None
    bm = sc.shape[0]
    mod_map = (lambda i, bb: (bb, 0, 0)) if bm > 1 else (lambda i, bb: (0, 0, 0))
    in_specs = [pl.BlockSpec((1, tm, d), lambda i, bb: (bb, i, 0)),
                pl.BlockSpec((1, 1, d), mod_map),
                pl.BlockSpec((1, 1, d), mod_map),
                pl.BlockSpec((1, d), lambda i, bb: (0, 0)),
                pl.BlockSpec((d, n), lambda i, bb: (0, 0))]
    args = [x, sc, sh, g.reshape(1, d), w]
    if rope:
        in_specs += [pl.BlockSpec((tm, LANES), lambda i, bb: (i, 0))] * 3
        args += list(tabs)
    if full:
        out_shape = (jax.ShapeDtypeStruct((b, L, 256), BF16),
                     jax.ShapeDtypeStruct((b, L, 256), BF16),
                     jax.ShapeDtypeStruct((b, L, MIX_IN_W), F32))
        out_specs = (pl.BlockSpec((1, tm, 256), lambda i, bb: (bb, i, 0)),
                     pl.BlockSpec((1, tm, 256), lambda i, bb: (bb, i, 0)),
                     pl.BlockSpec((1, tm, MIX_IN_W), lambda i, bb: (bb, i, 0)))
    else:
        out_shape = (jax.ShapeDtypeStruct((b, L, 256), BF16),)
        out_specs = (pl.BlockSpec((1, tm, 256), lambda i, bb: (bb, i, 0)),)
    return pl.pallas_call(
        functools.partial(_inproj_kernel, rope=rope, full=full),
        grid=(L // tm, b),
        in_specs=in_specs, out_specs=out_specs, out_shape=out_shape,
        compiler_params=_cparams(("arbitrary", "arbitrary")),
        name="inproj_x" if rope else "inproj_ctx",
    )(*args)


def _attn_kernel(sink_ref, q_ref, kv_ref, kvc_ref, g_ref, o_ref, *, local, seq_len):
    n = pl.program_id(1)
    q = q_ref[0]
    lane = lax.broadcasted_iota(jnp.int32, (1, LANES), 1)
    row = lax.broadcasted_iota(jnp.int32, (2 * ATT_BLOCK, 1), 0)
    kvc = kvc_ref[0]
    kc, vc = kvc[:, :LANES], kvc[:, LANES:]
    nt = (((1,), (1,)), ((), ()))
    if local:
        start = jnp.clip((n - 1) * ATT_BLOCK, 0, seq_len - 3 * ATT_BLOCK)
        start = pl.multiple_of(start, ATT_BLOCK)
        kw = kv_ref[0, pl.ds(start, 3 * ATT_BLOCK), :]
        kl, vl = kw[:, :LANES], kw[:, LANES:]
        qpos = n * ATT_BLOCK + lax.rem(
            lax.broadcasted_iota(jnp.int32, (2 * ATT_BLOCK, 3 * ATT_BLOCK), 0), ATT_BLOCK)
        kpos = start + lax.broadcasted_iota(jnp.int32, (2 * ATT_BLOCK, 3 * ATT_BLOCK), 1)
        band = jnp.abs(kpos - qpos) <= WINDOW
    zero = jnp.zeros((), BF16)
    outs = []
    for kh in range(N_KV_HEADS):
        lm = (lane < HEAD_DIM) if kh == 0 else (lane >= HEAD_DIM)
        lhs = jnp.concatenate([jnp.where(lm, q[:, :LANES], zero), jnp.where(lm, q[:, LANES:], zero)], axis=0)
        sk = jnp.where(row < ATT_BLOCK, sink_ref[2 * kh], sink_ref[2 * kh + 1])
        s_c = lax.dot_general(lhs, kc, nt, preferred_element_type=F32)
        m = jnp.maximum(sk, jnp.max(s_c, axis=-1, keepdims=True))
        if local:
            s_l = lax.dot_general(lhs, kl, nt, preferred_element_type=F32)
            s_l = jnp.where(band, s_l, NEG_INF)
            m = jnp.maximum(m, jnp.max(s_l, axis=-1, keepdims=True))
        e_c = jnp.exp(s_c - m)
        den = jnp.exp(sk - m) + jnp.sum(e_c, axis=-1, keepdims=True)
        o = jnp.dot(e_c.astype(BF16), vc, preferred_element_type=F32)
        if local:
            e_l = jnp.exp(s_l - m)
            den = den + jnp.sum(e_l, axis=-1, keepdims=True)
            o = o + jnp.dot(e_l.astype(BF16), vl, preferred_element_type=F32)
        outs.append(o / den)
    o01, o23 = outs
    lm0 = lane < HEAD_DIM
    att = jnp.concatenate([jnp.where(lm0, o01[:ATT_BLOCK], o23[:ATT_BLOCK]),
                           jnp.where(lm0, o01[ATT_BLOCK:], o23[ATT_BLOCK:])], axis=1)
    ms = jnp.mean(att * att, axis=-1, keepdims=True)
    o_ref[0] = (att * lax.rsqrt(ms + EPS) * g_ref[...]).astype(BF16)


def _attention(q, kv, kvc, sink, g_att, *, local):
    b, L, _ = q.shape
    lc = kvc.shape[1]
    nb = L // ATT_BLOCK
    return pl.pallas_call(
        functools.partial(_attn_kernel, local=local, seq_len=L),
        grid=(b, nb),
        in_specs=[pl.BlockSpec(memory_space=pltpu.SMEM),
                  pl.BlockSpec((1, ATT_BLOCK, 256), lambda bb, i: (bb, i, 0)),
                  pl.BlockSpec((1, kv.shape[1], 256), lambda bb, i: (bb, 0, 0)),
                  pl.BlockSpec((1, lc, 256), lambda bb, i: (bb, 0, 0)),
                  pl.BlockSpec((1, 256), lambda bb, i: (0, 0))],
        out_specs=pl.BlockSpec((1, ATT_BLOCK, 256), lambda bb, i: (bb, i, 0)),
        out_shape=jax.ShapeDtypeStruct((b, L, 256), BF16),
        compiler_params=_cparams(("arbitrary", "arbitrary")),
        name="attn_window" if local else "attn_ctx",
    )(sink, q, kv, kvc, g_att.reshape(1, 256))


def _layernorm(x, g, b):
    mu = jnp.mean(x, axis=-1, keepdims=True)
    xc = x - mu
    return xc * lax.rsqrt(jnp.mean(xc * xc, axis=-1, keepdims=True) + EPS) * g + b


def _group_norm_store(y, g):
    ms = jnp.mean(y * y, axis=-1, keepdims=True)
    return (y * lax.rsqrt(ms + EPS) * g).astype(BF16)


def _mix_kernel(cur_ref, prev_ref, next_ref, dww_ref, dwb_ref, clg_ref, clb_ref, cpw_ref,
                glg_ref, glb_ref, gws_ref, gbs_ref, pw_ref, ps_ref, gn_ref, o_ref,
                hc_ref, hp_ref, *, ts, seq_len):
    i = pl.program_id(1)
    nt = pl.num_programs(1)
    has_prev = (i > 0).astype(F32)
    has_next = (i < nt - 1).astype(F32)

    def glu(blk):
        return blk[:, 0:256] * jax.nn.sigmoid(blk[:, 256:512])

    pv = prev_ref[0]
    nx = next_ref[0]
    hc_ref[0:HALO, :] = glu(pv) * has_prev
    hc_ref[HALO + ts:2 * HALO + ts, :] = glu(nx) * has_next
    hp_ref[0:HALO, :] = pv[:, 1024:1280] * has_prev
    hp_ref[HALO + ts:2 * HALO + ts, :] = nx[:, 1024:1280] * has_next
    hc_ref[HALO:HALO + ts, :] = glu(cur_ref[0, :, 0:512])
    hp_ref[HALO:HALO + ts, :] = cur_ref[0, :, 1024:1280]

    lane = lax.broadcasted_iota(jnp.int32, (1, 256), 1)
    rc = 64
    for c in range(ts // rc):
        r0 = c * rc
        acc = jnp.zeros((rc, 256), F32)
        for j in range(CONV_WIDTH):
            acc = acc + hc_ref[r0 + 1 + j:r0 + 1 + j + rc, :] * dww_ref[j:j + 1, :]
        hcv = _layernorm(acc + dwb_ref[...], clg_ref[...], clb_ref[...])
        hcv = hcv * jax.nn.sigmoid(hcv)
        conv = jnp.dot(hcv.astype(BF16), cpw_ref[...], preferred_element_type=F32)
        o_ref[0, r0:r0 + rc, 0:256] = _group_norm_store(conv, gn_ref[:, 0:256])

        def tap(off):
            return hp_ref[r0 + HALO + off:r0 + HALO + off + rc, :]
        hcur = tap(0)
        s2 = tap(-1) + hcur
        s4 = s2 + tap(-2) + tap(1)
        s8 = s4 + tap(-4) + tap(-3) + tap(2) + tap(3)
        s16 = s8 + tap(-8) + tap(-7) + tap(-6) + tap(-5) + tap(4) + tap(5) + tap(6) + tap(7)
        wsum = jnp.where(lane < 64, s2, jnp.where(lane < 128, s4, jnp.where(lane < 192, s8, s16)))
        half = jnp.where(lane < 64, 1, jnp.where(lane < 128, 2, jnp.where(lane < 192, 4, 8)))
        t = i * ts + r0 + lax.broadcasted_iota(jnp.int32, (rc, 1), 0)
        cnt = jnp.clip(t + half, 0, seq_len) - jnp.clip(t - half, 0, seq_len)
        y = wsum / cnt.astype(F32) - hcur
        pool = jnp.dot(y.astype(BF16), pw_ref[...], preferred_element_type=F32) * ps_ref[...]
        o_ref[0, r0:r0 + rc, 512:768] = _group_norm_store(pool, gn_ref[:, 512:768])

    for c in range(ts // CHUNK):
        r0 = c * CHUNK
        u = cur_ref[0, r0:r0 + CHUNK, 512:768]
        v = _layernorm(cur_ref[0, r0:r0 + CHUNK, 768:1024], glg_ref[...], glb_ref[...])
        r = jnp.dot(gws_ref[...], v.astype(BF16), preferred_element_type=F32)
        mixed = jnp.where(lane < 64, r[0:CHUNK],
                          jnp.where(lane < 128, r[CHUNK:2 * CHUNK],
                                    jnp.where(lane < 192, r[2 * CHUNK:3 * CHUNK], r[3 * CHUNK:])))
        gm = u * (mixed + gbs_ref[...])
        o_ref[0, r0:r0 + CHUNK, 256:512] = _group_norm_store(gm, gn_ref[:, 256:512])


def _local_mixers(mix, lp):
    b, L, w = mix.shape
    ts = min(512, L)
    nt = L // ts
    hb = ts // HALO
    last_h = L // HALO - 1
    full = lambda shape: pl.BlockSpec(shape, lambda bb, i: (0,) * len(shape))
    params = [lp["dw_w"], lp["dw_b"], lp["cln_g"], lp["cln_b"], lp["cpw"], lp["gln_g"], lp["gln_b"],
              lp["gws"], lp["gbs"], lp["pw"], lp["ps"], lp["gn_local"]]
    return pl.pallas_call(
        functools.partial(_mix_kernel, ts=ts, seq_len=L),
        grid=(b, nt),
        in_specs=[pl.BlockSpec((1, ts, w), lambda bb, i: (bb, i, 0)),
                  pl.BlockSpec((1, HALO, w), lambda bb, i: (bb, jnp.maximum(i * hb - 1, 0), 0)),
                  pl.BlockSpec((1, HALO, w), lambda bb, i: (bb, jnp.minimum((i + 1) * hb, last_h), 0))]
                 + [full(p.shape) for p in params],
        out_specs=pl.BlockSpec((1, ts, LOCAL_W), lambda bb, i: (bb, i, 0)),
        out_shape=jax.ShapeDtypeStruct((b, L, LOCAL_W), BF16),
        scratch_shapes=[pltpu.VMEM((ts + 2 * HALO, 256), F32), pltpu.VMEM((ts + 2 * HALO, 256), F32)],
        compiler_params=_cparams(("arbitrary", "arbitrary")),
        name="local_mixers",
    )(mix, mix, mix, *params)


def _merge_kernel(*refs, n_alias):
    (att_ref, loc_ref, x_ref, g1_ref, sc_ref, sh_ref, n2g_ref, woa_ref, wol_ref,
     rw_ref, rb_ref, ltri_ref, cnt_in_ref) = refs[:13]
    xo_ref, h2_ref, gate_ref, idx_ref, rank_ref, cnt_ref, run_ref = refs[13 + n_alias:]

    @pl.when(jnp.logical_and(pl.program_id(0) == 0, pl.program_id(1) == 0))
    def _():
        run_ref[...] = cnt_in_ref[...]

    y = (jnp.dot(att_ref[0], woa_ref[...], preferred_element_type=F32)
         + jnp.dot(loc_ref[0], wol_ref[...], preferred_element_type=F32))
    xn = x_ref[0] + g1_ref[0] * y
    xo_ref[0] = xn
    h2 = _modulated_rmsnorm(xn, n2g_ref[...], sc_ref[0], sh_ref[0]).astype(BF16)
    h2_ref[...] = h2
    logits = jnp.dot(h2, rw_ref[...], preferred_element_type=F32) + rb_ref[...]
    lane = lax.broadcasted_iota(jnp.int32, logits.shape, 1)
    vals = jnp.zeros(logits.shape, F32)
    idxs = jnp.zeros(logits.shape, jnp.int32)
    hits = jnp.zeros(logits.shape, F32)
    sels = []
    top = None
    for k in range(TOP_K):
        m = jnp.max(logits, axis=-1, keepdims=True)
        sel = jnp.min(jnp.where(logits == m, lane, ROUTE_W), axis=-1, keepdims=True)
        if top is None:
            top = m
        chosen = lane == sel
        vals = jnp.where(lane == k, jnp.exp(m - top), vals)
        idxs = jnp.where(lane == k, sel, idxs)
        hits = jnp.where(chosen, 1.0, hits)
        logits = jnp.where(chosen, NEG_INF * 2.0, logits)
        sels.append(chosen)
    gate_ref[...] = vals / jnp.sum(vals, axis=-1, keepdims=True)
    idx_ref[...] = idxs
    base = run_ref[...] + jnp.dot(ltri_ref[...], hits.astype(BF16), preferred_element_type=F32)
    ranks = jnp.zeros(logits.shape, F32)
    for k in range(TOP_K):
        rk = jnp.sum(jnp.where(sels[k], base, 0.0), axis=-1, keepdims=True)
        ranks = jnp.where(lane == k, rk, ranks)
    rank_ref[...] = ranks.astype(jnp.int32)
    run_ref[...] = run_ref[...] + jnp.sum(hits, axis=0, keepdims=True)
    cnt_ref[...] = run_ref[...]


def _merge(att, loc, x, g1, sc2, sh2, n2g, woa, wol, rw, rb, cnt_in, bufs, row_off, t_total):
    b, L, d = x.shape
    tm = min(512, L)
    nt = L // tm
    blk_off = row_off // tm
    bm = g1.shape[0]
    mod_map = (lambda bb, i: (bb, 0, 0)) if bm > 1 else (lambda bb, i: (0, 0, 0))
    tile = lambda w: pl.BlockSpec((1, tm, w), lambda bb, i: (bb, i, 0))
    flat = lambda w: pl.BlockSpec((tm, w), lambda bb, i: (blk_off + bb * nt + i, 0))
    full = lambda shape: pl.BlockSpec(shape, lambda bb, i: (0,) * len(shape))
    ltri = (np.arange(tm)[:, None] > np.arange(tm)[None, :]).astype(np.float32)
    ltri = jnp.asarray(ltri, BF16)
    n_alias = 0 if bufs is None else len(bufs)
    n_in = 13
    return pl.pallas_call(
        functools.partial(_merge_kernel, n_alias=n_alias),
        grid=(b, nt),
        in_specs=[tile(256), tile(LOCAL_W), tile(d),
                  pl.BlockSpec((1, 1, d), mod_map), pl.BlockSpec((1, 1, d), mod_map),
                  pl.BlockSpec((1, 1, d), mod_map),
                  full((1, d)), full(woa.shape), full(wol.shape), full(rw.shape), full(rb.shape),
                  full((tm, tm)), full((1, ROUTE_W))]
                 + [pl.BlockSpec(memory_space=pl.ANY)] * n_alias,
        out_specs=(tile(d), flat(d), flat(ROUTE_W), flat(ROUTE_W), flat(ROUTE_W), full((1, ROUTE_W))),
        out_shape=(jax.ShapeDtypeStruct((b, L, d), F32),
                   jax.ShapeDtypeStruct((t_total, d), BF16),
                   jax.ShapeDtypeStruct((t_total, ROUTE_W), F32),
                   jax.ShapeDtypeStruct((t_total, ROUTE_W), jnp.int32),
                   jax.ShapeDtypeStruct((t_total, ROUTE_W), jnp.int32),
                   jax.ShapeDtypeStruct((1, ROUTE_W), F32)),
        scratch_shapes=[pltpu.VMEM((1, ROUTE_W), F32)],
        input_output_aliases={n_in + j: 1 + j for j in range(n_alias)},
        compiler_params=_cparams(("arbitrary", "arbitrary")),
        name="merge_router",
    )(att, loc, x, g1, sc2, sh2, n2g.reshape(1, d), woa, wol, rw, rb, ltri, cnt_in,
      *(bufs if bufs is not None else ()))


def _moe_kernel(blk_e_ref, n_used_ref, x_ref, w1_ref, b1_ref, w2_ref, b2_ref, o_ref, w1s_ref, w2s_ref):
    i = pl.program_id(0)
    active = i < n_used_ref[0]
    new_expert = jnp.logical_or(i == 0, blk_e_ref[i] != blk_e_ref[jnp.maximum(i - 1, 0)])

    @pl.when(jnp.logical_and(active, new_expert))
    def _():
        rows = 128
        for r in range(D_MODEL // rows):
            w1s_ref[r * rows:(r + 1) * rows, :] = w1_ref[0, r * rows:(r + 1) * rows, :].astype(BF16)
        for r in range(D_FF // rows):
            w2s_ref[r * rows:(r + 1) * rows, :] = w2_ref[0, r * rows:(r + 1) * rows, :].astype(BF16)

    @pl.when(active)
    def _():
        x = x_ref[...]
        acc = jnp.zeros((x.shape[0], D_MODEL), F32)
        for c in range(D_FF // FF_CHUNK):
            lo = c * FF_CHUNK
            glu = jnp.dot(x, w1s_ref[:, lo:lo + FF_CHUNK], preferred_element_type=F32) \
                + b1_ref[0, :, lo:lo + FF_CHUNK]
            lin = jnp.dot(x, w1s_ref[:, D_FF + lo:D_FF + lo + FF_CHUNK], preferred_element_type=F32) \
                + b1_ref[0, :, D_FF + lo:D_FF + lo + FF_CHUNK]
            glu = jnp.minimum(glu, SWIGLU_LIMIT)
            lin = jnp.clip(lin, -SWIGLU_LIMIT, SWIGLU_LIMIT)
            act = glu * jax.nn.sigmoid(SWIGLU_ALPHA * glu) * (lin + 1.0)
            acc = acc + jnp.dot(act.astype(BF16), w2s_ref[lo:lo + FF_CHUNK, :], preferred_element_type=F32)
        o_ref[...] = (acc + b2_ref[0]).astype(o_ref.dtype)

    @pl.when(i >= n_used_ref[0])
    def _():
        o_ref[...] = jnp.zeros(o_ref.shape, o_ref.dtype)


def _moe_ffn(xp, blk_e, n_used, w1, b1, w2, b2):
    n_rows, d = xp.shape
    n_blk = n_rows // MOE_TM
    e = w1.shape[0]

    def row_map(i, be, nu):
        return (jnp.minimum(i, nu[0] - 1), 0)

    def w_map(i, be, nu):
        return (be[i], 0, 0)

    return pl.pallas_call(
        _moe_kernel,
        grid_spec=pltpu.PrefetchScalarGridSpec(
            num_scalar_prefetch=2,
            grid=(n_blk,),
            in_specs=[pl.BlockSpec((MOE_TM, d), row_map),
                      pl.BlockSpec((1, d, 2 * D_FF), w_map),
                      pl.BlockSpec((1, 1, 2 * D_FF), w_map),
                      pl.BlockSpec((1, D_FF, d), w_map),
                      pl.BlockSpec((1, 1, d), w_map)],
            out_specs=pl.BlockSpec((MOE_TM, d), lambda i, be, nu: (i, 0)),
            scratch_shapes=[pltpu.VMEM((d, 2 * D_FF), BF16), pltpu.VMEM((D_FF, d), BF16)]),
        out_shape=jax.ShapeDtypeStruct((n_rows, d), BF16),
        compiler_params=_cparams(("arbitrary",)),
        name="moe_ffn",
    )(blk_e, n_used, xp, w1, b1.reshape(e, 1, 2 * D_FF), w2, b2.reshape(e, 1, d))


def _combine_kernel(yg_ref, gate_ref, x_ref, g2_ref, fg_ref, o_ref, *, final):
    gates = gate_ref[...]
    ff = jnp.zeros(x_ref.shape[1:], F32)
    for k in range(TOP_K):
        ff = ff + gates[:, k:k + 1] * yg_ref[k, 0].astype(F32)
    xo = x_ref[0] + g2_ref[0] * ff
    if final:
        ms = jnp.mean(xo * xo, axis=-1, keepdims=True)
        xo = xo * lax.rsqrt(ms + EPS) * fg_ref[...]
    o_ref[0] = xo


def _combine(yg, gates, x, g2, fg, row_off, *, final):
    b, L, d = x.shape
    tm = min(256, L)
    nt = L // tm
    blk_off = row_off // tm
    bm = g2.shape[0]
    mod_map = (lambda bb, i: (bb, 0, 0)) if bm > 1 else (lambda bb, i: (0, 0, 0))
    return pl.pallas_call(
        functools.partial(_combine_kernel, final=final),
        grid=(b, nt),
        in_specs=[pl.BlockSpec((TOP_K, 1, tm, d), lambda bb, i: (0, bb, i, 0)),
                  pl.BlockSpec((tm, ROUTE_W), lambda bb, i: (blk_off + bb * nt + i, 0)),
                  pl.BlockSpec((1, tm, d), lambda bb, i: (bb, i, 0)),
                  pl.BlockSpec((1, 1, d), mod_map),
                  pl.BlockSpec((1, d), lambda bb, i: (0, 0))],
        out_specs=pl.BlockSpec((1, tm, d), lambda bb, i: (bb, i, 0)),
        out_shape=jax.ShapeDtypeStruct((b, L, d), F32),
        compiler_params=_cparams(("arbitrary", "arbitrary")),
        name="combine",
    )(yg, gates, x, g2, fg.reshape(1, d))


def _dispatch_plan(idx, rank, counts):
    n_tok = idx.shape[0]
    n_asg = n_tok * TOP_K
    padded = (counts + MOE_TM - 1) // MOE_TM * MOE_TM
    pad_end = jnp.cumsum(padded)
    pad_start = pad_end - padded
    onehot = idx[..., None] == jnp.arange(N_EXPERTS, dtype=jnp.int32)
    pos = jnp.sum(jnp.where(onehot, pad_start.astype(jnp.int32), 0), axis=-1) + rank
    n_rows = -(-n_asg // MOE_TM) * MOE_TM + N_EXPERTS * MOE_TM
    n_blk = n_rows // MOE_TM
    tok = jnp.broadcast_to(jnp.arange(n_tok, dtype=jnp.int32)[:, None], (n_tok, TOP_K))
    row_tok = jnp.zeros((n_rows,), jnp.int32).at[pos.reshape(-1)].add(tok.reshape(-1))
    blk_start = jnp.arange(n_blk, dtype=jnp.int32) * MOE_TM
    blk_e = jnp.minimum(jnp.sum((pad_end[None, :] <= blk_start[:, None]).astype(jnp.int32), axis=1),
                        N_EXPERTS - 1)
    n_used = (pad_end[-1] // MOE_TM).astype(jnp.int32).reshape(1)
    return row_tok, pos, blk_e, n_used


_Q_PERM = np.concatenate([np.arange(0, 64), np.arange(128, 192), np.arange(64, 128), np.arange(192, 256)])


def _rope_tables(L):
    t = jnp.arange(L)
    row = (t // GRID_W).astype(F32)
    col = (t % GRID_W).astype(F32)
    half = HEAD_DIM // 2
    inv = ROPE_BASE ** (-jnp.arange(0, half, 2, dtype=F32) / half)
    ar, ac = row[:, None] * inv[None, :], col[:, None] * inv[None, :]
    z = jnp.zeros_like(ar)
    cos = jnp.concatenate([jnp.cos(ar), jnp.cos(ar), jnp.cos(ac), jnp.cos(ac)], axis=1)
    s1 = jnp.concatenate([-jnp.sin(ar), z, -jnp.sin(ac), z], axis=1)
    s2 = jnp.concatenate([z, jnp.sin(ar), z, jnp.sin(ac)], axis=1)
    rep = lambda a: jnp.concatenate([a, a], axis=1)
    return rep(cos), rep(s1), rep(s2)


def _block_diag(pw):
    g, n, _ = pw.shape
    out = jnp.zeros((g * n, g * n), pw.dtype)
    for gi in range(g):
        out = out.at[gi * n:(gi + 1) * n, gi * n:(gi + 1) * n].set(pw[gi])
    return out


def kernel(x, c, ctx, c_ctx, ada_w, ada_b, norm1_g, norm2_g, w_in, attn_sink, conv_dw_w, conv_dw_b, conv_ln_g, conv_ln_b, conv_pw_w, gmlp_ln_g, gmlp_ln_b, gmlp_ws, gmlp_bs, pool_w, pool_scale, group_norm_g, w_out, router_w, router_b, exp_w1, exp_b1, exp_w2, exp_b2, final_norm_g):
    b, L, d = x.shape
    lc = ctx.shape[1]
    depth = ada_w.shape[0]
    tabs = _rope_tables(L)

    r = -(-(b + 1) // 8) * 8
    cond = jnp.zeros((r, d), F32).at[:b].set(c).at[b].set(c_ctx)
    mods = _adaln(cond, ada_w, ada_b).reshape(depth, r, 6, d)

    for l in range(depth):
        last = l == depth - 1
        mx = [mods[l, :b, i].reshape(b, 1, d) for i in range(6)]
        mc = [mods[l, b:b + 1, i].reshape(1, 1, d) for i in range(6)]
        sh1, sc1, g1, sh2, sc2, g2 = mx
        sh1c, sc1c, g1c, sh2c, sc2c, g2c = mc

        wl = w_in[l]
        wq = wl[:, :OFF_K][:, _Q_PERM] * (HEAD_DIM ** -0.5)
        w_full = jnp.concatenate([wq, wl[:, OFF_K:]], axis=1).astype(BF16)
        gn = group_norm_g[l]
        g_att = gn[:256][_Q_PERM]
        wo = w_out[l]
        woa = wo[:256][_Q_PERM].astype(BF16)
        wol = wo[256:].astype(BF16)
        lp = dict(
            dw_w=conv_dw_w[l], dw_b=conv_dw_b[l].reshape(1, -1),
            cln_g=conv_ln_g[l].reshape(1, -1), cln_b=conv_ln_b[l].reshape(1, -1),
            cpw=conv_pw_w[l].astype(BF16),
            gln_g=gmlp_ln_g[l].reshape(1, -1), gln_b=gmlp_ln_b[l].reshape(1, -1),
            gws=gmlp_ws[l].reshape(GMLP_HEADS * CHUNK, CHUNK).astype(BF16),
            gbs=jnp.repeat(gmlp_bs[l].T, GROUP_W // GMLP_HEADS, axis=1),
            pw=_block_diag(pool_w[l]).astype(BF16), ps=pool_scale[l].reshape(1, -1),
            gn_local=gn[256:].reshape(1, -1))
        rw = jnp.zeros((d, ROUTE_W), F32).at[:, :N_EXPERTS].set(router_w[l]).astype(BF16)
        rb = jnp.full((1, ROUTE_W), NEG_INF, F32).at[0, :N_EXPERTS].set(router_b[l])

        q, kv, mix = _inproj(x, sc1, sh1, norm1_g[l], w_full, tabs, full=True)
        if last:
            (kvc,) = _inproj(ctx, sc1c, sh1c, norm1_g[l], wl[:, OFF_K:OFF_CONV].astype(BF16), None, full=False)
        else:
            qc, kvc, mixc = _inproj(ctx, sc1c, sh1c, norm1_g[l], w_full, None, full=True)
        att = _attention(q, kv, kvc, attn_sink[l], g_att, local=True)
        loc = _local_mixers(mix, lp)
        t_total = b * L if last else b * (L + lc)
        cnt0 = jnp.zeros((1, ROUTE_W), F32)
        x, h2, gates, idx, rank, cnt = _merge(att, loc, x, g1, sc2, sh2, norm2_g[l], woa, wol, rw, rb,
                                              cnt0, None, 0, t_total)
        if not last:
            attc = _attention(qc, kvc, kvc, attn_sink[l], g_att, local=False)
            locc = _local_mixers(mixc, lp)
            ctx, h2, gates, idx, rank, cnt = _merge(attc, locc, ctx, g1c, sc2c, sh2c, norm2_g[l], woa, wol,
                                                    rw, rb, cnt, (h2, gates, idx, rank), b * L, t_total)

        counts = cnt[0, :N_EXPERTS].astype(jnp.int32)
        row_tok, pos, blk_e, n_used = _dispatch_plan(idx[:, :TOP_K], rank[:, :TOP_K], counts)
        xp = h2[row_tok]
        yp = _moe_ffn(xp, blk_e, n_used, exp_w1[l], exp_b1[l], exp_w2[l], exp_b2[l])
        ygx = yp[pos[:b * L].T].reshape(TOP_K, b, L, d)
        x = _combine(ygx, gates, x, g2, final_norm_g, 0, final=last)
        if not last:
            ygc = yp[pos[b * L:].T].reshape(TOP_K, b, lc, d)
            ctx = _combine(ygc, gates, ctx, g2c, final_norm_g, b * L, final=False)
    return x
```

```python
import functools

import jax
import jax.numpy as jnp
import numpy as np
from jax import lax
from jax.experimental import pallas as pl
from jax.experimental.pallas import tpu as pltpu

F32 = jnp.float32
BF16 = jnp.bfloat16

D_MODEL = 1024
GRID_W = 64
GROUP_W = 256
HEAD_DIM = 64
N_Q_HEADS = 4
N_KV_HEADS = 2
WINDOW = 128
ATT_BLOCK = 128
ROPE_BASE = 10000.0
CONV_WIDTH = 31
CHUNK = 128
GMLP_HEADS = 4
POOL_WINDOWS = (2, 4, 8, 16)
N_EXPERTS = 32
TOP_K = 4
D_FF = 1024
SWIGLU_ALPHA = 1.702
SWIGLU_LIMIT = 7.0
EPS = 1e-6
NEG_INF = -1e30

OFF_K = 256
OFF_V = 384
OFF_CONV = 512
IN_W = 1792
MIX_IN_W = IN_W - OFF_CONV
LOCAL_W = 3 * GROUP_W

LANES = 128
HALO = 16
ROUTE_W = LANES
MOE_TM = 512
FF_CHUNK = 512
VMEM_LIMIT = 56 * 1024 * 1024


def _cparams(sem):
    return pltpu.CompilerParams(dimension_semantics=sem, vmem_limit_bytes=VMEM_LIMIT)


def _ada_kernel(c_ref, w_ref, b_ref, o_ref):
    c = c_ref[...]
    s = c * jax.nn.sigmoid(c)
    o_ref[0] = jnp.dot(s.astype(BF16), w_ref[0].astype(BF16), preferred_element_type=F32) + b_ref[0]


def _adaln(cond, ada_w, ada_b):
    depth, d, n = ada_w.shape
    r = cond.shape[0]
    tn = 1536
    return pl.pallas_call(
        _ada_kernel,
        grid=(depth, n // tn),
        in_specs=[pl.BlockSpec((r, d), lambda l, j: (0, 0)),
                  pl.BlockSpec((1, d, tn), lambda l, j: (l, 0, j)),
                  pl.BlockSpec((1, 1, tn), lambda l, j: (l, 0, j))],
        out_specs=pl.BlockSpec((1, r, tn), lambda l, j: (l, 0, j)),
        out_shape=jax.ShapeDtypeStruct((depth, r, n), F32),
        compiler_params=_cparams(("arbitrary", "arbitrary")),
        name="adaln",
    )(cond, ada_w, ada_b.reshape(depth, 1, n))


def _modulated_rmsnorm(x, g, sc, sh):
    ms = jnp.mean(x * x, axis=-1, keepdims=True)
    return x * lax.rsqrt(ms + EPS) * (g * (1.0 + sc)) + sh


def _inproj_kernel(*refs, rope, full):
    if rope:
        x_ref, sc_ref, sh_ref, g_ref, w_ref, cos_ref, s1_ref, s2_ref = refs[:8]
        outs = refs[8:]
    else:
        x_ref, sc_ref, sh_ref, g_ref, w_ref = refs[:5]
        outs = refs[5:]
    h = _modulated_rmsnorm(x_ref[0], g_ref[...], sc_ref[0], sh_ref[0])
    p = jnp.dot(h.astype(BF16), w_ref[...], preferred_element_type=F32)

    def roped(xs):
        if not rope:
            return xs
        return (xs * cos_ref[...] + pltpu.roll(xs, LANES - 16, 1) * s1_ref[...]
                + pltpu.roll(xs, 16, 1) * s2_ref[...])

    if full:
        q_ref, kv_ref, mix_ref = outs
        q_ref[0] = jnp.concatenate([roped(p[:, 0:128]), roped(p[:, 128:256])], axis=1).astype(BF16)
        kv_ref[0] = jnp.concatenate([roped(p[:, 256:384]), p[:, 384:512]], axis=1).astype(BF16)
        mix_ref[0] = p[:, OFF_CONV:]
    else:
        (kv_ref,) = outs
        kv_ref[0] = p.astype(BF16)


def _inproj(x, sc, sh, g, w, tabs, *, full):
    b, L, d = x.shape
    n = w.shape[1]
    tm = min(512, L)
    rope = tabs is not None
    bm = sc.shape[0]
    mod_map = (lambda i, bb: (bb, 0, 0)) if bm > 1 else (lambda i, bb: (0, 0, 0))
    in_specs = [pl.BlockSpec((1, tm, d), lambda i, bb: (bb, i, 0)),
                pl.BlockSpec((1, 1, d), mod_map),
                pl.BlockSpec((1, 1, d), mod_map),
                pl.BlockSpec((1, d), lambda i, bb: (0, 0)),
                pl.BlockSpec((d, n), lambda i, bb: (0, 0))]
    args = [x, sc, sh, g.reshape(1, d), w]
    if rope:
        in_specs += [pl.BlockSpec((tm, LANES), lambda i, bb: (i, 0))] * 3
        args += list(tabs)
    if full:
        out_shape = (jax.ShapeDtypeStruct((b, L, 256), BF16),
                     jax.ShapeDtypeStruct((b, L, 256), BF16),
                     jax.ShapeDtypeStruct((b, L, MIX_IN_W), F32))
        out_specs = (pl.BlockSpec((1, tm, 256), lambda i, bb: (bb, i, 0)),
                     pl.BlockSpec((1, tm, 256), lambda i, bb: (bb, i, 0)),
                     pl.BlockSpec((1, tm, MIX_IN_W), lambda i, bb: (bb, i, 0)))
    else:
        out_shape = (jax.ShapeDtypeStruct((b, L, 256), BF16),)
        out_specs = (pl.BlockSpec((1, tm, 256), lambda i, bb: (bb, i, 0)),)
    return pl.pallas_call(
        functools.partial(_inproj_kernel, rope=rope, full=full),
        grid=(L // tm, b),
        in_specs=in_specs, out_specs=out_specs, out_shape=out_shape,
        compiler_params=_cparams(("arbitrary", "arbitrary")),
        name="inproj_x" if rope else "inproj_ctx",
    )(*args)


def _attn_kernel(sink_ref, q_ref, kv_ref, kvc_ref, g_ref, o_ref, *, local, seq_len):
    n = pl.program_id(1)
    q = q_ref[0]
    lane = lax.broadcasted_iota(jnp.int32, (1, LANES), 1)
    row = lax.broadcasted_iota(jnp.int32, (2 * ATT_BLOCK, 1), 0)
    kvc = kvc_ref[0]
    kc, vc = kvc[:, :LANES], kvc[:, LANES:]
    nt = (((1,), (1,)), ((), ()))
    if local:
        start = jnp.clip((n - 1) * ATT_BLOCK, 0, seq_len - 3 * ATT_BLOCK)
        start = pl.multiple_of(start, ATT_BLOCK)
        kw = kv_ref[0, pl.ds(start, 3 * ATT_BLOCK), :]
        kl, vl = kw[:, :LANES], kw[:, LANES:]
        qpos = n * ATT_BLOCK + lax.rem(
            lax.broadcasted_iota(jnp.int32, (2 * ATT_BLOCK, 3 * ATT_BLOCK), 0), ATT_BLOCK)
        kpos = start + lax.broadcasted_iota(jnp.int32, (2 * ATT_BLOCK, 3 * ATT_BLOCK), 1)
        band = jnp.abs(kpos - qpos) <= WINDOW
    zero = jnp.zeros((), BF16)
    outs = []
    for kh in range(N_KV_HEADS):
        lm = (lane < HEAD_DIM) if kh == 0 else (lane >= HEAD_DIM)
        lhs = jnp.concatenate([jnp.where(lm, q[:, :LANES], zero), jnp.where(lm, q[:, LANES:], zero)], axis=0)
        sk = jnp.where(row < ATT_BLOCK, sink_ref[2 * kh], sink_ref[2 * kh + 1])
        s_c = lax.dot_general(lhs, kc, nt, preferred_element_type=F32)
        m = jnp.maximum(sk, jnp.max(s_c, axis=-1, keepdims=True))
        if local:
            s_l = lax.dot_general(lhs, kl, nt, preferred_element_type=F32)
            s_l = jnp.where(band, s_l, NEG_INF)
            m = jnp.maximum(m, jnp.max(s_l, axis=-1, keepdims=True))
        e_c = jnp.exp(s_c - m)
        den = jnp.exp(sk - m) + jnp.sum(e_c, axis=-1, keepdims=True)
        o = jnp.dot(e_c.astype(BF16), vc, preferred_element_type=F32)
        if local:
            e_l = jnp.exp(s_l - m)
            den = den + jnp.sum(e_l, axis=-1, keepdims=True)
            o = o + jnp.dot(e_l.astype(BF16), vl, preferred_element_type=F32)
        outs.append(o / den)
    o01, o23 = outs
    lm0 = lane < HEAD_DIM
    att = jnp.concatenate([jnp.where(lm0, o01[:ATT_BLOCK], o23[:ATT_BLOCK]),
                           jnp.where(lm0, o01[ATT_BLOCK:], o23[ATT_BLOCK:])], axis=1)
    ms = jnp.mean(att * att, axis=-1, keepdims=True)
    o_ref[0] = (att * lax.rsqrt(ms + EPS) * g_ref[...]).astype(BF16)


def _attention(q, kv, kvc, sink, g_att, *, local):
    b, L, _ = q.shape
    lc = kvc.shape[1]
    nb = L // ATT_BLOCK
    return pl.pallas_call(
        functools.partial(_attn_kernel, local=local, seq_len=L),
        grid=(b, nb),
        in_specs=[pl.BlockSpec(memory_space=pltpu.SMEM),
                  pl.BlockSpec((1, ATT_BLOCK, 256), lambda bb, i: (bb, i, 0)),
                  pl.BlockSpec((1, kv.shape[1], 256), lambda bb, i: (bb, 0, 0)),
                  pl.BlockSpec((1, lc, 256), lambda bb, i: (bb, 0, 0)),
                  pl.BlockSpec((1, 256), lambda bb, i: (0, 0))],
        out_specs=pl.BlockSpec((1, ATT_BLOCK, 256), lambda bb, i: (bb, i, 0)),
        out_shape=jax.ShapeDtypeStruct((b, L, 256), BF16),
        compiler_params=_cparams(("arbitrary", "arbitrary")),
        name="attn_window" if local else "attn_ctx",
    )(sink, q, kv, kvc, g_att.reshape(1, 256))


def _layernorm(x, g, b):
    mu = jnp.mean(x, axis=-1, keepdims=True)
    xc = x - mu
    return xc * lax.rsqrt(jnp.mean(xc * xc, axis=-1, keepdims=True) + EPS) * g + b


def _group_norm_store(y, g):
    ms = jnp.mean(y * y, axis=-1, keepdims=True)
    return (y * lax.rsqrt(ms + EPS) * g).astype(BF16)


def _mix_kernel(cur_ref, prev_ref, next_ref, dww_ref, dwb_ref, clg_ref, clb_ref, cpw_ref,
                glg_ref, glb_ref, gws_ref, gbs_ref, pw_ref, ps_ref, gn_ref, o_ref,
                hc_ref, hp_ref, *, ts, seq_len):
    i = pl.program_id(1)
    nt = pl.num_programs(1)
    has_prev = (i > 0).astype(F32)
    has_next = (i < nt - 1).astype(F32)

    def glu(blk):
        return blk[:, 0:256] * jax.nn.sigmoid(blk[:, 256:512])

    pv = prev_ref[0]
    nx = next_ref[0]
    hc_ref[0:HALO, :] = glu(pv) * has_prev
    hc_ref[HALO + ts:2 * HALO + ts, :] = glu(nx) * has_next
    hp_ref[0:HALO, :] = pv[:, 1024:1280] * has_prev
    hp_ref[HALO + ts:2 * HALO + ts, :] = nx[:, 1024:1280] * has_next
    hc_ref[HALO:HALO + ts, :] = glu(cur_ref[0, :, 0:512])
    hp_ref[HALO:HALO + ts, :] = cur_ref[0, :, 1024:1280]

    lane = lax.broadcasted_iota(jnp.int32, (1, 256), 1)
    rc = 64
    for c in range(ts // rc):
        r0 = c * rc
        acc = jnp.zeros((rc, 256), F32)
        for j in range(CONV_WIDTH):
            acc = acc + hc_ref[r0 + 1 + j:r0 + 1 + j + rc, :] * dww_ref[j:j + 1, :]
        hcv = _layernorm(acc + dwb_ref[...], clg_ref[...], clb_ref[...])
        hcv = hcv * jax.nn.sigmoid(hcv)
        conv = jnp.dot(hcv.astype(BF16), cpw_ref[...], preferred_element_type=F32)
        o_ref[0, r0:r0 + rc, 0:256] = _group_norm_store(conv, gn_ref[:, 0:256])

        def tap(off):
            return hp_ref[r0 + HALO + off:r0 + HALO + off + rc, :]
        hcur = tap(0)
        s2 = tap(-1) + hcur
        s4 = s2 + tap(-2) + tap(1)
        s8 = s4 + tap(-4) + tap(-3) + tap(2) + tap(3)
        s16 = s8 + tap(-8) + tap(-7) + tap(-6) + tap(-5) + tap(4) + tap(5) + tap(6) + tap(7)
        wsum = jnp.where(lane < 64, s2, jnp.where(lane < 128, s4, jnp.where(lane < 192, s8, s16)))
        half = jnp.where(lane < 64, 1, jnp.where(lane < 128, 2, jnp.where(lane < 192, 4, 8)))
        t = i * ts + r0 + lax.broadcasted_iota(jnp.int32, (rc, 1), 0)
        cnt = jnp.clip(t + half, 0, seq_len) - jnp.clip(t - half, 0, seq_len)
        y = wsum / cnt.astype(F32) - hcur
        pool = jnp.dot(y.astype(BF16), pw_ref[...], preferred_element_type=F32) * ps_ref[...]
        o_ref[0, r0:r0 + rc, 512:768] = _group_norm_store(pool, gn_ref[:, 512:768])

    for c in range(ts // CHUNK):
        r0 = c * CHUNK
        u = cur_ref[0, r0:r0 + CHUNK, 512:768]
        v = _layernorm(cur_ref[0, r0:r0 + CHUNK, 768:1024], glg_ref[...], glb_ref[...])
        r = jnp.dot(gws_ref[...], v.astype(BF16), preferred_element_type=F32)
        mixed = jnp.where(lane < 64, r[0:CHUNK],
                          jnp.where(lane < 128, r[CHUNK:2 * CHUNK],
                                    jnp.where(lane < 192, r[2 * CHUNK:3 * CHUNK], r[3 * CHUNK:])))
        gm = u * (mixed + gbs_ref[...])
        o_ref[0, r0:r0 + CHUNK, 256:512] = _group_norm_store(gm, gn_ref[:, 256:512])


def _local_mixers(mix, lp):
    b, L, w = mix.shape
    ts = min(512, L)
    nt = L // ts
    hb = ts // HALO
    last_h = L // HALO - 1
    full = lambda shape: pl.BlockSpec(shape, lambda bb, i: (0,) * len(shape))
    params = [lp["dw_w"], lp["dw_b"], lp["cln_g"], lp["cln_b"], lp["cpw"], lp["gln_g"], lp["gln_b"],
              lp["gws"], lp["gbs"], lp["pw"], lp["ps"], lp["gn_local"]]
    return pl.pallas_call(
        functools.partial(_mix_kernel, ts=ts, seq_len=L),
        grid=(b, nt),
        in_specs=[pl.BlockSpec((1, ts, w), lambda bb, i: (bb, i, 0)),
                  pl.BlockSpec((1, HALO, w), lambda bb, i: (bb, jnp.maximum(i * hb - 1, 0), 0)),
                  pl.BlockSpec((1, HALO, w), lambda bb, i: (bb, jnp.minimum((i + 1) * hb, last_h), 0))]
                 + [full(p.shape) for p in params],
        out_specs=pl.BlockSpec((1, ts, LOCAL_W), lambda bb, i: (bb, i, 0)),
        out_shape=jax.ShapeDtypeStruct((b, L, LOCAL_W), BF16),
        scratch_shapes=[pltpu.VMEM((ts + 2 * HALO, 256), F32), pltpu.VMEM((ts + 2 * HALO, 256), F32)],
        compiler_params=_cparams(("arbitrary", "arbitrary")),
        name="local_mixers",
    )(mix, mix, mix, *params)


def _merge_kernel(*refs, n_alias):
    (att_ref, loc_ref, x_ref, g1_ref, sc_ref, sh_ref, n2g_ref, woa_ref, wol_ref,
     rw_ref, rb_ref, ltri_ref, cnt_in_ref) = refs[:13]
    xo_ref, h2_ref, gate_ref, idx_ref, rank_ref, cnt_ref, run_ref = refs[13 + n_alias:]

    @pl.when(jnp.logical_and(pl.program_id(0) == 0, pl.program_id(1) == 0))
    def _():
        run_ref[...] = cnt_in_ref[...]

    y = (jnp.dot(att_ref[0], woa_ref[...], preferred_element_type=F32)
         + jnp.dot(loc_ref[0], wol_ref[...], preferred_element_type=F32))
    xn = x_ref[0] + g1_ref[0] * y
    xo_ref[0] = xn
    h2 = _modulated_rmsnorm(xn, n2g_ref[...], sc_ref[0], sh_ref[0]).astype(BF16)
    h2_ref[...] = h2
    logits = jnp.dot(h2, rw_ref[...], preferred_element_type=F32) + rb_ref[...]
    lane = lax.broadcasted_iota(jnp.int32, logits.shape, 1)
    vals = jnp.zeros(logits.shape, F32)
    idxs = jnp.zeros(logits.shape, jnp.int32)
    hits = jnp.zeros(logits.shape, F32)
    sels = []
    top = None
    for k in range(TOP_K):
        m = jnp.max(logits, axis=-1, keepdims=True)
        sel = jnp.min(jnp.where(logits == m, lane, ROUTE_W), axis=-1, keepdims=True)
        if top is None:
            top = m
        chosen = lane == sel
        vals = jnp.where(lane == k, jnp.exp(m - top), vals)
        idxs = jnp.where(lane == k, sel, idxs)
        hits = jnp.where(chosen, 1.0, hits)
        logits = jnp.where(chosen, NEG_INF * 2.0, logits)
        sels.append(chosen)
    gate_ref[...] = vals / jnp.sum(vals, axis=-1, keepdims=True)
    idx_ref[...] = idxs
    base = run_ref[...] + jnp.dot(ltri_ref[...], hits.astype(BF16), preferred_element_type=F32)
    ranks = jnp.zeros(logits.shape, F32)
    for k in range(TOP_K):
        rk = jnp.sum(jnp.where(sels[k], base, 0.0), axis=-1, keepdims=True)
        ranks = jnp.where(lane == k, rk, ranks)
    rank_ref[...] = ranks.astype(jnp.int32)
    run_ref[...] = run_ref[...] + jnp.sum(hits, axis=0, keepdims=True)
    cnt_ref[...] = run_ref[...]


def _merge(att, loc, x, g1, sc2, sh2, n2g, woa, wol, rw, rb, cnt_in, bufs, row_off, t_total):
    b, L, d = x.shape
    tm = min(512, L)
    nt = L // tm
    blk_off = row_off // tm
    bm = g1.shape[0]
    mod_map = (lambda bb, i: (bb, 0, 0)) if bm > 1 else (lambda bb, i: (0, 0, 0))
    tile = lambda w: pl.BlockSpec((1, tm, w), lambda bb, i: (bb, i, 0))
    flat = lambda w: pl.BlockSpec((tm, w), lambda bb, i: (blk_off + bb * nt + i, 0))
    full = lambda shape: pl.BlockSpec(shape, lambda bb, i: (0,) * len(shape))
    ltri = (np.arange(tm)[:, None] > np.arange(tm)[None, :]).astype(np.float32)
    ltri = jnp.asarray(ltri, BF16)
    n_alias = 0 if bufs is None else len(bufs)
    n_in = 13
    return pl.pallas_call(
        functools.partial(_merge_kernel, n_alias=n_alias),
        grid=(b, nt),
        in_specs=[tile(256), tile(LOCAL_W), tile(d),
                  pl.BlockSpec((1, 1, d), mod_map), pl.BlockSpec((1, 1, d), mod_map),
                  pl.BlockSpec((1, 1, d), mod_map),
                  full((1, d)), full(woa.shape), full(wol.shape), full(rw.shape), full(rb.shape),
                  full((tm, tm)), full((1, ROUTE_W))]
                 + [pl.BlockSpec(memory_space=pl.ANY)] * n_alias,
        out_specs=(tile(d), flat(d), flat(ROUTE_W), flat(ROUTE_W), flat(ROUTE_W), full((1, ROUTE_W))),
        out_shape=(jax.ShapeDtypeStruct((b, L, d), F32),
                   jax.ShapeDtypeStruct((t_total, d), BF16),
                   jax.ShapeDtypeStruct((t_total, ROUTE_W), F32),
                   jax.ShapeDtypeStruct((t_total, ROUTE_W), jnp.int32),
                   jax.ShapeDtypeStruct((t_total, ROUTE_W), jnp.int32),
                   jax.ShapeDtypeStruct((1, ROUTE_W), F32)),
        scratch_shapes=[pltpu.VMEM((1, ROUTE_W), F32)],
        input_output_aliases={n_in + j: 1 + j for j in range(n_alias)},
        compiler_params=_cparams(("arbitrary", "arbitrary")),
        name="merge_router",
    )(att, loc, x, g1, sc2, sh2, n2g.reshape(1, d), woa, wol, rw, rb, ltri, cnt_in,
      *(bufs if bufs is not None else ()))


def _moe_kernel(blk_e_ref, n_used_ref, x_ref, w1_ref, b1_ref, w2_ref, b2_ref, o_ref, w1s_ref, w2s_ref):
    i = pl.program_id(0)
    active = i < n_used_ref[0]
    new_expert = jnp.logical_or(i == 0, blk_e_ref[i] != blk_e_ref[jnp.maximum(i - 1, 0)])

    @pl.when(jnp.logical_and(active, new_expert))
    def _():
        rows = 128
        for r in range(D_MODEL // rows):
            w1s_ref[r * rows:(r + 1) * rows, :] = w1_ref[0, 0, r * rows:(r + 1) * rows, :].astype(BF16)
        for r in range(D_FF // rows):
            w2s_ref[r * rows:(r + 1) * rows, :] = w2_ref[0, 0, r * rows:(r + 1) * rows, :].astype(BF16)

    @pl.when(active)
    def _():
        x = x_ref[...]
        acc = jnp.zeros((x.shape[0], D_MODEL), F32)
        for c in range(D_FF // FF_CHUNK):
            lo = c * FF_CHUNK
            glu = jnp.dot(x, w1s_ref[:, lo:lo + FF_CHUNK], preferred_element_type=F32) \
                + b1_ref[0, 0, :, lo:lo + FF_CHUNK]
            lin = jnp.dot(x, w1s_ref[:, D_FF + lo:D_FF + lo + FF_CHUNK], preferred_element_type=F32) \
                + b1_ref[0, 0, :, D_FF + lo:D_FF + lo + FF_CHUNK]
            glu = jnp.minimum(glu, SWIGLU_LIMIT)
            lin = jnp.clip(lin, -SWIGLU_LIMIT, SWIGLU_LIMIT)
            act = glu * jax.nn.sigmoid(SWIGLU_ALPHA * glu) * (lin + 1.0)
            acc = acc + jnp.dot(act.astype(BF16), w2s_ref[lo:lo + FF_CHUNK, :], preferred_element_type=F32)
        o_ref[...] = (acc + b2_ref[0, 0]).astype(o_ref.dtype)

    @pl.when(i >= n_used_ref[0])
    def _():
        o_ref[...] = jnp.zeros(o_ref.shape, o_ref.dtype)


def _moe_ffn(xp, blk_e, n_used, w1, b1, w2, b2, layer):
    n_rows, d = xp.shape
    n_blk = n_rows // MOE_TM
    depth, e = w1.shape[:2]

    def row_map(i, be, nu):
        return (jnp.maximum(jnp.minimum(i, nu[0] - 1), 0), 0)

    def w_map(i, be, nu):
        return (layer, be[i], 0, 0)

    return pl.pallas_call(
        _moe_kernel,
        grid_spec=pltpu.PrefetchScalarGridSpec(
            num_scalar_prefetch=2,
            grid=(n_blk,),
            in_specs=[pl.BlockSpec((MOE_TM, d), row_map),
                      pl.BlockSpec((1, 1, d, 2 * D_FF), w_map),
                      pl.BlockSpec((1, 1, 1, 2 * D_FF), w_map),
                      pl.BlockSpec((1, 1, D_FF, d), w_map),
                      pl.BlockSpec((1, 1, 1, d), w_map)],
            out_specs=pl.BlockSpec((MOE_TM, d), lambda i, be, nu: (i, 0)),
            scratch_shapes=[pltpu.VMEM((d, 2 * D_FF), BF16), pltpu.VMEM((D_FF, d), BF16)]),
        out_shape=jax.ShapeDtypeStruct((n_rows, d), BF16),
        compiler_params=_cparams(("arbitrary",)),
        name="moe_ffn",
    )(blk_e, n_used, xp, w1, b1.reshape(depth, e, 1, 2 * D_FF), w2, b2.reshape(depth, e, 1, d))


def _combine_kernel(yg_ref, gate_ref, x_ref, g2_ref, fg_ref, o_ref, *, final):
    gates = gate_ref[...]
    ff = jnp.zeros(x_ref.shape[1:], F32)
    for k in range(TOP_K):
        ff = ff + gates[:, k:k + 1] * yg_ref[k, 0].astype(F32)
    xo = x_ref[0] + g2_ref[0] * ff
    if final:
        ms = jnp.mean(xo * xo, axis=-1, keepdims=True)
        xo = xo * lax.rsqrt(ms + EPS) * fg_ref[...]
    o_ref[0] = xo


def _combine(yg, gates, x, g2, fg, row_off, *, final):
    b, L, d = x.shape
    tm = min(256, L)
    nt = L // tm
    blk_off = row_off // tm
    bm = g2.shape[0]
    mod_map = (lambda bb, i: (bb, 0, 0)) if bm > 1 else (lambda bb, i: (0, 0, 0))
    return pl.pallas_call(
        functools.partial(_combine_kernel, final=final),
        grid=(b, nt),
        in_specs=[pl.BlockSpec((TOP_K, 1, tm, d), lambda bb, i: (0, bb, i, 0)),
                  pl.BlockSpec((tm, ROUTE_W), lambda bb, i: (blk_off + bb * nt + i, 0)),
                  pl.BlockSpec((1, tm, d), lambda bb, i: (bb, i, 0)),
                  pl.BlockSpec((1, 1, d), mod_map),
                  pl.BlockSpec((1, d), lambda bb, i: (0, 0))],
        out_specs=pl.BlockSpec((1, tm, d), lambda bb, i: (bb, i, 0)),
        out_shape=jax.ShapeDtypeStruct((b, L, d), F32),
        compiler_params=_cparams(("arbitrary", "arbitrary")),
        name="combine",
    )(yg, gates, x, g2, fg.reshape(1, d))


def _dispatch_plan(idx, rank, counts):
    n_tok = idx.shape[0]
    n_asg = n_tok * TOP_K
    padded = (counts + MOE_TM - 1) // MOE_TM * MOE_TM
    pad_end = jnp.cumsum(padded)
    pad_start = pad_end - padded
    onehot = idx[..., None] == jnp.arange(N_EXPERTS, dtype=jnp.int32)
    pos = jnp.sum(jnp.where(onehot, pad_start.astype(jnp.int32), 0), axis=-1) + rank
    n_rows = -(-n_asg // MOE_TM) * MOE_TM + N_EXPERTS * MOE_TM
    n_blk = n_rows // MOE_TM
    tok = jnp.broadcast_to(jnp.arange(n_tok, dtype=jnp.int32)[:, None], (n_tok, TOP_K))
    fill = jnp.arange(n_rows, dtype=jnp.int32) % n_tok
    row_tok = fill.at[pos.reshape(-1)].add((tok - pos % n_tok).reshape(-1), mode="promise_in_bounds")
    blk_start = jnp.arange(n_blk, dtype=jnp.int32) * MOE_TM
    blk_e = jnp.minimum(jnp.sum((pad_end[None, :] <= blk_start[:, None]).astype(jnp.int32), axis=1),
                        N_EXPERTS - 1)
    n_used = (pad_end[-1] // MOE_TM).astype(jnp.int32).reshape(1)
    return row_tok, pos, blk_e, n_used


_Q_PERM = np.concatenate([np.arange(0, 64), np.arange(128, 192), np.arange(64, 128), np.arange(192, 256)])


def _rope_tables(L):
    t = jnp.arange(L)
    row = (t // GRID_W).astype(F32)
    col = (t % GRID_W).astype(F32)
    half = HEAD_DIM // 2
    inv = ROPE_BASE ** (-jnp.arange(0, half, 2, dtype=F32) / half)
    ar, ac = row[:, None] * inv[None, :], col[:, None] * inv[None, :]
    z = jnp.zeros_like(ar)
    cos = jnp.concatenate([jnp.cos(ar), jnp.cos(ar), jnp.cos(ac), jnp.cos(ac)], axis=1)
    s1 = jnp.concatenate([-jnp.sin(ar), z, -jnp.sin(ac), z], axis=1)
    s2 = jnp.concatenate([z, jnp.sin(ar), z, jnp.sin(ac)], axis=1)
    rep = lambda a: jnp.concatenate([a, a], axis=1)
    return rep(cos), rep(s1), rep(s2)


def _block_diag(pw):
    g, n, _ = pw.shape
    out = jnp.zeros((g * n, g * n), pw.dtype)
    for gi in range(g):
        out = out.at[gi * n:(gi + 1) * n, gi * n:(gi + 1) * n].set(pw[gi])
    return out


def kernel(x, c, ctx, c_ctx, ada_w, ada_b, norm1_g, norm2_g, w_in, attn_sink, conv_dw_w, conv_dw_b, conv_ln_g, conv_ln_b, conv_pw_w, gmlp_ln_g, gmlp_ln_b, gmlp_ws, gmlp_bs, pool_w, pool_scale, group_norm_g, w_out, router_w, router_b, exp_w1, exp_b1, exp_w2, exp_b2, final_norm_g):
    b, L, d = x.shape
    lc = ctx.shape[1]
    depth = ada_w.shape[0]
    tabs = _rope_tables(L)

    r = -(-(b + 1) // 8) * 8
    cond = jnp.zeros((r, d), F32).at[:b].set(c).at[b].set(c_ctx)
    mods = _adaln(cond, ada_w, ada_b).reshape(depth, r, 6, d)

    for l in range(depth):
        last = l == depth - 1
        mx = [mods[l, :b, i].reshape(b, 1, d) for i in range(6)]
        mc = [mods[l, b:b + 1, i].reshape(1, 1, d) for i in range(6)]
        sh1, sc1, g1, sh2, sc2, g2 = mx
        sh1c, sc1c, g1c, sh2c, sc2c, g2c = mc

        wl = w_in[l]
        wq = wl[:, :OFF_K][:, _Q_PERM] * (HEAD_DIM ** -0.5)
        w_full = jnp.concatenate([wq, wl[:, OFF_K:]], axis=1).astype(BF16)
        gn = group_norm_g[l]
        g_att = gn[:256][_Q_PERM]
        wo = w_out[l]
        woa = wo[:256][_Q_PERM].astype(BF16)
        wol = wo[256:].astype(BF16)
        lp = dict(
            dw_w=conv_dw_w[l], dw_b=conv_dw_b[l].reshape(1, -1),
            cln_g=conv_ln_g[l].reshape(1, -1), cln_b=conv_ln_b[l].reshape(1, -1),
            cpw=conv_pw_w[l].astype(BF16),
            gln_g=gmlp_ln_g[l].reshape(1, -1), gln_b=gmlp_ln_b[l].reshape(1, -1),
            gws=gmlp_ws[l].reshape(GMLP_HEADS * CHUNK, CHUNK).astype(BF16),
            gbs=jnp.repeat(gmlp_bs[l].T, GROUP_W // GMLP_HEADS, axis=1),
            pw=_block_diag(pool_w[l]).astype(BF16), ps=pool_scale[l].reshape(1, -1),
            gn_local=gn[256:].reshape(1, -1))
        rw = jnp.zeros((d, ROUTE_W), F32).at[:, :N_EXPERTS].set(router_w[l]).astype(BF16)
        rb = jnp.full((1, ROUTE_W), NEG_INF, F32).at[0, :N_EXPERTS].set(router_b[l])

        q, kv, mix = _inproj(x, sc1, sh1, norm1_g[l], w_full, tabs, full=True)
        if last:
            (kvc,) = _inproj(ctx, sc1c, sh1c, norm1_g[l], wl[:, OFF_K:OFF_CONV].astype(BF16), None, full=False)
        else:
            qc, kvc, mixc = _inproj(ctx, sc1c, sh1c, norm1_g[l], w_full, None, full=True)
        att = _attention(q, kv, kvc, attn_sink[l], g_att, local=True)
        loc = _local_mixers(mix, lp)
        t_total = b * L if last else b * (L + lc)
        cnt0 = jnp.zeros((1, ROUTE_W), F32)
        x, h2, gates, idx, rank, cnt = _merge(att, loc, x, g1, sc2, sh2, norm2_g[l], woa, wol, rw, rb,
                                              cnt0, None, 0, t_total)
        if not last:
            attc = _attention(qc, kvc, kvc, attn_sink[l], g_att, local=False)
            locc = _local_mixers(mixc, lp)
            ctx, h2, gates, idx, rank, cnt = _merge(attc, locc, ctx, g1c, sc2c, sh2c, norm2_g[l], woa, wol,
                                                    rw, rb, cnt, (h2, gates, idx, rank), b * L, t_total)

        counts = cnt[0, :N_EXPERTS].astype(jnp.int32)
        row_tok, pos, blk_e, n_used = _dispatch_plan(idx[:, :TOP_K], rank[:, :TOP_K], counts)
        take = lambda a, i: a.at[i].get(mode="promise_in_bounds")
        xp = take(h2, row_tok)
        yp = _moe_ffn(xp, blk_e, n_used, exp_w1, exp_b1, exp_w2, exp_b2, l)
        ygx = take(yp, pos[:b * L].T).reshape(TOP_K, b, L, d)
        x = _combine(ygx, gates, x, g2, final_norm_g, 0, final=last)
        if not last:
            ygc = take(yp, pos[b * L:].T).reshape(TOP_K, b, lc, d)
            ctx = _combine(ygc, gates, ctx, g2c, final_norm_g, b * L, final=False)
    return x
```

```python
import functools

import jax
import jax.numpy as jnp
import numpy as np
from jax import lax
from jax.experimental import pallas as pl
from jax.experimental.pallas import tpu as pltpu

F32 = jnp.float32
BF16 = jnp.bfloat16

D_MODEL = 1024
GRID_W = 64
GROUP_W = 256
HEAD_DIM = 64
N_Q_HEADS = 4
N_KV_HEADS = 2
WINDOW = 128
ATT_BLOCK = 128
ROPE_BASE = 10000.0
CONV_WIDTH = 31
CHUNK = 128
GMLP_HEADS = 4
POOL_WINDOWS = (2, 4, 8, 16)
N_EXPERTS = 32
TOP_K = 4
D_FF = 1024
SWIGLU_ALPHA = 1.702
SWIGLU_LIMIT = 7.0
EPS = 1e-6
NEG_INF = -1e30

OFF_K = 256
OFF_V = 384
OFF_CONV = 512
IN_W = 1792
MIX_IN_W = IN_W - OFF_CONV
LOCAL_W = 3 * GROUP_W

LANES = 128
HALO = 16
ROUTE_W = LANES
MOE_TM = 512
FF_CHUNK = 512
BATCH_GROUPS = 2
VMEM_LIMIT = 56 * 1024 * 1024


def _cparams(sem):
    return pltpu.CompilerParams(dimension_semantics=sem, vmem_limit_bytes=VMEM_LIMIT)


def _ada_kernel(c_ref, w_ref, b_ref, o_ref):
    c = c_ref[...]
    s = c * jax.nn.sigmoid(c)
    o_ref[0] = jnp.dot(s.astype(BF16), w_ref[0].astype(BF16), preferred_element_type=F32) + b_ref[0]


def _adaln(cond, ada_w, ada_b):
    depth, d, n = ada_w.shape
    r = cond.shape[0]
    tn = 1536
    return pl.pallas_call(
        _ada_kernel,
        grid=(depth, n // tn),
        in_specs=[pl.BlockSpec((r, d), lambda l, j: (0, 0)),
                  pl.BlockSpec((1, d, tn), lambda l, j: (l, 0, j)),
                  pl.BlockSpec((1, 1, tn), lambda l, j: (l, 0, j))],
        out_specs=pl.BlockSpec((1, r, tn), lambda l, j: (l, 0, j)),
        out_shape=jax.ShapeDtypeStruct((depth, r, n), F32),
        compiler_params=_cparams(("arbitrary", "arbitrary")),
        name="adaln",
    )(cond, ada_w, ada_b.reshape(depth, 1, n))


def _modulated_rmsnorm(x, g, sc, sh):
    ms = jnp.mean(x * x, axis=-1, keepdims=True)
    return x * lax.rsqrt(ms + EPS) * (g * (1.0 + sc)) + sh


def _inproj_kernel(*refs, rope, full):
    if rope:
        x_ref, sc_ref, sh_ref, g_ref, w_ref, cos_ref, s1_ref, s2_ref = refs[:8]
        outs = refs[8:]
    else:
        x_ref, sc_ref, sh_ref, g_ref, w_ref = refs[:5]
        outs = refs[5:]
    h = _modulated_rmsnorm(x_ref[0], g_ref[...], sc_ref[0], sh_ref[0])
    p = jnp.dot(h.astype(BF16), w_ref[...], preferred_element_type=F32)

    def roped(xs):
        if not rope:
            return xs
        return (xs * cos_ref[...] + pltpu.roll(xs, LANES - 16, 1) * s1_ref[...]
                + pltpu.roll(xs, 16, 1) * s2_ref[...])

    if full:
        q_ref, kv_ref, mix_ref = outs
        q_ref[0] = jnp.concatenate([roped(p[:, 0:128]), roped(p[:, 128:256])], axis=1).astype(BF16)
        kv_ref[0] = jnp.concatenate([roped(p[:, 256:384]), p[:, 384:512]], axis=1).astype(BF16)
        mix_ref[0] = p[:, OFF_CONV:]
    else:
        (kv_ref,) = outs
        kv_ref[0] = p.astype(BF16)


def _inproj(x, xoff, b, sc, sh, moff, g, w, tabs, *, full):
    _, L, d = x.shape
    n = w.shape[1]
    tm = min(512, L)
    rope = tabs is not None
    bm = sc.shape[0]
    mod_map = (lambda i, bb: (moff + bb, 0, 0)) if bm > 1 else (lambda i, bb: (0, 0, 0))
    in_specs = [pl.BlockSpec((1, tm, d), lambda i, bb: (xoff + bb, i, 0)),
                pl.BlockSpec((1, 1, d), mod_map),
                pl.BlockSpec((1, 1, d), mod_map),
                pl.BlockSpec((1, d), lambda i, bb: (0, 0)),
                pl.BlockSpec((d, n), lambda i, bb: (0, 0))]
    args = [x, sc, sh, g.reshape(1, d), w]
    if rope:
        in_specs += [pl.BlockSpec((tm, LANES), lambda i, bb: (i, 0))] * 3
        args += list(tabs)
    if full:
        out_shape = (jax.ShapeDtypeStruct((b, L, 256), BF16),
                     jax.ShapeDtypeStruct((b, L, 256), BF16),
                     jax.ShapeDtypeStruct((b, L, MIX_IN_W), F32))
        out_specs = (pl.BlockSpec((1, tm, 256), lambda i, bb: (bb, i, 0)),
                     pl.BlockSpec((1, tm, 256), lambda i, bb: (bb, i, 0)),
                     pl.BlockSpec((1, tm, MIX_IN_W), lambda i, bb: (bb, i, 0)))
    else:
        out_shape = (jax.ShapeDtypeStruct((b, L, 256), BF16),)
        out_specs = (pl.BlockSpec((1, tm, 256), lambda i, bb: (bb, i, 0)),)
    return pl.pallas_call(
        functools.partial(_inproj_kernel, rope=rope, full=full),
        grid=(L // tm, b),
        in_specs=in_specs, out_specs=out_specs, out_shape=out_shape,
        compiler_params=_cparams(("arbitrary", "arbitrary")),
        name="inproj_x" if rope else "inproj_ctx",
    )(*args)


def _attn_kernel(sink_ref, q_ref, kv_ref, kvc_ref, g_ref, o_ref, *, local, seq_len):
    n = pl.program_id(1)
    q = q_ref[0]
    lane = lax.broadcasted_iota(jnp.int32, (1, LANES), 1)
    row = lax.broadcasted_iota(jnp.int32, (2 * ATT_BLOCK, 1), 0)
    kvc = kvc_ref[0]
    kc, vc = kvc[:, :LANES], kvc[:, LANES:]
    nt = (((1,), (1,)), ((), ()))
    if local:
        start = jnp.clip((n - 1) * ATT_BLOCK, 0, seq_len - 3 * ATT_BLOCK)
        start = pl.multiple_of(start, ATT_BLOCK)
        kw = kv_ref[0, pl.ds(start, 3 * ATT_BLOCK), :]
        kl, vl = kw[:, :LANES], kw[:, LANES:]
        qpos = n * ATT_BLOCK + lax.rem(
            lax.broadcasted_iota(jnp.int32, (2 * ATT_BLOCK, 3 * ATT_BLOCK), 0), ATT_BLOCK)
        kpos = start + lax.broadcasted_iota(jnp.int32, (2 * ATT_BLOCK, 3 * ATT_BLOCK), 1)
        band = jnp.abs(kpos - qpos) <= WINDOW
    zero = jnp.zeros((), BF16)
    outs = []
    for kh in range(N_KV_HEADS):
        lm = (lane < HEAD_DIM) if kh == 0 else (lane >= HEAD_DIM)
        lhs = jnp.concatenate([jnp.where(lm, q[:, :LANES], zero), jnp.where(lm, q[:, LANES:], zero)], axis=0)
        sk = jnp.where(row < ATT_BLOCK, sink_ref[2 * kh], sink_ref[2 * kh + 1])
        s_c = lax.dot_general(lhs, kc, nt, preferred_element_type=F32)
        m = jnp.maximum(sk, jnp.max(s_c, axis=-1, keepdims=True))
        if local:
            s_l = lax.dot_general(lhs, kl, nt, preferred_element_type=F32)
            s_l = jnp.where(band, s_l, NEG_INF)
            m = jnp.maximum(m, jnp.max(s_l, axis=-1, keepdims=True))
        e_c = jnp.exp(s_c - m)
        den = jnp.exp(sk - m) + jnp.sum(e_c, axis=-1, keepdims=True)
        o = jnp.dot(e_c.astype(BF16), vc, preferred_element_type=F32)
        if local:
            e_l = jnp.exp(s_l - m)
            den = den + jnp.sum(e_l, axis=-1, keepdims=True)
            o = o + jnp.dot(e_l.astype(BF16), vl, preferred_element_type=F32)
        outs.append(o / den)
    o01, o23 = outs
    lm0 = lane < HEAD_DIM
    att = jnp.concatenate([jnp.where(lm0, o01[:ATT_BLOCK], o23[:ATT_BLOCK]),
                           jnp.where(lm0, o01[ATT_BLOCK:], o23[ATT_BLOCK:])], axis=1)
    ms = jnp.mean(att * att, axis=-1, keepdims=True)
    o_ref[0] = (att * lax.rsqrt(ms + EPS) * g_ref[...]).astype(BF16)


def _attention(q, kv, kvc, sink, g_att, *, local):
    b, L, _ = q.shape
    lc = kvc.shape[1]
    nb = L // ATT_BLOCK
    return pl.pallas_call(
        functools.partial(_attn_kernel, local=local, seq_len=L),
        grid=(b, nb),
        in_specs=[pl.BlockSpec(memory_space=pltpu.SMEM),
                  pl.BlockSpec((1, ATT_BLOCK, 256), lambda bb, i: (bb, i, 0)),
                  pl.BlockSpec((1, kv.shape[1], 256), lambda bb, i: (bb, 0, 0)),
                  pl.BlockSpec((1, lc, 256), lambda bb, i: (bb, 0, 0)),
                  pl.BlockSpec((1, 256), lambda bb, i: (0, 0))],
        out_specs=pl.BlockSpec((1, ATT_BLOCK, 256), lambda bb, i: (bb, i, 0)),
        out_shape=jax.ShapeDtypeStruct((b, L, 256), BF16),
        compiler_params=_cparams(("arbitrary", "arbitrary")),
        name="attn_window" if local else "attn_ctx",
    )(sink, q, kv, kvc, g_att.reshape(1, 256))


def _layernorm(x, g, b):
    mu = jnp.mean(x, axis=-1, keepdims=True)
    xc = x - mu
    return xc * lax.rsqrt(jnp.mean(xc * xc, axis=-1, keepdims=True) + EPS) * g + b


def _group_norm_store(y, g):
    ms = jnp.mean(y * y, axis=-1, keepdims=True)
    return (y * lax.rsqrt(ms + EPS) * g).astype(BF16)


def _mix_kernel(cur_ref, prev_ref, next_ref, dww_ref, dwb_ref, clg_ref, clb_ref, cpw_ref,
                glg_ref, glb_ref, gws_ref, gbs_ref, pw_ref, ps_ref, gn_ref, o_ref,
                hc_ref, hp_ref, *, ts, seq_len):
    i = pl.program_id(1)
    nt = pl.num_programs(1)
    has_prev = (i > 0).astype(F32)
    has_next = (i < nt - 1).astype(F32)

    def glu(blk):
        return blk[:, 0:256] * jax.nn.sigmoid(blk[:, 256:512])

    pv = prev_ref[0]
    nx = next_ref[0]
    hc_ref[0:HALO, :] = glu(pv) * has_prev
    hc_ref[HALO + ts:2 * HALO + ts, :] = glu(nx) * has_next
    hp_ref[0:HALO, :] = pv[:, 1024:1280] * has_prev
    hp_ref[HALO + ts:2 * HALO + ts, :] = nx[:, 1024:1280] * has_next
    hc_ref[HALO:HALO + ts, :] = glu(cur_ref[0, :, 0:512])
    hp_ref[HALO:HALO + ts, :] = cur_ref[0, :, 1024:1280]

    lane = lax.broadcasted_iota(jnp.int32, (1, 256), 1)
    rc = 64
    for c in range(ts // rc):
        r0 = c * rc
        acc = jnp.zeros((rc, 256), F32)
        for j in range(CONV_WIDTH):
            acc = acc + hc_ref[r0 + 1 + j:r0 + 1 + j + rc, :] * dww_ref[j:j + 1, :]
        hcv = _layernorm(acc + dwb_ref[...], clg_ref[...], clb_ref[...])
        hcv = hcv * jax.nn.sigmoid(hcv)
        conv = jnp.dot(hcv.astype(BF16), cpw_ref[...], preferred_element_type=F32)
        o_ref[0, r0:r0 + rc, 0:256] = _group_norm_store(conv, gn_ref[:, 0:256])

        def tap(off):
            return hp_ref[r0 + HALO + off:r0 + HALO + off + rc, :]
        hcur = tap(0)
        s2 = tap(-1) + hcur
        s4 = s2 + tap(-2) + tap(1)
        s8 = s4 + tap(-4) + tap(-3) + tap(2) + tap(3)
        s16 = s8 + tap(-8) + tap(-7) + tap(-6) + tap(-5) + tap(4) + tap(5) + tap(6) + tap(7)
        wsum = jnp.where(lane < 64, s2, jnp.where(lane < 128, s4, jnp.where(lane < 192, s8, s16)))
        half = jnp.where(lane < 64, 1, jnp.where(lane < 128, 2, jnp.where(lane < 192, 4, 8)))
        t = i * ts + r0 + lax.broadcasted_iota(jnp.int32, (rc, 1), 0)
        cnt = jnp.clip(t + half, 0, seq_len) - jnp.clip(t - half, 0, seq_len)
        y = wsum / cnt.astype(F32) - hcur
        pool = jnp.dot(y.astype(BF16), pw_ref[...], preferred_element_type=F32) * ps_ref[...]
        o_ref[0, r0:r0 + rc, 512:768] = _group_norm_store(pool, gn_ref[:, 512:768])

    for c in range(ts // CHUNK):
        r0 = c * CHUNK
        u = cur_ref[0, r0:r0 + CHUNK, 512:768]
        v = _layernorm(cur_ref[0, r0:r0 + CHUNK, 768:1024], glg_ref[...], glb_ref[...])
        r = jnp.dot(gws_ref[...], v.astype(BF16), preferred_element_type=F32)
        mixed = jnp.where(lane < 64, r[0:CHUNK],
                          jnp.where(lane < 128, r[CHUNK:2 * CHUNK],
                                    jnp.where(lane < 192, r[2 * CHUNK:3 * CHUNK], r[3 * CHUNK:])))
        gm = u * (mixed + gbs_ref[...])
        o_ref[0, r0:r0 + CHUNK, 256:512] = _group_norm_store(gm, gn_ref[:, 256:512])


def _local_mixers(mix, lp):
    b, L, w = mix.shape
    ts = min(512, L)
    nt = L // ts
    hb = ts // HALO
    last_h = L // HALO - 1
    full = lambda shape: pl.BlockSpec(shape, lambda bb, i: (0,) * len(shape))
    params = [lp["dw_w"], lp["dw_b"], lp["cln_g"], lp["cln_b"], lp["cpw"], lp["gln_g"], lp["gln_b"],
              lp["gws"], lp["gbs"], lp["pw"], lp["ps"], lp["gn_local"]]
    return pl.pallas_call(
        functools.partial(_mix_kernel, ts=ts, seq_len=L),
        grid=(b, nt),
        in_specs=[pl.BlockSpec((1, ts, w), lambda bb, i: (bb, i, 0)),
                  pl.BlockSpec((1, HALO, w), lambda bb, i: (bb, jnp.maximum(i * hb - 1, 0), 0)),
                  pl.BlockSpec((1, HALO, w), lambda bb, i: (bb, jnp.minimum((i + 1) * hb, last_h), 0))]
                 + [full(p.shape) for p in params],
        out_specs=pl.BlockSpec((1, ts, LOCAL_W), lambda bb, i: (bb, i, 0)),
        out_shape=jax.ShapeDtypeStruct((b, L, LOCAL_W), BF16),
        scratch_shapes=[pltpu.VMEM((ts + 2 * HALO, 256), F32), pltpu.VMEM((ts + 2 * HALO, 256), F32)],
        compiler_params=_cparams(("arbitrary", "arbitrary")),
        name="local_mixers",
    )(mix, mix, mix, *params)


def _merge_kernel(*refs, n_alias):
    (att_ref, loc_ref, x_ref, g1_ref, sc_ref, sh_ref, n2g_ref, woa_ref, wol_ref,
     rw_ref, rb_ref, ltri_ref, cnt_in_ref) = refs[:13]
    xo_ref, h2_ref, gate_ref, idx_ref, rank_ref, cnt_ref, run_ref = refs[13 + n_alias:]

    @pl.when(jnp.logical_and(pl.program_id(0) == 0, pl.program_id(1) == 0))
    def _():
        run_ref[...] = cnt_in_ref[...]

    y = (jnp.dot(att_ref[0], woa_ref[...], preferred_element_type=F32)
         + jnp.dot(loc_ref[0], wol_ref[...], preferred_element_type=F32))
    xn = x_ref[0] + g1_ref[0] * y
    xo_ref[0] = xn
    h2 = _modulated_rmsnorm(xn, n2g_ref[...], sc_ref[0], sh_ref[0]).astype(BF16)
    h2_ref[...] = h2
    logits = jnp.dot(h2, rw_ref[...], preferred_element_type=F32) + rb_ref[...]
    lane = lax.broadcasted_iota(jnp.int32, logits.shape, 1)
    vals = jnp.zeros(logits.shape, F32)
    idxs = jnp.zeros(logits.shape, jnp.int32)
    hits = jnp.zeros(logits.shape, F32)
    sels = []
    top = None
    for k in range(TOP_K):
        m = jnp.max(logits, axis=-1, keepdims=True)
        sel = jnp.min(jnp.where(logits == m, lane, ROUTE_W), axis=-1, keepdims=True)
        if top is None:
            top = m
        chosen = lane == sel
        vals = jnp.where(lane == k, jnp.exp(m - top), vals)
        idxs = jnp.where(lane == k, sel, idxs)
        hits = jnp.where(chosen, 1.0, hits)
        logits = jnp.where(chosen, NEG_INF * 2.0, logits)
        sels.append(chosen)
    gate_ref[...] = vals / jnp.sum(vals, axis=-1, keepdims=True)
    idx_ref[...] = idxs
    base = run_ref[...] + jnp.dot(ltri_ref[...], hits.astype(BF16), preferred_element_type=F32)
    ranks = jnp.zeros(logits.shape, F32)
    for k in range(TOP_K):
        rk = jnp.sum(jnp.where(sels[k], base, 0.0), axis=-1, keepdims=True)
        ranks = jnp.where(lane == k, rk, ranks)
    rank_ref[...] = ranks.astype(jnp.int32)
    run_ref[...] = run_ref[...] + jnp.sum(hits, axis=0, keepdims=True)
    cnt_ref[...] = run_ref[...]


def _merge(att, loc, x, xoff, g1, sc2, sh2, moff, n2g, woa, wol, rw, rb, cnt_in, bufs, row_off, t_total):
    b = att.shape[0]
    _, L, d = x.shape
    tm = min(512, L)
    nt = L // tm
    blk_off = row_off // tm
    bm = g1.shape[0]
    mod_map = (lambda bb, i: (moff + bb, 0, 0)) if bm > 1 else (lambda bb, i: (0, 0, 0))
    tile = lambda w: pl.BlockSpec((1, tm, w), lambda bb, i: (bb, i, 0))
    xtile = pl.BlockSpec((1, tm, d), lambda bb, i: (xoff + bb, i, 0))
    flat = lambda w: pl.BlockSpec((tm, w), lambda bb, i: (blk_off + bb * nt + i, 0))
    full = lambda shape: pl.BlockSpec(shape, lambda bb, i: (0,) * len(shape))
    ltri = (np.arange(tm)[:, None] > np.arange(tm)[None, :]).astype(np.float32)
    ltri = jnp.asarray(ltri, BF16)
    n_alias = 0 if bufs is None else len(bufs)
    n_in = 13
    return pl.pallas_call(
        functools.partial(_merge_kernel, n_alias=n_alias),
        grid=(b, nt),
        in_specs=[tile(256), tile(LOCAL_W), xtile,
                  pl.BlockSpec((1, 1, d), mod_map), pl.BlockSpec((1, 1, d), mod_map),
                  pl.BlockSpec((1, 1, d), mod_map),
                  full((1, d)), full(woa.shape), full(wol.shape), full(rw.shape), full(rb.shape),
                  full((tm, tm)), full((1, ROUTE_W))]
                 + [pl.BlockSpec(memory_space=pl.ANY)] * n_alias,
        out_specs=(tile(d), flat(d), flat(ROUTE_W), flat(ROUTE_W), flat(ROUTE_W), full((1, ROUTE_W))),
        out_shape=(jax.ShapeDtypeStruct((b, L, d), F32),
                   jax.ShapeDtypeStruct((t_total, d), BF16),
                   jax.ShapeDtypeStruct((t_total, ROUTE_W), F32),
                   jax.ShapeDtypeStruct((t_total, ROUTE_W), jnp.int32),
                   jax.ShapeDtypeStruct((t_total, ROUTE_W), jnp.int32),
                   jax.ShapeDtypeStruct((1, ROUTE_W), F32)),
        scratch_shapes=[pltpu.VMEM((1, ROUTE_W), F32)],
        input_output_aliases={n_in + j: 1 + j for j in range(n_alias)},
        compiler_params=_cparams(("arbitrary", "arbitrary")),
        name="merge_router",
    )(att, loc, x, g1, sc2, sh2, n2g.reshape(1, d), woa, wol, rw, rb, ltri, cnt_in,
      *(bufs if bufs is not None else ()))


def _moe_kernel(blk_e_ref, n_used_ref, x_ref, w1_ref, b1_ref, w2_ref, b2_ref, o_ref, w1s_ref, w2s_ref):
    i = pl.program_id(0)
    active = i < n_used_ref[0]
    new_expert = jnp.logical_or(i == 0, blk_e_ref[i] != blk_e_ref[jnp.maximum(i - 1, 0)])

    @pl.when(jnp.logical_and(active, new_expert))
    def _():
        rows = 128
        for r in range(D_MODEL // rows):
            w1s_ref[r * rows:(r + 1) * rows, :] = w1_ref[0, 0, r * rows:(r + 1) * rows, :].astype(BF16)
        for r in range(D_FF // rows):
            w2s_ref[r * rows:(r + 1) * rows, :] = w2_ref[0, 0, r * rows:(r + 1) * rows, :].astype(BF16)

    @pl.when(active)
    def _():
        x = x_ref[...]
        acc = jnp.zeros((x.shape[0], D_MODEL), F32)
        for c in range(D_FF // FF_CHUNK):
            lo = c * FF_CHUNK
            glu = jnp.dot(x, w1s_ref[:, lo:lo + FF_CHUNK], preferred_element_type=F32) \
                + b1_ref[0, 0, :, lo:lo + FF_CHUNK]
            lin = jnp.dot(x, w1s_ref[:, D_FF + lo:D_FF + lo + FF_CHUNK], preferred_element_type=F32) \
                + b1_ref[0, 0, :, D_FF + lo:D_FF + lo + FF_CHUNK]
            glu = jnp.minimum(glu, SWIGLU_LIMIT)
            lin = jnp.clip(lin, -SWIGLU_LIMIT, SWIGLU_LIMIT)
            act = glu * jax.nn.sigmoid(SWIGLU_ALPHA * glu) * (lin + 1.0)
            acc = acc + jnp.dot(act.astype(BF16), w2s_ref[lo:lo + FF_CHUNK, :], preferred_element_type=F32)
        o_ref[...] = (acc + b2_ref[0, 0]).astype(o_ref.dtype)

    @pl.when(i >= n_used_ref[0])
    def _():
        o_ref[...] = jnp.zeros(o_ref.shape, o_ref.dtype)


def _moe_ffn(xp, blk_e, n_used, w1, b1, w2, b2, layer):
    n_rows, d = xp.shape
    n_blk = n_rows // MOE_TM
    depth, e = w1.shape[:2]

    def row_map(i, be, nu):
        return (jnp.maximum(jnp.minimum(i, nu[0] - 1), 0), 0)

    def w_map(i, be, nu):
        return (layer, be[i], 0, 0)

    return pl.pallas_call(
        _moe_kernel,
        grid_spec=pltpu.PrefetchScalarGridSpec(
            num_scalar_prefetch=2,
            grid=(n_blk,),
            in_specs=[pl.BlockSpec((MOE_TM, d), row_map),
                      pl.BlockSpec((1, 1, d, 2 * D_FF), w_map),
                      pl.BlockSpec((1, 1, 1, 2 * D_FF), w_map),
                      pl.BlockSpec((1, 1, D_FF, d), w_map),
                      pl.BlockSpec((1, 1, 1, d), w_map)],
            out_specs=pl.BlockSpec((MOE_TM, d), lambda i, be, nu: (i, 0)),
            scratch_shapes=[pltpu.VMEM((d, 2 * D_FF), BF16), pltpu.VMEM((D_FF, d), BF16)]),
        out_shape=jax.ShapeDtypeStruct((n_rows, d), BF16),
        compiler_params=_cparams(("arbitrary",)),
        name="moe_ffn",
    )(blk_e, n_used, xp, w1, b1.reshape(depth, e, 1, 2 * D_FF), w2, b2.reshape(depth, e, 1, d))


def _combine_kernel(*refs, final):
    yg_ref, gate_ref, x_ref, g2_ref, fg_ref = refs[:5]
    o_ref = refs[-1]
    gates = gate_ref[...]
    ff = jnp.zeros(x_ref.shape[1:], F32)
    for k in range(TOP_K):
        ff = ff + gates[:, k:k + 1] * yg_ref[k, 0].astype(F32)
    xo = x_ref[0] + g2_ref[0] * ff
    if final:
        ms = jnp.mean(xo * xo, axis=-1, keepdims=True)
        xo = xo * lax.rsqrt(ms + EPS) * fg_ref[...]
    o_ref[0] = xo


def _combine(yg, gates, x, g2, moff, fg, row_off, *, final, out_buf=None, ooff=0, out_b=None):
    b, L, d = x.shape
    tm = min(256, L)
    nt = L // tm
    blk_off = row_off // tm
    bm = g2.shape[0]
    out_b = b if out_b is None else out_b
    mod_map = (lambda bb, i: (moff + bb, 0, 0)) if bm > 1 else (lambda bb, i: (0, 0, 0))
    extra = () if out_buf is None else (out_buf,)
    return pl.pallas_call(
        functools.partial(_combine_kernel, final=final),
        grid=(b, nt),
        in_specs=[pl.BlockSpec((TOP_K, 1, tm, d), lambda bb, i: (0, bb, i, 0)),
                  pl.BlockSpec((tm, ROUTE_W), lambda bb, i: (blk_off + bb * nt + i, 0)),
                  pl.BlockSpec((1, tm, d), lambda bb, i: (bb, i, 0)),
                  pl.BlockSpec((1, 1, d), mod_map),
                  pl.BlockSpec((1, d), lambda bb, i: (0, 0))]
                 + [pl.BlockSpec(memory_space=pl.ANY)] * len(extra),
        out_specs=pl.BlockSpec((1, tm, d), lambda bb, i: (ooff + bb, i, 0)),
        out_shape=jax.ShapeDtypeStruct((out_b, L, d), F32),
        input_output_aliases={5: 0} if extra else {},
        compiler_params=_cparams(("arbitrary", "arbitrary")),
        name="combine",
    )(yg, gates, x, g2, fg.reshape(1, d), *extra)


def _dispatch_plan(idx, rank, counts):
    n_tok = idx.shape[0]
    n_asg = n_tok * TOP_K
    padded = (counts + MOE_TM - 1) // MOE_TM * MOE_TM
    pad_end = jnp.cumsum(padded)
    pad_start = pad_end - padded
    onehot = idx[..., None] == jnp.arange(N_EXPERTS, dtype=jnp.int32)
    pos = jnp.sum(jnp.where(onehot, pad_start.astype(jnp.int32), 0), axis=-1) + rank
    n_rows = -(-n_asg // MOE_TM) * MOE_TM + N_EXPERTS * MOE_TM
    n_blk = n_rows // MOE_TM
    tok = jnp.broadcast_to(jnp.arange(n_tok, dtype=jnp.int32)[:, None], (n_tok, TOP_K))
    fill = jnp.arange(n_rows, dtype=jnp.int32) % n_tok
    row_tok = fill.at[pos.reshape(-1)].add((tok - pos % n_tok).reshape(-1), mode="promise_in_bounds")
    blk_start = jnp.arange(n_blk, dtype=jnp.int32) * MOE_TM
    blk_e = jnp.minimum(jnp.sum((pad_end[None, :] <= blk_start[:, None]).astype(jnp.int32), axis=1),
                        N_EXPERTS - 1)
    n_used = (pad_end[-1] // MOE_TM).astype(jnp.int32).reshape(1)
    return row_tok, pos, blk_e, n_used


_Q_PERM = np.concatenate([np.arange(0, 64), np.arange(128, 192), np.arange(64, 128), np.arange(192, 256)])


def _rope_tables(L):
    t = jnp.arange(L)
    row = (t // GRID_W).astype(F32)
    col = (t % GRID_W).astype(F32)
    half = HEAD_DIM // 2
    inv = ROPE_BASE ** (-jnp.arange(0, half, 2, dtype=F32) / half)
    ar, ac = row[:, None] * inv[None, :], col[:, None] * inv[None, :]
    z = jnp.zeros_like(ar)
    cos = jnp.concatenate([jnp.cos(ar), jnp.cos(ar), jnp.cos(ac), jnp.cos(ac)], axis=1)
    s1 = jnp.concatenate([-jnp.sin(ar), z, -jnp.sin(ac), z], axis=1)
    s2 = jnp.concatenate([z, jnp.sin(ar), z, jnp.sin(ac)], axis=1)
    rep = lambda a: jnp.concatenate([a, a], axis=1)
    return rep(cos), rep(s1), rep(s2)


def _block_diag(pw):
    g, n, _ = pw.shape
    out = jnp.zeros((g * n, g * n), pw.dtype)
    for gi in range(g):
        out = out.at[gi * n:(gi + 1) * n, gi * n:(gi + 1) * n].set(pw[gi])
    return out


def kernel(x, c, ctx, c_ctx, ada_w, ada_b, norm1_g, norm2_g, w_in, attn_sink, conv_dw_w, conv_dw_b, conv_ln_g, conv_ln_b, conv_pw_w, gmlp_ln_g, gmlp_ln_b, gmlp_ws, gmlp_bs, pool_w, pool_scale, group_norm_g, w_out, router_w, router_b, exp_w1, exp_b1, exp_w2, exp_b2, final_norm_g):
    b, L, d = x.shape
    lc = ctx.shape[1]
    depth = ada_w.shape[0]
    tabs = _rope_tables(L)

    r = -(-(b + 1) // 8) * 8
    cond = jnp.zeros((r, d), F32).at[:b].set(c).at[b].set(c_ctx)
    mods = _adaln(cond, ada_w, ada_b).reshape(depth, r, 6, d)

    n_groups = BATCH_GROUPS if b % BATCH_GROUPS == 0 else 1
    bh = b // n_groups
    xs, cs = [x] * n_groups, [ctx] * n_groups

    for l in range(depth):
        last = l == depth - 1
        mx = [mods[l, :b, i].reshape(b, 1, d) for i in range(6)]
        mc = [mods[l, b:b + 1, i].reshape(1, 1, d) for i in range(6)]
        sh1, sc1, g1, sh2, sc2, g2 = mx
        sh1c, sc1c, g1c, sh2c, sc2c, g2c = mc

        wl = w_in[l]
        wq = wl[:, :OFF_K][:, _Q_PERM] * (HEAD_DIM ** -0.5)
        w_full = jnp.concatenate([wq, wl[:, OFF_K:]], axis=1).astype(BF16)
        gn = group_norm_g[l]
        g_att = gn[:256][_Q_PERM]
        wo = w_out[l]
        woa = wo[:256][_Q_PERM].astype(BF16)
        wol = wo[256:].astype(BF16)
        lp = dict(
            dw_w=conv_dw_w[l], dw_b=conv_dw_b[l].reshape(1, -1),
            cln_g=conv_ln_g[l].reshape(1, -1), cln_b=conv_ln_b[l].reshape(1, -1),
            cpw=conv_pw_w[l].astype(BF16),
            gln_g=gmlp_ln_g[l].reshape(1, -1), gln_b=gmlp_ln_b[l].reshape(1, -1),
            gws=gmlp_ws[l].reshape(GMLP_HEADS * CHUNK, CHUNK).astype(BF16),
            gbs=jnp.repeat(gmlp_bs[l].T, GROUP_W // GMLP_HEADS, axis=1),
            pw=_block_diag(pool_w[l]).astype(BF16), ps=pool_scale[l].reshape(1, -1),
            gn_local=gn[256:].reshape(1, -1))
        rw = jnp.zeros((d, ROUTE_W), F32).at[:, :N_EXPERTS].set(router_w[l]).astype(BF16)
        rb = jnp.full((1, ROUTE_W), NEG_INF, F32).at[0, :N_EXPERTS].set(router_b[l])

        w_kv = wl[:, OFF_K:OFF_CONV].astype(BF16)
        take = lambda a, i: a.at[i].get(mode="promise_in_bounds")
        out = None
        for h in range(n_groups):
            moff = h * bh
            xh, ch = xs[h], cs[h]
            xoff = moff if l == 0 else 0

            q, kv, mix = _inproj(xh, xoff, bh, sc1, sh1, moff, norm1_g[l], w_full, tabs, full=True)
            if last:
                (kvc,) = _inproj(ch, xoff, bh, sc1c, sh1c, 0, norm1_g[l], w_kv, None, full=False)
            else:
                qc, kvc, mixc = _inproj(ch, xoff, bh, sc1c, sh1c, 0, norm1_g[l], w_full, None, full=True)
            att = _attention(q, kv, kvc, attn_sink[l], g_att, local=True)
            loc = _local_mixers(mix, lp)
            t_total = bh * L if last else bh * (L + lc)
            cnt0 = jnp.zeros((1, ROUTE_W), F32)
            xh, h2, gates, idx, rank, cnt = _merge(att, loc, xh, xoff, g1, sc2, sh2, moff, norm2_g[l], woa, wol,
                                                   rw, rb, cnt0, None, 0, t_total)
            if not last:
                attc = _attention(qc, kvc, kvc, attn_sink[l], g_att, local=False)
                locc = _local_mixers(mixc, lp)
                ch, h2, gates, idx, rank, cnt = _merge(attc, locc, ch, xoff, g1c, sc2c, sh2c, 0, norm2_g[l],
                                                       woa, wol, rw, rb, cnt, (h2, gates, idx, rank),
                                                       bh * L, t_total)

            counts = cnt[0, :N_EXPERTS].astype(jnp.int32)
            row_tok, pos, blk_e, n_used = _dispatch_plan(idx[:, :TOP_K], rank[:, :TOP_K], counts)
            xp = take(h2, row_tok)
            yp = _moe_ffn(xp, blk_e, n_used, exp_w1, exp_b1, exp_w2, exp_b2, l)
            ygx = take(yp, pos[:bh * L].T).reshape(TOP_K, bh, L, d)
            if last:
                out = _combine(ygx, gates, xh, g2, moff, final_norm_g, 0, final=True,
                               out_buf=out, ooff=moff, out_b=b)
            else:
                xs[h] = _combine(ygx, gates, xh, g2, moff, final_norm_g, 0, final=False)
                ygc = take(yp, pos[bh * L:].T).reshape(TOP_K, bh, lc, d)
                cs[h] = _combine(ygc, gates, ch, g2c, 0, final_norm_g, bh * L, final=False)
    return out
```

```python
import functools

import jax
import jax.numpy as jnp
import numpy as np
from jax import lax
from jax.experimental import pallas as pl
from jax.experimental.pallas import tpu as pltpu

F32 = jnp.float32
BF16 = jnp.bfloat16

D_MODEL = 1024
GRID_W = 64
GROUP_W = 256
HEAD_DIM = 64
N_Q_HEADS = 4
N_KV_HEADS = 2
WINDOW = 128
ATT_BLOCK = 128
ROPE_BASE = 10000.0
CONV_WIDTH = 31
CHUNK = 128
GMLP_HEADS = 4
POOL_WINDOWS = (2, 4, 8, 16)
N_EXPERTS = 32
TOP_K = 4
D_FF = 1024
SWIGLU_ALPHA = 1.702
SWIGLU_LIMIT = 7.0
EPS = 1e-6
NEG_INF = -1e30

OFF_K = 256
OFF_V = 384
OFF_CONV = 512
IN_W = 1792
MIX_IN_W = IN_W - OFF_CONV
LOCAL_W = 3 * GROUP_W

LANES = 128
HALO = 16
ROUTE_W = LANES
MOE_TM = 512
FF_CHUNK = 512
BATCH_GROUPS = 2
VMEM_LIMIT = 56 * 1024 * 1024


def _cparams(sem):
    return pltpu.CompilerParams(dimension_semantics=sem, vmem_limit_bytes=VMEM_LIMIT)


def _ada_kernel(c_ref, w_ref, b_ref, o_ref):
    c = c_ref[...]
    s = c * jax.nn.sigmoid(c)
    o_ref[0] = jnp.dot(s.astype(BF16), w_ref[0].astype(BF16), preferred_element_type=F32) + b_ref[0]


def _adaln(cond, ada_w, ada_b):
    depth, d, n = ada_w.shape
    r = cond.shape[0]
    tn = 1536
    return pl.pallas_call(
        _ada_kernel,
        grid=(depth, n // tn),
        in_specs=[pl.BlockSpec((r, d), lambda l, j: (0, 0)),
                  pl.BlockSpec((1, d, tn), lambda l, j: (l, 0, j)),
                  pl.BlockSpec((1, 1, tn), lambda l, j: (l, 0, j))],
        out_specs=pl.BlockSpec((1, r, tn), lambda l, j: (l, 0, j)),
        out_shape=jax.ShapeDtypeStruct((depth, r, n), F32),
        compiler_params=_cparams(("arbitrary", "arbitrary")),
        name="adaln",
    )(cond, ada_w, ada_b.reshape(depth, 1, n))


def _modulated_rmsnorm(x, g, sc, sh):
    ms = jnp.mean(x * x, axis=-1, keepdims=True)
    return x * lax.rsqrt(ms + EPS) * (g * (1.0 + sc)) + sh


def _inproj_kernel(*refs, rope, full):
    if rope:
        x_ref, sc_ref, sh_ref, g_ref, w_ref, cos_ref, s1_ref, s2_ref = refs[:8]
        outs = refs[8:]
    else:
        x_ref, sc_ref, sh_ref, g_ref, w_ref = refs[:5]
        outs = refs[5:]
    h = _modulated_rmsnorm(x_ref[0], g_ref[...], sc_ref[0], sh_ref[0])
    p = jnp.dot(h.astype(BF16), w_ref[...], preferred_element_type=F32)

    def roped(xs):
        if not rope:
            return xs
        return (xs * cos_ref[...] + pltpu.roll(xs, LANES - 16, 1) * s1_ref[...]
                + pltpu.roll(xs, 16, 1) * s2_ref[...])

    if full:
        q_ref, kv_ref, mix_ref = outs
        q_ref[0] = jnp.concatenate([roped(p[:, 0:128]), roped(p[:, 128:256])], axis=1).astype(BF16)
        kv_ref[0] = jnp.concatenate([roped(p[:, 256:384]), p[:, 384:512]], axis=1).astype(BF16)
        mix_ref[0] = p[:, OFF_CONV:]
    else:
        (kv_ref,) = outs
        kv_ref[0] = p.astype(BF16)


def _inproj(x, xoff, b, sc, sh, moff, g, w, tabs, *, full):
    _, L, d = x.shape
    n = w.shape[1]
    tm = min(512, L)
    rope = tabs is not None
    bm = sc.shape[0]
    mod_map = (lambda i, bb: (moff + bb, 0, 0)) if bm > 1 else (lambda i, bb: (0, 0, 0))
    in_specs = [pl.BlockSpec((1, tm, d), lambda i, bb: (xoff + bb, i, 0)),
                pl.BlockSpec((1, 1, d), mod_map),
                pl.BlockSpec((1, 1, d), mod_map),
                pl.BlockSpec((1, d), lambda i, bb: (0, 0)),
                pl.BlockSpec((d, n), lambda i, bb: (0, 0))]
    args = [x, sc, sh, g.reshape(1, d), w]
    if rope:
        in_specs += [pl.BlockSpec((tm, LANES), lambda i, bb: (i, 0))] * 3
        args += list(tabs)
    if full:
        out_shape = (jax.ShapeDtypeStruct((b, L, 256), BF16),
                     jax.ShapeDtypeStruct((b, L, 256), BF16),
                     jax.ShapeDtypeStruct((b, L, MIX_IN_W), F32))
        out_specs = (pl.BlockSpec((1, tm, 256), lambda i, bb: (bb, i, 0)),
                     pl.BlockSpec((1, tm, 256), lambda i, bb: (bb, i, 0)),
                     pl.BlockSpec((1, tm, MIX_IN_W), lambda i, bb: (bb, i, 0)))
    else:
        out_shape = (jax.ShapeDtypeStruct((b, L, 256), BF16),)
        out_specs = (pl.BlockSpec((1, tm, 256), lambda i, bb: (bb, i, 0)),)
    return pl.pallas_call(
        functools.partial(_inproj_kernel, rope=rope, full=full),
        grid=(L // tm, b),
        in_specs=in_specs, out_specs=out_specs, out_shape=out_shape,
        compiler_params=_cparams(("arbitrary", "arbitrary")),
        name="inproj_x" if rope else "inproj_ctx",
    )(*args)


def _attn_kernel(*refs, local, seq_len, q_blocks):
    if local:
        sink_ref, q_ref, kv_ref, kvc_ref, g_ref, bias_ref, o_ref = refs
    else:
        sink_ref, q_ref, kv_ref, kvc_ref, g_ref, o_ref = refs
    step = pl.program_id(1)
    lane = lax.broadcasted_iota(jnp.int32, (1, LANES), 1)
    row = lax.broadcasted_iota(jnp.int32, (2 * ATT_BLOCK, 1), 0)
    kvc = kvc_ref[0]
    kc, vc = kvc[:, :LANES], kvc[:, LANES:]
    lc = kvc.shape[0]
    nt = (((1,), (1,)), ((), ()))
    zero = jnp.zeros((), BF16)
    lm0 = lane < HEAD_DIM
    for sb in range(q_blocks):
        n = step * q_blocks + sb
        q = q_ref[0, sb * ATT_BLOCK:(sb + 1) * ATT_BLOCK, :]
        if local:
            blk0 = jnp.clip(n - 1, 0, seq_len // ATT_BLOCK - 3)
            kw = kv_ref[0, pl.ds(pl.multiple_of(blk0 * ATT_BLOCK, ATT_BLOCK), 3 * ATT_BLOCK), :]
            kl, vl = kw[:, :LANES], kw[:, LANES:]
            bias = bias_ref[n - blk0]
        outs = []
        for kh in range(N_KV_HEADS):
            lm = lm0 if kh == 0 else jnp.logical_not(lm0)
            lhs = jnp.concatenate([jnp.where(lm, q[:, :LANES], zero), jnp.where(lm, q[:, LANES:], zero)],
                                  axis=0)
            sk = jnp.where(row < ATT_BLOCK, sink_ref[2 * kh], sink_ref[2 * kh + 1])
            s = lax.dot_general(lhs, kc, nt, preferred_element_type=F32)
            if local:
                s_l = lax.dot_general(lhs, kl, nt, preferred_element_type=F32) + bias
                s = jnp.concatenate([s, s_l], axis=1)
            m = jnp.maximum(sk, jnp.max(s, axis=-1, keepdims=True))
            e = jnp.exp(s - m)
            den = jnp.exp(sk - m) + jnp.sum(e, axis=-1, keepdims=True)
            e = e.astype(BF16)
            o = jnp.dot(e[:, :lc], vc, preferred_element_type=F32)
            if local:
                o = o + jnp.dot(e[:, lc:], vl, preferred_element_type=F32)
            outs.append(o / den)
        o01, o23 = outs
        att = jnp.concatenate([jnp.where(lm0, o01[:ATT_BLOCK], o23[:ATT_BLOCK]),
                               jnp.where(lm0, o01[ATT_BLOCK:], o23[ATT_BLOCK:])], axis=1)
        ms = jnp.mean(att * att, axis=-1, keepdims=True)
        o_ref[0, sb * ATT_BLOCK:(sb + 1) * ATT_BLOCK, :] = (att * lax.rsqrt(ms + EPS) * g_ref[...]).astype(BF16)


def _band_bias():
    i = np.arange(2 * ATT_BLOCK)[None, :, None] % ATT_BLOCK
    j = np.arange(3 * ATT_BLOCK)[None, None, :]
    v = np.arange(3)[:, None, None]
    return jnp.asarray(np.where(np.abs(j - i - ATT_BLOCK * v) <= WINDOW, 0.0, NEG_INF), F32)


def _attention(q, kv, kvc, sink, g_att, *, local):
    b, L, _ = q.shape
    lc = kvc.shape[1]
    nb = L // ATT_BLOCK
    assert not local or nb >= 3
    q_blocks = 2 if nb % 2 == 0 else 1
    tq = q_blocks * ATT_BLOCK
    in_specs = [pl.BlockSpec(memory_space=pltpu.SMEM),
                pl.BlockSpec((1, tq, 256), lambda bb, i: (bb, i, 0)),
                pl.BlockSpec((1, kv.shape[1], 256), lambda bb, i: (bb, 0, 0)),
                pl.BlockSpec((1, lc, 256), lambda bb, i: (bb, 0, 0)),
                pl.BlockSpec((1, 256), lambda bb, i: (0, 0))]
    args = [sink, q, kv, kvc, g_att.reshape(1, 256)]
    if local:
        in_specs.append(pl.BlockSpec((3, 2 * ATT_BLOCK, 3 * ATT_BLOCK), lambda bb, i: (0, 0, 0)))
        args.append(_band_bias())
    return pl.pallas_call(
        functools.partial(_attn_kernel, local=local, seq_len=L, q_blocks=q_blocks),
        grid=(b, nb // q_blocks),
        in_specs=in_specs,
        out_specs=pl.BlockSpec((1, tq, 256), lambda bb, i: (bb, i, 0)),
        out_shape=jax.ShapeDtypeStruct((b, L, 256), BF16),
        compiler_params=_cparams(("arbitrary", "arbitrary")),
        name="attn_window" if local else "attn_ctx",
    )(*args)


def _layernorm(x, g, b):
    mu = jnp.mean(x, axis=-1, keepdims=True)
    xc = x - mu
    return xc * lax.rsqrt(jnp.mean(xc * xc, axis=-1, keepdims=True) + EPS) * g + b


def _group_norm_store(y, g):
    ms = jnp.mean(y * y, axis=-1, keepdims=True)
    return (y * lax.rsqrt(ms + EPS) * g).astype(BF16)


def _mix_kernel(cur_ref, prev_ref, next_ref, dww_ref, dwb_ref, clg_ref, clb_ref, cpw_ref,
                glg_ref, glb_ref, gws_ref, gbs_ref, pw_ref, ps_ref, gn_ref, o_ref,
                hc_ref, sh_ref, hp_ref, c2_ref, c4_ref, c8_ref, *, ts, seq_len):
    i = pl.program_id(1)
    nt = pl.num_programs(1)
    has_prev = (i > 0).astype(F32)
    has_next = (i < nt - 1).astype(F32)

    def glu(blk):
        return blk[:, 0:256] * jax.nn.sigmoid(blk[:, 256:512])

    pv = prev_ref[0]
    nx = next_ref[0]
    hc_ref[0:HALO, :] = glu(pv) * has_prev
    hc_ref[HALO + ts:2 * HALO + ts, :] = glu(nx) * has_next
    hp_ref[0:HALO, :] = pv[:, 1024:1280] * has_prev
    hp_ref[HALO + ts:2 * HALO + ts, :] = nx[:, 1024:1280] * has_next
    hp_ref[2 * HALO + ts:3 * HALO + ts, :] = jnp.zeros((HALO, 256), F32)
    hc_ref[HALO:HALO + ts, :] = glu(cur_ref[0, :, 0:512])
    hp_ref[HALO:HALO + ts, :] = cur_ref[0, :, 1024:1280]

    for r in range(1, 8):
        sh_ref[r - 1] = hc_ref[r:r + ts + 24, :]
    c2_ref[8:ts + 40, :] = hp_ref[8:ts + 40, :] + hp_ref[9:ts + 41, :]
    c4_ref[8:ts + 32, :] = c2_ref[8:ts + 32, :] + c2_ref[10:ts + 34, :]
    c8_ref[8:ts + 24, :] = c4_ref[8:ts + 24, :] + c4_ref[12:ts + 28, :]

    lane = lax.broadcasted_iota(jnp.int32, (1, 256), 1)
    rc = 64
    for c in range(ts // rc):
        r0 = c * rc
        acc = jnp.zeros((rc, 256), F32)
        for j in range(CONV_WIDTH):
            a, r = divmod(1 + j, 8)
            lo = r0 + 8 * a
            tap = hc_ref[lo:lo + rc, :] if r == 0 else sh_ref[r - 1, lo:lo + rc, :]
            acc = acc + tap * dww_ref[j:j + 1, :]
        hcv = _layernorm(acc + dwb_ref[...], clg_ref[...], clb_ref[...])
        hcv = hcv * jax.nn.sigmoid(hcv)
        conv = jnp.dot(hcv.astype(BF16), cpw_ref[...], preferred_element_type=F32)
        o_ref[0, r0:r0 + rc, 0:256] = _group_norm_store(conv, gn_ref[:, 0:256])

        s0 = r0 + HALO
        hcur = hp_ref[s0:s0 + rc, :]
        s2 = c2_ref[s0 - 1:s0 - 1 + rc, :]
        s4 = c4_ref[s0 - 2:s0 - 2 + rc, :]
        s8 = c8_ref[s0 - 4:s0 - 4 + rc, :]
        s16 = c8_ref[s0 - 8:s0 - 8 + rc, :] + c8_ref[s0:s0 + rc, :]
        wsum = jnp.where(lane < 64, s2, jnp.where(lane < 128, s4, jnp.where(lane < 192, s8, s16)))
        half = jnp.where(lane < 64, 1, jnp.where(lane < 128, 2, jnp.where(lane < 192, 4, 8)))
        t = i * ts + r0 + lax.broadcasted_iota(jnp.int32, (rc, 1), 0)
        cnt = jnp.clip(t + half, 0, seq_len) - jnp.clip(t - half, 0, seq_len)
        y = wsum / cnt.astype(F32) - hcur
        pool = jnp.dot(y.astype(BF16), pw_ref[...], preferred_element_type=F32) * ps_ref[...]
        o_ref[0, r0:r0 + rc, 512:768] = _group_norm_store(pool, gn_ref[:, 512:768])

    for c in range(ts // CHUNK):
        r0 = c * CHUNK
        u = cur_ref[0, r0:r0 + CHUNK, 512:768]
        v = _layernorm(cur_ref[0, r0:r0 + CHUNK, 768:1024], glg_ref[...], glb_ref[...])
        r = jnp.dot(gws_ref[...], v.astype(BF16), preferred_element_type=F32)
        mixed = jnp.where(lane < 64, r[0:CHUNK],
                          jnp.where(lane < 128, r[CHUNK:2 * CHUNK],
                                    jnp.where(lane < 192, r[2 * CHUNK:3 * CHUNK], r[3 * CHUNK:])))
        gm = u * (mixed + gbs_ref[...])
        o_ref[0, r0:r0 + CHUNK, 256:512] = _group_norm_store(gm, gn_ref[:, 256:512])


def _local_mixers(mix, lp):
    b, L, w = mix.shape
    ts = min(512, L)
    nt = L // ts
    hb = ts // HALO
    last_h = L // HALO - 1
    full = lambda shape: pl.BlockSpec(shape, lambda bb, i: (0,) * len(shape))
    params = [lp["dw_w"], lp["dw_b"], lp["cln_g"], lp["cln_b"], lp["cpw"], lp["gln_g"], lp["gln_b"],
              lp["gws"], lp["gbs"], lp["pw"], lp["ps"], lp["gn_local"]]
    return pl.pallas_call(
        functools.partial(_mix_kernel, ts=ts, seq_len=L),
        grid=(b, nt),
        in_specs=[pl.BlockSpec((1, ts, w), lambda bb, i: (bb, i, 0)),
                  pl.BlockSpec((1, HALO, w), lambda bb, i: (bb, jnp.maximum(i * hb - 1, 0), 0)),
                  pl.BlockSpec((1, HALO, w), lambda bb, i: (bb, jnp.minimum((i + 1) * hb, last_h), 0))]
                 + [full(p.shape) for p in params],
        out_specs=pl.BlockSpec((1, ts, LOCAL_W), lambda bb, i: (bb, i, 0)),
        out_shape=jax.ShapeDtypeStruct((b, L, LOCAL_W), BF16),
        scratch_shapes=[pltpu.VMEM((ts + 2 * HALO, 256), F32), pltpu.VMEM((7, ts + 24, 256), F32)]
                       + [pltpu.VMEM((ts + 3 * HALO, 256), F32)] * 4,
        compiler_params=_cparams(("arbitrary", "arbitrary")),
        name="local_mixers",
    )(mix, mix, mix, *params)


def _merge_kernel(*refs, n_alias):
    (att_ref, loc_ref, x_ref, g1_ref, sc_ref, sh_ref, n2g_ref, woa_ref, wol_ref,
     rw_ref, rb_ref, ltri_ref, cnt_in_ref) = refs[:13]
    xo_ref, h2_ref, gate_ref, idx_ref, rank_ref, cnt_ref, run_ref = refs[13 + n_alias:]

    @pl.when(jnp.logical_and(pl.program_id(0) == 0, pl.program_id(1) == 0))
    def _():
        run_ref[...] = cnt_in_ref[...]

    y = (jnp.dot(att_ref[0], woa_ref[...], preferred_element_type=F32)
         + jnp.dot(loc_ref[0], wol_ref[...], preferred_element_type=F32))
    xn = x_ref[0] + g1_ref[0] * y
    xo_ref[0] = xn
    h2 = _modulated_rmsnorm(xn, n2g_ref[...], sc_ref[0], sh_ref[0]).astype(BF16)
    h2_ref[...] = h2
    logits = jnp.dot(h2, rw_ref[...], preferred_element_type=F32) + rb_ref[...]
    lane = lax.broadcasted_iota(jnp.int32, logits.shape, 1)
    vals = jnp.zeros(logits.shape, F32)
    idxs = jnp.zeros(logits.shape, jnp.int32)
    hits = jnp.zeros(logits.shape, F32)
    sels = []
    top = None
    for k in range(TOP_K):
        m = jnp.max(logits, axis=-1, keepdims=True)
        sel = jnp.min(jnp.where(logits == m, lane, ROUTE_W), axis=-1, keepdims=True)
        if top is None:
            top = m
        chosen = lane == sel
        vals = jnp.where(lane == k, jnp.exp(m - top), vals)
        idxs = jnp.where(lane == k, sel, idxs)
        hits = jnp.where(chosen, 1.0, hits)
        logits = jnp.where(chosen, NEG_INF * 2.0, logits)
        sels.append(chosen)
    gate_ref[...] = vals / jnp.sum(vals, axis=-1, keepdims=True)
    idx_ref[...] = idxs
    base = run_ref[...] + jnp.dot(ltri_ref[...], hits.astype(BF16), preferred_element_type=F32)
    ranks = jnp.zeros(logits.shape, F32)
    for k in range(TOP_K):
        rk = jnp.sum(jnp.where(sels[k], base, 0.0), axis=-1, keepdims=True)
        ranks = jnp.where(lane == k, rk, ranks)
    rank_ref[...] = ranks.astype(jnp.int32)
    run_ref[...] = run_ref[...] + jnp.sum(hits, axis=0, keepdims=True)
    cnt_ref[...] = run_ref[...]


def _merge(att, loc, x, xoff, g1, sc2, sh2, moff, n2g, woa, wol, rw, rb, cnt_in, bufs, row_off, t_total):
    b = att.shape[0]
    _, L, d = x.shape
    tm = min(512, L)
    nt = L // tm
    blk_off = row_off // tm
    bm = g1.shape[0]
    mod_map = (lambda bb, i: (moff + bb, 0, 0)) if bm > 1 else (lambda bb, i: (0, 0, 0))
    tile = lambda w: pl.BlockSpec((1, tm, w), lambda bb, i: (bb, i, 0))
    xtile = pl.BlockSpec((1, tm, d), lambda bb, i: (xoff + bb, i, 0))
    flat = lambda w: pl.BlockSpec((tm, w), lambda bb, i: (blk_off + bb * nt + i, 0))
    full = lambda shape: pl.BlockSpec(shape, lambda bb, i: (0,) * len(shape))
    ltri = (np.arange(tm)[:, None] > np.arange(tm)[None, :]).astype(np.float32)
    ltri = jnp.asarray(ltri, BF16)
    n_alias = 0 if bufs is None else len(bufs)
    n_in = 13
    return pl.pallas_call(
        functools.partial(_merge_kernel, n_alias=n_alias),
        grid=(b, nt),
        in_specs=[tile(256), tile(LOCAL_W), xtile,
                  pl.BlockSpec((1, 1, d), mod_map), pl.BlockSpec((1, 1, d), mod_map),
                  pl.BlockSpec((1, 1, d), mod_map),
                  full((1, d)), full(woa.shape), full(wol.shape), full(rw.shape), full(rb.shape),
                  full((tm, tm)), full((1, ROUTE_W))]
                 + [pl.BlockSpec(memory_space=pl.ANY)] * n_alias,
        out_specs=(tile(d), flat(d), flat(ROUTE_W), flat(ROUTE_W), flat(ROUTE_W), full((1, ROUTE_W))),
        out_shape=(jax.ShapeDtypeStruct((b, L, d), F32),
                   jax.ShapeDtypeStruct((t_total, d), BF16),
                   jax.ShapeDtypeStruct((t_total, ROUTE_W), F32),
                   jax.ShapeDtypeStruct((t_total, ROUTE_W), jnp.int32),
                   jax.ShapeDtypeStruct((t_total, ROUTE_W), jnp.int32),
                   jax.ShapeDtypeStruct((1, ROUTE_W), F32)),
        scratch_shapes=[pltpu.VMEM((1, ROUTE_W), F32)],
        input_output_aliases={n_in + j: 1 + j for j in range(n_alias)},
        compiler_params=_cparams(("arbitrary", "arbitrary")),
        name="merge_router",
    )(att, loc, x, g1, sc2, sh2, n2g.reshape(1, d), woa, wol, rw, rb, ltri, cnt_in,
      *(bufs if bufs is not None else ()))


def _moe_kernel(blk_e_ref, n_used_ref, x_ref, w1_ref, b1_ref, w2_ref, b2_ref, o_ref, w1s_ref, w2s_ref):
    i = pl.program_id(0)
    active = i < n_used_ref[0]
    new_expert = jnp.logical_or(i == 0, blk_e_ref[i] != blk_e_ref[jnp.maximum(i - 1, 0)])

    @pl.when(jnp.logical_and(active, new_expert))
    def _():
        rows = 128
        for r in range(D_MODEL // rows):
            w1s_ref[r * rows:(r + 1) * rows, :] = w1_ref[0, 0, r * rows:(r + 1) * rows, :].astype(BF16)
        for r in range(D_FF // rows):
            w2s_ref[r * rows:(r + 1) * rows, :] = w2_ref[0, 0, r * rows:(r + 1) * rows, :].astype(BF16)

    @pl.when(active)
    def _():
        x = x_ref[...]
        acc = jnp.zeros((x.shape[0], D_MODEL), F32)
        for c in range(D_FF // FF_CHUNK):
            lo = c * FF_CHUNK
            glu = jnp.dot(x, w1s_ref[:, lo:lo + FF_CHUNK], preferred_element_type=F32) \
                + b1_ref[0, 0, :, lo:lo + FF_CHUNK]
            lin = jnp.dot(x, w1s_ref[:, D_FF + lo:D_FF + lo + FF_CHUNK], preferred_element_type=F32) \
                + b1_ref[0, 0, :, D_FF + lo:D_FF + lo + FF_CHUNK]
            glu = jnp.minimum(glu, SWIGLU_LIMIT)
            lin = jnp.clip(lin, -SWIGLU_LIMIT, SWIGLU_LIMIT)
            act = glu * jax.nn.sigmoid(SWIGLU_ALPHA * glu) * (lin + 1.0)
            acc = acc + jnp.dot(act.astype(BF16), w2s_ref[lo:lo + FF_CHUNK, :], preferred_element_type=F32)
        o_ref[...] = (acc + b2_ref[0, 0]).astype(o_ref.dtype)

    @pl.when(i >= n_used_ref[0])
    def _():
        o_ref[...] = jnp.zeros(o_ref.shape, o_ref.dtype)


def _moe_ffn(xp, blk_e, n_used, w1, b1, w2, b2, layer):
    n_rows, d = xp.shape
    n_blk = n_rows // MOE_TM
    depth, e = w1.shape[:2]

    def row_map(i, be, nu):
        return (jnp.maximum(jnp.minimum(i, nu[0] - 1), 0), 0)

    def w_map(i, be, nu):
        return (layer, be[i], 0, 0)

    return pl.pallas_call(
        _moe_kernel,
        grid_spec=pltpu.PrefetchScalarGridSpec(
            num_scalar_prefetch=2,
            grid=(n_blk,),
            in_specs=[pl.BlockSpec((MOE_TM, d), row_map),
                      pl.BlockSpec((1, 1, d, 2 * D_FF), w_map),
                      pl.BlockSpec((1, 1, 1, 2 * D_FF), w_map),
                      pl.BlockSpec((1, 1, D_FF, d), w_map),
                      pl.BlockSpec((1, 1, 1, d), w_map)],
            out_specs=pl.BlockSpec((MOE_TM, d), lambda i, be, nu: (i, 0)),
            scratch_shapes=[pltpu.VMEM((d, 2 * D_FF), BF16), pltpu.VMEM((D_FF, d), BF16)]),
        out_shape=jax.ShapeDtypeStruct((n_rows, d), BF16),
        compiler_params=_cparams(("arbitrary",)),
        name="moe_ffn",
    )(blk_e, n_used, xp, w1, b1.reshape(depth, e, 1, 2 * D_FF), w2, b2.reshape(depth, e, 1, d))


def _combine_kernel(*refs, final):
    yg_ref, gate_ref, x_ref, g2_ref, fg_ref = refs[:5]
    o_ref = refs[-1]
    gates = gate_ref[...]
    ff = jnp.zeros(x_ref.shape[1:], F32)
    for k in range(TOP_K):
        ff = ff + gates[:, k:k + 1] * yg_ref[k, 0].astype(F32)
    xo = x_ref[0] + g2_ref[0] * ff
    if final:
        ms = jnp.mean(xo * xo, axis=-1, keepdims=True)
        xo = xo * lax.rsqrt(ms + EPS) * fg_ref[...]
    o_ref[0] = xo


def _combine(yg, gates, x, g2, moff, fg, row_off, *, final, out_buf=None, ooff=0, out_b=None):
    b, L, d = x.shape
    tm = min(256, L)
    nt = L // tm
    blk_off = row_off // tm
    bm = g2.shape[0]
    out_b = b if out_b is None else out_b
    mod_map = (lambda bb, i: (moff + bb, 0, 0)) if bm > 1 else (lambda bb, i: (0, 0, 0))
    extra = () if out_buf is None else (out_buf,)
    return pl.pallas_call(
        functools.partial(_combine_kernel, final=final),
        grid=(b, nt),
        in_specs=[pl.BlockSpec((TOP_K, 1, tm, d), lambda bb, i: (0, bb, i, 0)),
                  pl.BlockSpec((tm, ROUTE_W), lambda bb, i: (blk_off + bb * nt + i, 0)),
                  pl.BlockSpec((1, tm, d), lambda bb, i: (bb, i, 0)),
                  pl.BlockSpec((1, 1, d), mod_map),
                  pl.BlockSpec((1, d), lambda bb, i: (0, 0))]
                 + [pl.BlockSpec(memory_space=pl.ANY)] * len(extra),
        out_specs=pl.BlockSpec((1, tm, d), lambda bb, i: (ooff + bb, i, 0)),
        out_shape=jax.ShapeDtypeStruct((out_b, L, d), F32),
        input_output_aliases={5: 0} if extra else {},
        compiler_params=_cparams(("arbitrary", "arbitrary")),
        name="combine",
    )(yg, gates, x, g2, fg.reshape(1, d), *extra)


def _dispatch_plan(idx, rank, counts):
    n_tok = idx.shape[0]
    n_asg = n_tok * TOP_K
    padded = (counts + MOE_TM - 1) // MOE_TM * MOE_TM
    pad_end = jnp.cumsum(padded)
    pad_start = pad_end - padded
    onehot = idx[..., None] == jnp.arange(N_EXPERTS, dtype=jnp.int32)
    pos = jnp.sum(jnp.where(onehot, pad_start.astype(jnp.int32), 0), axis=-1) + rank
    n_rows = -(-n_asg // MOE_TM) * MOE_TM + N_EXPERTS * MOE_TM
    n_blk = n_rows // MOE_TM
    tok = jnp.broadcast_to(jnp.arange(n_tok, dtype=jnp.int32)[:, None], (n_tok, TOP_K))
    fill = jnp.arange(n_rows, dtype=jnp.int32) % n_tok
    row_tok = fill.at[pos.reshape(-1)].add((tok - pos % n_tok).reshape(-1), mode="promise_in_bounds")
    blk_start = jnp.arange(n_blk, dtype=jnp.int32) * MOE_TM
    blk_e = jnp.minimum(jnp.sum((pad_end[None, :] <= blk_start[:, None]).astype(jnp.int32), axis=1),
                        N_EXPERTS - 1)
    n_used = (pad_end[-1] // MOE_TM).astype(jnp.int32).reshape(1)
    return row_tok, pos, blk_e, n_used


_Q_PERM = np.concatenate([np.arange(0, 64), np.arange(128, 192), np.arange(64, 128), np.arange(192, 256)])


def _rope_tables(L):
    t = jnp.arange(L)
    row = (t // GRID_W).astype(F32)
    col = (t % GRID_W).astype(F32)
    half = HEAD_DIM // 2
    inv = ROPE_BASE ** (-jnp.arange(0, half, 2, dtype=F32) / half)
    ar, ac = row[:, None] * inv[None, :], col[:, None] * inv[None, :]
    z = jnp.zeros_like(ar)
    cos = jnp.concatenate([jnp.cos(ar), jnp.cos(ar), jnp.cos(ac), jnp.cos(ac)], axis=1)
    s1 = jnp.concatenate([-jnp.sin(ar), z, -jnp.sin(ac), z], axis=1)
    s2 = jnp.concatenate([z, jnp.sin(ar), z, jnp.sin(ac)], axis=1)
    rep = lambda a: jnp.concatenate([a, a], axis=1)
    return rep(cos), rep(s1), rep(s2)


def _block_diag(pw):
    g, n, _ = pw.shape
    out = jnp.zeros((g * n, g * n), pw.dtype)
    for gi in range(g):
        out = out.at[gi * n:(gi + 1) * n, gi * n:(gi + 1) * n].set(pw[gi])
    return out


def kernel(x, c, ctx, c_ctx, ada_w, ada_b, norm1_g, norm2_g, w_in, attn_sink, conv_dw_w, conv_dw_b, conv_ln_g, conv_ln_b, conv_pw_w, gmlp_ln_g, gmlp_ln_b, gmlp_ws, gmlp_bs, pool_w, pool_scale, group_norm_g, w_out, router_w, router_b, exp_w1, exp_b1, exp_w2, exp_b2, final_norm_g):
    b, L, d = x.shape
    lc = ctx.shape[1]
    depth = ada_w.shape[0]
    tabs = _rope_tables(L)

    r = -(-(b + 1) // 8) * 8
    cond = jnp.zeros((r, d), F32).at[:b].set(c).at[b].set(c_ctx)
    mods = _adaln(cond, ada_w, ada_b).reshape(depth, r, 6, d)

    n_groups = BATCH_GROUPS if b % BATCH_GROUPS == 0 else 1
    bh = b // n_groups
    xs, cs = [x] * n_groups, [ctx] * n_groups

    for l in range(depth):
        last = l == depth - 1
        mx = [mods[l, :b, i].reshape(b, 1, d) for i in range(6)]
        mc = [mods[l, b:b + 1, i].reshape(1, 1, d) for i in range(6)]
        sh1, sc1, g1, sh2, sc2, g2 = mx
        sh1c, sc1c, g1c, sh2c, sc2c, g2c = mc

        wl = w_in[l]
        wq = wl[:, :OFF_K][:, _Q_PERM] * (HEAD_DIM ** -0.5)
        w_full = jnp.concatenate([wq, wl[:, OFF_K:]], axis=1).astype(BF16)
        gn = group_norm_g[l]
        g_att = gn[:256][_Q_PERM]
        wo = w_out[l]
        woa = wo[:256][_Q_PERM].astype(BF16)
        wol = wo[256:].astype(BF16)
        lp = dict(
            dw_w=conv_dw_w[l], dw_b=conv_dw_b[l].reshape(1, -1),
            cln_g=conv_ln_g[l].reshape(1, -1), cln_b=conv_ln_b[l].reshape(1, -1),
            cpw=conv_pw_w[l].astype(BF16),
            gln_g=gmlp_ln_g[l].reshape(1, -1), gln_b=gmlp_ln_b[l].reshape(1, -1),
            gws=gmlp_ws[l].reshape(GMLP_HEADS * CHUNK, CHUNK).astype(BF16),
            gbs=jnp.repeat(gmlp_bs[l].T, GROUP_W // GMLP_HEADS, axis=1),
            pw=_block_diag(pool_w[l]).astype(BF16), ps=pool_scale[l].reshape(1, -1),
            gn_local=gn[256:].reshape(1, -1))
        rw = jnp.zeros((d, ROUTE_W), F32).at[:, :N_EXPERTS].set(router_w[l]).astype(BF16)
        rb = jnp.full((1, ROUTE_W), NEG_INF, F32).at[0, :N_EXPERTS].set(router_b[l])

        w_kv = wl[:, OFF_K:OFF_CONV].astype(BF16)
        take = lambda a, i: a.at[i].get(mode="promise_in_bounds")
        out = None
        for h in range(n_groups):
            moff = h * bh
            xh, ch = xs[h], cs[h]
            xoff = moff if l == 0 else 0

            q, kv, mix = _inproj(xh, xoff, bh, sc1, sh1, moff, norm1_g[l], w_full, tabs, full=True)
            if last:
                (kvc,) = _inproj(ch, xoff, bh, sc1c, sh1c, 0, norm1_g[l], w_kv, None, full=False)
            else:
                qc, kvc, mixc = _inproj(ch, xoff, bh, sc1c, sh1c, 0, norm1_g[l], w_full, None, full=True)
            att = _attention(q, kv, kvc, attn_sink[l], g_att, local=True)
            loc = _local_mixers(mix, lp)
            t_total = bh * L if last else bh * (L + lc)
            cnt0 = jnp.zeros((1, ROUTE_W), F32)
            xh, h2, gates, idx, rank, cnt = _merge(att, loc, xh, xoff, g1, sc2, sh2, moff, norm2_g[l], woa, wol,
                                                   rw, rb, cnt0, None, 0, t_total)
            if not last:
                attc = _attention(qc, kvc, kvc, attn_sink[l], g_att, local=False)
                locc = _local_mixers(mixc, lp)
                ch, h2, gates, idx, rank, cnt = _merge(attc, locc, ch, xoff, g1c, sc2c, sh2c, 0, norm2_g[l],
                                                       woa, wol, rw, rb, cnt, (h2, gates, idx, rank),
                                                       bh * L, t_total)

            counts = cnt[0, :N_EXPERTS].astype(jnp.int32)
            row_tok, pos, blk_e, n_used = _dispatch_plan(idx[:, :TOP_K], rank[:, :TOP_K], counts)
            xp = take(h2, row_tok)
            yp = _moe_ffn(xp, blk_e, n_used, exp_w1, exp_b1, exp_w2, exp_b2, l)
            ygx = take(yp, pos[:bh * L].T).reshape(TOP_K, bh, L, d)
            if last:
                out = _combine(ygx, gates, xh, g2, moff, final_norm_g, 0, final=True,
                               out_buf=out, ooff=moff, out_b=b)
            else:
                xs[h] = _combine(ygx, gates, xh, g2, moff, final_norm_g, 0, final=False)
                ygc = take(yp, pos[bh * L:].T).reshape(TOP_K, bh, lc, d)
                cs[h] = _combine(ygc, gates, ch, g2c, 0, final_norm_g, bh * L, final=False)
    return out
```

```python
import functools

import jax
import jax.numpy as jnp
import numpy as np
from jax import lax
from jax.experimental import pallas as pl
from jax.experimental.pallas import tpu as pltpu

F32 = jnp.float32
BF16 = jnp.bfloat16

D_MODEL = 1024
GRID_W = 64
GROUP_W = 256
HEAD_DIM = 64
N_Q_HEADS = 4
N_KV_HEADS = 2
WINDOW = 128
ATT_BLOCK = 128
ROPE_BASE = 10000.0
CONV_WIDTH = 31
CHUNK = 128
GMLP_HEADS = 4
POOL_WINDOWS = (2, 4, 8, 16)
N_EXPERTS = 32
TOP_K = 4
D_FF = 1024
SWIGLU_ALPHA = 1.702
SWIGLU_LIMIT = 7.0
EPS = 1e-6
NEG_INF = -1e30

OFF_K = 256
OFF_V = 384
OFF_CONV = 512
IN_W = 1792
MIX_IN_W = IN_W - OFF_CONV
LOCAL_W = 3 * GROUP_W

LANES = 128
HALO = 16
ROUTE_W = LANES
MOE_TM = 512
FF_CHUNK = 512
BATCH_GROUPS = 2
VMEM_LIMIT = 56 * 1024 * 1024


def _cparams(sem):
    return pltpu.CompilerParams(dimension_semantics=sem, vmem_limit_bytes=VMEM_LIMIT)


def _ada_kernel(c_ref, w_ref, b_ref, o_ref):
    c = c_ref[...]
    s = c * jax.nn.sigmoid(c)
    o_ref[0] = jnp.dot(s.astype(BF16), w_ref[0].astype(BF16), preferred_element_type=F32) + b_ref[0]


def _adaln(cond, ada_w, ada_b):
    depth, d, n = ada_w.shape
    r = cond.shape[0]
    tn = 1536
    return pl.pallas_call(
        _ada_kernel,
        grid=(depth, n // tn),
        in_specs=[pl.BlockSpec((r, d), lambda l, j: (0, 0)),
                  pl.BlockSpec((1, d, tn), lambda l, j: (l, 0, j)),
                  pl.BlockSpec((1, 1, tn), lambda l, j: (l, 0, j))],
        out_specs=pl.BlockSpec((1, r, tn), lambda l, j: (l, 0, j)),
        out_shape=jax.ShapeDtypeStruct((depth, r, n), F32),
        compiler_params=_cparams(("arbitrary", "arbitrary")),
        name="adaln",
    )(cond, ada_w, ada_b.reshape(depth, 1, n))


def _modulated_rmsnorm(x, g, sc, sh):
    ms = jnp.mean(x * x, axis=-1, keepdims=True)
    return x * lax.rsqrt(ms + EPS) * (g * (1.0 + sc)) + sh


def _inproj_kernel(*refs, rope, full):
    if rope:
        x_ref, sc_ref, sh_ref, g_ref, w_ref, cos_ref, s1_ref, s2_ref = refs[:8]
        outs = refs[8:]
    else:
        x_ref, sc_ref, sh_ref, g_ref, w_ref = refs[:5]
        outs = refs[5:]
    h = _modulated_rmsnorm(x_ref[0], g_ref[...], sc_ref[0], sh_ref[0])
    p = jnp.dot(h.astype(BF16), w_ref[...], preferred_element_type=F32)

    def roped(xs):
        if not rope:
            return xs
        return (xs * cos_ref[...] + pltpu.roll(xs, LANES - 16, 1) * s1_ref[...]
                + pltpu.roll(xs, 16, 1) * s2_ref[...])

    if full:
        q_ref, kv_ref, mix_ref = outs
        q_ref[0] = jnp.concatenate([roped(p[:, 0:128]), roped(p[:, 128:256])], axis=1).astype(BF16)
        kv_ref[0] = jnp.concatenate([roped(p[:, 256:384]), p[:, 384:512]], axis=1).astype(BF16)
        mix_ref[0] = p[:, OFF_CONV:]
    else:
        (kv_ref,) = outs
        kv_ref[0] = p.astype(BF16)


def _inproj(x, xoff, b, sc, sh, moff, g, w, tabs, *, full):
    _, L, d = x.shape
    n = w.shape[1]
    tm = min(512, L)
    rope = tabs is not None
    bm = sc.shape[0]
    mod_map = (lambda i, bb: (moff + bb, 0, 0)) if bm > 1 else (lambda i, bb: (0, 0, 0))
    in_specs = [pl.BlockSpec((1, tm, d), lambda i, bb: (xoff + bb, i, 0)),
                pl.BlockSpec((1, 1, d), mod_map),
                pl.BlockSpec((1, 1, d), mod_map),
                pl.BlockSpec((1, d), lambda i, bb: (0, 0)),
                pl.BlockSpec((d, n), lambda i, bb: (0, 0))]
    args = [x, sc, sh, g.reshape(1, d), w]
    if rope:
        in_specs += [pl.BlockSpec((tm, LANES), lambda i, bb: (i, 0))] * 3
        args += list(tabs)
    if full:
        out_shape = (jax.ShapeDtypeStruct((b, L, 256), BF16),
                     jax.ShapeDtypeStruct((b, L, 256), BF16),
                     jax.ShapeDtypeStruct((b, L, MIX_IN_W), F32))
        out_specs = (pl.BlockSpec((1, tm, 256), lambda i, bb: (bb, i, 0)),
                     pl.BlockSpec((1, tm, 256), lambda i, bb: (bb, i, 0)),
                     pl.BlockSpec((1, tm, MIX_IN_W), lambda i, bb: (bb, i, 0)))
    else:
        out_shape = (jax.ShapeDtypeStruct((b, L, 256), BF16),)
        out_specs = (pl.BlockSpec((1, tm, 256), lambda i, bb: (bb, i, 0)),)
    return pl.pallas_call(
        functools.partial(_inproj_kernel, rope=rope, full=full),
        grid=(L // tm, b),
        in_specs=in_specs, out_specs=out_specs, out_shape=out_shape,
        compiler_params=_cparams(("arbitrary", "arbitrary")),
        name="inproj_x" if rope else "inproj_ctx",
    )(*args)


def _attn_kernel(*refs, local, seq_len, q_blocks):
    if local:
        sink_ref, q_ref, kv_ref, kvc_ref, g_ref, bias_ref, o_ref = refs
    else:
        sink_ref, q_ref, kv_ref, kvc_ref, g_ref, o_ref = refs
    step = pl.program_id(1)
    lane = lax.broadcasted_iota(jnp.int32, (1, LANES), 1)
    row = lax.broadcasted_iota(jnp.int32, (2 * ATT_BLOCK, 1), 0)
    kvc = kvc_ref[0]
    kc, vc = kvc[:, :LANES], kvc[:, LANES:]
    lc = kvc.shape[0]
    nt = (((1,), (1,)), ((), ()))
    zero = jnp.zeros((), BF16)
    lm0 = lane < HEAD_DIM
    for sb in range(q_blocks):
        n = step * q_blocks + sb
        q = q_ref[0, sb * ATT_BLOCK:(sb + 1) * ATT_BLOCK, :]
        if local:
            blk0 = jnp.clip(n - 1, 0, seq_len // ATT_BLOCK - 3)
            kw = kv_ref[0, pl.ds(pl.multiple_of(blk0 * ATT_BLOCK, ATT_BLOCK), 3 * ATT_BLOCK), :]
            kl, vl = kw[:, :LANES], kw[:, LANES:]
            bias = bias_ref[n - blk0]
        outs = []
        for kh in range(N_KV_HEADS):
            lm = lm0 if kh == 0 else jnp.logical_not(lm0)
            lhs = jnp.concatenate([jnp.where(lm, q[:, :LANES], zero), jnp.where(lm, q[:, LANES:], zero)],
                                  axis=0)
            sk = jnp.where(row < ATT_BLOCK, sink_ref[2 * kh], sink_ref[2 * kh + 1])
            s = lax.dot_general(lhs, kc, nt, preferred_element_type=F32)
            if local:
                s_l = lax.dot_general(lhs, kl, nt, preferred_element_type=F32) + bias
                s = jnp.concatenate([s, s_l], axis=1)
            m = jnp.maximum(sk, jnp.max(s, axis=-1, keepdims=True))
            e = jnp.exp(s - m)
            den = jnp.exp(sk - m) + jnp.sum(e, axis=-1, keepdims=True)
            e = e.astype(BF16)
            o = jnp.dot(e[:, :lc], vc, preferred_element_type=F32)
            if local:
                o = o + jnp.dot(e[:, lc:], vl, preferred_element_type=F32)
            outs.append(o / den)
        o01, o23 = outs
        att = jnp.concatenate([jnp.where(lm0, o01[:ATT_BLOCK], o23[:ATT_BLOCK]),
                               jnp.where(lm0, o01[ATT_BLOCK:], o23[ATT_BLOCK:])], axis=1)
        ms = jnp.mean(att * att, axis=-1, keepdims=True)
        o_ref[0, sb * ATT_BLOCK:(sb + 1) * ATT_BLOCK, :] = (att * lax.rsqrt(ms + EPS) * g_ref[...]).astype(BF16)


def _band_bias():
    i = np.arange(2 * ATT_BLOCK)[None, :, None] % ATT_BLOCK
    j = np.arange(3 * ATT_BLOCK)[None, None, :]
    v = np.arange(3)[:, None, None]
    return jnp.asarray(np.where(np.abs(j - i - ATT_BLOCK * v) <= WINDOW, 0.0, NEG_INF), F32)


def _attention(q, kv, kvc, sink, g_att, *, local):
    b, L, _ = q.shape
    lc = kvc.shape[1]
    nb = L // ATT_BLOCK
    assert not local or nb >= 3
    q_blocks = 2 if nb % 2 == 0 else 1
    tq = q_blocks * ATT_BLOCK
    in_specs = [pl.BlockSpec(memory_space=pltpu.SMEM),
                pl.BlockSpec((1, tq, 256), lambda bb, i: (bb, i, 0)),
                pl.BlockSpec((1, kv.shape[1], 256), lambda bb, i: (bb, 0, 0)),
                pl.BlockSpec((1, lc, 256), lambda bb, i: (bb, 0, 0)),
                pl.BlockSpec((1, 256), lambda bb, i: (0, 0))]
    args = [sink, q, kv, kvc, g_att.reshape(1, 256)]
    if local:
        in_specs.append(pl.BlockSpec((3, 2 * ATT_BLOCK, 3 * ATT_BLOCK), lambda bb, i: (0, 0, 0)))
        args.append(_band_bias())
    return pl.pallas_call(
        functools.partial(_attn_kernel, local=local, seq_len=L, q_blocks=q_blocks),
        grid=(b, nb // q_blocks),
        in_specs=in_specs,
        out_specs=pl.BlockSpec((1, tq, 256), lambda bb, i: (bb, i, 0)),
        out_shape=jax.ShapeDtypeStruct((b, L, 256), BF16),
        compiler_params=_cparams(("arbitrary", "arbitrary")),
        name="attn_window" if local else "attn_ctx",
    )(*args)


def _layernorm(x, g, b):
    mu = jnp.mean(x, axis=-1, keepdims=True)
    xc = x - mu
    return xc * lax.rsqrt(jnp.mean(xc * xc, axis=-1, keepdims=True) + EPS) * g + b


def _group_norm_store(y, g):
    ms = jnp.mean(y * y, axis=-1, keepdims=True)
    return (y * lax.rsqrt(ms + EPS) * g).astype(BF16)


def _mix_kernel(cur_ref, prev_ref, next_ref, dww_ref, dwb_ref, clg_ref, clb_ref, cpw_ref,
                glg_ref, glb_ref, gws_ref, gbs_ref, pw_ref, ps_ref, gn_ref, o_ref,
                hc_ref, sh_ref, hp_ref, c2_ref, c4_ref, c8_ref, *, ts, seq_len):
    i = pl.program_id(1)
    nt = pl.num_programs(1)
    has_prev = (i > 0).astype(F32)
    has_next = (i < nt - 1).astype(F32)

    def glu(blk):
        return blk[:, 0:256] * jax.nn.sigmoid(blk[:, 256:512])

    pv = prev_ref[0]
    nx = next_ref[0]
    hc_ref[0:HALO, :] = glu(pv) * has_prev
    hc_ref[HALO + ts:2 * HALO + ts, :] = glu(nx) * has_next
    hp_ref[0:HALO, :] = pv[:, 1024:1280] * has_prev
    hp_ref[HALO + ts:2 * HALO + ts, :] = nx[:, 1024:1280] * has_next
    hp_ref[2 * HALO + ts:3 * HALO + ts, :] = jnp.zeros((HALO, 256), F32)
    hc_ref[HALO:HALO + ts, :] = glu(cur_ref[0, :, 0:512])
    hp_ref[HALO:HALO + ts, :] = cur_ref[0, :, 1024:1280]

    for r in range(1, 8):
        sh_ref[r - 1] = hc_ref[r:r + ts + 24, :]
    c2_ref[8:ts + 40, :] = hp_ref[8:ts + 40, :] + hp_ref[9:ts + 41, :]
    c4_ref[8:ts + 32, :] = c2_ref[8:ts + 32, :] + c2_ref[10:ts + 34, :]
    c8_ref[8:ts + 24, :] = c4_ref[8:ts + 24, :] + c4_ref[12:ts + 28, :]

    lane = lax.broadcasted_iota(jnp.int32, (1, 256), 1)
    rc = 64
    for c in range(ts // rc):
        r0 = c * rc
        acc = jnp.zeros((rc, 256), F32)
        for j in range(CONV_WIDTH):
            a, r = divmod(1 + j, 8)
            lo = r0 + 8 * a
            tap = hc_ref[lo:lo + rc, :] if r == 0 else sh_ref[r - 1, lo:lo + rc, :]
            acc = acc + tap * dww_ref[j:j + 1, :]
        hcv = _layernorm(acc + dwb_ref[...], clg_ref[...], clb_ref[...])
        hcv = hcv * jax.nn.sigmoid(hcv)
        conv = jnp.dot(hcv.astype(BF16), cpw_ref[...], preferred_element_type=F32)
        o_ref[0, r0:r0 + rc, 0:256] = _group_norm_store(conv, gn_ref[:, 0:256])

        s0 = r0 + HALO
        hcur = hp_ref[s0:s0 + rc, :]
        s2 = c2_ref[s0 - 1:s0 - 1 + rc, :]
        s4 = c4_ref[s0 - 2:s0 - 2 + rc, :]
        s8 = c8_ref[s0 - 4:s0 - 4 + rc, :]
        s16 = c8_ref[s0 - 8:s0 - 8 + rc, :] + c8_ref[s0:s0 + rc, :]
        wsum = jnp.where(lane < 64, s2, jnp.where(lane < 128, s4, jnp.where(lane < 192, s8, s16)))
        half = jnp.where(lane < 64, 1, jnp.where(lane < 128, 2, jnp.where(lane < 192, 4, 8)))
        t = i * ts + r0 + lax.broadcasted_iota(jnp.int32, (rc, 1), 0)
        cnt = jnp.clip(t + half, 0, seq_len) - jnp.clip(t - half, 0, seq_len)
        y = wsum / cnt.astype(F32) - hcur
        pool = jnp.dot(y.astype(BF16), pw_ref[...], preferred_element_type=F32) * ps_ref[...]
        o_ref[0, r0:r0 + rc, 512:768] = _group_norm_store(pool, gn_ref[:, 512:768])

    for c in range(ts // CHUNK):
        r0 = c * CHUNK
        u = cur_ref[0, r0:r0 + CHUNK, 512:768]
        v = _layernorm(cur_ref[0, r0:r0 + CHUNK, 768:1024], glg_ref[...], glb_ref[...])
        r = jnp.dot(gws_ref[...], v.astype(BF16), preferred_element_type=F32)
        mixed = jnp.where(lane < 64, r[0:CHUNK],
                          jnp.where(lane < 128, r[CHUNK:2 * CHUNK],
                                    jnp.where(lane < 192, r[2 * CHUNK:3 * CHUNK], r[3 * CHUNK:])))
        gm = u * (mixed + gbs_ref[...])
        o_ref[0, r0:r0 + CHUNK, 256:512] = _group_norm_store(gm, gn_ref[:, 256:512])


def _local_mixers(mix, lp):
    b, L, w = mix.shape
    ts = min(512, L)
    nt = L // ts
    hb = ts // HALO
    last_h = L // HALO - 1
    full = lambda shape: pl.BlockSpec(shape, lambda bb, i: (0,) * len(shape))
    params = [lp["dw_w"], lp["dw_b"], lp["cln_g"], lp["cln_b"], lp["cpw"], lp["gln_g"], lp["gln_b"],
              lp["gws"], lp["gbs"], lp["pw"], lp["ps"], lp["gn_local"]]
    return pl.pallas_call(
        functools.partial(_mix_kernel, ts=ts, seq_len=L),
        grid=(b, nt),
        in_specs=[pl.BlockSpec((1, ts, w), lambda bb, i: (bb, i, 0)),
                  pl.BlockSpec((1, HALO, w), lambda bb, i: (bb, jnp.maximum(i * hb - 1, 0), 0)),
                  pl.BlockSpec((1, HALO, w), lambda bb, i: (bb, jnp.minimum((i + 1) * hb, last_h), 0))]
                 + [full(p.shape) for p in params],
        out_specs=pl.BlockSpec((1, ts, LOCAL_W), lambda bb, i: (bb, i, 0)),
        out_shape=jax.ShapeDtypeStruct((b, L, LOCAL_W), BF16),
        scratch_shapes=[pltpu.VMEM((ts + 2 * HALO, 256), F32), pltpu.VMEM((7, ts + 24, 256), F32)]
                       + [pltpu.VMEM((ts + 3 * HALO, 256), F32)] * 4,
        compiler_params=_cparams(("arbitrary", "arbitrary")),
        name="local_mixers",
    )(mix, mix, mix, *params)


def _merge_kernel(*refs, n_alias):
    (att_ref, loc_ref, x_ref, g1_ref, sc_ref, sh_ref, n2g_ref, woa_ref, wol_ref,
     rw_ref, rb_ref, ltri_ref, cnt_in_ref) = refs[:13]
    xo_ref, h2_ref, gate_ref, idx_ref, rank_ref, cnt_ref, run_ref = refs[13 + n_alias:]

    @pl.when(jnp.logical_and(pl.program_id(0) == 0, pl.program_id(1) == 0))
    def _():
        run_ref[...] = cnt_in_ref[...]

    y = (jnp.dot(att_ref[0], woa_ref[...], preferred_element_type=F32)
         + jnp.dot(loc_ref[0], wol_ref[...], preferred_element_type=F32))
    xn = x_ref[0] + g1_ref[0] * y
    xo_ref[0] = xn
    h2 = _modulated_rmsnorm(xn, n2g_ref[...], sc_ref[0], sh_ref[0]).astype(BF16)
    h2_ref[...] = h2
    logits = jnp.dot(h2, rw_ref[...], preferred_element_type=F32) + rb_ref[...]
    lane = lax.broadcasted_iota(jnp.int32, logits.shape, 1)
    vals = jnp.zeros(logits.shape, F32)
    idxs = jnp.zeros(logits.shape, jnp.int32)
    hits = jnp.zeros(logits.shape, F32)
    sels = []
    top = None
    for k in range(TOP_K):
        m = jnp.max(logits, axis=-1, keepdims=True)
        sel = jnp.min(jnp.where(logits == m, lane, ROUTE_W), axis=-1, keepdims=True)
        if top is None:
            top = m
        chosen = lane == sel
        vals = jnp.where(lane == k, jnp.exp(m - top), vals)
        idxs = jnp.where(lane == k, sel, idxs)
        hits = jnp.where(chosen, 1.0, hits)
        logits = jnp.where(chosen, NEG_INF * 2.0, logits)
        sels.append(chosen)
    gate_ref[...] = vals / jnp.sum(vals, axis=-1, keepdims=True)
    idx_ref[...] = idxs
    base = run_ref[...] + jnp.dot(ltri_ref[...], hits.astype(BF16), preferred_element_type=F32)
    ranks = jnp.zeros(logits.shape, F32)
    for k in range(TOP_K):
        rk = jnp.sum(jnp.where(sels[k], base, 0.0), axis=-1, keepdims=True)
        ranks = jnp.where(lane == k, rk, ranks)
    rank_ref[...] = ranks.astype(jnp.int32)
    run_ref[...] = run_ref[...] + jnp.sum(hits, axis=0, keepdims=True)
    cnt_ref[...] = run_ref[...]


def _merge(att, loc, x, xoff, g1, sc2, sh2, moff, n2g, woa, wol, rw, rb, cnt_in, bufs, row_off, t_total):
    b = att.shape[0]
    _, L, d = x.shape
    tm = min(512, L)
    nt = L // tm
    blk_off = row_off // tm
    bm = g1.shape[0]
    mod_map = (lambda bb, i: (moff + bb, 0, 0)) if bm > 1 else (lambda bb, i: (0, 0, 0))
    tile = lambda w: pl.BlockSpec((1, tm, w), lambda bb, i: (bb, i, 0))
    xtile = pl.BlockSpec((1, tm, d), lambda bb, i: (xoff + bb, i, 0))
    flat = lambda w: pl.BlockSpec((tm, w), lambda bb, i: (blk_off + bb * nt + i, 0))
    full = lambda shape: pl.BlockSpec(shape, lambda bb, i: (0,) * len(shape))
    ltri = (np.arange(tm)[:, None] > np.arange(tm)[None, :]).astype(np.float32)
    ltri = jnp.asarray(ltri, BF16)
    n_alias = 0 if bufs is None else len(bufs)
    n_in = 13
    return pl.pallas_call(
        functools.partial(_merge_kernel, n_alias=n_alias),
        grid=(b, nt),
        in_specs=[tile(256), tile(LOCAL_W), xtile,
                  pl.BlockSpec((1, 1, d), mod_map), pl.BlockSpec((1, 1, d), mod_map),
                  pl.BlockSpec((1, 1, d), mod_map),
                  full((1, d)), full(woa.shape), full(wol.shape), full(rw.shape), full(rb.shape),
                  full((tm, tm)), full((1, ROUTE_W))]
                 + [pl.BlockSpec(memory_space=pl.ANY)] * n_alias,
        out_specs=(tile(d), flat(d), flat(ROUTE_W), flat(ROUTE_W), flat(ROUTE_W), full((1, ROUTE_W))),
        out_shape=(jax.ShapeDtypeStruct((b, L, d), F32),
                   jax.ShapeDtypeStruct((t_total, d), BF16),
                   jax.ShapeDtypeStruct((t_total, ROUTE_W), F32),
                   jax.ShapeDtypeStruct((t_total, ROUTE_W), jnp.int32),
                   jax.ShapeDtypeStruct((t_total, ROUTE_W), jnp.int32),
                   jax.ShapeDtypeStruct((1, ROUTE_W), F32)),
        scratch_shapes=[pltpu.VMEM((1, ROUTE_W), F32)],
        input_output_aliases={n_in + j: 1 + j for j in range(n_alias)},
        compiler_params=_cparams(("arbitrary", "arbitrary")),
        name="merge_router",
    )(att, loc, x, g1, sc2, sh2, n2g.reshape(1, d), woa, wol, rw, rb, ltri, cnt_in,
      *(bufs if bufs is not None else ()))


def _moe_kernel(blk_e_ref, n_used_ref, x_ref, w1_ref, b1_ref, w2_ref, b2_ref, o_ref, w1s_ref, w2s_ref):
    i = pl.program_id(0)
    active = i < n_used_ref[0]
    new_expert = jnp.logical_or(i == 0, blk_e_ref[i] != blk_e_ref[jnp.maximum(i - 1, 0)])

    @pl.when(jnp.logical_and(active, new_expert))
    def _():
        rows = 128
        for r in range(D_MODEL // rows):
            w1s_ref[r * rows:(r + 1) * rows, :] = w1_ref[0, 0, r * rows:(r + 1) * rows, :].astype(BF16)
        for r in range(D_FF // rows):
            w2s_ref[r * rows:(r + 1) * rows, :] = w2_ref[0, 0, r * rows:(r + 1) * rows, :].astype(BF16)

    @pl.when(active)
    def _():
        x = x_ref[...]
        acc = jnp.zeros((x.shape[0], D_MODEL), F32)
        for c in range(D_FF // FF_CHUNK):
            lo = c * FF_CHUNK
            glu = jnp.dot(x, w1s_ref[:, lo:lo + FF_CHUNK], preferred_element_type=F32) \
                + b1_ref[0, 0, :, lo:lo + FF_CHUNK]
            lin = jnp.dot(x, w1s_ref[:, D_FF + lo:D_FF + lo + FF_CHUNK], preferred_element_type=F32) \
                + b1_ref[0, 0, :, D_FF + lo:D_FF + lo + FF_CHUNK]
            glu = jnp.minimum(glu, SWIGLU_LIMIT)
            lin = jnp.clip(lin, -SWIGLU_LIMIT, SWIGLU_LIMIT)
            act = glu * jax.nn.sigmoid(SWIGLU_ALPHA * glu) * (lin + 1.0)
            acc = acc + jnp.dot(act.astype(BF16), w2s_ref[lo:lo + FF_CHUNK, :], preferred_element_type=F32)
        o_ref[...] = (acc + b2_ref[0, 0]).astype(o_ref.dtype)

    @pl.when(i >= n_used_ref[0])
    def _():
        o_ref[...] = jnp.zeros(o_ref.shape, o_ref.dtype)


def _moe_ffn(xp, blk_e, n_used, w1, b1, w2, b2, layer):
    n_rows, d = xp.shape
    n_blk = n_rows // MOE_TM
    depth, e = w1.shape[:2]

    def row_map(i, be, nu):
        return (jnp.maximum(jnp.minimum(i, nu[0] - 1), 0), 0)

    def w_map(i, be, nu):
        return (layer, be[i], 0, 0)

    return pl.pallas_call(
        _moe_kernel,
        grid_spec=pltpu.PrefetchScalarGridSpec(
            num_scalar_prefetch=2,
            grid=(n_blk,),
            in_specs=[pl.BlockSpec((MOE_TM, d), row_map),
                      pl.BlockSpec((1, 1, d, 2 * D_FF), w_map),
                      pl.BlockSpec((1, 1, 1, 2 * D_FF), w_map),
                      pl.BlockSpec((1, 1, D_FF, d), w_map),
                      pl.BlockSpec((1, 1, 1, d), w_map)],
            out_specs=pl.BlockSpec((MOE_TM, d), lambda i, be, nu: (i, 0)),
            scratch_shapes=[pltpu.VMEM((d, 2 * D_FF), BF16), pltpu.VMEM((D_FF, d), BF16)]),
        out_shape=jax.ShapeDtypeStruct((n_rows, d), BF16),
        compiler_params=_cparams(("arbitrary",)),
        name="moe_ffn",
    )(blk_e, n_used, xp, w1, b1.reshape(depth, e, 1, 2 * D_FF), w2, b2.reshape(depth, e, 1, d))


def _combine_kernel(*refs, final):
    yg_ref, gate_ref, x_ref, g2_ref, fg_ref = refs[:5]
    o_ref = refs[-1]
    gates = gate_ref[...]
    ff = jnp.zeros(x_ref.shape[1:], F32)
    for k in range(TOP_K):
        ff = ff + gates[:, k:k + 1] * yg_ref[k, 0].astype(F32)
    xo = x_ref[0] + g2_ref[0] * ff
    if final:
        ms = jnp.mean(xo * xo, axis=-1, keepdims=True)
        xo = xo * lax.rsqrt(ms + EPS) * fg_ref[...]
    o_ref[0] = xo


def _combine(yg, gates, x, g2, moff, fg, row_off, *, final, out_buf=None, ooff=0, out_b=None):
    b, L, d = x.shape
    tm = min(256, L)
    nt = L // tm
    blk_off = row_off // tm
    bm = g2.shape[0]
    out_b = b if out_b is None else out_b
    mod_map = (lambda bb, i: (moff + bb, 0, 0)) if bm > 1 else (lambda bb, i: (0, 0, 0))
    extra = () if out_buf is None else (out_buf,)
    return pl.pallas_call(
        functools.partial(_combine_kernel, final=final),
        grid=(b, nt),
        in_specs=[pl.BlockSpec((TOP_K, 1, tm, d), lambda bb, i: (0, bb, i, 0)),
                  pl.BlockSpec((tm, ROUTE_W), lambda bb, i: (blk_off + bb * nt + i, 0)),
                  pl.BlockSpec((1, tm, d), lambda bb, i: (bb, i, 0)),
                  pl.BlockSpec((1, 1, d), mod_map),
                  pl.BlockSpec((1, d), lambda bb, i: (0, 0))]
                 + [pl.BlockSpec(memory_space=pl.ANY)] * len(extra),
        out_specs=pl.BlockSpec((1, tm, d), lambda bb, i: (ooff + bb, i, 0)),
        out_shape=jax.ShapeDtypeStruct((out_b, L, d), F32),
        input_output_aliases={5: 0} if extra else {},
        compiler_params=_cparams(("arbitrary", "arbitrary")),
        name="combine",
    )(yg, gates, x, g2, fg.reshape(1, d), *extra)


def _dispatch_plan(idx, rank, counts):
    n_tok = idx.shape[0]
    n_asg = n_tok * TOP_K
    padded = (counts + MOE_TM - 1) // MOE_TM * MOE_TM
    pad_end = jnp.cumsum(padded)
    pad_start = pad_end - padded
    onehot = idx[..., None] == jnp.arange(N_EXPERTS, dtype=jnp.int32)
    pos = jnp.sum(jnp.where(onehot, pad_start.astype(jnp.int32), 0), axis=-1) + rank
    n_rows = -(-n_asg // MOE_TM) * MOE_TM + N_EXPERTS * MOE_TM
    n_blk = n_rows // MOE_TM
    tok = jnp.broadcast_to(jnp.arange(n_tok, dtype=jnp.int32)[:, None], (n_tok, TOP_K))
    fill = jnp.arange(n_rows, dtype=jnp.int32) % n_tok
    row_tok = fill.at[pos.reshape(-1)].add((tok - pos % n_tok).reshape(-1), mode="promise_in_bounds")
    blk_start = jnp.arange(n_blk, dtype=jnp.int32) * MOE_TM
    blk_e = jnp.minimum(jnp.sum((pad_end[None, :] <= blk_start[:, None]).astype(jnp.int32), axis=1),
                        N_EXPERTS - 1)
    n_used = (pad_end[-1] // MOE_TM).astype(jnp.int32).reshape(1)
    return row_tok, pos, blk_e, n_used


_Q_PERM = np.concatenate([np.arange(0, 64), np.arange(128, 192), np.arange(64, 128), np.arange(192, 256)])


def _rope_tables(L):
    t = jnp.arange(L)
    row = (t // GRID_W).astype(F32)
    col = (t % GRID_W).astype(F32)
    half = HEAD_DIM // 2
    inv = ROPE_BASE ** (-jnp.arange(0, half, 2, dtype=F32) / half)
    ar, ac = row[:, None] * inv[None, :], col[:, None] * inv[None, :]
    z = jnp.zeros_like(ar)
    cos = jnp.concatenate([jnp.cos(ar), jnp.cos(ar), jnp.cos(ac), jnp.cos(ac)], axis=1)
    s1 = jnp.concatenate([-jnp.sin(ar), z, -jnp.sin(ac), z], axis=1)
    s2 = jnp.concatenate([z, jnp.sin(ar), z, jnp.sin(ac)], axis=1)
    rep = lambda a: jnp.concatenate([a, a], axis=1)
    return rep(cos), rep(s1), rep(s2)


def _block_diag(pw):
    g, n, _ = pw.shape
    out = jnp.zeros((g * n, g * n), pw.dtype)
    for gi in range(g):
        out = out.at[gi * n:(gi + 1) * n, gi * n:(gi + 1) * n].set(pw[gi])
    return out


def kernel(x, c, ctx, c_ctx, ada_w, ada_b, norm1_g, norm2_g, w_in, attn_sink, conv_dw_w, conv_dw_b, conv_ln_g, conv_ln_b, conv_pw_w, gmlp_ln_g, gmlp_ln_b, gmlp_ws, gmlp_bs, pool_w, pool_scale, group_norm_g, w_out, router_w, router_b, exp_w1, exp_b1, exp_w2, exp_b2, final_norm_g):
    b, L, d = x.shape
    lc = ctx.shape[1]
    depth = ada_w.shape[0]
    tabs = _rope_tables(L)

    r = -(-(b + 1) // 8) * 8
    cond = jnp.zeros((r, d), F32).at[:b].set(c).at[b].set(c_ctx)
    mods = _adaln(cond, ada_w, ada_b).reshape(depth, r, 6, d)

    n_groups = BATCH_GROUPS if b % BATCH_GROUPS == 0 else 1
    bh = b // n_groups
    xs, cs = [x] * n_groups, [ctx] * n_groups

    for l in range(depth):
        last = l == depth - 1
        mx = [mods[l, :b, i].reshape(b, 1, d) for i in range(6)]
        mc = [mods[l, b:b + 1, i].reshape(1, 1, d) for i in range(6)]
        sh1, sc1, g1, sh2, sc2, g2 = mx
        sh1c, sc1c, g1c, sh2c, sc2c, g2c = mc

        wl = w_in[l]
        wq = wl[:, :OFF_K][:, _Q_PERM] * (HEAD_DIM ** -0.5)
        w_full = jnp.concatenate([wq, wl[:, OFF_K:]], axis=1).astype(BF16)
        gn = group_norm_g[l]
        g_att = gn[:256][_Q_PERM]
        wo = w_out[l]
        woa = wo[:256][_Q_PERM].astype(BF16)
        wol = wo[256:].astype(BF16)
        lp = dict(
            dw_w=conv_dw_w[l], dw_b=conv_dw_b[l].reshape(1, -1),
            cln_g=conv_ln_g[l].reshape(1, -1), cln_b=conv_ln_b[l].reshape(1, -1),
            cpw=conv_pw_w[l].astype(BF16),
            gln_g=gmlp_ln_g[l].reshape(1, -1), gln_b=gmlp_ln_b[l].reshape(1, -1),
            gws=gmlp_ws[l].reshape(GMLP_HEADS * CHUNK, CHUNK).astype(BF16),
            gbs=jnp.repeat(gmlp_bs[l].T, GROUP_W // GMLP_HEADS, axis=1),
            pw=_block_diag(pool_w[l]).astype(BF16), ps=pool_scale[l].reshape(1, -1),
            gn_local=gn[256:].reshape(1, -1))
        rw = jnp.zeros((d, ROUTE_W), F32).at[:, :N_EXPERTS].set(router_w[l]).astype(BF16)
        rb = jnp.full((1, ROUTE_W), NEG_INF, F32).at[0, :N_EXPERTS].set(router_b[l])

        w_kv = wl[:, OFF_K:OFF_CONV].astype(BF16)
        take = lambda a, i: a.at[i].get(mode="promise_in_bounds")
        out = None
        staged = []
        for h in range(n_groups):
            moff = h * bh
            xh, ch = xs[h], cs[h]
            xoff = moff if l == 0 else 0

            q, kv, mix = _inproj(xh, xoff, bh, sc1, sh1, moff, norm1_g[l], w_full, tabs, full=True)
            if last:
                (kvc,) = _inproj(ch, xoff, bh, sc1c, sh1c, 0, norm1_g[l], w_kv, None, full=False)
            else:
                qc, kvc, mixc = _inproj(ch, xoff, bh, sc1c, sh1c, 0, norm1_g[l], w_full, None, full=True)
            att = _attention(q, kv, kvc, attn_sink[l], g_att, local=True)
            loc = _local_mixers(mix, lp)
            if staged:
                prev = staged[-1]
                prev["xp"], loc = lax.optimization_barrier((prev["xp"], loc))
            t_total = bh * L if last else bh * (L + lc)
            cnt0 = jnp.zeros((1, ROUTE_W), F32)
            xh, h2, gates, idx, rank, cnt = _merge(att, loc, xh, xoff, g1, sc2, sh2, moff, norm2_g[l], woa, wol,
                                                   rw, rb, cnt0, None, 0, t_total)
            if not last:
                attc = _attention(qc, kvc, kvc, attn_sink[l], g_att, local=False)
                locc = _local_mixers(mixc, lp)
                ch, h2, gates, idx, rank, cnt = _merge(attc, locc, ch, xoff, g1c, sc2c, sh2c, 0, norm2_g[l],
                                                       woa, wol, rw, rb, cnt, (h2, gates, idx, rank),
                                                       bh * L, t_total)
            counts = cnt[0, :N_EXPERTS].astype(jnp.int32)
            row_tok, pos, blk_e, n_used = _dispatch_plan(idx[:, :TOP_K], rank[:, :TOP_K], counts)
            staged.append(dict(xh=xh, ch=ch, gates=gates, pos=pos, blk_e=blk_e, n_used=n_used,
                               xp=take(h2, row_tok)))

        for h, st in enumerate(staged):
            moff = h * bh
            xh, ch, gates, pos = st["xh"], st["ch"], st["gates"], st["pos"]
            yp = _moe_ffn(st["xp"], st["blk_e"], st["n_used"], exp_w1, exp_b1, exp_w2, exp_b2, l)
            ygx = take(yp, pos[:bh * L].T).reshape(TOP_K, bh, L, d)
            if last:
                out = _combine(ygx, gates, xh, g2, moff, final_norm_g, 0, final=True,
                               out_buf=out, ooff=moff, out_b=b)
            else:
                xs[h] = _combine(ygx, gates, xh, g2, moff, final_norm_g, 0, final=False)
                ygc = take(yp, pos[bh * L:].T).reshape(TOP_K, bh, lc, d)
                cs[h] = _combine(ygc, gates, ch, g2c, 0, final_norm_g, bh * L, final=False)
    return out
```

```python
import functools

import jax
import jax.numpy as jnp
import numpy as np
from jax import lax
from jax.experimental import pallas as pl
from jax.experimental.pallas import tpu as pltpu

F32 = jnp.float32
BF16 = jnp.bfloat16

D_MODEL = 1024
GRID_W = 64
GROUP_W = 256
HEAD_DIM = 64
N_Q_HEADS = 4
N_KV_HEADS = 2
WINDOW = 128
ATT_BLOCK = 128
ROPE_BASE = 10000.0
CONV_WIDTH = 31
CHUNK = 128
GMLP_HEADS = 4
POOL_WINDOWS = (2, 4, 8, 16)
N_EXPERTS = 32
TOP_K = 4
D_FF = 1024
SWIGLU_ALPHA = 1.702
SWIGLU_LIMIT = 7.0
EPS = 1e-6
NEG_INF = -1e30

OFF_K = 256
OFF_V = 384
OFF_CONV = 512
IN_W = 1792
MIX_IN_W = IN_W - OFF_CONV
LOCAL_W = 3 * GROUP_W

LANES = 128
HALO = 16
ROUTE_W = LANES
MOE_TM = 512
FF_CHUNK = 512
BATCH_GROUPS = 2
VMEM_LIMIT = 56 * 1024 * 1024


def _cparams(sem):
    return pltpu.CompilerParams(dimension_semantics=sem, vmem_limit_bytes=VMEM_LIMIT)


def _ada_kernel(c_ref, w_ref, b_ref, o_ref):
    c = c_ref[...]
    s = c * jax.nn.sigmoid(c)
    o_ref[0] = jnp.dot(s.astype(BF16), w_ref[0].astype(BF16), preferred_element_type=F32) + b_ref[0]


def _adaln(cond, ada_w, ada_b):
    depth, d, n = ada_w.shape
    r = cond.shape[0]
    tn = 1536
    return pl.pallas_call(
        _ada_kernel,
        grid=(depth, n // tn),
        in_specs=[pl.BlockSpec((r, d), lambda l, j: (0, 0)),
                  pl.BlockSpec((1, d, tn), lambda l, j: (l, 0, j)),
                  pl.BlockSpec((1, 1, tn), lambda l, j: (l, 0, j))],
        out_specs=pl.BlockSpec((1, r, tn), lambda l, j: (l, 0, j)),
        out_shape=jax.ShapeDtypeStruct((depth, r, n), F32),
        compiler_params=_cparams(("arbitrary", "arbitrary")),
        name="adaln",
    )(cond, ada_w, ada_b.reshape(depth, 1, n))


def _modulated_rmsnorm(x, g, sc, sh):
    ms = jnp.mean(x * x, axis=-1, keepdims=True)
    return x * lax.rsqrt(ms + EPS) * (g * (1.0 + sc)) + sh


def _inproj_kernel(*refs, rope, full):
    if rope:
        x_ref, sc_ref, sh_ref, g_ref, w_ref, cos_ref, s1_ref, s2_ref = refs[:8]
        outs = refs[8:]
    else:
        x_ref, sc_ref, sh_ref, g_ref, w_ref = refs[:5]
        outs = refs[5:]
    h = _modulated_rmsnorm(x_ref[0], g_ref[...], sc_ref[0], sh_ref[0])
    p = jnp.dot(h.astype(BF16), w_ref[...], preferred_element_type=F32)

    def roped(xs):
        if not rope:
            return xs
        return (xs * cos_ref[...] + pltpu.roll(xs, LANES - 16, 1) * s1_ref[...]
                + pltpu.roll(xs, 16, 1) * s2_ref[...])

    if full:
        q_ref, kv_ref, mix_ref = outs
        q_ref[0] = jnp.concatenate([roped(p[:, 0:128]), roped(p[:, 128:256])], axis=1).astype(BF16)
        kv_ref[0] = jnp.concatenate([roped(p[:, 256:384]), p[:, 384:512]], axis=1).astype(BF16)
        mix_ref[0] = p[:, OFF_CONV:]
    else:
        (kv_ref,) = outs
        kv_ref[0] = p.astype(BF16)


def _inproj(x, xoff, b, sc, sh, moff, g, w, tabs, *, full):
    _, L, d = x.shape
    n = w.shape[1]
    tm = min(512, L)
    rope = tabs is not None
    bm = sc.shape[0]
    mod_map = (lambda i, bb: (moff + bb, 0, 0)) if bm > 1 else (lambda i, bb: (0, 0, 0))
    in_specs = [pl.BlockSpec((1, tm, d), lambda i, bb: (xoff + bb, i, 0)),
                pl.BlockSpec((1, 1, d), mod_map),
                pl.BlockSpec((1, 1, d), mod_map),
                pl.BlockSpec((1, d), lambda i, bb: (0, 0)),
                pl.BlockSpec((d, n), lambda i, bb: (0, 0))]
    args = [x, sc, sh, g.reshape(1, d), w]
    if rope:
        in_specs += [pl.BlockSpec((tm, LANES), lambda i, bb: (i, 0))] * 3
        args += list(tabs)
    if full:
        out_shape = (jax.ShapeDtypeStruct((b, L, 256), BF16),
                     jax.ShapeDtypeStruct((b, L, 256), BF16),
                     jax.ShapeDtypeStruct((b, L, MIX_IN_W), F32))
        out_specs = (pl.BlockSpec((1, tm, 256), lambda i, bb: (bb, i, 0)),
                     pl.BlockSpec((1, tm, 256), lambda i, bb: (bb, i, 0)),
                     pl.BlockSpec((1, tm, MIX_IN_W), lambda i, bb: (bb, i, 0)))
    else:
        out_shape = (jax.ShapeDtypeStruct((b, L, 256), BF16),)
        out_specs = (pl.BlockSpec((1, tm, 256), lambda i, bb: (bb, i, 0)),)
    return pl.pallas_call(
        functools.partial(_inproj_kernel, rope=rope, full=full),
        grid=(L // tm, b),
        in_specs=in_specs, out_specs=out_specs, out_shape=out_shape,
        compiler_params=_cparams(("arbitrary", "arbitrary")),
        name="inproj_x" if rope else "inproj_ctx",
    )(*args)


def _attn_kernel(*refs, local, seq_len, q_blocks):
    if local:
        sink_ref, q_ref, kv_ref, kvc_ref, g_ref, bias_ref, o_ref = refs
    else:
        sink_ref, q_ref, kv_ref, kvc_ref, g_ref, o_ref = refs
    step = pl.program_id(1)
    lane = lax.broadcasted_iota(jnp.int32, (1, LANES), 1)
    row = lax.broadcasted_iota(jnp.int32, (2 * ATT_BLOCK, 1), 0)
    kvc = kvc_ref[0]
    kc, vc = kvc[:, :LANES], kvc[:, LANES:]
    lc = kvc.shape[0]
    nt = (((1,), (1,)), ((), ()))
    zero = jnp.zeros((), BF16)
    lm0 = lane < HEAD_DIM
    for sb in range(q_blocks):
        n = step * q_blocks + sb
        q = q_ref[0, sb * ATT_BLOCK:(sb + 1) * ATT_BLOCK, :]
        if local:
            blk0 = jnp.clip(n - 1, 0, seq_len // ATT_BLOCK - 3)
            kw = kv_ref[0, pl.ds(pl.multiple_of(blk0 * ATT_BLOCK, ATT_BLOCK), 3 * ATT_BLOCK), :]
            kl, vl = kw[:, :LANES], kw[:, LANES:]
            bias = bias_ref[n - blk0]
        outs = []
        for kh in range(N_KV_HEADS):
            lm = lm0 if kh == 0 else jnp.logical_not(lm0)
            lhs = jnp.concatenate([jnp.where(lm, q[:, :LANES], zero), jnp.where(lm, q[:, LANES:], zero)],
                                  axis=0)
            sk = jnp.where(row < ATT_BLOCK, sink_ref[2 * kh], sink_ref[2 * kh + 1])
            s = lax.dot_general(lhs, kc, nt, preferred_element_type=F32)
            if local:
                s_l = lax.dot_general(lhs, kl, nt, preferred_element_type=F32) + bias
                s = jnp.concatenate([s, s_l], axis=1)
            m = jnp.maximum(sk, jnp.max(s, axis=-1, keepdims=True))
            e = jnp.exp(s - m)
            den = jnp.exp(sk - m) + jnp.sum(e, axis=-1, keepdims=True)
            e = e.astype(BF16)
            o = jnp.dot(e[:, :lc], vc, preferred_element_type=F32)
            if local:
                o = o + jnp.dot(e[:, lc:], vl, preferred_element_type=F32)
            outs.append(o / den)
        o01, o23 = outs
        att = jnp.concatenate([jnp.where(lm0, o01[:ATT_BLOCK], o23[:ATT_BLOCK]),
                               jnp.where(lm0, o01[ATT_BLOCK:], o23[ATT_BLOCK:])], axis=1)
        ms = jnp.mean(att * att, axis=-1, keepdims=True)
        o_ref[0, sb * ATT_BLOCK:(sb + 1) * ATT_BLOCK, :] = (att * lax.rsqrt(ms + EPS) * g_ref[...]).astype(BF16)


def _band_bias():
    i = np.arange(2 * ATT_BLOCK)[None, :, None] % ATT_BLOCK
    j = np.arange(3 * ATT_BLOCK)[None, None, :]
    v = np.arange(3)[:, None, None]
    return jnp.asarray(np.where(np.abs(j - i - ATT_BLOCK * v) <= WINDOW, 0.0, NEG_INF), F32)


def _attention(q, kv, kvc, sink, g_att, *, local):
    b, L, _ = q.shape
    lc = kvc.shape[1]
    nb = L // ATT_BLOCK
    assert not local or nb >= 3
    q_blocks = 2 if nb % 2 == 0 else 1
    tq = q_blocks * ATT_BLOCK
    in_specs = [pl.BlockSpec(memory_space=pltpu.SMEM),
                pl.BlockSpec((1, tq, 256), lambda bb, i: (bb, i, 0)),
                pl.BlockSpec((1, kv.shape[1], 256), lambda bb, i: (bb, 0, 0)),
                pl.BlockSpec((1, lc, 256), lambda bb, i: (bb, 0, 0)),
                pl.BlockSpec((1, 256), lambda bb, i: (0, 0))]
    args = [sink, q, kv, kvc, g_att.reshape(1, 256)]
    if local:
        in_specs.append(pl.BlockSpec((3, 2 * ATT_BLOCK, 3 * ATT_BLOCK), lambda bb, i: (0, 0, 0)))
        args.append(_band_bias())
    return pl.pallas_call(
        functools.partial(_attn_kernel, local=local, seq_len=L, q_blocks=q_blocks),
        grid=(b, nb // q_blocks),
        in_specs=in_specs,
        out_specs=pl.BlockSpec((1, tq, 256), lambda bb, i: (bb, i, 0)),
        out_shape=jax.ShapeDtypeStruct((b, L, 256), BF16),
        compiler_params=_cparams(("arbitrary", "arbitrary")),
        name="attn_window" if local else "attn_ctx",
    )(*args)


def _layernorm(x, g, b):
    mu = jnp.mean(x, axis=-1, keepdims=True)
    xc = x - mu
    return xc * lax.rsqrt(jnp.mean(xc * xc, axis=-1, keepdims=True) + EPS) * g + b


def _group_norm_store(y, g):
    ms = jnp.mean(y * y, axis=-1, keepdims=True)
    return (y * lax.rsqrt(ms + EPS) * g).astype(BF16)


def _mix_kernel(cur_ref, prev_ref, next_ref, dww_ref, dwb_ref, clg_ref, clb_ref, cpw_ref,
                glg_ref, glb_ref, gws_ref, gbs_ref, pw_ref, ps_ref, gn_ref, o_ref,
                hc_ref, sh_ref, hp_ref, c2_ref, c4_ref, c8_ref, *, ts, seq_len):
    i = pl.program_id(1)
    nt = pl.num_programs(1)
    has_prev = (i > 0).astype(F32)
    has_next = (i < nt - 1).astype(F32)

    def glu(blk):
        return blk[:, 0:256] * jax.nn.sigmoid(blk[:, 256:512])

    pv = prev_ref[0]
    nx = next_ref[0]
    hc_ref[0:HALO, :] = glu(pv) * has_prev
    hc_ref[HALO + ts:2 * HALO + ts, :] = glu(nx) * has_next
    hp_ref[0:HALO, :] = pv[:, 1024:1280] * has_prev
    hp_ref[HALO + ts:2 * HALO + ts, :] = nx[:, 1024:1280] * has_next
    hp_ref[2 * HALO + ts:3 * HALO + ts, :] = jnp.zeros((HALO, 256), F32)
    hc_ref[HALO:HALO + ts, :] = glu(cur_ref[0, :, 0:512])
    hp_ref[HALO:HALO + ts, :] = cur_ref[0, :, 1024:1280]

    for r in range(1, 8):
        sh_ref[r - 1] = hc_ref[r:r + ts + 24, :]
    c2_ref[8:ts + 40, :] = hp_ref[8:ts + 40, :] + hp_ref[9:ts + 41, :]
    c4_ref[8:ts + 32, :] = c2_ref[8:ts + 32, :] + c2_ref[10:ts + 34, :]
    c8_ref[8:ts + 24, :] = c4_ref[8:ts + 24, :] + c4_ref[12:ts + 28, :]

    lane = lax.broadcasted_iota(jnp.int32, (1, 256), 1)
    rc = 64
    for c in range(ts // rc):
        r0 = c * rc
        acc = jnp.zeros((rc, 256), F32)
        for j in range(CONV_WIDTH):
            a, r = divmod(1 + j, 8)
            lo = r0 + 8 * a
            tap = hc_ref[lo:lo + rc, :] if r == 0 else sh_ref[r - 1, lo:lo + rc, :]
            acc = acc + tap * dww_ref[j:j + 1, :]
        hcv = _layernorm(acc + dwb_ref[...], clg_ref[...], clb_ref[...])
        hcv = hcv * jax.nn.sigmoid(hcv)
        conv = jnp.dot(hcv.astype(BF16), cpw_ref[...], preferred_element_type=F32)
        o_ref[0, r0:r0 + rc, 0:256] = _group_norm_store(conv, gn_ref[:, 0:256])

        s0 = r0 + HALO
        hcur = hp_ref[s0:s0 + rc, :]
        s2 = c2_ref[s0 - 1:s0 - 1 + rc, :]
        s4 = c4_ref[s0 - 2:s0 - 2 + rc, :]
        s8 = c8_ref[s0 - 4:s0 - 4 + rc, :]
        s16 = c8_ref[s0 - 8:s0 - 8 + rc, :] + c8_ref[s0:s0 + rc, :]
        wsum = jnp.where(lane < 64, s2, jnp.where(lane < 128, s4, jnp.where(lane < 192, s8, s16)))
        half = jnp.where(lane < 64, 1, jnp.where(lane < 128, 2, jnp.where(lane < 192, 4, 8)))
        t = i * ts + r0 + lax.broadcasted_iota(jnp.int32, (rc, 1), 0)
        cnt = jnp.clip(t + half, 0, seq_len) - jnp.clip(t - half, 0, seq_len)
        y = wsum / cnt.astype(F32) - hcur
        pool = jnp.dot(y.astype(BF16), pw_ref[...], preferred_element_type=F32) * ps_ref[...]
        o_ref[0, r0:r0 + rc, 512:768] = _group_norm_store(pool, gn_ref[:, 512:768])

    for c in range(ts // CHUNK):
        r0 = c * CHUNK
        u = cur_ref[0, r0:r0 + CHUNK, 512:768]
        v = _layernorm(cur_ref[0, r0:r0 + CHUNK, 768:1024], glg_ref[...], glb_ref[...])
        r = jnp.dot(gws_ref[...], v.astype(BF16), preferred_element_type=F32)
        mixed = jnp.where(lane < 64, r[0:CHUNK],
                          jnp.where(lane < 128, r[CHUNK:2 * CHUNK],
                                    jnp.where(lane < 192, r[2 * CHUNK:3 * CHUNK], r[3 * CHUNK:])))
        gm = u * (mixed + gbs_ref[...])
        o_ref[0, r0:r0 + CHUNK, 256:512] = _group_norm_store(gm, gn_ref[:, 256:512])


def _local_mixers(mix, lp):
    b, L, w = mix.shape
    ts = min(512, L)
    nt = L // ts
    hb = ts // HALO
    last_h = L // HALO - 1
    full = lambda shape: pl.BlockSpec(shape, lambda bb, i: (0,) * len(shape))
    params = [lp["dw_w"], lp["dw_b"], lp["cln_g"], lp["cln_b"], lp["cpw"], lp["gln_g"], lp["gln_b"],
              lp["gws"], lp["gbs"], lp["pw"], lp["ps"], lp["gn_local"]]
    return pl.pallas_call(
        functools.partial(_mix_kernel, ts=ts, seq_len=L),
        grid=(b, nt),
        in_specs=[pl.BlockSpec((1, ts, w), lambda bb, i: (bb, i, 0)),
                  pl.BlockSpec((1, HALO, w), lambda bb, i: (bb, jnp.maximum(i * hb - 1, 0), 0)),
                  pl.BlockSpec((1, HALO, w), lambda bb, i: (bb, jnp.minimum((i + 1) * hb, last_h), 0))]
                 + [full(p.shape) for p in params],
        out_specs=pl.BlockSpec((1, ts, LOCAL_W), lambda bb, i: (bb, i, 0)),
        out_shape=jax.ShapeDtypeStruct((b, L, LOCAL_W), BF16),
        scratch_shapes=[pltpu.VMEM((ts + 2 * HALO, 256), F32), pltpu.VMEM((7, ts + 24, 256), F32)]
                       + [pltpu.VMEM((ts + 3 * HALO, 256), F32)] * 4,
        compiler_params=_cparams(("arbitrary", "arbitrary")),
        name="local_mixers",
    )(mix, mix, mix, *params)


def _merge_kernel(*refs, n_alias):
    (att_ref, loc_ref, x_ref, g1_ref, sc_ref, sh_ref, n2g_ref, woa_ref, wol_ref,
     rw_ref, rb_ref, ltri_ref, cnt_in_ref) = refs[:13]
    xo_ref, h2_ref, gate_ref, idx_ref, rank_ref, cnt_ref, run_ref = refs[13 + n_alias:]

    @pl.when(jnp.logical_and(pl.program_id(0) == 0, pl.program_id(1) == 0))
    def _():
        run_ref[...] = cnt_in_ref[...]

    y = (jnp.dot(att_ref[0], woa_ref[...], preferred_element_type=F32)
         + jnp.dot(loc_ref[0], wol_ref[...], preferred_element_type=F32))
    xn = x_ref[0] + g1_ref[0] * y
    xo_ref[0] = xn
    h2 = _modulated_rmsnorm(xn, n2g_ref[...], sc_ref[0], sh_ref[0]).astype(BF16)
    h2_ref[...] = h2
    logits = jnp.dot(h2, rw_ref[...], preferred_element_type=F32) + rb_ref[...]
    lane = lax.broadcasted_iota(jnp.int32, logits.shape, 1)
    vals = jnp.zeros(logits.shape, F32)
    idxs = jnp.zeros(logits.shape, jnp.int32)
    hits = jnp.zeros(logits.shape, F32)
    sels = []
    top = None
    for k in range(TOP_K):
        m = jnp.max(logits, axis=-1, keepdims=True)
        sel = jnp.min(jnp.where(logits == m, lane, ROUTE_W), axis=-1, keepdims=True)
        if top is None:
            top = m
        chosen = lane == sel
        vals = jnp.where(lane == k, jnp.exp(m - top), vals)
        idxs = jnp.where(lane == k, sel, idxs)
        hits = jnp.where(chosen, 1.0, hits)
        logits = jnp.where(chosen, NEG_INF * 2.0, logits)
        sels.append(chosen)
    gate_ref[...] = vals / jnp.sum(vals, axis=-1, keepdims=True)
    idx_ref[...] = idxs
    base = run_ref[...] + jnp.dot(ltri_ref[...], hits.astype(BF16), preferred_element_type=F32)
    ranks = jnp.zeros(logits.shape, F32)
    for k in range(TOP_K):
        rk = jnp.sum(jnp.where(sels[k], base, 0.0), axis=-1, keepdims=True)
        ranks = jnp.where(lane == k, rk, ranks)
    rank_ref[...] = ranks.astype(jnp.int32)
    run_ref[...] = run_ref[...] + jnp.sum(hits, axis=0, keepdims=True)
    cnt_ref[...] = run_ref[...]


def _merge(att, loc, x, xoff, g1, sc2, sh2, moff, n2g, woa, wol, rw, rb, cnt_in, bufs, row_off, t_total):
    b = att.shape[0]
    _, L, d = x.shape
    tm = min(512, L)
    nt = L // tm
    blk_off = row_off // tm
    bm = g1.shape[0]
    mod_map = (lambda bb, i: (moff + bb, 0, 0)) if bm > 1 else (lambda bb, i: (0, 0, 0))
    tile = lambda w: pl.BlockSpec((1, tm, w), lambda bb, i: (bb, i, 0))
    xtile = pl.BlockSpec((1, tm, d), lambda bb, i: (xoff + bb, i, 0))
    flat = lambda w: pl.BlockSpec((tm, w), lambda bb, i: (blk_off + bb * nt + i, 0))
    full = lambda shape: pl.BlockSpec(shape, lambda bb, i: (0,) * len(shape))
    ltri = (np.arange(tm)[:, None] > np.arange(tm)[None, :]).astype(np.float32)
    ltri = jnp.asarray(ltri, BF16)
    n_alias = 0 if bufs is None else len(bufs)
    n_in = 13
    return pl.pallas_call(
        functools.partial(_merge_kernel, n_alias=n_alias),
        grid=(b, nt),
        in_specs=[tile(256), tile(LOCAL_W), xtile,
                  pl.BlockSpec((1, 1, d), mod_map), pl.BlockSpec((1, 1, d), mod_map),
                  pl.BlockSpec((1, 1, d), mod_map),
                  full((1, d)), full(woa.shape), full(wol.shape), full(rw.shape), full(rb.shape),
                  full((tm, tm)), full((1, ROUTE_W))]
                 + [pl.BlockSpec(memory_space=pl.ANY)] * n_alias,
        out_specs=(tile(d), flat(d), flat(ROUTE_W), flat(ROUTE_W), flat(ROUTE_W), full((1, ROUTE_W))),
        out_shape=(jax.ShapeDtypeStruct((b, L, d), F32),
                   jax.ShapeDtypeStruct((t_total, d), BF16),
                   jax.ShapeDtypeStruct((t_total, ROUTE_W), F32),
                   jax.ShapeDtypeStruct((t_total, ROUTE_W), jnp.int32),
                   jax.ShapeDtypeStruct((t_total, ROUTE_W), jnp.int32),
                   jax.ShapeDtypeStruct((1, ROUTE_W), F32)),
        scratch_shapes=[pltpu.VMEM((1, ROUTE_W), F32)],
        input_output_aliases={n_in + j: 1 + j for j in range(n_alias)},
        compiler_params=_cparams(("arbitrary", "arbitrary")),
        name="merge_router",
    )(att, loc, x, g1, sc2, sh2, n2g.reshape(1, d), woa, wol, rw, rb, ltri, cnt_in,
      *(bufs if bufs is not None else ()))


def _moe_kernel(blk_e_ref, n_used_ref, x_ref, w1_ref, b1_ref, w2_ref, b2_ref, o_ref, w1s_ref, w2s_ref):
    i = pl.program_id(0)
    active = i < n_used_ref[0]
    new_expert = jnp.logical_or(i == 0, blk_e_ref[i] != blk_e_ref[jnp.maximum(i - 1, 0)])

    @pl.when(jnp.logical_and(active, new_expert))
    def _():
        rows = 128
        for r in range(D_MODEL // rows):
            w1s_ref[r * rows:(r + 1) * rows, :] = w1_ref[0, 0, r * rows:(r + 1) * rows, :].astype(BF16)
        for r in range(D_FF // rows):
            w2s_ref[r * rows:(r + 1) * rows, :] = w2_ref[0, 0, r * rows:(r + 1) * rows, :].astype(BF16)

    @pl.when(active)
    def _():
        x = x_ref[...]
        acc = jnp.zeros((x.shape[0], D_MODEL), F32)
        for c in range(D_FF // FF_CHUNK):
            lo = c * FF_CHUNK
            glu = jnp.dot(x, w1s_ref[:, lo:lo + FF_CHUNK], preferred_element_type=F32) \
                + b1_ref[0, 0, :, lo:lo + FF_CHUNK]
            lin = jnp.dot(x, w1s_ref[:, D_FF + lo:D_FF + lo + FF_CHUNK], preferred_element_type=F32) \
                + b1_ref[0, 0, :, D_FF + lo:D_FF + lo + FF_CHUNK]
            glu = jnp.minimum(glu, SWIGLU_LIMIT)
            lin = jnp.clip(lin, -SWIGLU_LIMIT, SWIGLU_LIMIT)
            act = glu * jax.nn.sigmoid(SWIGLU_ALPHA * glu) * (lin + 1.0)
            acc = acc + jnp.dot(act.astype(BF16), w2s_ref[lo:lo + FF_CHUNK, :], preferred_element_type=F32)
        o_ref[...] = (acc + b2_ref[0, 0]).astype(o_ref.dtype)

    @pl.when(i >= n_used_ref[0])
    def _():
        o_ref[...] = jnp.zeros(o_ref.shape, o_ref.dtype)


def _moe_ffn(xp, blk_e, n_used, w1, b1, w2, b2, layer):
    n_rows, d = xp.shape
    n_blk = n_rows // MOE_TM
    depth, e = w1.shape[:2]

    def row_map(i, be, nu):
        return (jnp.maximum(jnp.minimum(i, nu[0] - 1), 0), 0)

    def w_map(i, be, nu):
        return (layer, be[i], 0, 0)

    return pl.pallas_call(
        _moe_kernel,
        grid_spec=pltpu.PrefetchScalarGridSpec(
            num_scalar_prefetch=2,
            grid=(n_blk,),
            in_specs=[pl.BlockSpec((MOE_TM, d), row_map),
                      pl.BlockSpec((1, 1, d, 2 * D_FF), w_map),
                      pl.BlockSpec((1, 1, 1, 2 * D_FF), w_map),
                      pl.BlockSpec((1, 1, D_FF, d), w_map),
                      pl.BlockSpec((1, 1, 1, d), w_map)],
            out_specs=pl.BlockSpec((MOE_TM, d), lambda i, be, nu: (i, 0)),
            scratch_shapes=[pltpu.VMEM((d, 2 * D_FF), BF16), pltpu.VMEM((D_FF, d), BF16)]),
        out_shape=jax.ShapeDtypeStruct((n_rows, d), BF16),
        compiler_params=_cparams(("arbitrary",)),
        name="moe_ffn",
    )(blk_e, n_used, xp, w1, b1.reshape(depth, e, 1, 2 * D_FF), w2, b2.reshape(depth, e, 1, d))


def _combine_kernel(*refs, final):
    yg_ref, gate_ref, x_ref, g2_ref, fg_ref = refs[:5]
    o_ref = refs[-1]
    gates = gate_ref[...]
    ff = jnp.zeros(x_ref.shape[1:], F32)
    for k in range(TOP_K):
        ff = ff + gates[:, k:k + 1] * yg_ref[k, 0].astype(F32)
    xo = x_ref[0] + g2_ref[0] * ff
    if final:
        ms = jnp.mean(xo * xo, axis=-1, keepdims=True)
        xo = xo * lax.rsqrt(ms + EPS) * fg_ref[...]
    o_ref[0] = xo


def _combine(yg, gates, x, g2, moff, fg, row_off, *, final, out_buf=None, ooff=0, out_b=None):
    b, L, d = x.shape
    tm = min(256, L)
    nt = L // tm
    blk_off = row_off // tm
    bm = g2.shape[0]
    out_b = b if out_b is None else out_b
    mod_map = (lambda bb, i: (moff + bb, 0, 0)) if bm > 1 else (lambda bb, i: (0, 0, 0))
    extra = () if out_buf is None else (out_buf,)
    return pl.pallas_call(
        functools.partial(_combine_kernel, final=final),
        grid=(b, nt),
        in_specs=[pl.BlockSpec((TOP_K, 1, tm, d), lambda bb, i: (0, bb, i, 0)),
                  pl.BlockSpec((tm, ROUTE_W), lambda bb, i: (blk_off + bb * nt + i, 0)),
                  pl.BlockSpec((1, tm, d), lambda bb, i: (bb, i, 0)),
                  pl.BlockSpec((1, 1, d), mod_map),
                  pl.BlockSpec((1, d), lambda bb, i: (0, 0))]
                 + [pl.BlockSpec(memory_space=pl.ANY)] * len(extra),
        out_specs=pl.BlockSpec((1, tm, d), lambda bb, i: (ooff + bb, i, 0)),
        out_shape=jax.ShapeDtypeStruct((out_b, L, d), F32),
        input_output_aliases={5: 0} if extra else {},
        compiler_params=_cparams(("arbitrary", "arbitrary")),
        name="combine",
    )(yg, gates, x, g2, fg.reshape(1, d), *extra)


def _dispatch_plan(idx, rank, counts):
    n_tok = idx.shape[0]
    n_asg = n_tok * TOP_K
    padded = (counts + MOE_TM - 1) // MOE_TM * MOE_TM
    pad_end = jnp.cumsum(padded)
    pad_start = pad_end - padded
    onehot = idx[..., None] == jnp.arange(N_EXPERTS, dtype=jnp.int32)
    pos = jnp.sum(jnp.where(onehot, pad_start.astype(jnp.int32), 0), axis=-1) + rank
    n_rows = -(-n_asg // MOE_TM) * MOE_TM + N_EXPERTS * MOE_TM
    n_blk = n_rows // MOE_TM
    tok = jnp.broadcast_to(jnp.arange(n_tok, dtype=jnp.int32)[:, None], (n_tok, TOP_K))
    fill = jnp.arange(n_rows, dtype=jnp.int32) % n_tok
    row_tok = fill.at[pos.reshape(-1)].add((tok - pos % n_tok).reshape(-1), mode="promise_in_bounds")
    blk_start = jnp.arange(n_blk, dtype=jnp.int32) * MOE_TM
    blk_e = jnp.minimum(jnp.sum((pad_end[None, :] <= blk_start[:, None]).astype(jnp.int32), axis=1),
                        N_EXPERTS - 1)
    n_used = (pad_end[-1] // MOE_TM).astype(jnp.int32).reshape(1)
    return row_tok, pos, blk_e, n_used


_Q_PERM = np.concatenate([np.arange(0, 64), np.arange(128, 192), np.arange(64, 128), np.arange(192, 256)])


def _rope_tables(L):
    t = jnp.arange(L)
    row = (t // GRID_W).astype(F32)
    col = (t % GRID_W).astype(F32)
    half = HEAD_DIM // 2
    inv = ROPE_BASE ** (-jnp.arange(0, half, 2, dtype=F32) / half)
    ar, ac = row[:, None] * inv[None, :], col[:, None] * inv[None, :]
    z = jnp.zeros_like(ar)
    cos = jnp.concatenate([jnp.cos(ar), jnp.cos(ar), jnp.cos(ac), jnp.cos(ac)], axis=1)
    s1 = jnp.concatenate([-jnp.sin(ar), z, -jnp.sin(ac), z], axis=1)
    s2 = jnp.concatenate([z, jnp.sin(ar), z, jnp.sin(ac)], axis=1)
    rep = lambda a: jnp.concatenate([a, a], axis=1)
    return rep(cos), rep(s1), rep(s2)


def _block_diag(pw):
    g, n, _ = pw.shape
    out = jnp.zeros((g * n, g * n), pw.dtype)
    for gi in range(g):
        out = out.at[gi * n:(gi + 1) * n, gi * n:(gi + 1) * n].set(pw[gi])
    return out


def kernel(x, c, ctx, c_ctx, ada_w, ada_b, norm1_g, norm2_g, w_in, attn_sink, conv_dw_w, conv_dw_b, conv_ln_g, conv_ln_b, conv_pw_w, gmlp_ln_g, gmlp_ln_b, gmlp_ws, gmlp_bs, pool_w, pool_scale, group_norm_g, w_out, router_w, router_b, exp_w1, exp_b1, exp_w2, exp_b2, final_norm_g):
    b, L, d = x.shape
    lc = ctx.shape[1]
    depth = ada_w.shape[0]
    tabs = _rope_tables(L)

    r = -(-(b + 1) // 8) * 8
    cond = jnp.zeros((r, d), F32).at[:b].set(c).at[b].set(c_ctx)
    mods = _adaln(cond, ada_w, ada_b).reshape(depth, r, 6, d)

    n_groups = BATCH_GROUPS if b % BATCH_GROUPS == 0 else 1
    bh = b // n_groups
    xs, cs = [x] * n_groups, [ctx] * n_groups
    pending = [None] * n_groups

    def finish(st, out_buf):
        if st["last"]:
            return _combine(st["ygx"], st["gates"], st["xh"], st["g2"], st["moff"], final_norm_g, 0,
                            final=True, out_buf=out_buf, ooff=st["moff"], out_b=b)
        x_new = _combine(st["ygx"], st["gates"], st["xh"], st["g2"], st["moff"], final_norm_g, 0, final=False)
        c_new = _combine(st["ygc"], st["gates"], st["ch"], st["g2c"], 0, final_norm_g, bh * L, final=False)
        return x_new, c_new

    for l in range(depth):
        last = l == depth - 1
        mx = [mods[l, :b, i].reshape(b, 1, d) for i in range(6)]
        mc = [mods[l, b:b + 1, i].reshape(1, 1, d) for i in range(6)]
        sh1, sc1, g1, sh2, sc2, g2 = mx
        sh1c, sc1c, g1c, sh2c, sc2c, g2c = mc

        wl = w_in[l]
        wq = wl[:, :OFF_K][:, _Q_PERM] * (HEAD_DIM ** -0.5)
        w_full = jnp.concatenate([wq, wl[:, OFF_K:]], axis=1).astype(BF16)
        gn = group_norm_g[l]
        g_att = gn[:256][_Q_PERM]
        wo = w_out[l]
        woa = wo[:256][_Q_PERM].astype(BF16)
        wol = wo[256:].astype(BF16)
        lp = dict(
            dw_w=conv_dw_w[l], dw_b=conv_dw_b[l].reshape(1, -1),
            cln_g=conv_ln_g[l].reshape(1, -1), cln_b=conv_ln_b[l].reshape(1, -1),
            cpw=conv_pw_w[l].astype(BF16),
            gln_g=gmlp_ln_g[l].reshape(1, -1), gln_b=gmlp_ln_b[l].reshape(1, -1),
            gws=gmlp_ws[l].reshape(GMLP_HEADS * CHUNK, CHUNK).astype(BF16),
            gbs=jnp.repeat(gmlp_bs[l].T, GROUP_W // GMLP_HEADS, axis=1),
            pw=_block_diag(pool_w[l]).astype(BF16), ps=pool_scale[l].reshape(1, -1),
            gn_local=gn[256:].reshape(1, -1))
        rw = jnp.zeros((d, ROUTE_W), F32).at[:, :N_EXPERTS].set(router_w[l]).astype(BF16)
        rb = jnp.full((1, ROUTE_W), NEG_INF, F32).at[0, :N_EXPERTS].set(router_b[l])

        w_kv = wl[:, OFF_K:OFF_CONV].astype(BF16)
        take = lambda a, i: a.at[i].get(mode="promise_in_bounds")
        staged = []
        for h in range(n_groups):
            moff = h * bh
            if pending[h] is not None:
                if staged:
                    pending[h]["ygx"], staged[-1]["xh"] = lax.optimization_barrier(
                        (pending[h]["ygx"], staged[-1]["xh"]))
                xs[h], cs[h] = finish(pending[h], False)
            xh, ch = xs[h], cs[h]
            xoff = moff if l == 0 else 0

            q, kv, mix = _inproj(xh, xoff, bh, sc1, sh1, moff, norm1_g[l], w_full, tabs, full=True)
            if last:
                (kvc,) = _inproj(ch, xoff, bh, sc1c, sh1c, 0, norm1_g[l], w_kv, None, full=False)
            else:
                qc, kvc, mixc = _inproj(ch, xoff, bh, sc1c, sh1c, 0, norm1_g[l], w_full, None, full=True)
            att = _attention(q, kv, kvc, attn_sink[l], g_att, local=True)
            loc = _local_mixers(mix, lp)
            if staged:
                prev = staged[-1]
                prev["xp"], loc = lax.optimization_barrier((prev["xp"], loc))
            t_total = bh * L if last else bh * (L + lc)
            cnt0 = jnp.zeros((1, ROUTE_W), F32)
            xh, h2, gates, idx, rank, cnt = _merge(att, loc, xh, xoff, g1, sc2, sh2, moff, norm2_g[l], woa, wol,
                                                   rw, rb, cnt0, None, 0, t_total)
            if not last:
                attc = _attention(qc, kvc, kvc, attn_sink[l], g_att, local=False)
                locc = _local_mixers(mixc, lp)
                ch, h2, gates, idx, rank, cnt = _merge(attc, locc, ch, xoff, g1c, sc2c, sh2c, 0, norm2_g[l],
                                                       woa, wol, rw, rb, cnt, (h2, gates, idx, rank),
                                                       bh * L, t_total)
            counts = cnt[0, :N_EXPERTS].astype(jnp.int32)
            row_tok, pos, blk_e, n_used = _dispatch_plan(idx[:, :TOP_K], rank[:, :TOP_K], counts)
            staged.append(dict(xh=xh, ch=ch, gates=gates, pos=pos, blk_e=blk_e, n_used=n_used,
                               xp=take(h2, row_tok), moff=moff, g2=g2, g2c=g2c, last=last))

        yps = []
        for h, st in enumerate(staged):
            if yps:
                yps[-1], st["xp"] = lax.optimization_barrier((yps[-1], st["xp"]))
            yps.append(_moe_ffn(st["xp"], st["blk_e"], st["n_used"], exp_w1, exp_b1, exp_w2, exp_b2, l))
        for h, st in enumerate(staged):
            if h == n_groups - 1 and h > 0:
                yps[h], staged[0]["ygx"] = lax.optimization_barrier((yps[h], staged[0]["ygx"]))
            pos = st["pos"]
            st["ygx"] = take(yps[h], pos[:bh * L].T).reshape(TOP_K, bh, L, d)
            if not last:
                st["ygc"] = take(yps[h], pos[bh * L:].T).reshape(TOP_K, bh, lc, d)
            pending[h] = st

    out = None
    for h in range(n_groups):
        out = finish(pending[h], out)
    return out
```

```python
import functools

import jax
import jax.numpy as jnp
import numpy as np
from jax import lax
from jax.experimental import pallas as pl
from jax.experimental.pallas import tpu as pltpu

F32 = jnp.float32
BF16 = jnp.bfloat16

D_MODEL = 1024
GRID_W = 64
GROUP_W = 256
HEAD_DIM = 64
N_Q_HEADS = 4
N_KV_HEADS = 2
WINDOW = 128
ATT_BLOCK = 128
ROPE_BASE = 10000.0
CONV_WIDTH = 31
CHUNK = 128
GMLP_HEADS = 4
POOL_WINDOWS = (2, 4, 8, 16)
N_EXPERTS = 32
TOP_K = 4
D_FF = 1024
SWIGLU_ALPHA = 1.702
SWIGLU_LIMIT = 7.0
EPS = 1e-6
NEG_INF = -1e30

OFF_K = 256
OFF_V = 384
OFF_CONV = 512
IN_W = 1792
MIX_IN_W = IN_W - OFF_CONV
LOCAL_W = 3 * GROUP_W

LANES = 128
HALO = 16
ROUTE_W = LANES
MOE_TM = 512
FF_CHUNK = 512
MERGE_SUB = 512
BATCH_GROUPS = 2
VMEM_LIMIT = 56 * 1024 * 1024


def _cparams(sem):
    return pltpu.CompilerParams(dimension_semantics=sem, vmem_limit_bytes=VMEM_LIMIT)


def _ada_kernel(c_ref, w_ref, b_ref, o_ref):
    c = c_ref[...]
    s = c * jax.nn.sigmoid(c)
    o_ref[0] = jnp.dot(s.astype(BF16), w_ref[0].astype(BF16), preferred_element_type=F32) + b_ref[0]


def _adaln(cond, ada_w, ada_b):
    depth, d, n = ada_w.shape
    r = cond.shape[0]
    tn = 1536
    return pl.pallas_call(
        _ada_kernel,
        grid=(depth, n // tn),
        in_specs=[pl.BlockSpec((r, d), lambda l, j: (0, 0)),
                  pl.BlockSpec((1, d, tn), lambda l, j: (l, 0, j)),
                  pl.BlockSpec((1, 1, tn), lambda l, j: (l, 0, j))],
        out_specs=pl.BlockSpec((1, r, tn), lambda l, j: (l, 0, j)),
        out_shape=jax.ShapeDtypeStruct((depth, r, n), F32),
        compiler_params=_cparams(("arbitrary", "arbitrary")),
        name="adaln",
    )(cond, ada_w, ada_b.reshape(depth, 1, n))


def _modulated_rmsnorm(x, g, sc, sh):
    ms = jnp.mean(x * x, axis=-1, keepdims=True)
    return x * lax.rsqrt(ms + EPS) * (g * (1.0 + sc)) + sh


def _weighted_expert_sum(yg_ref, gates):
    ff = gates[:, 0:1] * yg_ref[0, 0].astype(F32)
    for k in range(1, TOP_K):
        ff = ff + gates[:, k:k + 1] * yg_ref[k, 0].astype(F32)
    return ff


def _inproj_kernel(*refs, rope, full, fused, emit_x):
    refs = list(refs)
    if fused:
        yg_ref, gate_ref, g2_ref = refs[:3]
        refs = refs[3:]
    if rope:
        x_ref, sc_ref, sh_ref, g_ref, w_ref, cos_ref, s1_ref, s2_ref = refs[:8]
        outs = refs[8:]
    else:
        x_ref, sc_ref, sh_ref, g_ref, w_ref = refs[:5]
        outs = refs[5:]
    x = x_ref[0]
    if fused:
        x = x + g2_ref[0] * _weighted_expert_sum(yg_ref, gate_ref[...])
        if emit_x:
            outs[0][0] = x
            outs = outs[1:]
    h = _modulated_rmsnorm(x, g_ref[...], sc_ref[0], sh_ref[0])
    p = jnp.dot(h.astype(BF16), w_ref[...], preferred_element_type=F32)

    def roped(xs):
        if not rope:
            return xs
        return (xs * cos_ref[...] + pltpu.roll(xs, LANES - 16, 1) * s1_ref[...]
                + pltpu.roll(xs, 16, 1) * s2_ref[...])

    if full:
        q_ref, kv_ref, mix_ref = outs
        q_ref[0] = jnp.concatenate([roped(p[:, 0:128]), roped(p[:, 128:256])], axis=1).astype(BF16)
        kv_ref[0] = jnp.concatenate([roped(p[:, 256:384]), p[:, 384:512]], axis=1).astype(BF16)
        mix_ref[0] = p[:, OFF_CONV:]
    else:
        (kv_ref,) = outs
        kv_ref[0] = p.astype(BF16)


def _inproj(x, xoff, b, sc, sh, moff, g, w, tabs, *, full, comb=None, emit_x=True):
    _, L, d = x.shape
    n = w.shape[1]
    tm = min(512, L)
    nt = L // tm
    assert L % tm == 0
    rope = tabs is not None
    bm = sc.shape[0]
    mod_map = (lambda i, bb: (moff + bb, 0, 0)) if bm > 1 else (lambda i, bb: (0, 0, 0))
    in_specs, args = [], []
    if comb is not None:
        yg, gates, row_off, g2, g2off = comb
        assert row_off % tm == 0
        blk_off = row_off // tm
        g2_map = (lambda i, bb: (g2off + bb, 0, 0)) if g2.shape[0] > 1 else (lambda i, bb: (0, 0, 0))
        in_specs += [pl.BlockSpec((TOP_K, 1, tm, d), lambda i, bb: (0, bb, i, 0)),
                     pl.BlockSpec((tm, ROUTE_W), lambda i, bb: (blk_off + bb * nt + i, 0)),
                     pl.BlockSpec((1, 1, d), g2_map)]
        args += [yg, gates, g2]
    in_specs += [pl.BlockSpec((1, tm, d), lambda i, bb: (xoff + bb, i, 0)),
                 pl.BlockSpec((1, 1, d), mod_map),
                 pl.BlockSpec((1, 1, d), mod_map),
                 pl.BlockSpec((1, d), lambda i, bb: (0, 0)),
                 pl.BlockSpec((d, n), lambda i, bb: (0, 0))]
    args += [x, sc, sh, g.reshape(1, d), w]
    if rope:
        in_specs += [pl.BlockSpec((tm, LANES), lambda i, bb: (i, 0))] * 3
        args += list(tabs)
    tile = lambda wd: pl.BlockSpec((1, tm, wd), lambda i, bb: (bb, i, 0))
    out_shape, out_specs = [], []
    emit_x = emit_x and comb is not None
    if emit_x:
        out_shape.append(jax.ShapeDtypeStruct((b, L, d), F32))
        out_specs.append(tile(d))
    if full:
        out_shape += [jax.ShapeDtypeStruct((b, L, 256), BF16),
                      jax.ShapeDtypeStruct((b, L, 256), BF16),
                      jax.ShapeDtypeStruct((b, L, MIX_IN_W), F32)]
        out_specs += [tile(256), tile(256), tile(MIX_IN_W)]
    else:
        out_shape.append(jax.ShapeDtypeStruct((b, L, 256), BF16))
        out_specs.append(tile(256))
    return pl.pallas_call(
        functools.partial(_inproj_kernel, rope=rope, full=full, fused=comb is not None, emit_x=emit_x),
        grid=(nt, b),
        in_specs=in_specs, out_specs=tuple(out_specs), out_shape=tuple(out_shape),
        compiler_params=_cparams(("arbitrary", "arbitrary")),
        name="inproj_x" if rope else "inproj_ctx",
    )(*args)


def _attn_kernel(*refs, local, seq_len, q_blocks):
    if local:
        sink_ref, q_ref, kv_ref, kvc_ref, g_ref, bias_ref, o_ref = refs
    else:
        sink_ref, q_ref, kv_ref, kvc_ref, g_ref, o_ref = refs
    step = pl.program_id(1)
    lane = lax.broadcasted_iota(jnp.int32, (1, LANES), 1)
    row = lax.broadcasted_iota(jnp.int32, (2 * ATT_BLOCK, 1), 0)
    kvc = kvc_ref[0]
    kc, vc = kvc[:, :LANES], kvc[:, LANES:]
    lc = kvc.shape[0]
    nt = (((1,), (1,)), ((), ()))
    zero = jnp.zeros((), BF16)
    lm0 = lane < HEAD_DIM
    for sb in range(q_blocks):
        n = step * q_blocks + sb
        q = q_ref[0, sb * ATT_BLOCK:(sb + 1) * ATT_BLOCK, :]
        if local:
            blk0 = jnp.clip(n - 1, 0, seq_len // ATT_BLOCK - 3)
            kw = kv_ref[0, pl.ds(pl.multiple_of(blk0 * ATT_BLOCK, ATT_BLOCK), 3 * ATT_BLOCK), :]
            kl, vl = kw[:, :LANES], kw[:, LANES:]
            bias = bias_ref[n - blk0]
        outs = []
        for kh in range(N_KV_HEADS):
            lm = lm0 if kh == 0 else jnp.logical_not(lm0)
            lhs = jnp.concatenate([jnp.where(lm, q[:, :LANES], zero), jnp.where(lm, q[:, LANES:], zero)],
                                  axis=0)
            sk = jnp.where(row < ATT_BLOCK, sink_ref[2 * kh], sink_ref[2 * kh + 1])
            s = lax.dot_general(lhs, kc, nt, preferred_element_type=F32)
            if local:
                s_l = lax.dot_general(lhs, kl, nt, preferred_element_type=F32) + bias
                s = jnp.concatenate([s, s_l], axis=1)
            m = jnp.maximum(sk, jnp.max(s, axis=-1, keepdims=True))
            e = jnp.exp(s - m)
            den = jnp.exp(sk - m) + jnp.sum(e, axis=-1, keepdims=True)
            e = e.astype(BF16)
            o = jnp.dot(e[:, :lc], vc, preferred_element_type=F32)
            if local:
                o = o + jnp.dot(e[:, lc:], vl, preferred_element_type=F32)
            outs.append(o / den)
        o01, o23 = outs
        att = jnp.concatenate([jnp.where(lm0, o01[:ATT_BLOCK], o23[:ATT_BLOCK]),
                               jnp.where(lm0, o01[ATT_BLOCK:], o23[ATT_BLOCK:])], axis=1)
        ms = jnp.mean(att * att, axis=-1, keepdims=True)
        o_ref[0, sb * ATT_BLOCK:(sb + 1) * ATT_BLOCK, :] = (att * lax.rsqrt(ms + EPS) * g_ref[...]).astype(BF16)


def _band_bias():
    i = np.arange(2 * ATT_BLOCK)[None, :, None] % ATT_BLOCK
    j = np.arange(3 * ATT_BLOCK)[None, None, :]
    v = np.arange(3)[:, None, None]
    return jnp.asarray(np.where(np.abs(j - i - ATT_BLOCK * v) <= WINDOW, 0.0, NEG_INF), F32)


def _attention(q, kv, kvc, sink, g_att, *, local):
    b, L, _ = q.shape
    lc = kvc.shape[1]
    nb = L // ATT_BLOCK
    assert not local or nb >= 3
    q_blocks = 2 if nb % 2 == 0 else 1
    tq = q_blocks * ATT_BLOCK
    in_specs = [pl.BlockSpec(memory_space=pltpu.SMEM),
                pl.BlockSpec((1, tq, 256), lambda bb, i: (bb, i, 0)),
                pl.BlockSpec((1, kv.shape[1], 256), lambda bb, i: (bb, 0, 0)),
                pl.BlockSpec((1, lc, 256), lambda bb, i: (bb, 0, 0)),
                pl.BlockSpec((1, 256), lambda bb, i: (0, 0))]
    args = [sink, q, kv, kvc, g_att.reshape(1, 256)]
    if local:
        in_specs.append(pl.BlockSpec((3, 2 * ATT_BLOCK, 3 * ATT_BLOCK), lambda bb, i: (0, 0, 0)))
        args.append(_band_bias())
    return pl.pallas_call(
        functools.partial(_attn_kernel, local=local, seq_len=L, q_blocks=q_blocks),
        grid=(b, nb // q_blocks),
        in_specs=in_specs,
        out_specs=pl.BlockSpec((1, tq, 256), lambda bb, i: (bb, i, 0)),
        out_shape=jax.ShapeDtypeStruct((b, L, 256), BF16),
        compiler_params=_cparams(("arbitrary", "arbitrary")),
        name="attn_window" if local else "attn_ctx",
    )(*args)


def _layernorm(x, g, b):
    mu = jnp.mean(x, axis=-1, keepdims=True)
    xc = x - mu
    return xc * lax.rsqrt(jnp.mean(xc * xc, axis=-1, keepdims=True) + EPS) * g + b


def _group_norm_store(y, g):
    ms = jnp.mean(y * y, axis=-1, keepdims=True)
    return (y * lax.rsqrt(ms + EPS) * g).astype(BF16)


def _mix_kernel(cur_ref, prev_ref, next_ref, dww_ref, dwb_ref, clg_ref, clb_ref, cpw_ref,
                glg_ref, glb_ref, gws_ref, gbs_ref, pw_ref, ps_ref, gn_ref, o_ref,
                hc_ref, sh_ref, hp_ref, c2_ref, c4_ref, c8_ref, *, ts, seq_len):
    i = pl.program_id(1)
    nt = pl.num_programs(1)
    has_prev = (i > 0).astype(F32)
    has_next = (i < nt - 1).astype(F32)

    def glu(blk):
        return blk[:, 0:256] * jax.nn.sigmoid(blk[:, 256:512])

    pv = prev_ref[0]
    nx = next_ref[0]
    hc_ref[0:HALO, :] = glu(pv) * has_prev
    hc_ref[HALO + ts:2 * HALO + ts, :] = glu(nx) * has_next
    hp_ref[0:HALO, :] = pv[:, 1024:1280] * has_prev
    hp_ref[HALO + ts:2 * HALO + ts, :] = nx[:, 1024:1280] * has_next
    hp_ref[2 * HALO + ts:3 * HALO + ts, :] = jnp.zeros((HALO, 256), F32)
    hc_ref[HALO:HALO + ts, :] = glu(cur_ref[0, :, 0:512])
    hp_ref[HALO:HALO + ts, :] = cur_ref[0, :, 1024:1280]

    for r in range(1, 8):
        sh_ref[r - 1] = hc_ref[r:r + ts + 24, :]
    c2_ref[8:ts + 40, :] = hp_ref[8:ts + 40, :] + hp_ref[9:ts + 41, :]
    c4_ref[8:ts + 32, :] = c2_ref[8:ts + 32, :] + c2_ref[10:ts + 34, :]
    c8_ref[8:ts + 24, :] = c4_ref[8:ts + 24, :] + c4_ref[12:ts + 28, :]

    lane = lax.broadcasted_iota(jnp.int32, (1, 256), 1)
    rc = 64
    for c in range(ts // rc):
        r0 = c * rc
        acc = jnp.zeros((rc, 256), F32)
        for j in range(CONV_WIDTH):
            a, r = divmod(1 + j, 8)
            lo = r0 + 8 * a
            tap = hc_ref[lo:lo + rc, :] if r == 0 else sh_ref[r - 1, lo:lo + rc, :]
            acc = acc + tap * dww_ref[j:j + 1, :]
        hcv = _layernorm(acc + dwb_ref[...], clg_ref[...], clb_ref[...])
        hcv = hcv * jax.nn.sigmoid(hcv)
        conv = jnp.dot(hcv.astype(BF16), cpw_ref[...], preferred_element_type=F32)
        o_ref[0, r0:r0 + rc, 0:256] = _group_norm_store(conv, gn_ref[:, 0:256])

        s0 = r0 + HALO
        hcur = hp_ref[s0:s0 + rc, :]
        s2 = c2_ref[s0 - 1:s0 - 1 + rc, :]
        s4 = c4_ref[s0 - 2:s0 - 2 + rc, :]
        s8 = c8_ref[s0 - 4:s0 - 4 + rc, :]
        s16 = c8_ref[s0 - 8:s0 - 8 + rc, :] + c8_ref[s0:s0 + rc, :]
        wsum = jnp.where(lane < 64, s2, jnp.where(lane < 128, s4, jnp.where(lane < 192, s8, s16)))
        half = jnp.where(lane < 64, 1, jnp.where(lane < 128, 2, jnp.where(lane < 192, 4, 8)))
        t = i * ts + r0 + lax.broadcasted_iota(jnp.int32, (rc, 1), 0)
        cnt = jnp.clip(t + half, 0, seq_len) - jnp.clip(t - half, 0, seq_len)
        y = wsum / cnt.astype(F32) - hcur
        pool = jnp.dot(y.astype(BF16), pw_ref[...], preferred_element_type=F32) * ps_ref[...]
        o_ref[0, r0:r0 + rc, 512:768] = _group_norm_store(pool, gn_ref[:, 512:768])

    for c in range(ts // CHUNK):
        r0 = c * CHUNK
        u = cur_ref[0, r0:r0 + CHUNK, 512:768]
        v = _layernorm(cur_ref[0, r0:r0 + CHUNK, 768:1024], glg_ref[...], glb_ref[...])
        r = jnp.dot(gws_ref[...], v.astype(BF16), preferred_element_type=F32)
        mixed = jnp.where(lane < 64, r[0:CHUNK],
                          jnp.where(lane < 128, r[CHUNK:2 * CHUNK],
                                    jnp.where(lane < 192, r[2 * CHUNK:3 * CHUNK], r[3 * CHUNK:])))
        gm = u * (mixed + gbs_ref[...])
        o_ref[0, r0:r0 + CHUNK, 256:512] = _group_norm_store(gm, gn_ref[:, 256:512])


def _local_mixers(mix, lp):
    b, L, w = mix.shape
    ts = min(512, L)
    nt = L // ts
    hb = ts // HALO
    last_h = L // HALO - 1
    full = lambda shape: pl.BlockSpec(shape, lambda bb, i: (0,) * len(shape))
    params = [lp["dw_w"], lp["dw_b"], lp["cln_g"], lp["cln_b"], lp["cpw"], lp["gln_g"], lp["gln_b"],
              lp["gws"], lp["gbs"], lp["pw"], lp["ps"], lp["gn_local"]]
    return pl.pallas_call(
        functools.partial(_mix_kernel, ts=ts, seq_len=L),
        grid=(b, nt),
        in_specs=[pl.BlockSpec((1, ts, w), lambda bb, i: (bb, i, 0)),
                  pl.BlockSpec((1, HALO, w), lambda bb, i: (bb, jnp.maximum(i * hb - 1, 0), 0)),
                  pl.BlockSpec((1, HALO, w), lambda bb, i: (bb, jnp.minimum((i + 1) * hb, last_h), 0))]
                 + [full(p.shape) for p in params],
        out_specs=pl.BlockSpec((1, ts, LOCAL_W), lambda bb, i: (bb, i, 0)),
        out_shape=jax.ShapeDtypeStruct((b, L, LOCAL_W), BF16),
        scratch_shapes=[pltpu.VMEM((ts + 2 * HALO, 256), F32), pltpu.VMEM((7, ts + 24, 256), F32)]
                       + [pltpu.VMEM((ts + 3 * HALO, 256), F32)] * 4,
        compiler_params=_cparams(("arbitrary", "arbitrary")),
        name="local_mixers",
    )(mix, mix, mix, *params)


def _merge_kernel(*refs, n_alias):
    (att_ref, loc_ref, x_ref, g1_ref, sc_ref, sh_ref, n2g_ref, woa_ref, wol_ref,
     rw_ref, rb_ref, ltri_ref, cnt_in_ref) = refs[:13]
    xo_ref, h2_ref, gate_ref, idx_ref, rank_ref, cnt_ref, run_ref = refs[13 + n_alias:]

    @pl.when(jnp.logical_and(pl.program_id(0) == 0, pl.program_id(1) == 0))
    def _():
        run_ref[...] = cnt_in_ref[...]

    tm = x_ref.shape[1]
    sub = ltri_ref.shape[0]
    lane = lax.broadcasted_iota(jnp.int32, (sub, ROUTE_W), 1).astype(F32)
    for s in range(tm // sub):
        r0 = s * sub
        y = (jnp.dot(att_ref[0, r0:r0 + sub, :], woa_ref[...], preferred_element_type=F32)
             + jnp.dot(loc_ref[0, r0:r0 + sub, :], wol_ref[...], preferred_element_type=F32))
        xn = x_ref[0, r0:r0 + sub, :] + g1_ref[0] * y
        xo_ref[0, r0:r0 + sub, :] = xn
        h2 = _modulated_rmsnorm(xn, n2g_ref[...], sc_ref[0], sh_ref[0]).astype(BF16)
        h2_ref[r0:r0 + sub, :] = h2
        logits = jnp.dot(h2, rw_ref[...], preferred_element_type=F32) + rb_ref[...]
        vals = jnp.zeros(logits.shape, F32)
        idxs = jnp.zeros(logits.shape, F32)
        hits = jnp.zeros(logits.shape, F32)
        sels = []
        top = None
        for k in range(TOP_K):
            m = jnp.max(logits, axis=-1, keepdims=True)
            sel = jnp.min(jnp.where(logits == m, lane, float(ROUTE_W)), axis=-1, keepdims=True)
            if top is None:
                top = m
            chosen = lane == sel
            vals = jnp.where(lane == float(k), jnp.exp(m - top), vals)
            idxs = jnp.where(lane == float(k), sel, idxs)
            hits = jnp.where(chosen, 1.0, hits)
            logits = jnp.where(chosen, NEG_INF * 2.0, logits)
            sels.append(chosen)
        gate_ref[r0:r0 + sub, :] = vals / jnp.sum(vals, axis=-1, keepdims=True)
        idx_ref[r0:r0 + sub, :] = idxs.astype(jnp.int32)
        base = run_ref[...] + jnp.dot(ltri_ref[...], hits.astype(BF16), preferred_element_type=F32)
        ranks = jnp.zeros(logits.shape, F32)
        for k in range(TOP_K):
            rk = jnp.sum(jnp.where(sels[k], base, 0.0), axis=-1, keepdims=True)
            ranks = jnp.where(lane == float(k), rk, ranks)
        rank_ref[r0:r0 + sub, :] = ranks.astype(jnp.int32)
        run_ref[...] = run_ref[...] + jnp.sum(hits, axis=0, keepdims=True)
    cnt_ref[...] = run_ref[...]


def _merge(att, loc, x, xoff, g1, sc2, sh2, moff, n2g, woa, wol, rw, rb, cnt_in, bufs, row_off, t_total):
    b = att.shape[0]
    _, L, d = x.shape
    tm = min(512, L)
    nt = L // tm
    blk_off = row_off // tm
    bm = g1.shape[0]
    mod_map = (lambda bb, i: (moff + bb, 0, 0)) if bm > 1 else (lambda bb, i: (0, 0, 0))
    tile = lambda w: pl.BlockSpec((1, tm, w), lambda bb, i: (bb, i, 0))
    xtile = pl.BlockSpec((1, tm, d), lambda bb, i: (xoff + bb, i, 0))
    flat = lambda w: pl.BlockSpec((tm, w), lambda bb, i: (blk_off + bb * nt + i, 0))
    full = lambda shape: pl.BlockSpec(shape, lambda bb, i: (0,) * len(shape))
    sub = min(MERGE_SUB, tm)
    assert L % tm == 0 and tm % sub == 0 and row_off % tm == 0
    ltri = (np.arange(sub)[:, None] > np.arange(sub)[None, :]).astype(np.float32)
    ltri = jnp.asarray(ltri, BF16)
    n_alias = 0 if bufs is None else len(bufs)
    n_in = 13
    return pl.pallas_call(
        functools.partial(_merge_kernel, n_alias=n_alias),
        grid=(b, nt),
        in_specs=[tile(256), tile(LOCAL_W), xtile,
                  pl.BlockSpec((1, 1, d), mod_map), pl.BlockSpec((1, 1, d), mod_map),
                  pl.BlockSpec((1, 1, d), mod_map),
                  full((1, d)), full(woa.shape), full(wol.shape), full(rw.shape), full(rb.shape),
                  full((sub, sub)), full((1, ROUTE_W))]
                 + [pl.BlockSpec(memory_space=pl.ANY)] * n_alias,
        out_specs=(tile(d), flat(d), flat(ROUTE_W), flat(ROUTE_W), flat(ROUTE_W), full((1, ROUTE_W))),
        out_shape=(jax.ShapeDtypeStruct((b, L, d), F32),
                   jax.ShapeDtypeStruct((t_total, d), BF16),
                   jax.ShapeDtypeStruct((t_total, ROUTE_W), F32),
                   jax.ShapeDtypeStruct((t_total, ROUTE_W), jnp.int32),
                   jax.ShapeDtypeStruct((t_total, ROUTE_W), jnp.int32),
                   jax.ShapeDtypeStruct((1, ROUTE_W), F32)),
        scratch_shapes=[pltpu.VMEM((1, ROUTE_W), F32)],
        input_output_aliases={n_in + j: 1 + j for j in range(n_alias)},
        compiler_params=_cparams(("arbitrary", "arbitrary")),
        name="merge_router",
    )(att, loc, x, g1, sc2, sh2, n2g.reshape(1, d), woa, wol, rw, rb, ltri, cnt_in,
      *(bufs if bufs is not None else ()))


def _moe_kernel(blk_e_ref, n_used_ref, x_ref, w1_ref, b1_ref, w2_ref, b2_ref, o_ref, w1s_ref, w2s_ref):
    i = pl.program_id(0)
    active = i < n_used_ref[0]
    new_expert = jnp.logical_or(i == 0, blk_e_ref[i] != blk_e_ref[jnp.maximum(i - 1, 0)])

    @pl.when(jnp.logical_and(active, new_expert))
    def _():
        rows = 128
        for r in range(D_MODEL // rows):
            w1s_ref[r * rows:(r + 1) * rows, :] = w1_ref[0, 0, r * rows:(r + 1) * rows, :].astype(BF16)
        for r in range(D_FF // rows):
            w2s_ref[r * rows:(r + 1) * rows, :] = w2_ref[0, 0, r * rows:(r + 1) * rows, :].astype(BF16)

    @pl.when(active)
    def _():
        x = x_ref[...]
        acc = jnp.zeros((x.shape[0], D_MODEL), F32)
        for c in range(D_FF // FF_CHUNK):
            lo = c * FF_CHUNK
            glu = jnp.dot(x, w1s_ref[:, lo:lo + FF_CHUNK], preferred_element_type=F32) \
                + b1_ref[0, 0, :, lo:lo + FF_CHUNK]
            lin = jnp.dot(x, w1s_ref[:, D_FF + lo:D_FF + lo + FF_CHUNK], preferred_element_type=F32) \
                + b1_ref[0, 0, :, D_FF + lo:D_FF + lo + FF_CHUNK]
            glu = jnp.minimum(glu, SWIGLU_LIMIT)
            lin = jnp.clip(lin, -SWIGLU_LIMIT, SWIGLU_LIMIT)
            act = glu * jax.nn.sigmoid(SWIGLU_ALPHA * glu) * (lin + 1.0)
            acc = acc + jnp.dot(act.astype(BF16), w2s_ref[lo:lo + FF_CHUNK, :], preferred_element_type=F32)
        o_ref[...] = (acc + b2_ref[0, 0]).astype(o_ref.dtype)

    @pl.when(i >= n_used_ref[0])
    def _():
        o_ref[...] = jnp.zeros(o_ref.shape, o_ref.dtype)


def _moe_ffn(xp, blk_e, n_used, w1, b1, w2, b2, layer):
    n_rows, d = xp.shape
    n_blk = n_rows // MOE_TM
    depth, e = w1.shape[:2]

    def row_map(i, be, nu):
        return (jnp.maximum(jnp.minimum(i, nu[0] - 1), 0), 0)

    def w_map(i, be, nu):
        return (layer, be[i], 0, 0)

    return pl.pallas_call(
        _moe_kernel,
        grid_spec=pltpu.PrefetchScalarGridSpec(
            num_scalar_prefetch=2,
            grid=(n_blk,),
            in_specs=[pl.BlockSpec((MOE_TM, d), row_map),
                      pl.BlockSpec((1, 1, d, 2 * D_FF), w_map),
                      pl.BlockSpec((1, 1, 1, 2 * D_FF), w_map),
                      pl.BlockSpec((1, 1, D_FF, d), w_map),
                      pl.BlockSpec((1, 1, 1, d), w_map)],
            out_specs=pl.BlockSpec((MOE_TM, d), lambda i, be, nu: (i, 0)),
            scratch_shapes=[pltpu.VMEM((d, 2 * D_FF), BF16), pltpu.VMEM((D_FF, d), BF16)]),
        out_shape=jax.ShapeDtypeStruct((n_rows, d), BF16),
        compiler_params=_cparams(("arbitrary",)),
        name="moe_ffn",
    )(blk_e, n_used, xp, w1, b1.reshape(depth, e, 1, 2 * D_FF), w2, b2.reshape(depth, e, 1, d))


def _combine_kernel(*refs, final):
    yg_ref, gate_ref, x_ref, g2_ref, fg_ref = refs[:5]
    o_ref = refs[-1]
    gates = gate_ref[...]
    ff = jnp.zeros(x_ref.shape[1:], F32)
    for k in range(TOP_K):
        ff = ff + gates[:, k:k + 1] * yg_ref[k, 0].astype(F32)
    xo = x_ref[0] + g2_ref[0] * ff
    if final:
        ms = jnp.mean(xo * xo, axis=-1, keepdims=True)
        xo = xo * lax.rsqrt(ms + EPS) * fg_ref[...]
    o_ref[0] = xo


def _combine(yg, gates, x, g2, moff, fg, row_off, *, final, out_buf=None, ooff=0, out_b=None):
    b, L, d = x.shape
    tm = min(256, L)
    nt = L // tm
    blk_off = row_off // tm
    bm = g2.shape[0]
    out_b = b if out_b is None else out_b
    mod_map = (lambda bb, i: (moff + bb, 0, 0)) if bm > 1 else (lambda bb, i: (0, 0, 0))
    extra = () if out_buf is None else (out_buf,)
    return pl.pallas_call(
        functools.partial(_combine_kernel, final=final),
        grid=(b, nt),
        in_specs=[pl.BlockSpec((TOP_K, 1, tm, d), lambda bb, i: (0, bb, i, 0)),
                  pl.BlockSpec((tm, ROUTE_W), lambda bb, i: (blk_off + bb * nt + i, 0)),
                  pl.BlockSpec((1, tm, d), lambda bb, i: (bb, i, 0)),
                  pl.BlockSpec((1, 1, d), mod_map),
                  pl.BlockSpec((1, d), lambda bb, i: (0, 0))]
                 + [pl.BlockSpec(memory_space=pl.ANY)] * len(extra),
        out_specs=pl.BlockSpec((1, tm, d), lambda bb, i: (ooff + bb, i, 0)),
        out_shape=jax.ShapeDtypeStruct((out_b, L, d), F32),
        input_output_aliases={5: 0} if extra else {},
        compiler_params=_cparams(("arbitrary", "arbitrary")),
        name="combine",
    )(yg, gates, x, g2, fg.reshape(1, d), *extra)


def _dispatch_plan(idx, rank, counts):
    n_tok = idx.shape[0]
    n_asg = n_tok * TOP_K
    padded = (counts + MOE_TM - 1) // MOE_TM * MOE_TM
    pad_end = jnp.cumsum(padded)
    pad_start = pad_end - padded
    onehot = idx[..., None] == jnp.arange(N_EXPERTS, dtype=jnp.int32)
    pos = jnp.sum(jnp.where(onehot, pad_start.astype(jnp.int32), 0), axis=-1) + rank
    n_rows = -(-n_asg // MOE_TM) * MOE_TM + N_EXPERTS * MOE_TM
    n_blk = n_rows // MOE_TM
    tok = jnp.broadcast_to(jnp.arange(n_tok, dtype=jnp.int32)[:, None], (n_tok, TOP_K))
    fill = jnp.arange(n_rows, dtype=jnp.int32) % n_tok
    row_tok = fill.at[pos.reshape(-1)].add((tok - pos % n_tok).reshape(-1), mode="promise_in_bounds")
    blk_start = jnp.arange(n_blk, dtype=jnp.int32) * MOE_TM
    blk_e = jnp.minimum(jnp.sum((pad_end[None, :] <= blk_start[:, None]).astype(jnp.int32), axis=1),
                        N_EXPERTS - 1)
    n_used = (pad_end[-1] // MOE_TM).astype(jnp.int32).reshape(1)
    return row_tok, pos, blk_e, n_used


_Q_PERM = np.concatenate([np.arange(0, 64), np.arange(128, 192), np.arange(64, 128), np.arange(192, 256)])


def _rope_tables(L):
    t = jnp.arange(L)
    row = (t // GRID_W).astype(F32)
    col = (t % GRID_W).astype(F32)
    half = HEAD_DIM // 2
    inv = ROPE_BASE ** (-jnp.arange(0, half, 2, dtype=F32) / half)
    ar, ac = row[:, None] * inv[None, :], col[:, None] * inv[None, :]
    z = jnp.zeros_like(ar)
    cos = jnp.concatenate([jnp.cos(ar), jnp.cos(ar), jnp.cos(ac), jnp.cos(ac)], axis=1)
    s1 = jnp.concatenate([-jnp.sin(ar), z, -jnp.sin(ac), z], axis=1)
    s2 = jnp.concatenate([z, jnp.sin(ar), z, jnp.sin(ac)], axis=1)
    rep = lambda a: jnp.concatenate([a, a], axis=1)
    return rep(cos), rep(s1), rep(s2)


def _block_diag(pw):
    g, n, _ = pw.shape
    out = jnp.zeros((g * n, g * n), pw.dtype)
    for gi in range(g):
        out = out.at[gi * n:(gi + 1) * n, gi * n:(gi + 1) * n].set(pw[gi])
    return out


def kernel(x, c, ctx, c_ctx, ada_w, ada_b, norm1_g, norm2_g, w_in, attn_sink, conv_dw_w, conv_dw_b, conv_ln_g, conv_ln_b, conv_pw_w, gmlp_ln_g, gmlp_ln_b, gmlp_ws, gmlp_bs, pool_w, pool_scale, group_norm_g, w_out, router_w, router_b, exp_w1, exp_b1, exp_w2, exp_b2, final_norm_g):
    b, L, d = x.shape
    lc = ctx.shape[1]
    depth = ada_w.shape[0]
    tabs = _rope_tables(L)

    r = -(-(b + 1) // 8) * 8
    cond = jnp.zeros((r, d), F32).at[:b].set(c).at[b].set(c_ctx)
    mods = _adaln(cond, ada_w, ada_b).reshape(depth, r, 6, d)

    n_groups = BATCH_GROUPS if b % BATCH_GROUPS == 0 else 1
    bh = b // n_groups
    xs, cs = [x] * n_groups, [ctx] * n_groups
    pending = [None] * n_groups

    def finish(st, out_buf):
        if st["last"]:
            return _combine(st["ygx"], st["gates"], st["xh"], st["g2"], st["moff"], final_norm_g, 0,
                            final=True, out_buf=out_buf, ooff=st["moff"], out_b=b)
        x_new = _combine(st["ygx"], st["gates"], st["xh"], st["g2"], st["moff"], final_norm_g, 0, final=False)
        c_new = _combine(st["ygc"], st["gates"], st["ch"], st["g2c"], 0, final_norm_g, bh * L, final=False)
        return x_new, c_new

    for l in range(depth):
        last = l == depth - 1
        mx = [mods[l, :b, i].reshape(b, 1, d) for i in range(6)]
        mc = [mods[l, b:b + 1, i].reshape(1, 1, d) for i in range(6)]
        sh1, sc1, g1, sh2, sc2, g2 = mx
        sh1c, sc1c, g1c, sh2c, sc2c, g2c = mc

        wl = w_in[l]
        wq = wl[:, :OFF_K][:, _Q_PERM] * (HEAD_DIM ** -0.5)
        w_full = jnp.concatenate([wq, wl[:, OFF_K:]], axis=1).astype(BF16)
        gn = group_norm_g[l]
        g_att = gn[:256][_Q_PERM]
        wo = w_out[l]
        woa = wo[:256][_Q_PERM].astype(BF16)
        wol = wo[256:].astype(BF16)
        lp = dict(
            dw_w=conv_dw_w[l], dw_b=conv_dw_b[l].reshape(1, -1),
            cln_g=conv_ln_g[l].reshape(1, -1), cln_b=conv_ln_b[l].reshape(1, -1),
            cpw=conv_pw_w[l].astype(BF16),
            gln_g=gmlp_ln_g[l].reshape(1, -1), gln_b=gmlp_ln_b[l].reshape(1, -1),
            gws=gmlp_ws[l].reshape(GMLP_HEADS * CHUNK, CHUNK).astype(BF16),
            gbs=jnp.repeat(gmlp_bs[l].T, GROUP_W // GMLP_HEADS, axis=1),
            pw=_block_diag(pool_w[l]).astype(BF16), ps=pool_scale[l].reshape(1, -1),
            gn_local=gn[256:].reshape(1, -1))
        rw = jnp.zeros((d, ROUTE_W), F32).at[:, :N_EXPERTS].set(router_w[l]).astype(BF16)
        rb = jnp.full((1, ROUTE_W), NEG_INF, F32).at[0, :N_EXPERTS].set(router_b[l])

        w_kv = wl[:, OFF_K:OFF_CONV].astype(BF16)
        take = lambda a, i: a.at[i].get(mode="promise_in_bounds")
        staged = []
        for h in range(n_groups):
            moff = h * bh
            pd = pending[h]
            xoff = moff if l == 0 else 0

            if pd is None:
                xh, ch = xs[h], cs[h]
                q, kv, mix = _inproj(xh, xoff, bh, sc1, sh1, moff, norm1_g[l], w_full, tabs, full=True)
                if last:
                    (kvc,) = _inproj(ch, xoff, bh, sc1c, sh1c, 0, norm1_g[l], w_kv, None, full=False)
                else:
                    qc, kvc, mixc = _inproj(ch, xoff, bh, sc1c, sh1c, 0, norm1_g[l], w_full, None, full=True)
            else:
                if staged:
                    pd["ygx"], staged[-1]["xh"] = lax.optimization_barrier((pd["ygx"], staged[-1]["xh"]))
                combx = (pd["ygx"], pd["gates"], 0, pd["g2"], pd["moff"])
                combc = (pd["ygc"], pd["gates"], bh * L, pd["g2c"], 0)
                xh, q, kv, mix = _inproj(pd["xh"], 0, bh, sc1, sh1, moff, norm1_g[l], w_full, tabs,
                                         full=True, comb=combx)
                if last:
                    ch = None
                    (kvc,) = _inproj(pd["ch"], 0, bh, sc1c, sh1c, 0, norm1_g[l], w_kv, None,
                                     full=False, comb=combc, emit_x=False)
                else:
                    ch, qc, kvc, mixc = _inproj(pd["ch"], 0, bh, sc1c, sh1c, 0, norm1_g[l], w_full, None,
                                                full=True, comb=combc)
                xoff = 0
            att = _attention(q, kv, kvc, attn_sink[l], g_att, local=True)
            loc = _local_mixers(mix, lp)
            if staged:
                prev = staged[-1]
                prev["xp"], loc = lax.optimization_barrier((prev["xp"], loc))
            nxt = pending[h + 1] if h + 1 < n_groups else None
            if nxt is not None:
                nxt["ygx"], loc = lax.optimization_barrier((nxt["ygx"], loc))
            t_total = bh * L if last else bh * (L + lc)
            cnt0 = jnp.zeros((1, ROUTE_W), F32)
            xh, h2, gates, idx, rank, cnt = _merge(att, loc, xh, xoff, g1, sc2, sh2, moff, norm2_g[l], woa, wol,
                                                   rw, rb, cnt0, None, 0, t_total)
            if not last:
                attc = _attention(qc, kvc, kvc, attn_sink[l], g_att, local=False)
                locc = _local_mixers(mixc, lp)
                ch, h2, gates, idx, rank, cnt = _merge(attc, locc, ch, xoff, g1c, sc2c, sh2c, 0, norm2_g[l],
                                                       woa, wol, rw, rb, cnt, (h2, gates, idx, rank),
                                                       bh * L, t_total)
            counts = cnt[0, :N_EXPERTS].astype(jnp.int32)
            row_tok, pos, blk_e, n_used = _dispatch_plan(idx[:, :TOP_K], rank[:, :TOP_K], counts)
            staged.append(dict(xh=xh, ch=ch, gates=gates, pos=pos, blk_e=blk_e, n_used=n_used,
                               xp=take(h2, row_tok), moff=moff, g2=g2, g2c=g2c, last=last))

        yps = []
        for h, st in enumerate(staged):
            if yps:
                yps[-1], st["xp"] = lax.optimization_barrier((yps[-1], st["xp"]))
            yps.append(_moe_ffn(st["xp"], st["blk_e"], st["n_used"], exp_w1, exp_b1, exp_w2, exp_b2, l))
        for h, st in enumerate(staged):
            if h == n_groups - 1 and h > 0:
                yps[h], staged[0]["ygx"] = lax.optimization_barrier((yps[h], staged[0]["ygx"]))
            pos = st["pos"]
            st["ygx"] = take(yps[h], pos[:bh * L].T).reshape(TOP_K, bh, L, d)
            if not last:
                st["ygc"] = take(yps[h], pos[bh * L:].T).reshape(TOP_K, bh, lc, d)
            pending[h] = st

    out = None
    for h in range(n_groups):
        out = finish(pending[h], out)
    return out
```

```python
import functools

import jax
import jax.numpy as jnp
import numpy as np
from jax import lax
from jax.experimental import pallas as pl
from jax.experimental.pallas import tpu as pltpu

F32 = jnp.float32
BF16 = jnp.bfloat16

D_MODEL = 1024
GRID_W = 64
GROUP_W = 256
HEAD_DIM = 64
N_Q_HEADS = 4
N_KV_HEADS = 2
WINDOW = 128
ATT_BLOCK = 128
ROPE_BASE = 10000.0
CONV_WIDTH = 31
CHUNK = 128
GMLP_HEADS = 4
POOL_WINDOWS = (2, 4, 8, 16)
N_EXPERTS = 32
TOP_K = 4
D_FF = 1024
SWIGLU_ALPHA = 1.702
SWIGLU_LIMIT = 7.0
EPS = 1e-6
NEG_INF = -1e30

OFF_K = 256
OFF_V = 384
OFF_CONV = 512
IN_W = 1792
MIX_IN_W = IN_W - OFF_CONV
LOCAL_W = 3 * GROUP_W

LANES = 128
HALO = 16
ROUTE_W = LANES
MOE_TM = 512
FF_CHUNK = 512
MERGE_SUB = 512
BATCH_GROUPS = 2
VMEM_LIMIT = 56 * 1024 * 1024


def _cparams(sem):
    return pltpu.CompilerParams(dimension_semantics=sem, vmem_limit_bytes=VMEM_LIMIT)


def _ada_kernel(c_ref, w_ref, b_ref, o_ref):
    c = c_ref[...]
    s = c * jax.nn.sigmoid(c)
    o_ref[0] = jnp.dot(s.astype(BF16), w_ref[0].astype(BF16), preferred_element_type=F32) + b_ref[0]


def _adaln(cond, ada_w, ada_b):
    depth, d, n = ada_w.shape
    r = cond.shape[0]
    tn = 1536
    return pl.pallas_call(
        _ada_kernel,
        grid=(depth, n // tn),
        in_specs=[pl.BlockSpec((r, d), lambda l, j: (0, 0)),
                  pl.BlockSpec((1, d, tn), lambda l, j: (l, 0, j)),
                  pl.BlockSpec((1, 1, tn), lambda l, j: (l, 0, j))],
        out_specs=pl.BlockSpec((1, r, tn), lambda l, j: (l, 0, j)),
        out_shape=jax.ShapeDtypeStruct((depth, r, n), F32),
        compiler_params=_cparams(("arbitrary", "arbitrary")),
        name="adaln",
    )(cond, ada_w, ada_b.reshape(depth, 1, n))


def _modulated_rmsnorm(x, g, sc, sh):
    ms = jnp.mean(x * x, axis=-1, keepdims=True)
    return x * lax.rsqrt(ms + EPS) * (g * (1.0 + sc)) + sh


def _weighted_expert_sum(yg_ref, gates):
    ff = gates[:, 0:1] * yg_ref[0, 0].astype(F32)
    for k in range(1, TOP_K):
        ff = ff + gates[:, k:k + 1] * yg_ref[k, 0].astype(F32)
    return ff


def _inproj_kernel(*refs, rope, full, fused, emit_x):
    refs = list(refs)
    if fused:
        yg_ref, gate_ref, g2_ref = refs[:3]
        refs = refs[3:]
    if rope:
        x_ref, sc_ref, sh_ref, g_ref, w_ref, cos_ref, s1_ref, s2_ref = refs[:8]
        outs = refs[8:]
    else:
        x_ref, sc_ref, sh_ref, g_ref, w_ref = refs[:5]
        outs = refs[5:]
    x = x_ref[0]
    if fused:
        x = x + g2_ref[0] * _weighted_expert_sum(yg_ref, gate_ref[...])
        if emit_x:
            outs[0][0] = x
            outs = outs[1:]
    h = _modulated_rmsnorm(x, g_ref[...], sc_ref[0], sh_ref[0])
    p = jnp.dot(h.astype(BF16), w_ref[...], preferred_element_type=F32)

    def roped(xs):
        if not rope:
            return xs
        return (xs * cos_ref[...] + pltpu.roll(xs, LANES - 16, 1) * s1_ref[...]
                + pltpu.roll(xs, 16, 1) * s2_ref[...])

    if full:
        q_ref, kv_ref, mix_ref = outs
        q_ref[0] = jnp.concatenate([roped(p[:, 0:128]), roped(p[:, 128:256])], axis=1).astype(BF16)
        kv_ref[0] = jnp.concatenate([roped(p[:, 256:384]), p[:, 384:512]], axis=1).astype(BF16)
        mix_ref[0] = p[:, OFF_CONV:]
    else:
        (kv_ref,) = outs
        kv_ref[0] = p.astype(BF16)


def _inproj(x, xoff, b, sc, sh, moff, g, w, tabs, *, full, comb=None, emit_x=True):
    _, L, d = x.shape
    n = w.shape[1]
    tm = min(512, L)
    nt = L // tm
    assert L % tm == 0
    rope = tabs is not None
    bm = sc.shape[0]
    mod_map = (lambda i, bb: (moff + bb, 0, 0)) if bm > 1 else (lambda i, bb: (0, 0, 0))
    in_specs, args = [], []
    if comb is not None:
        yg, gates, row_off, g2, g2off = comb
        assert row_off % tm == 0
        blk_off = row_off // tm
        g2_map = (lambda i, bb: (g2off + bb, 0, 0)) if g2.shape[0] > 1 else (lambda i, bb: (0, 0, 0))
        in_specs += [pl.BlockSpec((TOP_K, 1, tm, d), lambda i, bb: (0, bb, i, 0)),
                     pl.BlockSpec((tm, ROUTE_W), lambda i, bb: (blk_off + bb * nt + i, 0)),
                     pl.BlockSpec((1, 1, d), g2_map)]
        args += [yg, gates, g2]
    in_specs += [pl.BlockSpec((1, tm, d), lambda i, bb: (xoff + bb, i, 0)),
                 pl.BlockSpec((1, 1, d), mod_map),
                 pl.BlockSpec((1, 1, d), mod_map),
                 pl.BlockSpec((1, d), lambda i, bb: (0, 0)),
                 pl.BlockSpec((d, n), lambda i, bb: (0, 0))]
    args += [x, sc, sh, g.reshape(1, d), w]
    if rope:
        in_specs += [pl.BlockSpec((tm, LANES), lambda i, bb: (i, 0))] * 3
        args += list(tabs)
    tile = lambda wd: pl.BlockSpec((1, tm, wd), lambda i, bb: (bb, i, 0))
    out_shape, out_specs = [], []
    emit_x = emit_x and comb is not None
    if emit_x:
        out_shape.append(jax.ShapeDtypeStruct((b, L, d), F32))
        out_specs.append(tile(d))
    if full:
        out_shape += [jax.ShapeDtypeStruct((b, L, 256), BF16),
                      jax.ShapeDtypeStruct((b, L, 256), BF16),
                      jax.ShapeDtypeStruct((b, L, MIX_IN_W), F32)]
        out_specs += [tile(256), tile(256), tile(MIX_IN_W)]
    else:
        out_shape.append(jax.ShapeDtypeStruct((b, L, 256), BF16))
        out_specs.append(tile(256))
    return pl.pallas_call(
        functools.partial(_inproj_kernel, rope=rope, full=full, fused=comb is not None, emit_x=emit_x),
        grid=(nt, b),
        in_specs=in_specs, out_specs=tuple(out_specs), out_shape=tuple(out_shape),
        compiler_params=_cparams(("arbitrary", "arbitrary")),
        name="inproj_x" if rope else "inproj_ctx",
    )(*args)


def _attn_kernel(*refs, local, seq_len, q_blocks):
    if local:
        sink_ref, q_ref, kv_ref, kvc_ref, g_ref, bias_ref, o_ref = refs
    else:
        sink_ref, q_ref, kv_ref, kvc_ref, g_ref, o_ref = refs
    step = pl.program_id(1)
    lane = lax.broadcasted_iota(jnp.int32, (1, LANES), 1)
    row = lax.broadcasted_iota(jnp.int32, (2 * ATT_BLOCK, 1), 0)
    kvc = kvc_ref[0]
    kc, vc = kvc[:, :LANES], kvc[:, LANES:]
    lc = kvc.shape[0]
    nt = (((1,), (1,)), ((), ()))
    zero = jnp.zeros((), BF16)
    lm0 = lane < HEAD_DIM
    for sb in range(q_blocks):
        n = step * q_blocks + sb
        q = q_ref[0, sb * ATT_BLOCK:(sb + 1) * ATT_BLOCK, :]
        if local:
            blk0 = jnp.clip(n - 1, 0, seq_len // ATT_BLOCK - 3)
            kw = kv_ref[0, pl.ds(pl.multiple_of(blk0 * ATT_BLOCK, ATT_BLOCK), 3 * ATT_BLOCK), :]
            kl, vl = kw[:, :LANES], kw[:, LANES:]
            bias = bias_ref[n - blk0]
        outs = []
        for kh in range(N_KV_HEADS):
            lm = lm0 if kh == 0 else jnp.logical_not(lm0)
            lhs = jnp.concatenate([jnp.where(lm, q[:, :LANES], zero), jnp.where(lm, q[:, LANES:], zero)],
                                  axis=0)
            sk = jnp.where(row < ATT_BLOCK, sink_ref[2 * kh], sink_ref[2 * kh + 1])
            s = lax.dot_general(lhs, kc, nt, preferred_element_type=F32)
            if local:
                s_l = lax.dot_general(lhs, kl, nt, preferred_element_type=F32) + bias
                s = jnp.concatenate([s, s_l], axis=1)
            m = jnp.maximum(sk, jnp.max(s, axis=-1, keepdims=True))
            e = jnp.exp(s - m)
            den = jnp.exp(sk - m) + jnp.sum(e, axis=-1, keepdims=True)
            e = e.astype(BF16)
            o = jnp.dot(e[:, :lc], vc, preferred_element_type=F32)
            if local:
                o = o + jnp.dot(e[:, lc:], vl, preferred_element_type=F32)
            outs.append(o / den)
        o01, o23 = outs
        att = jnp.concatenate([jnp.where(lm0, o01[:ATT_BLOCK], o23[:ATT_BLOCK]),
                               jnp.where(lm0, o01[ATT_BLOCK:], o23[ATT_BLOCK:])], axis=1)
        ms = jnp.mean(att * att, axis=-1, keepdims=True)
        o_ref[0, sb * ATT_BLOCK:(sb + 1) * ATT_BLOCK, :] = (att * lax.rsqrt(ms + EPS) * g_ref[...]).astype(BF16)


def _band_bias():
    i = np.arange(2 * ATT_BLOCK)[None, :, None] % ATT_BLOCK
    j = np.arange(3 * ATT_BLOCK)[None, None, :]
    v = np.arange(3)[:, None, None]
    return jnp.asarray(np.where(np.abs(j - i - ATT_BLOCK * v) <= WINDOW, 0.0, NEG_INF), F32)


def _attention(q, kv, kvc, sink, g_att, *, local):
    b, L, _ = q.shape
    lc = kvc.shape[1]
    nb = L // ATT_BLOCK
    assert not local or nb >= 3
    q_blocks = 2 if nb % 2 == 0 else 1
    tq = q_blocks * ATT_BLOCK
    in_specs = [pl.BlockSpec(memory_space=pltpu.SMEM),
                pl.BlockSpec((1, tq, 256), lambda bb, i: (bb, i, 0)),
                pl.BlockSpec((1, kv.shape[1], 256), lambda bb, i: (bb, 0, 0)),
                pl.BlockSpec((1, lc, 256), lambda bb, i: (bb, 0, 0)),
                pl.BlockSpec((1, 256), lambda bb, i: (0, 0))]
    args = [sink, q, kv, kvc, g_att.reshape(1, 256)]
    if local:
        in_specs.append(pl.BlockSpec((3, 2 * ATT_BLOCK, 3 * ATT_BLOCK), lambda bb, i: (0, 0, 0)))
        args.append(_band_bias())
    return pl.pallas_call(
        functools.partial(_attn_kernel, local=local, seq_len=L, q_blocks=q_blocks),
        grid=(b, nb // q_blocks),
        in_specs=in_specs,
        out_specs=pl.BlockSpec((1, tq, 256), lambda bb, i: (bb, i, 0)),
        out_shape=jax.ShapeDtypeStruct((b, L, 256), BF16),
        compiler_params=_cparams(("arbitrary", "arbitrary")),
        name="attn_window" if local else "attn_ctx",
    )(*args)


def _layernorm(x, g, b):
    mu = jnp.mean(x, axis=-1, keepdims=True)
    xc = x - mu
    return xc * lax.rsqrt(jnp.mean(xc * xc, axis=-1, keepdims=True) + EPS) * g + b


def _group_norm_store(y, g):
    ms = jnp.mean(y * y, axis=-1, keepdims=True)
    return (y * lax.rsqrt(ms + EPS) * g).astype(BF16)


def _mix_kernel(cur_ref, prev_ref, next_ref, dww_ref, dwb_ref, clg_ref, clb_ref, cpw_ref,
                glg_ref, glb_ref, gws_ref, gbs_ref, pw_ref, ps_ref, gn_ref, o_ref,
                hc_ref, sh_ref, hp_ref, c2_ref, c4_ref, c8_ref, *, ts, seq_len):
    i = pl.program_id(1)
    nt = pl.num_programs(1)
    has_prev = (i > 0).astype(F32)
    has_next = (i < nt - 1).astype(F32)

    def glu(blk):
        return blk[:, 0:256] * jax.nn.sigmoid(blk[:, 256:512])

    pv = prev_ref[0]
    nx = next_ref[0]
    hc_ref[0:HALO, :] = glu(pv) * has_prev
    hc_ref[HALO + ts:2 * HALO + ts, :] = glu(nx) * has_next
    hp_ref[0:HALO, :] = pv[:, 1024:1280] * has_prev
    hp_ref[HALO + ts:2 * HALO + ts, :] = nx[:, 1024:1280] * has_next
    hp_ref[2 * HALO + ts:3 * HALO + ts, :] = jnp.zeros((HALO, 256), F32)
    hc_ref[HALO:HALO + ts, :] = glu(cur_ref[0, :, 0:512])
    hp_ref[HALO:HALO + ts, :] = cur_ref[0, :, 1024:1280]

    for r in range(1, 8):
        sh_ref[r - 1] = hc_ref[r:r + ts + 24, :]
    c2_ref[8:ts + 40, :] = hp_ref[8:ts + 40, :] + hp_ref[9:ts + 41, :]
    c4_ref[8:ts + 32, :] = c2_ref[8:ts + 32, :] + c2_ref[10:ts + 34, :]
    c8_ref[8:ts + 24, :] = c4_ref[8:ts + 24, :] + c4_ref[12:ts + 28, :]

    lane = lax.broadcasted_iota(jnp.int32, (1, 256), 1)
    rc = 64
    for c in range(ts // rc):
        r0 = c * rc
        acc = jnp.zeros((rc, 256), F32)
        for j in range(CONV_WIDTH):
            a, r = divmod(1 + j, 8)
            lo = r0 + 8 * a
            tap = hc_ref[lo:lo + rc, :] if r == 0 else sh_ref[r - 1, lo:lo + rc, :]
            acc = acc + tap * dww_ref[j:j + 1, :]
        hcv = _layernorm(acc + dwb_ref[...], clg_ref[...], clb_ref[...])
        hcv = hcv * jax.nn.sigmoid(hcv)
        conv = jnp.dot(hcv.astype(BF16), cpw_ref[...], preferred_element_type=F32)
        o_ref[0, r0:r0 + rc, 0:256] = _group_norm_store(conv, gn_ref[:, 0:256])

        s0 = r0 + HALO
        hcur = hp_ref[s0:s0 + rc, :]
        s2 = c2_ref[s0 - 1:s0 - 1 + rc, :]
        s4 = c4_ref[s0 - 2:s0 - 2 + rc, :]
        s8 = c8_ref[s0 - 4:s0 - 4 + rc, :]
        s16 = c8_ref[s0 - 8:s0 - 8 + rc, :] + c8_ref[s0:s0 + rc, :]
        wsum = jnp.where(lane < 64, s2, jnp.where(lane < 128, s4, jnp.where(lane < 192, s8, s16)))
        half = jnp.where(lane < 64, 1, jnp.where(lane < 128, 2, jnp.where(lane < 192, 4, 8)))
        t = i * ts + r0 + lax.broadcasted_iota(jnp.int32, (rc, 1), 0)
        cnt = jnp.clip(t + half, 0, seq_len) - jnp.clip(t - half, 0, seq_len)
        y = wsum / cnt.astype(F32) - hcur
        pool = jnp.dot(y.astype(BF16), pw_ref[...], preferred_element_type=F32) * ps_ref[...]
        o_ref[0, r0:r0 + rc, 512:768] = _group_norm_store(pool, gn_ref[:, 512:768])

    for c in range(ts // CHUNK):
        r0 = c * CHUNK
        u = cur_ref[0, r0:r0 + CHUNK, 512:768]
        v = _layernorm(cur_ref[0, r0:r0 + CHUNK, 768:1024], glg_ref[...], glb_ref[...])
        r = jnp.dot(gws_ref[...], v.astype(BF16), preferred_element_type=F32)
        mixed = jnp.where(lane < 64, r[0:CHUNK],
                          jnp.where(lane < 128, r[CHUNK:2 * CHUNK],
                                    jnp.where(lane < 192, r[2 * CHUNK:3 * CHUNK], r[3 * CHUNK:])))
        gm = u * (mixed + gbs_ref[...])
        o_ref[0, r0:r0 + CHUNK, 256:512] = _group_norm_store(gm, gn_ref[:, 256:512])


def _local_mixers(mix, lp):
    b, L, w = mix.shape
    ts = min(512, L)
    nt = L // ts
    hb = ts // HALO
    last_h = L // HALO - 1
    full = lambda shape: pl.BlockSpec(shape, lambda bb, i: (0,) * len(shape))
    params = [lp["dw_w"], lp["dw_b"], lp["cln_g"], lp["cln_b"], lp["cpw"], lp["gln_g"], lp["gln_b"],
              lp["gws"], lp["gbs"], lp["pw"], lp["ps"], lp["gn_local"]]
    return pl.pallas_call(
        functools.partial(_mix_kernel, ts=ts, seq_len=L),
        grid=(b, nt),
        in_specs=[pl.BlockSpec((1, ts, w), lambda bb, i: (bb, i, 0)),
                  pl.BlockSpec((1, HALO, w), lambda bb, i: (bb, jnp.maximum(i * hb - 1, 0), 0)),
                  pl.BlockSpec((1, HALO, w), lambda bb, i: (bb, jnp.minimum((i + 1) * hb, last_h), 0))]
                 + [full(p.shape) for p in params],
        out_specs=pl.BlockSpec((1, ts, LOCAL_W), lambda bb, i: (bb, i, 0)),
        out_shape=jax.ShapeDtypeStruct((b, L, LOCAL_W), BF16),
        scratch_shapes=[pltpu.VMEM((ts + 2 * HALO, 256), F32), pltpu.VMEM((7, ts + 24, 256), F32)]
                       + [pltpu.VMEM((ts + 3 * HALO, 256), F32)] * 4,
        compiler_params=_cparams(("arbitrary", "arbitrary")),
        name="local_mixers",
    )(mix, mix, mix, *params)


def _merge_kernel(*refs, n_alias):
    (att_ref, loc_ref, x_ref, g1_ref, sc_ref, sh_ref, n2g_ref, woa_ref, wol_ref,
     rw_ref, rb_ref, ltri_ref, cnt_in_ref) = refs[:13]
    xo_ref, h2_ref, gate_ref, ir_ref, cnt_ref, run_ref = refs[13 + n_alias:]

    @pl.when(jnp.logical_and(pl.program_id(0) == 0, pl.program_id(1) == 0))
    def _():
        run_ref[...] = cnt_in_ref[...]

    tm = x_ref.shape[1]
    sub = ltri_ref.shape[0]
    lane = lax.broadcasted_iota(jnp.int32, (sub, ROUTE_W), 1).astype(F32)
    for s in range(tm // sub):
        r0 = s * sub
        y = (jnp.dot(att_ref[0, r0:r0 + sub, :], woa_ref[...], preferred_element_type=F32)
             + jnp.dot(loc_ref[0, r0:r0 + sub, :], wol_ref[...], preferred_element_type=F32))
        xn = x_ref[0, r0:r0 + sub, :] + g1_ref[0] * y
        xo_ref[0, r0:r0 + sub, :] = xn
        h2 = _modulated_rmsnorm(xn, n2g_ref[...], sc_ref[0], sh_ref[0]).astype(BF16)
        h2_ref[r0:r0 + sub, :] = h2
        logits = jnp.dot(h2, rw_ref[...], preferred_element_type=F32) + rb_ref[...]
        vals = jnp.zeros(logits.shape, F32)
        idxs = jnp.zeros(logits.shape, F32)
        hits = jnp.zeros(logits.shape, F32)
        sels = []
        top = None
        for k in range(TOP_K):
            m = jnp.max(logits, axis=-1, keepdims=True)
            sel = jnp.min(jnp.where(logits == m, lane, float(ROUTE_W)), axis=-1, keepdims=True)
            if top is None:
                top = m
            chosen = lane == sel
            vals = jnp.where(lane == float(k), jnp.exp(m - top), vals)
            idxs = jnp.where(lane == float(k), sel, idxs)
            hits = jnp.where(chosen, 1.0, hits)
            logits = jnp.where(chosen, NEG_INF * 2.0, logits)
            sels.append(chosen)
        gate_ref[r0:r0 + sub, :] = vals / jnp.sum(vals, axis=-1, keepdims=True)
        base = run_ref[...] + jnp.dot(ltri_ref[...], hits.astype(BF16), preferred_element_type=F32)
        packed = idxs
        for k in range(TOP_K):
            rk = jnp.sum(jnp.where(sels[k], base, 0.0), axis=-1, keepdims=True)
            packed = jnp.where(lane == float(TOP_K + k), rk, packed)
        ir_ref[:, r0:r0 + sub] = packed.T[0:2 * TOP_K, :].astype(jnp.int32)
        run_ref[...] = run_ref[...] + jnp.sum(hits, axis=0, keepdims=True)
    cnt_ref[...] = run_ref[...]


def _merge(att, loc, x, xoff, g1, sc2, sh2, moff, n2g, woa, wol, rw, rb, cnt_in, bufs, row_off, t_total):
    b = att.shape[0]
    _, L, d = x.shape
    tm = min(512, L)
    nt = L // tm
    blk_off = row_off // tm
    bm = g1.shape[0]
    mod_map = (lambda bb, i: (moff + bb, 0, 0)) if bm > 1 else (lambda bb, i: (0, 0, 0))
    tile = lambda w: pl.BlockSpec((1, tm, w), lambda bb, i: (bb, i, 0))
    xtile = pl.BlockSpec((1, tm, d), lambda bb, i: (xoff + bb, i, 0))
    flat = lambda w: pl.BlockSpec((tm, w), lambda bb, i: (blk_off + bb * nt + i, 0))
    full = lambda shape: pl.BlockSpec(shape, lambda bb, i: (0,) * len(shape))
    sub = min(MERGE_SUB, tm)
    assert L % tm == 0 and tm % sub == 0 and row_off % tm == 0
    ltri = (np.arange(sub)[:, None] > np.arange(sub)[None, :]).astype(np.float32)
    ltri = jnp.asarray(ltri, BF16)
    n_alias = 0 if bufs is None else len(bufs)
    n_in = 13
    return pl.pallas_call(
        functools.partial(_merge_kernel, n_alias=n_alias),
        grid=(b, nt),
        in_specs=[tile(256), tile(LOCAL_W), xtile,
                  pl.BlockSpec((1, 1, d), mod_map), pl.BlockSpec((1, 1, d), mod_map),
                  pl.BlockSpec((1, 1, d), mod_map),
                  full((1, d)), full(woa.shape), full(wol.shape), full(rw.shape), full(rb.shape),
                  full((sub, sub)), full((1, ROUTE_W))]
                 + [pl.BlockSpec(memory_space=pl.ANY)] * n_alias,
        out_specs=(tile(d), flat(d), flat(ROUTE_W),
                   pl.BlockSpec((2 * TOP_K, tm), lambda bb, i: (0, blk_off + bb * nt + i)),
                   full((1, ROUTE_W))),
        out_shape=(jax.ShapeDtypeStruct((b, L, d), F32),
                   jax.ShapeDtypeStruct((t_total, d), BF16),
                   jax.ShapeDtypeStruct((t_total, ROUTE_W), F32),
                   jax.ShapeDtypeStruct((2 * TOP_K, t_total), jnp.int32),
                   jax.ShapeDtypeStruct((1, ROUTE_W), F32)),
        scratch_shapes=[pltpu.VMEM((1, ROUTE_W), F32)],
        input_output_aliases={n_in + j: 1 + j for j in range(n_alias)},
        compiler_params=_cparams(("arbitrary", "arbitrary")),
        name="merge_router",
    )(att, loc, x, g1, sc2, sh2, n2g.reshape(1, d), woa, wol, rw, rb, ltri, cnt_in,
      *(bufs if bufs is not None else ()))


def _moe_kernel(blk_e_ref, n_used_ref, x_ref, w1_ref, b1_ref, w2_ref, b2_ref, o_ref, w1s_ref, w2s_ref):
    i = pl.program_id(0)
    active = i < n_used_ref[0]
    new_expert = jnp.logical_or(i == 0, blk_e_ref[i] != blk_e_ref[jnp.maximum(i - 1, 0)])

    @pl.when(jnp.logical_and(active, new_expert))
    def _():
        rows = 128
        for r in range(D_MODEL // rows):
            w1s_ref[r * rows:(r + 1) * rows, :] = w1_ref[0, 0, r * rows:(r + 1) * rows, :].astype(BF16)
        for r in range(D_FF // rows):
            w2s_ref[r * rows:(r + 1) * rows, :] = w2_ref[0, 0, r * rows:(r + 1) * rows, :].astype(BF16)

    @pl.when(active)
    def _():
        x = x_ref[...]
        acc = jnp.zeros((x.shape[0], D_MODEL), F32)
        for c in range(D_FF // FF_CHUNK):
            lo = c * FF_CHUNK
            glu = jnp.dot(x, w1s_ref[:, lo:lo + FF_CHUNK], preferred_element_type=F32) \
                + b1_ref[0, 0, :, lo:lo + FF_CHUNK]
            lin = jnp.dot(x, w1s_ref[:, D_FF + lo:D_FF + lo + FF_CHUNK], preferred_element_type=F32) \
                + b1_ref[0, 0, :, D_FF + lo:D_FF + lo + FF_CHUNK]
            glu = jnp.minimum(glu, SWIGLU_LIMIT)
            lin = jnp.clip(lin, -SWIGLU_LIMIT, SWIGLU_LIMIT)
            act = glu * jax.nn.sigmoid(SWIGLU_ALPHA * glu) * (lin + 1.0)
            acc = acc + jnp.dot(act.astype(BF16), w2s_ref[lo:lo + FF_CHUNK, :], preferred_element_type=F32)
        o_ref[...] = (acc + b2_ref[0, 0]).astype(o_ref.dtype)

    @pl.when(i >= n_used_ref[0])
    def _():
        o_ref[...] = jnp.zeros(o_ref.shape, o_ref.dtype)


def _moe_ffn(xp, blk_e, n_used, w1, b1, w2, b2, layer):
    n_rows, d = xp.shape
    n_blk = n_rows // MOE_TM
    depth, e = w1.shape[:2]

    def row_map(i, be, nu):
        return (jnp.maximum(jnp.minimum(i, nu[0] - 1), 0), 0)

    def w_map(i, be, nu):
        return (layer, be[i], 0, 0)

    return pl.pallas_call(
        _moe_kernel,
        grid_spec=pltpu.PrefetchScalarGridSpec(
            num_scalar_prefetch=2,
            grid=(n_blk,),
            in_specs=[pl.BlockSpec((MOE_TM, d), row_map),
                      pl.BlockSpec((1, 1, d, 2 * D_FF), w_map),
                      pl.BlockSpec((1, 1, 1, 2 * D_FF), w_map),
                      pl.BlockSpec((1, 1, D_FF, d), w_map),
                      pl.BlockSpec((1, 1, 1, d), w_map)],
            out_specs=pl.BlockSpec((MOE_TM, d), lambda i, be, nu: (i, 0)),
            scratch_shapes=[pltpu.VMEM((d, 2 * D_FF), BF16), pltpu.VMEM((D_FF, d), BF16)]),
        out_shape=jax.ShapeDtypeStruct((n_rows, d), BF16),
        compiler_params=_cparams(("arbitrary",)),
        name="moe_ffn",
    )(blk_e, n_used, xp, w1, b1.reshape(depth, e, 1, 2 * D_FF), w2, b2.reshape(depth, e, 1, d))


def _combine_kernel(*refs, final):
    yg_ref, gate_ref, x_ref, g2_ref, fg_ref = refs[:5]
    o_ref = refs[-1]
    gates = gate_ref[...]
    ff = jnp.zeros(x_ref.shape[1:], F32)
    for k in range(TOP_K):
        ff = ff + gates[:, k:k + 1] * yg_ref[k, 0].astype(F32)
    xo = x_ref[0] + g2_ref[0] * ff
    if final:
        ms = jnp.mean(xo * xo, axis=-1, keepdims=True)
        xo = xo * lax.rsqrt(ms + EPS) * fg_ref[...]
    o_ref[0] = xo


def _combine(yg, gates, x, g2, moff, fg, row_off, *, final, out_buf=None, ooff=0, out_b=None):
    b, L, d = x.shape
    tm = min(256, L)
    nt = L // tm
    blk_off = row_off // tm
    bm = g2.shape[0]
    out_b = b if out_b is None else out_b
    mod_map = (lambda bb, i: (moff + bb, 0, 0)) if bm > 1 else (lambda bb, i: (0, 0, 0))
    extra = () if out_buf is None else (out_buf,)
    return pl.pallas_call(
        functools.partial(_combine_kernel, final=final),
        grid=(b, nt),
        in_specs=[pl.BlockSpec((TOP_K, 1, tm, d), lambda bb, i: (0, bb, i, 0)),
                  pl.BlockSpec((tm, ROUTE_W), lambda bb, i: (blk_off + bb * nt + i, 0)),
                  pl.BlockSpec((1, tm, d), lambda bb, i: (bb, i, 0)),
                  pl.BlockSpec((1, 1, d), mod_map),
                  pl.BlockSpec((1, d), lambda bb, i: (0, 0))]
                 + [pl.BlockSpec(memory_space=pl.ANY)] * len(extra),
        out_specs=pl.BlockSpec((1, tm, d), lambda bb, i: (ooff + bb, i, 0)),
        out_shape=jax.ShapeDtypeStruct((out_b, L, d), F32),
        input_output_aliases={5: 0} if extra else {},
        compiler_params=_cparams(("arbitrary", "arbitrary")),
        name="combine",
    )(yg, gates, x, g2, fg.reshape(1, d), *extra)


def _dispatch_plan(idx, rank, counts):
    n_tok = idx.shape[1]
    n_asg = n_tok * TOP_K
    padded = (counts + MOE_TM - 1) // MOE_TM * MOE_TM
    pad_end = jnp.cumsum(padded)
    pad_start = (pad_end - padded).astype(jnp.int32)
    pos = rank
    for e in range(N_EXPERTS):
        pos = pos + jnp.where(idx == e, pad_start[e], 0)
    n_rows = -(-n_asg // MOE_TM) * MOE_TM + N_EXPERTS * MOE_TM
    n_blk = n_rows // MOE_TM
    tok = jnp.broadcast_to(jnp.arange(n_tok, dtype=jnp.int32)[None, :], (TOP_K, n_tok))
    fill = jnp.arange(n_rows, dtype=jnp.int32) % n_tok
    row_tok = fill.at[pos.reshape(-1)].add((tok - pos % n_tok).reshape(-1), mode="promise_in_bounds")
    blk_start = jnp.arange(n_blk, dtype=jnp.int32) * MOE_TM
    blk_e = jnp.minimum(jnp.sum((pad_end[None, :] <= blk_start[:, None]).astype(jnp.int32), axis=1),
                        N_EXPERTS - 1)
    n_used = (pad_end[-1] // MOE_TM).astype(jnp.int32).reshape(1)
    return row_tok, pos, blk_e, n_used


_Q_PERM = np.concatenate([np.arange(0, 64), np.arange(128, 192), np.arange(64, 128), np.arange(192, 256)])


def _rope_tables(L):
    t = jnp.arange(L)
    row = (t // GRID_W).astype(F32)
    col = (t % GRID_W).astype(F32)
    half = HEAD_DIM // 2
    inv = ROPE_BASE ** (-jnp.arange(0, half, 2, dtype=F32) / half)
    ar, ac = row[:, None] * inv[None, :], col[:, None] * inv[None, :]
    z = jnp.zeros_like(ar)
    cos = jnp.concatenate([jnp.cos(ar), jnp.cos(ar), jnp.cos(ac), jnp.cos(ac)], axis=1)
    s1 = jnp.concatenate([-jnp.sin(ar), z, -jnp.sin(ac), z], axis=1)
    s2 = jnp.concatenate([z, jnp.sin(ar), z, jnp.sin(ac)], axis=1)
    rep = lambda a: jnp.concatenate([a, a], axis=1)
    return rep(cos), rep(s1), rep(s2)


def _block_diag(pw):
    g, n, _ = pw.shape
    out = jnp.zeros((g * n, g * n), pw.dtype)
    for gi in range(g):
        out = out.at[gi * n:(gi + 1) * n, gi * n:(gi + 1) * n].set(pw[gi])
    return out


def kernel(x, c, ctx, c_ctx, ada_w, ada_b, norm1_g, norm2_g, w_in, attn_sink, conv_dw_w, conv_dw_b, conv_ln_g, conv_ln_b, conv_pw_w, gmlp_ln_g, gmlp_ln_b, gmlp_ws, gmlp_bs, pool_w, pool_scale, group_norm_g, w_out, router_w, router_b, exp_w1, exp_b1, exp_w2, exp_b2, final_norm_g):
    b, L, d = x.shape
    lc = ctx.shape[1]
    depth = ada_w.shape[0]
    tabs = _rope_tables(L)

    r = -(-(b + 1) // 8) * 8
    cond = jnp.zeros((r, d), F32).at[:b].set(c).at[b].set(c_ctx)
    mods = _adaln(cond, ada_w, ada_b).reshape(depth, r, 6, d)

    n_groups = BATCH_GROUPS if b % BATCH_GROUPS == 0 else 1
    bh = b // n_groups
    xs, cs = [x] * n_groups, [ctx] * n_groups
    pending = [None] * n_groups

    def finish(st, out_buf):
        if st["last"]:
            return _combine(st["ygx"], st["gates"], st["xh"], st["g2"], st["moff"], final_norm_g, 0,
                            final=True, out_buf=out_buf, ooff=st["moff"], out_b=b)
        x_new = _combine(st["ygx"], st["gates"], st["xh"], st["g2"], st["moff"], final_norm_g, 0, final=False)
        c_new = _combine(st["ygc"], st["gates"], st["ch"], st["g2c"], 0, final_norm_g, bh * L, final=False)
        return x_new, c_new

    for l in range(depth):
        last = l == depth - 1
        mx = [mods[l, :b, i].reshape(b, 1, d) for i in range(6)]
        mc = [mods[l, b:b + 1, i].reshape(1, 1, d) for i in range(6)]
        sh1, sc1, g1, sh2, sc2, g2 = mx
        sh1c, sc1c, g1c, sh2c, sc2c, g2c = mc

        wl = w_in[l]
        wq = wl[:, :OFF_K][:, _Q_PERM] * (HEAD_DIM ** -0.5)
        w_full = jnp.concatenate([wq, wl[:, OFF_K:]], axis=1).astype(BF16)
        gn = group_norm_g[l]
        g_att = gn[:256][_Q_PERM]
        wo = w_out[l]
        woa = wo[:256][_Q_PERM].astype(BF16)
        wol = wo[256:].astype(BF16)
        lp = dict(
            dw_w=conv_dw_w[l], dw_b=conv_dw_b[l].reshape(1, -1),
            cln_g=conv_ln_g[l].reshape(1, -1), cln_b=conv_ln_b[l].reshape(1, -1),
            cpw=conv_pw_w[l].astype(BF16),
            gln_g=gmlp_ln_g[l].reshape(1, -1), gln_b=gmlp_ln_b[l].reshape(1, -1),
            gws=gmlp_ws[l].reshape(GMLP_HEADS * CHUNK, CHUNK).astype(BF16),
            gbs=jnp.repeat(gmlp_bs[l].T, GROUP_W // GMLP_HEADS, axis=1),
            pw=_block_diag(pool_w[l]).astype(BF16), ps=pool_scale[l].reshape(1, -1),
            gn_local=gn[256:].reshape(1, -1))
        rw = jnp.zeros((d, ROUTE_W), F32).at[:, :N_EXPERTS].set(router_w[l]).astype(BF16)
        rb = jnp.full((1, ROUTE_W), NEG_INF, F32).at[0, :N_EXPERTS].set(router_b[l])

        w_kv = wl[:, OFF_K:OFF_CONV].astype(BF16)
        take = lambda a, i: a.at[i].get(mode="promise_in_bounds")
        staged = []
        for h in range(n_groups):
            moff = h * bh
            pd = pending[h]
            xoff = moff if l == 0 else 0

            if pd is None:
                xh, ch = xs[h], cs[h]
                q, kv, mix = _inproj(xh, xoff, bh, sc1, sh1, moff, norm1_g[l], w_full, tabs, full=True)
                if last:
                    (kvc,) = _inproj(ch, xoff, bh, sc1c, sh1c, 0, norm1_g[l], w_kv, None, full=False)
                else:
                    qc, kvc, mixc = _inproj(ch, xoff, bh, sc1c, sh1c, 0, norm1_g[l], w_full, None, full=True)
            else:
                if staged:
                    pd["ygx"], staged[-1]["xh"] = lax.optimization_barrier((pd["ygx"], staged[-1]["xh"]))
                combx = (pd["ygx"], pd["gates"], 0, pd["g2"], pd["moff"])
                combc = (pd["ygc"], pd["gates"], bh * L, pd["g2c"], 0)
                xh, q, kv, mix = _inproj(pd["xh"], 0, bh, sc1, sh1, moff, norm1_g[l], w_full, tabs,
                                         full=True, comb=combx)
                if last:
                    ch = None
                    (kvc,) = _inproj(pd["ch"], 0, bh, sc1c, sh1c, 0, norm1_g[l], w_kv, None,
                                     full=False, comb=combc, emit_x=False)
                else:
                    ch, qc, kvc, mixc = _inproj(pd["ch"], 0, bh, sc1c, sh1c, 0, norm1_g[l], w_full, None,
                                                full=True, comb=combc)
                xoff = 0
            att = _attention(q, kv, kvc, attn_sink[l], g_att, local=True)
            loc = _local_mixers(mix, lp)
            if staged:
                prev = staged[-1]
                prev["xp"], loc = lax.optimization_barrier((prev["xp"], loc))
            nxt = pending[h + 1] if h + 1 < n_groups else None
            if nxt is not None:
                nxt["ygx"], loc = lax.optimization_barrier((nxt["ygx"], loc))
            t_total = bh * L if last else bh * (L + lc)
            cnt0 = jnp.zeros((1, ROUTE_W), F32)
            xh, h2, gates, ir, cnt = _merge(att, loc, xh, xoff, g1, sc2, sh2, moff, norm2_g[l], woa, wol,
                                            rw, rb, cnt0, None, 0, t_total)
            if not last:
                attc = _attention(qc, kvc, kvc, attn_sink[l], g_att, local=False)
                locc = _local_mixers(mixc, lp)
                ch, h2, gates, ir, cnt = _merge(attc, locc, ch, xoff, g1c, sc2c, sh2c, 0, norm2_g[l],
                                                woa, wol, rw, rb, cnt, (h2, gates, ir), bh * L, t_total)
            counts = cnt[0, :N_EXPERTS].astype(jnp.int32)
            row_tok, pos, blk_e, n_used = _dispatch_plan(ir[:TOP_K], ir[TOP_K:], counts)
            staged.append(dict(xh=xh, ch=ch, gates=gates, pos=pos, blk_e=blk_e, n_used=n_used,
                               xp=take(h2, row_tok), moff=moff, g2=g2, g2c=g2c, last=last))

        yps = []
        for h, st in enumerate(staged):
            if yps:
                yps[-1], st["xp"] = lax.optimization_barrier((yps[-1], st["xp"]))
            yps.append(_moe_ffn(st["xp"], st["blk_e"], st["n_used"], exp_w1, exp_b1, exp_w2, exp_b2, l))
        for h, st in enumerate(staged):
            if h == n_groups - 1 and h > 0:
                yps[h], staged[0]["ygx"] = lax.optimization_barrier((yps[h], staged[0]["ygx"]))
            pos = st["pos"]
            st["ygx"] = take(yps[h], pos[:, :bh * L]).reshape(TOP_K, bh, L, d)
            if not last:
                st["ygc"] = take(yps[h], pos[:, bh * L:]).reshape(TOP_K, bh, lc, d)
            pending[h] = st

    out = None
    for h in range(n_groups):
        out = finish(pending[h], out)
    return out
```

```python
import functools

import jax
import jax.numpy as jnp
import numpy as np
from jax import lax
from jax.experimental import pallas as pl
from jax.experimental.pallas import tpu as pltpu

F32 = jnp.float32
BF16 = jnp.bfloat16

D_MODEL = 1024
GRID_W = 64
GROUP_W = 256
HEAD_DIM = 64
N_Q_HEADS = 4
N_KV_HEADS = 2
WINDOW = 128
ATT_BLOCK = 128
ROPE_BASE = 10000.0
CONV_WIDTH = 31
CHUNK = 128
GMLP_HEADS = 4
POOL_WINDOWS = (2, 4, 8, 16)
N_EXPERTS = 32
TOP_K = 4
D_FF = 1024
SWIGLU_ALPHA = 1.702
SWIGLU_LIMIT = 7.0
EPS = 1e-6
NEG_INF = -1e30

OFF_K = 256
OFF_V = 384
OFF_CONV = 512
IN_W = 1792
MIX_IN_W = IN_W - OFF_CONV
LOCAL_W = 3 * GROUP_W

LANES = 128
HALO = 16
ROUTE_W = LANES
MOE_TM = 512
FF_CHUNK = 512
MERGE_SUB = 512
BATCH_GROUPS = 2
VMEM_LIMIT = 56 * 1024 * 1024


def _cparams(sem):
    return pltpu.CompilerParams(dimension_semantics=sem, vmem_limit_bytes=VMEM_LIMIT)


def _ada_kernel(c_ref, w_ref, b_ref, o_ref):
    c = c_ref[...]
    s = c * jax.nn.sigmoid(c)
    o_ref[0] = jnp.dot(s.astype(BF16), w_ref[0].astype(BF16), preferred_element_type=F32) + b_ref[0]


def _adaln(cond, ada_w, ada_b):
    depth, d, n = ada_w.shape
    r = cond.shape[0]
    tn = 1536
    return pl.pallas_call(
        _ada_kernel,
        grid=(depth, n // tn),
        in_specs=[pl.BlockSpec((r, d), lambda l, j: (0, 0)),
                  pl.BlockSpec((1, d, tn), lambda l, j: (l, 0, j)),
                  pl.BlockSpec((1, 1, tn), lambda l, j: (l, 0, j))],
        out_specs=pl.BlockSpec((1, r, tn), lambda l, j: (l, 0, j)),
        out_shape=jax.ShapeDtypeStruct((depth, r, n), F32),
        compiler_params=_cparams(("arbitrary", "arbitrary")),
        name="adaln",
    )(cond, ada_w, ada_b.reshape(depth, 1, n))


def _modulated_rmsnorm(x, g, sc, sh):
    ms = jnp.mean(x * x, axis=-1, keepdims=True)
    return x * lax.rsqrt(ms + EPS) * (g * (1.0 + sc)) + sh


def _weighted_expert_sum(yg_ref, gates):
    ff = gates[:, 0:1] * yg_ref[0, 0].astype(F32)
    for k in range(1, TOP_K):
        ff = ff + gates[:, k:k + 1] * yg_ref[k, 0].astype(F32)
    return ff


def _inproj_kernel(*refs, rope, full, fused, emit_x):
    refs = list(refs)
    if fused:
        yg_ref, gate_ref, g2_ref = refs[:3]
        refs = refs[3:]
    if rope:
        x_ref, sc_ref, sh_ref, g_ref, w_ref, cos_ref, s1_ref, s2_ref = refs[:8]
        outs = refs[8:]
    else:
        x_ref, sc_ref, sh_ref, g_ref, w_ref = refs[:5]
        outs = refs[5:]
    x = x_ref[0]
    if fused:
        x = x + g2_ref[0] * _weighted_expert_sum(yg_ref, gate_ref[...])
        if emit_x:
            outs[0][0] = x
            outs = outs[1:]
    h = _modulated_rmsnorm(x, g_ref[...], sc_ref[0], sh_ref[0])
    p = jnp.dot(h.astype(BF16), w_ref[...], preferred_element_type=F32)

    def roped(xs):
        if not rope:
            return xs
        return (xs * cos_ref[...] + pltpu.roll(xs, LANES - 16, 1) * s1_ref[...]
                + pltpu.roll(xs, 16, 1) * s2_ref[...])

    if full:
        q_ref, kv_ref, mix_ref = outs
        q_ref[0] = jnp.concatenate([roped(p[:, 0:128]), roped(p[:, 128:256])], axis=1).astype(BF16)
        kv_ref[0] = jnp.concatenate([roped(p[:, 256:384]), p[:, 384:512]], axis=1).astype(BF16)
        mix_ref[0] = p[:, OFF_CONV:]
    else:
        (kv_ref,) = outs
        kv_ref[0] = p.astype(BF16)


def _inproj(x, xoff, b, sc, sh, moff, g, w, tabs, *, full, comb=None, emit_x=True):
    _, L, d = x.shape
    n = w.shape[1]
    tm = min(512, L)
    nt = L // tm
    assert L % tm == 0
    rope = tabs is not None
    bm = sc.shape[0]
    mod_map = (lambda i, bb: (moff + bb, 0, 0)) if bm > 1 else (lambda i, bb: (0, 0, 0))
    in_specs, args = [], []
    if comb is not None:
        yg, gates, row_off, g2, g2off = comb
        assert row_off % tm == 0
        blk_off = row_off // tm
        g2_map = (lambda i, bb: (g2off + bb, 0, 0)) if g2.shape[0] > 1 else (lambda i, bb: (0, 0, 0))
        in_specs += [pl.BlockSpec((TOP_K, 1, tm, d), lambda i, bb: (0, bb, i, 0)),
                     pl.BlockSpec((tm, ROUTE_W), lambda i, bb: (blk_off + bb * nt + i, 0)),
                     pl.BlockSpec((1, 1, d), g2_map)]
        args += [yg, gates, g2]
    in_specs += [pl.BlockSpec((1, tm, d), lambda i, bb: (xoff + bb, i, 0)),
                 pl.BlockSpec((1, 1, d), mod_map),
                 pl.BlockSpec((1, 1, d), mod_map),
                 pl.BlockSpec((1, d), lambda i, bb: (0, 0)),
                 pl.BlockSpec((d, n), lambda i, bb: (0, 0))]
    args += [x, sc, sh, g.reshape(1, d), w]
    if rope:
        in_specs += [pl.BlockSpec((tm, LANES), lambda i, bb: (i, 0))] * 3
        args += list(tabs)
    tile = lambda wd: pl.BlockSpec((1, tm, wd), lambda i, bb: (bb, i, 0))
    out_shape, out_specs = [], []
    emit_x = emit_x and comb is not None
    if emit_x:
        out_shape.append(jax.ShapeDtypeStruct((b, L, d), F32))
        out_specs.append(tile(d))
    if full:
        out_shape += [jax.ShapeDtypeStruct((b, L, 256), BF16),
                      jax.ShapeDtypeStruct((b, L, 256), BF16),
                      jax.ShapeDtypeStruct((b, L, MIX_IN_W), F32)]
        out_specs += [tile(256), tile(256), tile(MIX_IN_W)]
    else:
        out_shape.append(jax.ShapeDtypeStruct((b, L, 256), BF16))
        out_specs.append(tile(256))
    return pl.pallas_call(
        functools.partial(_inproj_kernel, rope=rope, full=full, fused=comb is not None, emit_x=emit_x),
        grid=(nt, b),
        in_specs=in_specs, out_specs=tuple(out_specs), out_shape=tuple(out_shape),
        compiler_params=_cparams(("arbitrary", "arbitrary")),
        name="inproj_x" if rope else "inproj_ctx",
    )(*args)


def _attn_kernel(*refs, local, seq_len, q_blocks):
    if local:
        sink_ref, q_ref, kv_ref, kvc_ref, g_ref, bias_ref, o_ref = refs
    else:
        sink_ref, q_ref, kv_ref, kvc_ref, g_ref, o_ref = refs
    step = pl.program_id(1)
    lane = lax.broadcasted_iota(jnp.int32, (1, LANES), 1)
    row = lax.broadcasted_iota(jnp.int32, (2 * ATT_BLOCK, 1), 0)
    kvc = kvc_ref[0]
    kc, vc = kvc[:, :LANES], kvc[:, LANES:]
    lc = kvc.shape[0]
    nt = (((1,), (1,)), ((), ()))
    zero = jnp.zeros((), BF16)
    lm0 = lane < HEAD_DIM
    for sb in range(q_blocks):
        n = step * q_blocks + sb
        q = q_ref[0, sb * ATT_BLOCK:(sb + 1) * ATT_BLOCK, :]
        if local:
            blk0 = jnp.clip(n - 1, 0, seq_len // ATT_BLOCK - 3)
            kw = kv_ref[0, pl.ds(pl.multiple_of(blk0 * ATT_BLOCK, ATT_BLOCK), 3 * ATT_BLOCK), :]
            kl, vl = kw[:, :LANES], kw[:, LANES:]
            bias = bias_ref[n - blk0]
        outs = []
        for kh in range(N_KV_HEADS):
            lm = lm0 if kh == 0 else jnp.logical_not(lm0)
            lhs = jnp.concatenate([jnp.where(lm, q[:, :LANES], zero), jnp.where(lm, q[:, LANES:], zero)],
                                  axis=0)
            sk = jnp.where(row < ATT_BLOCK, sink_ref[2 * kh], sink_ref[2 * kh + 1])
            s = lax.dot_general(lhs, kc, nt, preferred_element_type=F32)
            if local:
                s_l = lax.dot_general(lhs, kl, nt, preferred_element_type=F32) + bias
                s = jnp.concatenate([s, s_l], axis=1)
            m = jnp.maximum(sk, jnp.max(s, axis=-1, keepdims=True))
            e = jnp.exp(s - m)
            den = jnp.exp(sk - m) + jnp.sum(e, axis=-1, keepdims=True)
            e = e.astype(BF16)
            o = jnp.dot(e[:, :lc], vc, preferred_element_type=F32)
            if local:
                o = o + jnp.dot(e[:, lc:], vl, preferred_element_type=F32)
            outs.append(o / den)
        o01, o23 = outs
        att = jnp.concatenate([jnp.where(lm0, o01[:ATT_BLOCK], o23[:ATT_BLOCK]),
                               jnp.where(lm0, o01[ATT_BLOCK:], o23[ATT_BLOCK:])], axis=1)
        ms = jnp.mean(att * att, axis=-1, keepdims=True)
        o_ref[0, sb * ATT_BLOCK:(sb + 1) * ATT_BLOCK, :] = (att * lax.rsqrt(ms + EPS) * g_ref[...]).astype(BF16)


def _band_bias():
    i = np.arange(2 * ATT_BLOCK)[None, :, None] % ATT_BLOCK
    j = np.arange(3 * ATT_BLOCK)[None, None, :]
    v = np.arange(3)[:, None, None]
    return jnp.asarray(np.where(np.abs(j - i - ATT_BLOCK * v) <= WINDOW, 0.0, NEG_INF), F32)


def _attention(q, kv, kvc, sink, g_att, *, local):
    b, L, _ = q.shape
    lc = kvc.shape[1]
    nb = L // ATT_BLOCK
    assert not local or nb >= 3
    q_blocks = 2 if nb % 2 == 0 else 1
    tq = q_blocks * ATT_BLOCK
    in_specs = [pl.BlockSpec(memory_space=pltpu.SMEM),
                pl.BlockSpec((1, tq, 256), lambda bb, i: (bb, i, 0)),
                pl.BlockSpec((1, kv.shape[1], 256), lambda bb, i: (bb, 0, 0)),
                pl.BlockSpec((1, lc, 256), lambda bb, i: (bb, 0, 0)),
                pl.BlockSpec((1, 256), lambda bb, i: (0, 0))]
    args = [sink, q, kv, kvc, g_att.reshape(1, 256)]
    if local:
        in_specs.append(pl.BlockSpec((3, 2 * ATT_BLOCK, 3 * ATT_BLOCK), lambda bb, i: (0, 0, 0)))
        args.append(_band_bias())
    return pl.pallas_call(
        functools.partial(_attn_kernel, local=local, seq_len=L, q_blocks=q_blocks),
        grid=(b, nb // q_blocks),
        in_specs=in_specs,
        out_specs=pl.BlockSpec((1, tq, 256), lambda bb, i: (bb, i, 0)),
        out_shape=jax.ShapeDtypeStruct((b, L, 256), BF16),
        compiler_params=_cparams(("arbitrary", "arbitrary")),
        name="attn_window" if local else "attn_ctx",
    )(*args)


def _layernorm(x, g, b):
    mu = jnp.mean(x, axis=-1, keepdims=True)
    xc = x - mu
    return xc * lax.rsqrt(jnp.mean(xc * xc, axis=-1, keepdims=True) + EPS) * g + b


def _group_norm_store(y, g):
    ms = jnp.mean(y * y, axis=-1, keepdims=True)
    return (y * lax.rsqrt(ms + EPS) * g).astype(BF16)


def _mix_kernel(cur_ref, prev_ref, next_ref, dww_ref, dwb_ref, clg_ref, clb_ref, cpw_ref,
                glg_ref, glb_ref, gws_ref, gbs_ref, pw_ref, ps_ref, gn_ref, o_ref,
                hc_ref, sh_ref, hp_ref, c2_ref, c4_ref, c8_ref, *, ts, seq_len):
    i = pl.program_id(1)
    nt = pl.num_programs(1)
    has_prev = (i > 0).astype(F32)
    has_next = (i < nt - 1).astype(F32)

    def glu(blk):
        return blk[:, 0:256] * jax.nn.sigmoid(blk[:, 256:512])

    pv = prev_ref[0]
    nx = next_ref[0]
    hc_ref[0:HALO, :] = glu(pv) * has_prev
    hc_ref[HALO + ts:2 * HALO + ts, :] = glu(nx) * has_next
    hp_ref[0:HALO, :] = pv[:, 1024:1280] * has_prev
    hp_ref[HALO + ts:2 * HALO + ts, :] = nx[:, 1024:1280] * has_next
    hp_ref[2 * HALO + ts:3 * HALO + ts, :] = jnp.zeros((HALO, 256), F32)
    hc_ref[HALO:HALO + ts, :] = glu(cur_ref[0, :, 0:512])
    hp_ref[HALO:HALO + ts, :] = cur_ref[0, :, 1024:1280]

    for r in range(1, 8):
        sh_ref[r - 1] = hc_ref[r:r + ts + 24, :]
    c2_ref[8:ts + 40, :] = hp_ref[8:ts + 40, :] + hp_ref[9:ts + 41, :]
    c4_ref[8:ts + 32, :] = c2_ref[8:ts + 32, :] + c2_ref[10:ts + 34, :]
    c8_ref[8:ts + 24, :] = c4_ref[8:ts + 24, :] + c4_ref[12:ts + 28, :]

    lane = lax.broadcasted_iota(jnp.int32, (1, 256), 1)
    rc = 64
    for c in range(ts // rc):
        r0 = c * rc
        acc = jnp.zeros((rc, 256), F32)
        for j in range(CONV_WIDTH):
            a, r = divmod(1 + j, 8)
            lo = r0 + 8 * a
            tap = hc_ref[lo:lo + rc, :] if r == 0 else sh_ref[r - 1, lo:lo + rc, :]
            acc = acc + tap * dww_ref[j:j + 1, :]
        hcv = _layernorm(acc + dwb_ref[...], clg_ref[...], clb_ref[...])
        hcv = hcv * jax.nn.sigmoid(hcv)
        conv = jnp.dot(hcv.astype(BF16), cpw_ref[...], preferred_element_type=F32)
        o_ref[0, r0:r0 + rc, 0:256] = _group_norm_store(conv, gn_ref[:, 0:256])

        s0 = r0 + HALO
        hcur = hp_ref[s0:s0 + rc, :]
        s2 = c2_ref[s0 - 1:s0 - 1 + rc, :]
        s4 = c4_ref[s0 - 2:s0 - 2 + rc, :]
        s8 = c8_ref[s0 - 4:s0 - 4 + rc, :]
        s16 = c8_ref[s0 - 8:s0 - 8 + rc, :] + c8_ref[s0:s0 + rc, :]
        wsum = jnp.where(lane < 64, s2, jnp.where(lane < 128, s4, jnp.where(lane < 192, s8, s16)))
        half = jnp.where(lane < 64, 1, jnp.where(lane < 128, 2, jnp.where(lane < 192, 4, 8)))
        t = i * ts + r0 + lax.broadcasted_iota(jnp.int32, (rc, 1), 0)
        cnt = jnp.clip(t + half, 0, seq_len) - jnp.clip(t - half, 0, seq_len)
        y = wsum / cnt.astype(F32) - hcur
        pool = jnp.dot(y.astype(BF16), pw_ref[...], preferred_element_type=F32) * ps_ref[...]
        o_ref[0, r0:r0 + rc, 512:768] = _group_norm_store(pool, gn_ref[:, 512:768])

    for c in range(ts // CHUNK):
        r0 = c * CHUNK
        u = cur_ref[0, r0:r0 + CHUNK, 512:768]
        v = _layernorm(cur_ref[0, r0:r0 + CHUNK, 768:1024], glg_ref[...], glb_ref[...])
        r = jnp.dot(gws_ref[...], v.astype(BF16), preferred_element_type=F32)
        mixed = jnp.where(lane < 64, r[0:CHUNK],
                          jnp.where(lane < 128, r[CHUNK:2 * CHUNK],
                                    jnp.where(lane < 192, r[2 * CHUNK:3 * CHUNK], r[3 * CHUNK:])))
        gm = u * (mixed + gbs_ref[...])
        o_ref[0, r0:r0 + CHUNK, 256:512] = _group_norm_store(gm, gn_ref[:, 256:512])


def _local_mixers(mix, lp):
    b, L, w = mix.shape
    ts = min(512, L)
    nt = L // ts
    hb = ts // HALO
    last_h = L // HALO - 1
    full = lambda shape: pl.BlockSpec(shape, lambda bb, i: (0,) * len(shape))
    params = [lp["dw_w"], lp["dw_b"], lp["cln_g"], lp["cln_b"], lp["cpw"], lp["gln_g"], lp["gln_b"],
              lp["gws"], lp["gbs"], lp["pw"], lp["ps"], lp["gn_local"]]
    return pl.pallas_call(
        functools.partial(_mix_kernel, ts=ts, seq_len=L),
        grid=(b, nt),
        in_specs=[pl.BlockSpec((1, ts, w), lambda bb, i: (bb, i, 0)),
                  pl.BlockSpec((1, HALO, w), lambda bb, i: (bb, jnp.maximum(i * hb - 1, 0), 0)),
                  pl.BlockSpec((1, HALO, w), lambda bb, i: (bb, jnp.minimum((i + 1) * hb, last_h), 0))]
                 + [full(p.shape) for p in params],
        out_specs=pl.BlockSpec((1, ts, LOCAL_W), lambda bb, i: (bb, i, 0)),
        out_shape=jax.ShapeDtypeStruct((b, L, LOCAL_W), BF16),
        scratch_shapes=[pltpu.VMEM((ts + 2 * HALO, 256), F32), pltpu.VMEM((7, ts + 24, 256), F32)]
                       + [pltpu.VMEM((ts + 3 * HALO, 256), F32)] * 4,
        compiler_params=_cparams(("arbitrary", "arbitrary")),
        name="local_mixers",
    )(mix, mix, mix, *params)


def _merge_kernel(*refs, n_alias):
    (att_ref, loc_ref, x_ref, g1_ref, sc_ref, sh_ref, n2g_ref, woa_ref, wol_ref,
     rw_ref, rb_ref, ltri_ref, cnt_in_ref) = refs[:13]
    xo_ref, h2_ref, gate_ref, ir_ref, cnt_ref, run_ref = refs[13 + n_alias:]

    @pl.when(jnp.logical_and(pl.program_id(0) == 0, pl.program_id(1) == 0))
    def _():
        run_ref[...] = cnt_in_ref[...]

    tm = x_ref.shape[1]
    sub = ltri_ref.shape[0]
    lane = lax.broadcasted_iota(jnp.int32, (sub, ROUTE_W), 1).astype(F32)
    for s in range(tm // sub):
        r0 = s * sub
        y = (jnp.dot(att_ref[0, r0:r0 + sub, :], woa_ref[...], preferred_element_type=F32)
             + jnp.dot(loc_ref[0, r0:r0 + sub, :], wol_ref[...], preferred_element_type=F32))
        xn = x_ref[0, r0:r0 + sub, :] + g1_ref[0] * y
        xo_ref[0, r0:r0 + sub, :] = xn
        h2 = _modulated_rmsnorm(xn, n2g_ref[...], sc_ref[0], sh_ref[0]).astype(BF16)
        h2_ref[r0:r0 + sub, :] = h2
        logits = jnp.dot(h2, rw_ref[...], preferred_element_type=F32) + rb_ref[...]
        vals = jnp.zeros(logits.shape, F32)
        idxs = jnp.zeros(logits.shape, F32)
        hits = jnp.zeros(logits.shape, F32)
        sels = []
        top = None
        for k in range(TOP_K):
            m = jnp.max(logits, axis=-1, keepdims=True)
            sel = jnp.min(jnp.where(logits == m, lane, float(ROUTE_W)), axis=-1, keepdims=True)
            if top is None:
                top = m
            chosen = lane == sel
            vals = jnp.where(lane == float(k), jnp.exp(m - top), vals)
            idxs = jnp.where(lane == float(k), sel, idxs)
            hits = jnp.where(chosen, 1.0, hits)
            logits = jnp.where(chosen, NEG_INF * 2.0, logits)
            sels.append(chosen)
        gate_ref[r0:r0 + sub, :] = vals / jnp.sum(vals, axis=-1, keepdims=True)
        base = run_ref[...] + jnp.dot(ltri_ref[...], hits.astype(BF16), preferred_element_type=F32)
        packed = idxs
        for k in range(TOP_K):
            rk = jnp.sum(jnp.where(sels[k], base, 0.0), axis=-1, keepdims=True)
            packed = jnp.where(lane == float(TOP_K + k), rk, packed)
        ir_ref[:, r0:r0 + sub] = packed.T[0:2 * TOP_K, :].astype(jnp.int32)
        run_ref[...] = run_ref[...] + jnp.sum(hits, axis=0, keepdims=True)
    cnt_ref[...] = run_ref[...]


def _merge(att, loc, x, xoff, g1, sc2, sh2, moff, n2g, woa, wol, rw, rb, cnt_in, bufs, row_off, t_total):
    b = att.shape[0]
    _, L, d = x.shape
    tm = min(512, L)
    nt = L // tm
    blk_off = row_off // tm
    bm = g1.shape[0]
    mod_map = (lambda bb, i: (moff + bb, 0, 0)) if bm > 1 else (lambda bb, i: (0, 0, 0))
    tile = lambda w: pl.BlockSpec((1, tm, w), lambda bb, i: (bb, i, 0))
    xtile = pl.BlockSpec((1, tm, d), lambda bb, i: (xoff + bb, i, 0))
    flat = lambda w: pl.BlockSpec((tm, w), lambda bb, i: (blk_off + bb * nt + i, 0))
    full = lambda shape: pl.BlockSpec(shape, lambda bb, i: (0,) * len(shape))
    sub = min(MERGE_SUB, tm)
    assert L % tm == 0 and tm % sub == 0 and row_off % tm == 0
    ltri = (np.arange(sub)[:, None] > np.arange(sub)[None, :]).astype(np.float32)
    ltri = jnp.asarray(ltri, BF16)
    n_alias = 0 if bufs is None else len(bufs)
    n_in = 13
    return pl.pallas_call(
        functools.partial(_merge_kernel, n_alias=n_alias),
        grid=(b, nt),
        in_specs=[tile(256), tile(LOCAL_W), xtile,
                  pl.BlockSpec((1, 1, d), mod_map), pl.BlockSpec((1, 1, d), mod_map),
                  pl.BlockSpec((1, 1, d), mod_map),
                  full((1, d)), full(woa.shape), full(wol.shape), full(rw.shape), full(rb.shape),
                  full((sub, sub)), full((1, ROUTE_W))]
                 + [pl.BlockSpec(memory_space=pl.ANY)] * n_alias,
        out_specs=(tile(d), flat(d), flat(ROUTE_W),
                   pl.BlockSpec((2 * TOP_K, tm), lambda bb, i: (0, blk_off + bb * nt + i)),
                   full((1, ROUTE_W))),
        out_shape=(jax.ShapeDtypeStruct((b, L, d), F32),
                   jax.ShapeDtypeStruct((t_total, d), BF16),
                   jax.ShapeDtypeStruct((t_total, ROUTE_W), F32),
                   jax.ShapeDtypeStruct((2 * TOP_K, t_total), jnp.int32),
                   jax.ShapeDtypeStruct((1, ROUTE_W), F32)),
        scratch_shapes=[pltpu.VMEM((1, ROUTE_W), F32)],
        input_output_aliases={n_in + j: 1 + j for j in range(n_alias)},
        compiler_params=_cparams(("arbitrary", "arbitrary")),
        name="merge_router",
    )(att, loc, x, g1, sc2, sh2, n2g.reshape(1, d), woa, wol, rw, rb, ltri, cnt_in,
      *(bufs if bufs is not None else ()))


def _moe_kernel(blk_e_ref, n_used_ref, nxt_e_ref, x_ref, w1_hbm, b1_ref, w2_hbm, b2_ref, o_ref,
                w1f_ref, w2f_ref, w1s_ref, w2s_ref, sem, *, layer):
    i = pl.program_id(0)
    active = i < n_used_ref[0]
    e = blk_e_ref[i]
    new_expert = jnp.logical_or(i == 0, e != blk_e_ref[jnp.maximum(i - 1, 0)])

    def weight_copies(expert):
        return (pltpu.make_async_copy(w1_hbm.at[layer, expert], w1f_ref, sem.at[0]),
                pltpu.make_async_copy(w2_hbm.at[layer, expert], w2f_ref, sem.at[1]))

    @pl.when(jnp.logical_and(i == 0, active))
    def _():
        for cp in weight_copies(e):
            cp.start()

    @pl.when(jnp.logical_and(active, new_expert))
    def _():
        for cp in weight_copies(e):
            cp.wait()
        rows = 128
        for r in range(D_MODEL // rows):
            w1s_ref[r * rows:(r + 1) * rows, :] = w1f_ref[r * rows:(r + 1) * rows, :].astype(BF16)
        for r in range(D_FF // rows):
            w2s_ref[r * rows:(r + 1) * rows, :] = w2f_ref[r * rows:(r + 1) * rows, :].astype(BF16)
        nxt = nxt_e_ref[i]

        @pl.when(nxt >= 0)
        def _():
            for cp in weight_copies(nxt):
                cp.start()

    @pl.when(active)
    def _():
        x = x_ref[...]
        acc = jnp.zeros((x.shape[0], D_MODEL), F32)
        for c in range(D_FF // FF_CHUNK):
            lo = c * FF_CHUNK
            glu = jnp.dot(x, w1s_ref[:, lo:lo + FF_CHUNK], preferred_element_type=F32) \
                + b1_ref[0, 0, :, lo:lo + FF_CHUNK]
            lin = jnp.dot(x, w1s_ref[:, D_FF + lo:D_FF + lo + FF_CHUNK], preferred_element_type=F32) \
                + b1_ref[0, 0, :, D_FF + lo:D_FF + lo + FF_CHUNK]
            glu = jnp.minimum(glu, SWIGLU_LIMIT)
            lin = jnp.clip(lin, -SWIGLU_LIMIT, SWIGLU_LIMIT)
            act = glu * jax.nn.sigmoid(SWIGLU_ALPHA * glu) * (lin + 1.0)
            acc = acc + jnp.dot(act.astype(BF16), w2s_ref[lo:lo + FF_CHUNK, :], preferred_element_type=F32)
        o_ref[...] = (acc + b2_ref[0, 0]).astype(o_ref.dtype)

    @pl.when(i >= n_used_ref[0])
    def _():
        o_ref[...] = jnp.zeros(o_ref.shape, o_ref.dtype)


def _moe_ffn(xp, blk_e, n_used, nxt_e, w1, b1, w2, b2, layer):
    n_rows, d = xp.shape
    n_blk = n_rows // MOE_TM
    depth, e = w1.shape[:2]

    def row_map(i, be, nu, ne):
        return (jnp.maximum(jnp.minimum(i, nu[0] - 1), 0), 0)

    def b_map(i, be, nu, ne):
        return (layer, be[i], 0, 0)

    return pl.pallas_call(
        functools.partial(_moe_kernel, layer=layer),
        grid_spec=pltpu.PrefetchScalarGridSpec(
            num_scalar_prefetch=3,
            grid=(n_blk,),
            in_specs=[pl.BlockSpec((MOE_TM, d), row_map),
                      pl.BlockSpec(memory_space=pl.ANY),
                      pl.BlockSpec((1, 1, 1, 2 * D_FF), b_map),
                      pl.BlockSpec(memory_space=pl.ANY),
                      pl.BlockSpec((1, 1, 1, d), b_map)],
            out_specs=pl.BlockSpec((MOE_TM, d), lambda i, be, nu, ne: (i, 0)),
            scratch_shapes=[pltpu.VMEM((d, 2 * D_FF), F32), pltpu.VMEM((D_FF, d), F32),
                            pltpu.VMEM((d, 2 * D_FF), BF16), pltpu.VMEM((D_FF, d), BF16),
                            pltpu.SemaphoreType.DMA((2,))]),
        out_shape=jax.ShapeDtypeStruct((n_rows, d), BF16),
        compiler_params=_cparams(("arbitrary",)),
        name="moe_ffn",
    )(blk_e, n_used, nxt_e, xp, w1, b1.reshape(depth, e, 1, 2 * D_FF), w2, b2.reshape(depth, e, 1, d))


def _combine_kernel(*refs, final):
    yg_ref, gate_ref, x_ref, g2_ref, fg_ref = refs[:5]
    o_ref = refs[-1]
    gates = gate_ref[...]
    ff = jnp.zeros(x_ref.shape[1:], F32)
    for k in range(TOP_K):
        ff = ff + gates[:, k:k + 1] * yg_ref[k, 0].astype(F32)
    xo = x_ref[0] + g2_ref[0] * ff
    if final:
        ms = jnp.mean(xo * xo, axis=-1, keepdims=True)
        xo = xo * lax.rsqrt(ms + EPS) * fg_ref[...]
    o_ref[0] = xo


def _combine(yg, gates, x, g2, moff, fg, row_off, *, final, out_buf=None, ooff=0, out_b=None):
    b, L, d = x.shape
    tm = min(256, L)
    nt = L // tm
    blk_off = row_off // tm
    bm = g2.shape[0]
    out_b = b if out_b is None else out_b
    mod_map = (lambda bb, i: (moff + bb, 0, 0)) if bm > 1 else (lambda bb, i: (0, 0, 0))
    extra = () if out_buf is None else (out_buf,)
    return pl.pallas_call(
        functools.partial(_combine_kernel, final=final),
        grid=(b, nt),
        in_specs=[pl.BlockSpec((TOP_K, 1, tm, d), lambda bb, i: (0, bb, i, 0)),
                  pl.BlockSpec((tm, ROUTE_W), lambda bb, i: (blk_off + bb * nt + i, 0)),
                  pl.BlockSpec((1, tm, d), lambda bb, i: (bb, i, 0)),
                  pl.BlockSpec((1, 1, d), mod_map),
                  pl.BlockSpec((1, d), lambda bb, i: (0, 0))]
                 + [pl.BlockSpec(memory_space=pl.ANY)] * len(extra),
        out_specs=pl.BlockSpec((1, tm, d), lambda bb, i: (ooff + bb, i, 0)),
        out_shape=jax.ShapeDtypeStruct((out_b, L, d), F32),
        input_output_aliases={5: 0} if extra else {},
        compiler_params=_cparams(("arbitrary", "arbitrary")),
        name="combine",
    )(yg, gates, x, g2, fg.reshape(1, d), *extra)


def _dispatch_plan(idx, rank, counts):
    n_tok = idx.shape[1]
    n_asg = n_tok * TOP_K
    padded = (counts + MOE_TM - 1) // MOE_TM * MOE_TM
    pad_end = jnp.cumsum(padded)
    pad_start = (pad_end - padded).astype(jnp.int32)
    pos = rank
    for e in range(N_EXPERTS):
        pos = pos + jnp.where(idx == e, pad_start[e], 0)
    n_rows = -(-n_asg // MOE_TM) * MOE_TM + N_EXPERTS * MOE_TM
    n_blk = n_rows // MOE_TM
    tok = jnp.broadcast_to(jnp.arange(n_tok, dtype=jnp.int32)[None, :], (TOP_K, n_tok))
    fill = jnp.arange(n_rows, dtype=jnp.int32) % n_tok
    row_tok = fill.at[pos.reshape(-1)].add((tok - pos % n_tok).reshape(-1), mode="promise_in_bounds")
    blk_start = jnp.arange(n_blk, dtype=jnp.int32) * MOE_TM
    blk_e = jnp.minimum(jnp.sum((pad_end[None, :] <= blk_start[:, None]).astype(jnp.int32), axis=1),
                        N_EXPERTS - 1)
    n_used = (pad_end[-1] // MOE_TM).astype(jnp.int32).reshape(1)
    ids = jnp.arange(N_EXPERTS, dtype=jnp.int32)
    first_used_from = lax.cummin(jnp.where(padded > 0, ids, N_EXPERTS), axis=0, reverse=True)
    nxt_of = jnp.concatenate([first_used_from[1:], jnp.full((1,), N_EXPERTS, jnp.int32)])
    nxt_of = jnp.where(nxt_of >= N_EXPERTS, -1, nxt_of)
    nxt_e = jnp.sum(jnp.where(blk_e[:, None] == ids[None, :], nxt_of[None, :], 0), axis=1).astype(jnp.int32)
    return row_tok, pos, blk_e, n_used, nxt_e


_Q_PERM = np.concatenate([np.arange(0, 64), np.arange(128, 192), np.arange(64, 128), np.arange(192, 256)])


def _rope_tables(L):
    t = jnp.arange(L)
    row = (t // GRID_W).astype(F32)
    col = (t % GRID_W).astype(F32)
    half = HEAD_DIM // 2
    inv = ROPE_BASE ** (-jnp.arange(0, half, 2, dtype=F32) / half)
    ar, ac = row[:, None] * inv[None, :], col[:, None] * inv[None, :]
    z = jnp.zeros_like(ar)
    cos = jnp.concatenate([jnp.cos(ar), jnp.cos(ar), jnp.cos(ac), jnp.cos(ac)], axis=1)
    s1 = jnp.concatenate([-jnp.sin(ar), z, -jnp.sin(ac), z], axis=1)
    s2 = jnp.concatenate([z, jnp.sin(ar), z, jnp.sin(ac)], axis=1)
    rep = lambda a: jnp.concatenate([a, a], axis=1)
    return rep(cos), rep(s1), rep(s2)


def _block_diag(pw):
    g, n, _ = pw.shape
    out = jnp.zeros((g * n, g * n), pw.dtype)
    for gi in range(g):
        out = out.at[gi * n:(gi + 1) * n, gi * n:(gi + 1) * n].set(pw[gi])
    return out


def kernel(x, c, ctx, c_ctx, ada_w, ada_b, norm1_g, norm2_g, w_in, attn_sink, conv_dw_w, conv_dw_b, conv_ln_g, conv_ln_b, conv_pw_w, gmlp_ln_g, gmlp_ln_b, gmlp_ws, gmlp_bs, pool_w, pool_scale, group_norm_g, w_out, router_w, router_b, exp_w1, exp_b1, exp_w2, exp_b2, final_norm_g):
    b, L, d = x.shape
    lc = ctx.shape[1]
    depth = ada_w.shape[0]
    tabs = _rope_tables(L)

    r = -(-(b + 1) // 8) * 8
    cond = jnp.zeros((r, d), F32).at[:b].set(c).at[b].set(c_ctx)
    mods = _adaln(cond, ada_w, ada_b).reshape(depth, r, 6, d)

    n_groups = BATCH_GROUPS if b % BATCH_GROUPS == 0 else 1
    bh = b // n_groups
    xs, cs = [x] * n_groups, [ctx] * n_groups
    pending = [None] * n_groups

    def finish(st, out_buf):
        if st["last"]:
            return _combine(st["ygx"], st["gates"], st["xh"], st["g2"], st["moff"], final_norm_g, 0,
                            final=True, out_buf=out_buf, ooff=st["moff"], out_b=b)
        x_new = _combine(st["ygx"], st["gates"], st["xh"], st["g2"], st["moff"], final_norm_g, 0, final=False)
        c_new = _combine(st["ygc"], st["gates"], st["ch"], st["g2c"], 0, final_norm_g, bh * L, final=False)
        return x_new, c_new

    for l in range(depth):
        last = l == depth - 1
        mx = [mods[l, :b, i].reshape(b, 1, d) for i in range(6)]
        mc = [mods[l, b:b + 1, i].reshape(1, 1, d) for i in range(6)]
        sh1, sc1, g1, sh2, sc2, g2 = mx
        sh1c, sc1c, g1c, sh2c, sc2c, g2c = mc

        wl = w_in[l]
        wq = wl[:, :OFF_K][:, _Q_PERM] * (HEAD_DIM ** -0.5)
        w_full = jnp.concatenate([wq, wl[:, OFF_K:]], axis=1).astype(BF16)
        gn = group_norm_g[l]
        g_att = gn[:256][_Q_PERM]
        wo = w_out[l]
        woa = wo[:256][_Q_PERM].astype(BF16)
        wol = wo[256:].astype(BF16)
        lp = dict(
            dw_w=conv_dw_w[l], dw_b=conv_dw_b[l].reshape(1, -1),
            cln_g=conv_ln_g[l].reshape(1, -1), cln_b=conv_ln_b[l].reshape(1, -1),
            cpw=conv_pw_w[l].astype(BF16),
            gln_g=gmlp_ln_g[l].reshape(1, -1), gln_b=gmlp_ln_b[l].reshape(1, -1),
            gws=gmlp_ws[l].reshape(GMLP_HEADS * CHUNK, CHUNK).astype(BF16),
            gbs=jnp.repeat(gmlp_bs[l].T, GROUP_W // GMLP_HEADS, axis=1),
            pw=_block_diag(pool_w[l]).astype(BF16), ps=pool_scale[l].reshape(1, -1),
            gn_local=gn[256:].reshape(1, -1))
        rw = jnp.zeros((d, ROUTE_W), F32).at[:, :N_EXPERTS].set(router_w[l]).astype(BF16)
        rb = jnp.full((1, ROUTE_W), NEG_INF, F32).at[0, :N_EXPERTS].set(router_b[l])

        w_kv = wl[:, OFF_K:OFF_CONV].astype(BF16)
        take = lambda a, i: a.at[i].get(mode="promise_in_bounds")
        staged = []
        for h in range(n_groups):
            moff = h * bh
            pd = pending[h]
            xoff = moff if l == 0 else 0

            if pd is None:
                xh, ch = xs[h], cs[h]
                q, kv, mix = _inproj(xh, xoff, bh, sc1, sh1, moff, norm1_g[l], w_full, tabs, full=True)
                if last:
                    (kvc,) = _inproj(ch, xoff, bh, sc1c, sh1c, 0, norm1_g[l], w_kv, None, full=False)
                else:
                    qc, kvc, mixc = _inproj(ch, xoff, bh, sc1c, sh1c, 0, norm1_g[l], w_full, None, full=True)
            else:
                if staged:
                    pd["ygx"], staged[-1]["xh"] = lax.optimization_barrier((pd["ygx"], staged[-1]["xh"]))
                combx = (pd["ygx"], pd["gates"], 0, pd["g2"], pd["moff"])
                combc = (pd["ygc"], pd["gates"], bh * L, pd["g2c"], 0)
                xh, q, kv, mix = _inproj(pd["xh"], 0, bh, sc1, sh1, moff, norm1_g[l], w_full, tabs,
                                         full=True, comb=combx)
                if last:
                    ch = None
                    (kvc,) = _inproj(pd["ch"], 0, bh, sc1c, sh1c, 0, norm1_g[l], w_kv, None,
                                     full=False, comb=combc, emit_x=False)
                else:
                    ch, qc, kvc, mixc = _inproj(pd["ch"], 0, bh, sc1c, sh1c, 0, norm1_g[l], w_full, None,
                                                full=True, comb=combc)
                xoff = 0
            att = _attention(q, kv, kvc, attn_sink[l], g_att, local=True)
            loc = _local_mixers(mix, lp)
            if staged:
                prev = staged[-1]
                prev["xp"], loc = lax.optimization_barrier((prev["xp"], loc))
            nxt = pending[h + 1] if h + 1 < n_groups else None
            if nxt is not None:
                nxt["ygx"], loc = lax.optimization_barrier((nxt["ygx"], loc))
            t_total = bh * L if last else bh * (L + lc)
            cnt0 = jnp.zeros((1, ROUTE_W), F32)
            xh, h2, gates, ir, cnt = _merge(att, loc, xh, xoff, g1, sc2, sh2, moff, norm2_g[l], woa, wol,
                                            rw, rb, cnt0, None, 0, t_total)
            if not last:
                attc = _attention(qc, kvc, kvc, attn_sink[l], g_att, local=False)
                locc = _local_mixers(mixc, lp)
                ch, h2, gates, ir, cnt = _merge(attc, locc, ch, xoff, g1c, sc2c, sh2c, 0, norm2_g[l],
                                                woa, wol, rw, rb, cnt, (h2, gates, ir), bh * L, t_total)
            counts = cnt[0, :N_EXPERTS].astype(jnp.int32)
            row_tok, pos, blk_e, n_used, nxt_e = _dispatch_plan(ir[:TOP_K], ir[TOP_K:], counts)
            staged.append(dict(xh=xh, ch=ch, gates=gates, pos=pos, blk_e=blk_e, n_used=n_used, nxt_e=nxt_e,
                               xp=take(h2, row_tok), moff=moff, g2=g2, g2c=g2c, last=last))

        yps = []
        for h, st in enumerate(staged):
            if yps:
                yps[-1], st["xp"] = lax.optimization_barrier((yps[-1], st["xp"]))
            yps.append(_moe_ffn(st["xp"], st["blk_e"], st["n_used"], st["nxt_e"], exp_w1, exp_b1, exp_w2, exp_b2, l))
        for h, st in enumerate(staged):
            if h == n_groups - 1 and h > 0:
                yps[h], staged[0]["ygx"] = lax.optimization_barrier((yps[h], staged[0]["ygx"]))
            pos = st["pos"]
            st["ygx"] = take(yps[h], pos[:, :bh * L]).reshape(TOP_K, bh, L, d)
            if not last:
                st["ygc"] = take(yps[h], pos[:, bh * L:]).reshape(TOP_K, bh, lc, d)
            pending[h] = st

    out = None
    for h in range(n_groups):
        out = finish(pending[h], out)
    return out
```

```python
import functools

import jax
import jax.numpy as jnp
import numpy as np
from jax import lax
from jax.experimental import pallas as pl
from jax.experimental.pallas import tpu as pltpu

F32 = jnp.float32
BF16 = jnp.bfloat16

D_MODEL = 1024
GRID_W = 64
GROUP_W = 256
HEAD_DIM = 64
N_Q_HEADS = 4
N_KV_HEADS = 2
WINDOW = 128
ATT_BLOCK = 128
ROPE_BASE = 10000.0
CONV_WIDTH = 31
CHUNK = 128
GMLP_HEADS = 4
POOL_WINDOWS = (2, 4, 8, 16)
N_EXPERTS = 32
TOP_K = 4
D_FF = 1024
SWIGLU_ALPHA = 1.702
SWIGLU_LIMIT = 7.0
EPS = 1e-6
NEG_INF = -1e30

OFF_K = 256
OFF_V = 384
OFF_CONV = 512
IN_W = 1792
MIX_IN_W = IN_W - OFF_CONV
LOCAL_W = 3 * GROUP_W

LANES = 128
HALO = 16
ROUTE_W = LANES
MOE_TM = 512
FF_CHUNK = 512
MERGE_SUB = 512
BATCH_GROUPS = 2
VMEM_LIMIT = 56 * 1024 * 1024


def _cparams(sem):
    return pltpu.CompilerParams(dimension_semantics=sem, vmem_limit_bytes=VMEM_LIMIT)


def _ada_kernel(c_ref, w_ref, b_ref, o_ref):
    c = c_ref[...]
    s = c * jax.nn.sigmoid(c)
    o_ref[0] = jnp.dot(s.astype(BF16), w_ref[0].astype(BF16), preferred_element_type=F32) + b_ref[0]


def _adaln(cond, ada_w, ada_b):
    depth, d, n = ada_w.shape
    r = cond.shape[0]
    tn = 1536
    return pl.pallas_call(
        _ada_kernel,
        grid=(depth, n // tn),
        in_specs=[pl.BlockSpec((r, d), lambda l, j: (0, 0)),
                  pl.BlockSpec((1, d, tn), lambda l, j: (l, 0, j)),
                  pl.BlockSpec((1, 1, tn), lambda l, j: (l, 0, j))],
        out_specs=pl.BlockSpec((1, r, tn), lambda l, j: (l, 0, j)),
        out_shape=jax.ShapeDtypeStruct((depth, r, n), F32),
        compiler_params=_cparams(("arbitrary", "arbitrary")),
        name="adaln",
    )(cond, ada_w, ada_b.reshape(depth, 1, n))


def _modulated_rmsnorm(x, g, sc, sh):
    ms = jnp.mean(x * x, axis=-1, keepdims=True)
    return x * lax.rsqrt(ms + EPS) * (g * (1.0 + sc)) + sh


def _weighted_expert_sum(yg_ref, gates):
    ff = gates[:, 0:1] * yg_ref[0, 0].astype(F32)
    for k in range(1, TOP_K):
        ff = ff + gates[:, k:k + 1] * yg_ref[k, 0].astype(F32)
    return ff


def _inproj_kernel(*refs, rope, full, fused, emit_x):
    refs = list(refs)
    if fused:
        yg_ref, gate_ref, g2_ref = refs[:3]
        refs = refs[3:]
    if rope:
        x_ref, sc_ref, sh_ref, g_ref, w_ref, cos_ref, s1_ref, s2_ref = refs[:8]
        outs = refs[8:]
    else:
        x_ref, sc_ref, sh_ref, g_ref, w_ref = refs[:5]
        outs = refs[5:]
    x = x_ref[0]
    if fused:
        x = x + g2_ref[0] * _weighted_expert_sum(yg_ref, gate_ref[...])
        if emit_x:
            outs[0][0] = x
            outs = outs[1:]
    h = _modulated_rmsnorm(x, g_ref[...], sc_ref[0], sh_ref[0])
    p = jnp.dot(h.astype(BF16), w_ref[...], preferred_element_type=F32)

    def roped(xs):
        if not rope:
            return xs
        return (xs * cos_ref[...] + pltpu.roll(xs, LANES - 16, 1) * s1_ref[...]
                + pltpu.roll(xs, 16, 1) * s2_ref[...])

    if full:
        q_ref, kv_ref, mix_ref = outs
        q_ref[0] = jnp.concatenate([roped(p[:, 0:128]), roped(p[:, 128:256])], axis=1).astype(BF16)
        kv_ref[0] = jnp.concatenate([roped(p[:, 256:384]), p[:, 384:512]], axis=1).astype(BF16)
        mix_ref[0] = p[:, OFF_CONV:]
    else:
        (kv_ref,) = outs
        kv_ref[0] = p.astype(BF16)


def _inproj(x, xoff, b, sc, sh, moff, g, w, tabs, *, full, comb=None, emit_x=True):
    _, L, d = x.shape
    n = w.shape[1]
    tm = min(512, L)
    nt = L // tm
    assert L % tm == 0
    rope = tabs is not None
    bm = sc.shape[0]
    mod_map = (lambda i, bb: (moff + bb, 0, 0)) if bm > 1 else (lambda i, bb: (0, 0, 0))
    in_specs, args = [], []
    if comb is not None:
        yg, gates, row_off, g2, g2off = comb
        assert row_off % tm == 0
        blk_off = row_off // tm
        g2_map = (lambda i, bb: (g2off + bb, 0, 0)) if g2.shape[0] > 1 else (lambda i, bb: (0, 0, 0))
        in_specs += [pl.BlockSpec((TOP_K, 1, tm, d), lambda i, bb: (0, bb, i, 0)),
                     pl.BlockSpec((tm, ROUTE_W), lambda i, bb: (blk_off + bb * nt + i, 0)),
                     pl.BlockSpec((1, 1, d), g2_map)]
        args += [yg, gates, g2]
    in_specs += [pl.BlockSpec((1, tm, d), lambda i, bb: (xoff + bb, i, 0)),
                 pl.BlockSpec((1, 1, d), mod_map),
                 pl.BlockSpec((1, 1, d), mod_map),
                 pl.BlockSpec((1, d), lambda i, bb: (0, 0)),
                 pl.BlockSpec((d, n), lambda i, bb: (0, 0))]
    args += [x, sc, sh, g.reshape(1, d), w]
    if rope:
        in_specs += [pl.BlockSpec((tm, LANES), lambda i, bb: (i, 0))] * 3
        args += list(tabs)
    tile = lambda wd: pl.BlockSpec((1, tm, wd), lambda i, bb: (bb, i, 0))
    out_shape, out_specs = [], []
    emit_x = emit_x and comb is not None
    if emit_x:
        out_shape.append(jax.ShapeDtypeStruct((b, L, d), F32))
        out_specs.append(tile(d))
    if full:
        out_shape += [jax.ShapeDtypeStruct((b, L, 256), BF16),
                      jax.ShapeDtypeStruct((b, L, 256), BF16),
                      jax.ShapeDtypeStruct((b, L, MIX_IN_W), F32)]
        out_specs += [tile(256), tile(256), tile(MIX_IN_W)]
    else:
        out_shape.append(jax.ShapeDtypeStruct((b, L, 256), BF16))
        out_specs.append(tile(256))
    return pl.pallas_call(
        functools.partial(_inproj_kernel, rope=rope, full=full, fused=comb is not None, emit_x=emit_x),
        grid=(nt, b),
        in_specs=in_specs, out_specs=tuple(out_specs), out_shape=tuple(out_shape),
        compiler_params=_cparams(("arbitrary", "arbitrary")),
        name="inproj_x" if rope else "inproj_ctx",
    )(*args)


def _attn_kernel(*refs, local, seq_len, q_blocks):
    if local:
        sink_ref, q_ref, kv_ref, kvc_ref, g_ref, bias_ref, o_ref = refs
    else:
        sink_ref, q_ref, kv_ref, kvc_ref, g_ref, o_ref = refs
    step = pl.program_id(1)
    lane = lax.broadcasted_iota(jnp.int32, (1, LANES), 1)
    row = lax.broadcasted_iota(jnp.int32, (2 * ATT_BLOCK, 1), 0)
    kvc = kvc_ref[0]
    kc, vc = kvc[:, :LANES], kvc[:, LANES:]
    lc = kvc.shape[0]
    nt = (((1,), (1,)), ((), ()))
    zero = jnp.zeros((), BF16)
    lm0 = lane < HEAD_DIM
    for sb in range(q_blocks):
        n = step * q_blocks + sb
        q = q_ref[0, sb * ATT_BLOCK:(sb + 1) * ATT_BLOCK, :]
        if local:
            blk0 = jnp.clip(n - 1, 0, seq_len // ATT_BLOCK - 3)
            kw = kv_ref[0, pl.ds(pl.multiple_of(blk0 * ATT_BLOCK, ATT_BLOCK), 3 * ATT_BLOCK), :]
            kl, vl = kw[:, :LANES], kw[:, LANES:]
            bias = bias_ref[n - blk0]
        outs = []
        for kh in range(N_KV_HEADS):
            lm = lm0 if kh == 0 else jnp.logical_not(lm0)
            lhs = jnp.concatenate([jnp.where(lm, q[:, :LANES], zero), jnp.where(lm, q[:, LANES:], zero)],
                                  axis=0)
            sk = jnp.where(row < ATT_BLOCK, sink_ref[2 * kh], sink_ref[2 * kh + 1])
            s = lax.dot_general(lhs, kc, nt, preferred_element_type=F32)
            if local:
                s_l = lax.dot_general(lhs, kl, nt, preferred_element_type=F32) + bias
                s = jnp.concatenate([s, s_l], axis=1)
            m = jnp.maximum(sk, jnp.max(s, axis=-1, keepdims=True))
            e = jnp.exp(s - m)
            den = jnp.exp(sk - m) + jnp.sum(e, axis=-1, keepdims=True)
            e = e.astype(BF16)
            o = jnp.dot(e[:, :lc], vc, preferred_element_type=F32)
            if local:
                o = o + jnp.dot(e[:, lc:], vl, preferred_element_type=F32)
            outs.append(o / den)
        o01, o23 = outs
        att = jnp.concatenate([jnp.where(lm0, o01[:ATT_BLOCK], o23[:ATT_BLOCK]),
                               jnp.where(lm0, o01[ATT_BLOCK:], o23[ATT_BLOCK:])], axis=1)
        ms = jnp.mean(att * att, axis=-1, keepdims=True)
        o_ref[0, sb * ATT_BLOCK:(sb + 1) * ATT_BLOCK, :] = (att * lax.rsqrt(ms + EPS) * g_ref[...]).astype(BF16)


def _band_bias():
    i = np.arange(2 * ATT_BLOCK)[None, :, None] % ATT_BLOCK
    j = np.arange(3 * ATT_BLOCK)[None, None, :]
    v = np.arange(3)[:, None, None]
    return jnp.asarray(np.where(np.abs(j - i - ATT_BLOCK * v) <= WINDOW, 0.0, NEG_INF), F32)


def _attention(q, kv, kvc, sink, g_att, *, local):
    b, L, _ = q.shape
    lc = kvc.shape[1]
    nb = L // ATT_BLOCK
    assert not local or nb >= 3
    q_blocks = max(q for q in (4, 2, 1) if nb % q == 0)
    tq = q_blocks * ATT_BLOCK
    in_specs = [pl.BlockSpec(memory_space=pltpu.SMEM),
                pl.BlockSpec((1, tq, 256), lambda bb, i: (bb, i, 0)),
                pl.BlockSpec((1, kv.shape[1], 256), lambda bb, i: (bb, 0, 0)),
                pl.BlockSpec((1, lc, 256), lambda bb, i: (bb, 0, 0)),
                pl.BlockSpec((1, 256), lambda bb, i: (0, 0))]
    args = [sink, q, kv, kvc, g_att.reshape(1, 256)]
    if local:
        in_specs.append(pl.BlockSpec((3, 2 * ATT_BLOCK, 3 * ATT_BLOCK), lambda bb, i: (0, 0, 0)))
        args.append(_band_bias())
    return pl.pallas_call(
        functools.partial(_attn_kernel, local=local, seq_len=L, q_blocks=q_blocks),
        grid=(b, nb // q_blocks),
        in_specs=in_specs,
        out_specs=pl.BlockSpec((1, tq, 256), lambda bb, i: (bb, i, 0)),
        out_shape=jax.ShapeDtypeStruct((b, L, 256), BF16),
        compiler_params=_cparams(("arbitrary", "arbitrary")),
        name="attn_window" if local else "attn_ctx",
    )(*args)


def _layernorm(x, g, b):
    mu = jnp.mean(x, axis=-1, keepdims=True)
    xc = x - mu
    return xc * lax.rsqrt(jnp.mean(xc * xc, axis=-1, keepdims=True) + EPS) * g + b


def _group_norm_store(y, g):
    ms = jnp.mean(y * y, axis=-1, keepdims=True)
    return (y * lax.rsqrt(ms + EPS) * g).astype(BF16)


def _mix_kernel(cur_ref, prev_ref, next_ref, dww_ref, dwb_ref, clg_ref, clb_ref, cpw_ref,
                glg_ref, glb_ref, gws_ref, gbs_ref, pw_ref, ps_ref, gn_ref, o_ref,
                hc_ref, sh_ref, hp_ref, c2_ref, c4_ref, c8_ref, *, ts, seq_len):
    i = pl.program_id(1)
    nt = pl.num_programs(1)
    has_prev = (i > 0).astype(F32)
    has_next = (i < nt - 1).astype(F32)

    def glu(blk):
        return blk[:, 0:256] * jax.nn.sigmoid(blk[:, 256:512])

    pv = prev_ref[0]
    nx = next_ref[0]
    hc_ref[0:HALO, :] = glu(pv) * has_prev
    hc_ref[HALO + ts:2 * HALO + ts, :] = glu(nx) * has_next
    hp_ref[0:HALO, :] = pv[:, 1024:1280] * has_prev
    hp_ref[HALO + ts:2 * HALO + ts, :] = nx[:, 1024:1280] * has_next
    hp_ref[2 * HALO + ts:3 * HALO + ts, :] = jnp.zeros((HALO, 256), F32)
    hc_ref[HALO:HALO + ts, :] = glu(cur_ref[0, :, 0:512])
    hp_ref[HALO:HALO + ts, :] = cur_ref[0, :, 1024:1280]

    for r in range(1, 8):
        sh_ref[r - 1] = hc_ref[r:r + ts + 24, :]
    c2_ref[8:ts + 40, :] = hp_ref[8:ts + 40, :] + hp_ref[9:ts + 41, :]
    c4_ref[8:ts + 32, :] = c2_ref[8:ts + 32, :] + c2_ref[10:ts + 34, :]
    c8_ref[8:ts + 24, :] = c4_ref[8:ts + 24, :] + c4_ref[12:ts + 28, :]

    lane = lax.broadcasted_iota(jnp.int32, (1, 256), 1)
    rc = min(128, ts)
    for c in range(ts // rc):
        r0 = c * rc
        acc = jnp.zeros((rc, 256), F32)
        for j in range(CONV_WIDTH):
            a, r = divmod(1 + j, 8)
            lo = r0 + 8 * a
            tap = hc_ref[lo:lo + rc, :] if r == 0 else sh_ref[r - 1, lo:lo + rc, :]
            acc = acc + tap * dww_ref[j:j + 1, :]
        hcv = _layernorm(acc + dwb_ref[...], clg_ref[...], clb_ref[...])
        hcv = hcv * jax.nn.sigmoid(hcv)
        conv = jnp.dot(hcv.astype(BF16), cpw_ref[...], preferred_element_type=F32)
        o_ref[0, r0:r0 + rc, 0:256] = _group_norm_store(conv, gn_ref[:, 0:256])

        s0 = r0 + HALO
        hcur = hp_ref[s0:s0 + rc, :]
        s2 = c2_ref[s0 - 1:s0 - 1 + rc, :]
        s4 = c4_ref[s0 - 2:s0 - 2 + rc, :]
        s8 = c8_ref[s0 - 4:s0 - 4 + rc, :]
        s16 = c8_ref[s0 - 8:s0 - 8 + rc, :] + c8_ref[s0:s0 + rc, :]
        wsum = jnp.where(lane < 64, s2, jnp.where(lane < 128, s4, jnp.where(lane < 192, s8, s16)))
        half = jnp.where(lane < 64, 1, jnp.where(lane < 128, 2, jnp.where(lane < 192, 4, 8)))
        t = i * ts + r0 + lax.broadcasted_iota(jnp.int32, (rc, 1), 0)
        cnt = jnp.clip(t + half, 0, seq_len) - jnp.clip(t - half, 0, seq_len)
        y = wsum / cnt.astype(F32) - hcur
        pool = jnp.dot(y.astype(BF16), pw_ref[...], preferred_element_type=F32) * ps_ref[...]
        o_ref[0, r0:r0 + rc, 512:768] = _group_norm_store(pool, gn_ref[:, 512:768])

    for c in range(ts // CHUNK):
        r0 = c * CHUNK
        u = cur_ref[0, r0:r0 + CHUNK, 512:768]
        v = _layernorm(cur_ref[0, r0:r0 + CHUNK, 768:1024], glg_ref[...], glb_ref[...])
        r = jnp.dot(gws_ref[...], v.astype(BF16), preferred_element_type=F32)
        mixed = jnp.where(lane < 64, r[0:CHUNK],
                          jnp.where(lane < 128, r[CHUNK:2 * CHUNK],
                                    jnp.where(lane < 192, r[2 * CHUNK:3 * CHUNK], r[3 * CHUNK:])))
        gm = u * (mixed + gbs_ref[...])
        o_ref[0, r0:r0 + CHUNK, 256:512] = _group_norm_store(gm, gn_ref[:, 256:512])


def _local_mixers(mix, lp):
    b, L, w = mix.shape
    ts = min(512, L)
    nt = L // ts
    hb = ts // HALO
    last_h = L // HALO - 1
    full = lambda shape: pl.BlockSpec(shape, lambda bb, i: (0,) * len(shape))
    params = [lp["dw_w"], lp["dw_b"], lp["cln_g"], lp["cln_b"], lp["cpw"], lp["gln_g"], lp["gln_b"],
              lp["gws"], lp["gbs"], lp["pw"], lp["ps"], lp["gn_local"]]
    return pl.pallas_call(
        functools.partial(_mix_kernel, ts=ts, seq_len=L),
        grid=(b, nt),
        in_specs=[pl.BlockSpec((1, ts, w), lambda bb, i: (bb, i, 0)),
                  pl.BlockSpec((1, HALO, w), lambda bb, i: (bb, jnp.maximum(i * hb - 1, 0), 0)),
                  pl.BlockSpec((1, HALO, w), lambda bb, i: (bb, jnp.minimum((i + 1) * hb, last_h), 0))]
                 + [full(p.shape) for p in params],
        out_specs=pl.BlockSpec((1, ts, LOCAL_W), lambda bb, i: (bb, i, 0)),
        out_shape=jax.ShapeDtypeStruct((b, L, LOCAL_W), BF16),
        scratch_shapes=[pltpu.VMEM((ts + 2 * HALO, 256), F32), pltpu.VMEM((7, ts + 24, 256), F32)]
                       + [pltpu.VMEM((ts + 3 * HALO, 256), F32)] * 4,
        compiler_params=_cparams(("arbitrary", "arbitrary")),
        name="local_mixers",
    )(mix, mix, mix, *params)


def _merge_kernel(*refs, n_alias):
    (att_ref, loc_ref, x_ref, g1_ref, sc_ref, sh_ref, n2g_ref, woa_ref, wol_ref,
     rw_ref, rb_ref, ltri_ref, cnt_in_ref) = refs[:13]
    xo_ref, h2_ref, gate_ref, ir_ref, cnt_ref, run_ref = refs[13 + n_alias:]

    @pl.when(jnp.logical_and(pl.program_id(0) == 0, pl.program_id(1) == 0))
    def _():
        run_ref[...] = cnt_in_ref[...]

    tm = x_ref.shape[1]
    sub = ltri_ref.shape[0]
    lane = lax.broadcasted_iota(jnp.int32, (sub, ROUTE_W), 1).astype(F32)
    for s in range(tm // sub):
        r0 = s * sub
        y = (jnp.dot(att_ref[0, r0:r0 + sub, :], woa_ref[...], preferred_element_type=F32)
             + jnp.dot(loc_ref[0, r0:r0 + sub, :], wol_ref[...], preferred_element_type=F32))
        xn = x_ref[0, r0:r0 + sub, :] + g1_ref[0] * y
        xo_ref[0, r0:r0 + sub, :] = xn
        h2 = _modulated_rmsnorm(xn, n2g_ref[...], sc_ref[0], sh_ref[0]).astype(BF16)
        h2_ref[r0:r0 + sub, :] = h2
        logits = jnp.dot(h2, rw_ref[...], preferred_element_type=F32) + rb_ref[...]
        vals = jnp.zeros(logits.shape, F32)
        idxs = jnp.zeros(logits.shape, F32)
        hits = jnp.zeros(logits.shape, F32)
        sels = []
        top = None
        for k in range(TOP_K):
            m = jnp.max(logits, axis=-1, keepdims=True)
            sel = jnp.min(jnp.where(logits == m, lane, float(ROUTE_W)), axis=-1, keepdims=True)
            if top is None:
                top = m
            chosen = lane == sel
            vals = jnp.where(lane == float(k), jnp.exp(m - top), vals)
            idxs = jnp.where(lane == float(k), sel, idxs)
            hits = jnp.where(chosen, 1.0, hits)
            logits = jnp.where(chosen, NEG_INF * 2.0, logits)
            sels.append(chosen)
        gate_ref[r0:r0 + sub, :] = vals / jnp.sum(vals, axis=-1, keepdims=True)
        base = run_ref[...] + jnp.dot(ltri_ref[...], hits.astype(BF16), preferred_element_type=F32)
        packed = idxs
        for k in range(TOP_K):
            rk = jnp.sum(jnp.where(sels[k], base, 0.0), axis=-1, keepdims=True)
            packed = jnp.where(lane == float(TOP_K + k), rk, packed)
        ir_ref[:, r0:r0 + sub] = packed.T[0:2 * TOP_K, :].astype(jnp.int32)
        run_ref[...] = run_ref[...] + jnp.sum(hits, axis=0, keepdims=True)
    cnt_ref[...] = run_ref[...]


def _merge(att, loc, x, xoff, g1, sc2, sh2, moff, n2g, woa, wol, rw, rb, cnt_in, bufs, row_off, t_total):
    b = att.shape[0]
    _, L, d = x.shape
    tm = min(512, L)
    nt = L // tm
    blk_off = row_off // tm
    bm = g1.shape[0]
    mod_map = (lambda bb, i: (moff + bb, 0, 0)) if bm > 1 else (lambda bb, i: (0, 0, 0))
    tile = lambda w: pl.BlockSpec((1, tm, w), lambda bb, i: (bb, i, 0))
    xtile = pl.BlockSpec((1, tm, d), lambda bb, i: (xoff + bb, i, 0))
    flat = lambda w: pl.BlockSpec((tm, w), lambda bb, i: (blk_off + bb * nt + i, 0))
    full = lambda shape: pl.BlockSpec(shape, lambda bb, i: (0,) * len(shape))
    sub = min(MERGE_SUB, tm)
    assert L % tm == 0 and tm % sub == 0 and row_off % tm == 0
    ltri = (np.arange(sub)[:, None] > np.arange(sub)[None, :]).astype(np.float32)
    ltri = jnp.asarray(ltri, BF16)
    n_alias = 0 if bufs is None else len(bufs)
    n_in = 13
    return pl.pallas_call(
        functools.partial(_merge_kernel, n_alias=n_alias),
        grid=(b, nt),
        in_specs=[tile(256), tile(LOCAL_W), xtile,
                  pl.BlockSpec((1, 1, d), mod_map), pl.BlockSpec((1, 1, d), mod_map),
                  pl.BlockSpec((1, 1, d), mod_map),
                  full((1, d)), full(woa.shape), full(wol.shape), full(rw.shape), full(rb.shape),
                  full((sub, sub)), full((1, ROUTE_W))]
                 + [pl.BlockSpec(memory_space=pl.ANY)] * n_alias,
        out_specs=(tile(d), flat(d), flat(ROUTE_W),
                   pl.BlockSpec((2 * TOP_K, tm), lambda bb, i: (0, blk_off + bb * nt + i)),
                   full((1, ROUTE_W))),
        out_shape=(jax.ShapeDtypeStruct((b, L, d), F32),
                   jax.ShapeDtypeStruct((t_total, d), BF16),
                   jax.ShapeDtypeStruct((t_total, ROUTE_W), F32),
                   jax.ShapeDtypeStruct((2 * TOP_K, t_total), jnp.int32),
                   jax.ShapeDtypeStruct((1, ROUTE_W), F32)),
        scratch_shapes=[pltpu.VMEM((1, ROUTE_W), F32)],
        input_output_aliases={n_in + j: 1 + j for j in range(n_alias)},
        compiler_params=_cparams(("arbitrary", "arbitrary")),
        name="merge_router",
    )(att, loc, x, g1, sc2, sh2, n2g.reshape(1, d), woa, wol, rw, rb, ltri, cnt_in,
      *(bufs if bufs is not None else ()))


def _moe_kernel(blk_e_ref, n_used_ref, nxt_e_ref, x_ref, w1_hbm, b1_ref, w2_hbm, b2_ref, o_ref,
                w1f_ref, w2f_ref, w1s_ref, w2s_ref, sem, *, layer):
    i = pl.program_id(0)
    active = i < n_used_ref[0]
    e = blk_e_ref[i]
    new_expert = jnp.logical_or(i == 0, e != blk_e_ref[jnp.maximum(i - 1, 0)])

    def weight_copies(expert):
        return (pltpu.make_async_copy(w1_hbm.at[layer, expert], w1f_ref, sem.at[0]),
                pltpu.make_async_copy(w2_hbm.at[layer, expert], w2f_ref, sem.at[1]))

    @pl.when(jnp.logical_and(i == 0, active))
    def _():
        for cp in weight_copies(e):
            cp.start()

    @pl.when(jnp.logical_and(active, new_expert))
    def _():
        for cp in weight_copies(e):
            cp.wait()
        rows = 128
        for r in range(D_MODEL // rows):
            w1s_ref[r * rows:(r + 1) * rows, :] = w1f_ref[r * rows:(r + 1) * rows, :].astype(BF16)
        for r in range(D_FF // rows):
            w2s_ref[r * rows:(r + 1) * rows, :] = w2f_ref[r * rows:(r + 1) * rows, :].astype(BF16)
        nxt = nxt_e_ref[i]

        @pl.when(nxt >= 0)
        def _():
            for cp in weight_copies(nxt):
                cp.start()

    @pl.when(active)
    def _():
        x = x_ref[...]
        acc = jnp.zeros((x.shape[0], D_MODEL), F32)
        for c in range(D_FF // FF_CHUNK):
            lo = c * FF_CHUNK
            glu = jnp.dot(x, w1s_ref[:, lo:lo + FF_CHUNK], preferred_element_type=F32) \
                + b1_ref[0, 0, :, lo:lo + FF_CHUNK]
            lin = jnp.dot(x, w1s_ref[:, D_FF + lo:D_FF + lo + FF_CHUNK], preferred_element_type=F32) \
                + b1_ref[0, 0, :, D_FF + lo:D_FF + lo + FF_CHUNK]
            glu = jnp.minimum(glu, SWIGLU_LIMIT)
            lin = jnp.clip(lin, -SWIGLU_LIMIT, SWIGLU_LIMIT)
            act = glu * jax.nn.sigmoid(SWIGLU_ALPHA * glu) * (lin + 1.0)
            acc = acc + jnp.dot(act.astype(BF16), w2s_ref[lo:lo + FF_CHUNK, :], preferred_element_type=F32)
        o_ref[...] = (acc + b2_ref[0, 0]).astype(o_ref.dtype)

    @pl.when(i >= n_used_ref[0])
    def _():
        o_ref[...] = jnp.zeros(o_ref.shape, o_ref.dtype)


def _moe_ffn(xp, blk_e, n_used, nxt_e, w1, b1, w2, b2, layer):
    n_rows, d = xp.shape
    n_blk = n_rows // MOE_TM
    depth, e = w1.shape[:2]

    def row_map(i, be, nu, ne):
        return (jnp.maximum(jnp.minimum(i, nu[0] - 1), 0), 0)

    def b_map(i, be, nu, ne):
        return (layer, be[i], 0, 0)

    return pl.pallas_call(
        functools.partial(_moe_kernel, layer=layer),
        grid_spec=pltpu.PrefetchScalarGridSpec(
            num_scalar_prefetch=3,
            grid=(n_blk,),
            in_specs=[pl.BlockSpec((MOE_TM, d), row_map),
                      pl.BlockSpec(memory_space=pl.ANY),
                      pl.BlockSpec((1, 1, 1, 2 * D_FF), b_map),
                      pl.BlockSpec(memory_space=pl.ANY),
                      pl.BlockSpec((1, 1, 1, d), b_map)],
            out_specs=pl.BlockSpec((MOE_TM, d), lambda i, be, nu, ne: (i, 0)),
            scratch_shapes=[pltpu.VMEM((d, 2 * D_FF), F32), pltpu.VMEM((D_FF, d), F32),
                            pltpu.VMEM((d, 2 * D_FF), BF16), pltpu.VMEM((D_FF, d), BF16),
                            pltpu.SemaphoreType.DMA((2,))]),
        out_shape=jax.ShapeDtypeStruct((n_rows, d), BF16),
        compiler_params=_cparams(("arbitrary",)),
        name="moe_ffn",
    )(blk_e, n_used, nxt_e, xp, w1, b1.reshape(depth, e, 1, 2 * D_FF), w2, b2.reshape(depth, e, 1, d))


def _combine_kernel(*refs, final):
    yg_ref, gate_ref, x_ref, g2_ref, fg_ref = refs[:5]
    o_ref = refs[-1]
    gates = gate_ref[...]
    ff = jnp.zeros(x_ref.shape[1:], F32)
    for k in range(TOP_K):
        ff = ff + gates[:, k:k + 1] * yg_ref[k, 0].astype(F32)
    xo = x_ref[0] + g2_ref[0] * ff
    if final:
        ms = jnp.mean(xo * xo, axis=-1, keepdims=True)
        xo = xo * lax.rsqrt(ms + EPS) * fg_ref[...]
    o_ref[0] = xo


def _combine(yg, gates, x, g2, moff, fg, row_off, *, final, out_buf=None, ooff=0, out_b=None):
    b, L, d = x.shape
    tm = min(256, L)
    nt = L // tm
    blk_off = row_off // tm
    bm = g2.shape[0]
    out_b = b if out_b is None else out_b
    mod_map = (lambda bb, i: (moff + bb, 0, 0)) if bm > 1 else (lambda bb, i: (0, 0, 0))
    extra = () if out_buf is None else (out_buf,)
    return pl.pallas_call(
        functools.partial(_combine_kernel, final=final),
        grid=(b, nt),
        in_specs=[pl.BlockSpec((TOP_K, 1, tm, d), lambda bb, i: (0, bb, i, 0)),
                  pl.BlockSpec((tm, ROUTE_W), lambda bb, i: (blk_off + bb * nt + i, 0)),
                  pl.BlockSpec((1, tm, d), lambda bb, i: (bb, i, 0)),
                  pl.BlockSpec((1, 1, d), mod_map),
                  pl.BlockSpec((1, d), lambda bb, i: (0, 0))]
                 + [pl.BlockSpec(memory_space=pl.ANY)] * len(extra),
        out_specs=pl.BlockSpec((1, tm, d), lambda bb, i: (ooff + bb, i, 0)),
        out_shape=jax.ShapeDtypeStruct((out_b, L, d), F32),
        input_output_aliases={5: 0} if extra else {},
        compiler_params=_cparams(("arbitrary", "arbitrary")),
        name="combine",
    )(yg, gates, x, g2, fg.reshape(1, d), *extra)


def _dispatch_plan(idx, rank, counts):
    n_tok = idx.shape[1]
    n_asg = n_tok * TOP_K
    padded = (counts + MOE_TM - 1) // MOE_TM * MOE_TM
    pad_end = jnp.cumsum(padded)
    pad_start = (pad_end - padded).astype(jnp.int32)
    pos = rank
    for e in range(N_EXPERTS):
        pos = pos + jnp.where(idx == e, pad_start[e], 0)
    n_rows = -(-n_asg // MOE_TM) * MOE_TM + N_EXPERTS * MOE_TM
    n_blk = n_rows // MOE_TM
    tok = jnp.broadcast_to(jnp.arange(n_tok, dtype=jnp.int32)[None, :], (TOP_K, n_tok))
    fill = jnp.arange(n_rows, dtype=jnp.int32) % n_tok
    row_tok = fill.at[pos.reshape(-1)].add((tok - pos % n_tok).reshape(-1), mode="promise_in_bounds")
    blk_start = jnp.arange(n_blk, dtype=jnp.int32) * MOE_TM
    blk_e = jnp.minimum(jnp.sum((pad_end[None, :] <= blk_start[:, None]).astype(jnp.int32), axis=1),
                        N_EXPERTS - 1)
    n_used = (pad_end[-1] // MOE_TM).astype(jnp.int32).reshape(1)
    ids = jnp.arange(N_EXPERTS, dtype=jnp.int32)
    first_used_from = lax.cummin(jnp.where(padded > 0, ids, N_EXPERTS), axis=0, reverse=True)
    nxt_of = jnp.concatenate([first_used_from[1:], jnp.full((1,), N_EXPERTS, jnp.int32)])
    nxt_of = jnp.where(nxt_of >= N_EXPERTS, -1, nxt_of)
    nxt_e = jnp.sum(jnp.where(blk_e[:, None] == ids[None, :], nxt_of[None, :], 0), axis=1).astype(jnp.int32)
    return row_tok, pos, blk_e, n_used, nxt_e


_Q_PERM = np.concatenate([np.arange(0, 64), np.arange(128, 192), np.arange(64, 128), np.arange(192, 256)])


def _rope_tables(L):
    t = jnp.arange(L)
    row = (t // GRID_W).astype(F32)
    col = (t % GRID_W).astype(F32)
    half = HEAD_DIM // 2
    inv = ROPE_BASE ** (-jnp.arange(0, half, 2, dtype=F32) / half)
    ar, ac = row[:, None] * inv[None, :], col[:, None] * inv[None, :]
    z = jnp.zeros_like(ar)
    cos = jnp.concatenate([jnp.cos(ar), jnp.cos(ar), jnp.cos(ac), jnp.cos(ac)], axis=1)
    s1 = jnp.concatenate([-jnp.sin(ar), z, -jnp.sin(ac), z], axis=1)
    s2 = jnp.concatenate([z, jnp.sin(ar), z, jnp.sin(ac)], axis=1)
    rep = lambda a: jnp.concatenate([a, a], axis=1)
    return rep(cos), rep(s1), rep(s2)


def _block_diag(pw):
    g, n, _ = pw.shape
    out = jnp.zeros((g * n, g * n), pw.dtype)
    for gi in range(g):
        out = out.at[gi * n:(gi + 1) * n, gi * n:(gi + 1) * n].set(pw[gi])
    return out


def kernel(x, c, ctx, c_ctx, ada_w, ada_b, norm1_g, norm2_g, w_in, attn_sink, conv_dw_w, conv_dw_b, conv_ln_g, conv_ln_b, conv_pw_w, gmlp_ln_g, gmlp_ln_b, gmlp_ws, gmlp_bs, pool_w, pool_scale, group_norm_g, w_out, router_w, router_b, exp_w1, exp_b1, exp_w2, exp_b2, final_norm_g):
    b, L, d = x.shape
    lc = ctx.shape[1]
    depth = ada_w.shape[0]
    tabs = _rope_tables(L)

    r = -(-(b + 1) // 8) * 8
    cond = jnp.zeros((r, d), F32).at[:b].set(c).at[b].set(c_ctx)
    mods = _adaln(cond, ada_w, ada_b).reshape(depth, r, 6, d)

    n_groups = BATCH_GROUPS if b % BATCH_GROUPS == 0 else 1
    bh = b // n_groups
    xs, cs = [x] * n_groups, [ctx] * n_groups
    pending = [None] * n_groups

    def finish(st, out_buf):
        if st["last"]:
            return _combine(st["ygx"], st["gates"], st["xh"], st["g2"], st["moff"], final_norm_g, 0,
                            final=True, out_buf=out_buf, ooff=st["moff"], out_b=b)
        x_new = _combine(st["ygx"], st["gates"], st["xh"], st["g2"], st["moff"], final_norm_g, 0, final=False)
        c_new = _combine(st["ygc"], st["gates"], st["ch"], st["g2c"], 0, final_norm_g, bh * L, final=False)
        return x_new, c_new

    for l in range(depth):
        last = l == depth - 1
        mx = [mods[l, :b, i].reshape(b, 1, d) for i in range(6)]
        mc = [mods[l, b:b + 1, i].reshape(1, 1, d) for i in range(6)]
        sh1, sc1, g1, sh2, sc2, g2 = mx
        sh1c, sc1c, g1c, sh2c, sc2c, g2c = mc

        wl = w_in[l]
        wq = wl[:, :OFF_K][:, _Q_PERM] * (HEAD_DIM ** -0.5)
        w_full = jnp.concatenate([wq, wl[:, OFF_K:]], axis=1).astype(BF16)
        gn = group_norm_g[l]
        g_att = gn[:256][_Q_PERM]
        wo = w_out[l]
        woa = wo[:256][_Q_PERM].astype(BF16)
        wol = wo[256:].astype(BF16)
        lp = dict(
            dw_w=conv_dw_w[l], dw_b=conv_dw_b[l].reshape(1, -1),
            cln_g=conv_ln_g[l].reshape(1, -1), cln_b=conv_ln_b[l].reshape(1, -1),
            cpw=conv_pw_w[l].astype(BF16),
            gln_g=gmlp_ln_g[l].reshape(1, -1), gln_b=gmlp_ln_b[l].reshape(1, -1),
            gws=gmlp_ws[l].reshape(GMLP_HEADS * CHUNK, CHUNK).astype(BF16),
            gbs=jnp.repeat(gmlp_bs[l].T, GROUP_W // GMLP_HEADS, axis=1),
            pw=_block_diag(pool_w[l]).astype(BF16), ps=pool_scale[l].reshape(1, -1),
            gn_local=gn[256:].reshape(1, -1))
        rw = jnp.zeros((d, ROUTE_W), F32).at[:, :N_EXPERTS].set(router_w[l]).astype(BF16)
        rb = jnp.full((1, ROUTE_W), NEG_INF, F32).at[0, :N_EXPERTS].set(router_b[l])

        w_kv = wl[:, OFF_K:OFF_CONV].astype(BF16)
        take = lambda a, i: a.at[i].get(mode="promise_in_bounds")
        staged = []
        for h in range(n_groups):
            moff = h * bh
            pd = pending[h]
            xoff = moff if l == 0 else 0

            if pd is None:
                xh, ch = xs[h], cs[h]
                q, kv, mix = _inproj(xh, xoff, bh, sc1, sh1, moff, norm1_g[l], w_full, tabs, full=True)
                if last:
                    (kvc,) = _inproj(ch, xoff, bh, sc1c, sh1c, 0, norm1_g[l], w_kv, None, full=False)
                else:
                    qc, kvc, mixc = _inproj(ch, xoff, bh, sc1c, sh1c, 0, norm1_g[l], w_full, None, full=True)
            else:
                if staged:
                    pd["ygx"], staged[-1]["xh"] = lax.optimization_barrier((pd["ygx"], staged[-1]["xh"]))
                combx = (pd["ygx"], pd["gates"], 0, pd["g2"], pd["moff"])
                combc = (pd["ygc"], pd["gates"], bh * L, pd["g2c"], 0)
                xh, q, kv, mix = _inproj(pd["xh"], 0, bh, sc1, sh1, moff, norm1_g[l], w_full, tabs,
                                         full=True, comb=combx)
                if last:
                    ch = None
                    (kvc,) = _inproj(pd["ch"], 0, bh, sc1c, sh1c, 0, norm1_g[l], w_kv, None,
                                     full=False, comb=combc, emit_x=False)
                else:
                    ch, qc, kvc, mixc = _inproj(pd["ch"], 0, bh, sc1c, sh1c, 0, norm1_g[l], w_full, None,
                                                full=True, comb=combc)
                xoff = 0
            att = _attention(q, kv, kvc, attn_sink[l], g_att, local=True)
            loc = _local_mixers(mix, lp)
            if staged:
                prev = staged[-1]
                prev["xp"], loc = lax.optimization_barrier((prev["xp"], loc))
            nxt = pending[h + 1] if h + 1 < n_groups else None
            if nxt is not None:
                nxt["ygx"], loc = lax.optimization_barrier((nxt["ygx"], loc))
            t_total = bh * L if last else bh * (L + lc)
            cnt0 = jnp.zeros((1, ROUTE_W), F32)
            xh, h2, gates, ir, cnt = _merge(att, loc, xh, xoff, g1, sc2, sh2, moff, norm2_g[l], woa, wol,
                                            rw, rb, cnt0, None, 0, t_total)
            if not last:
                attc = _attention(qc, kvc, kvc, attn_sink[l], g_att, local=False)
                locc = _local_mixers(mixc, lp)
                ch, h2, gates, ir, cnt = _merge(attc, locc, ch, xoff, g1c, sc2c, sh2c, 0, norm2_g[l],
                                                woa, wol, rw, rb, cnt, (h2, gates, ir), bh * L, t_total)
            counts = cnt[0, :N_EXPERTS].astype(jnp.int32)
            row_tok, pos, blk_e, n_used, nxt_e = _dispatch_plan(ir[:TOP_K], ir[TOP_K:], counts)
            staged.append(dict(xh=xh, ch=ch, gates=gates, pos=pos, blk_e=blk_e, n_used=n_used, nxt_e=nxt_e,
                               xp=take(h2, row_tok), moff=moff, g2=g2, g2c=g2c, last=last))

        yps = []
        for h, st in enumerate(staged):
            if yps:
                yps[-1], st["xp"] = lax.optimization_barrier((yps[-1], st["xp"]))
            yps.append(_moe_ffn(st["xp"], st["blk_e"], st["n_used"], st["nxt_e"], exp_w1, exp_b1, exp_w2, exp_b2, l))
        for h, st in enumerate(staged):
            if h == n_groups - 1 and h > 0:
                yps[h], staged[0]["ygx"] = lax.optimization_barrier((yps[h], staged[0]["ygx"]))
            pos = st["pos"]
            st["ygx"] = take(yps[h], pos[:, :bh * L]).reshape(TOP_K, bh, L, d)
            if not last:
                st["ygc"] = take(yps[h], pos[:, bh * L:]).reshape(TOP_K, bh, lc, d)
            pending[h] = st

    out = None
    for h in range(n_groups):
        out = finish(pending[h], out)
    return out
```

```python
import functools

import jax
import jax.numpy as jnp
import numpy as np
from jax import lax
from jax.experimental import pallas as pl
from jax.experimental.pallas import tpu as pltpu

F32 = jnp.float32
BF16 = jnp.bfloat16

D_MODEL = 1024
GRID_W = 64
GROUP_W = 256
HEAD_DIM = 64
N_Q_HEADS = 4
N_KV_HEADS = 2
WINDOW = 128
ATT_BLOCK = 128
ROPE_BASE = 10000.0
CONV_WIDTH = 31
CHUNK = 128
GMLP_HEADS = 4
POOL_WINDOWS = (2, 4, 8, 16)
N_EXPERTS = 32
TOP_K = 4
D_FF = 1024
SWIGLU_ALPHA = 1.702
SWIGLU_LIMIT = 7.0
EPS = 1e-6
NEG_INF = -1e30

OFF_K = 256
OFF_V = 384
OFF_CONV = 512
IN_W = 1792
MIX_IN_W = IN_W - OFF_CONV
LOCAL_W = 3 * GROUP_W

LANES = 128
HALO = 16
ROUTE_W = LANES
ROW_TILE = 512
COMBINE_TILE = 256
ADA_TN = 1536
MOE_TM = 512
FF_CHUNK = 512
MERGE_SUB = 512
BATCH_GROUPS = 2
VMEM_LIMIT = 56 * 1024 * 1024


def _cparams(sem):
    return pltpu.CompilerParams(dimension_semantics=sem, vmem_limit_bytes=VMEM_LIMIT)


def _ada_kernel(c_ref, w_ref, b_ref, o_ref):
    c = c_ref[...]
    s = c * jax.nn.sigmoid(c)
    o_ref[0] = jnp.dot(s.astype(BF16), w_ref[0].astype(BF16), preferred_element_type=F32) + b_ref[0]


def _adaln(cond, ada_w, ada_b):
    depth, d, n = ada_w.shape
    r = cond.shape[0]
    tn = ADA_TN
    assert n % tn == 0
    return pl.pallas_call(
        _ada_kernel,
        grid=(depth, n // tn),
        in_specs=[pl.BlockSpec((r, d), lambda l, j: (0, 0)),
                  pl.BlockSpec((1, d, tn), lambda l, j: (l, 0, j)),
                  pl.BlockSpec((1, 1, tn), lambda l, j: (l, 0, j))],
        out_specs=pl.BlockSpec((1, r, tn), lambda l, j: (l, 0, j)),
        out_shape=jax.ShapeDtypeStruct((depth, r, n), F32),
        compiler_params=_cparams(("arbitrary", "arbitrary")),
        name="adaln",
    )(cond, ada_w, ada_b.reshape(depth, 1, n))


def _modulated_rmsnorm(x, g, sc, sh):
    ms = jnp.mean(x * x, axis=-1, keepdims=True)
    return x * lax.rsqrt(ms + EPS) * (g * (1.0 + sc)) + sh


def _weighted_expert_sum(yg_ref, gates):
    ff = gates[:, 0:1] * yg_ref[0, 0].astype(F32)
    for k in range(1, TOP_K):
        ff = ff + gates[:, k:k + 1] * yg_ref[k, 0].astype(F32)
    return ff


def _inproj_kernel(*refs, rope, full, fused, emit_x):
    refs = list(refs)
    if fused:
        yg_ref, gate_ref, g2_ref = refs[:3]
        refs = refs[3:]
    if rope:
        x_ref, sc_ref, sh_ref, g_ref, w_ref, cos_ref, s1_ref, s2_ref = refs[:8]
        outs = refs[8:]
    else:
        x_ref, sc_ref, sh_ref, g_ref, w_ref = refs[:5]
        outs = refs[5:]
    x = x_ref[0]
    if fused:
        x = x + g2_ref[0] * _weighted_expert_sum(yg_ref, gate_ref[...])
        if emit_x:
            outs[0][0] = x
            outs = outs[1:]
    h = _modulated_rmsnorm(x, g_ref[...], sc_ref[0], sh_ref[0])
    p = jnp.dot(h.astype(BF16), w_ref[...], preferred_element_type=F32)

    def roped(xs):
        if not rope:
            return xs
        return (xs * cos_ref[...] + pltpu.roll(xs, LANES - 16, 1) * s1_ref[...]
                + pltpu.roll(xs, 16, 1) * s2_ref[...])

    if full:
        q_ref, kv_ref, mix_ref = outs
        q_ref[0] = jnp.concatenate([roped(p[:, 0:128]), roped(p[:, 128:256])], axis=1).astype(BF16)
        kv_ref[0] = jnp.concatenate([roped(p[:, 256:384]), p[:, 384:512]], axis=1).astype(BF16)
        mix_ref[0] = p[:, OFF_CONV:]
    else:
        (kv_ref,) = outs
        kv_ref[0] = p.astype(BF16)


def _inproj(x, xoff, b, sc, sh, moff, g, w, tabs, *, full, comb=None, emit_x=True):
    _, L, d = x.shape
    n = w.shape[1]
    tm = min(ROW_TILE, L)
    nt = L // tm
    assert L % tm == 0
    rope = tabs is not None
    bm = sc.shape[0]
    mod_map = (lambda i, bb: (moff + bb, 0, 0)) if bm > 1 else (lambda i, bb: (0, 0, 0))
    in_specs, args = [], []
    if comb is not None:
        yg, gates, row_off, g2, g2off = comb
        assert row_off % tm == 0
        blk_off = row_off // tm
        g2_map = (lambda i, bb: (g2off + bb, 0, 0)) if g2.shape[0] > 1 else (lambda i, bb: (0, 0, 0))
        in_specs += [pl.BlockSpec((TOP_K, 1, tm, d), lambda i, bb: (0, bb, i, 0)),
                     pl.BlockSpec((tm, ROUTE_W), lambda i, bb: (blk_off + bb * nt + i, 0)),
                     pl.BlockSpec((1, 1, d), g2_map)]
        args += [yg, gates, g2]
    in_specs += [pl.BlockSpec((1, tm, d), lambda i, bb: (xoff + bb, i, 0)),
                 pl.BlockSpec((1, 1, d), mod_map),
                 pl.BlockSpec((1, 1, d), mod_map),
                 pl.BlockSpec((1, d), lambda i, bb: (0, 0)),
                 pl.BlockSpec((d, n), lambda i, bb: (0, 0))]
    args += [x, sc, sh, g.reshape(1, d), w]
    if rope:
        in_specs += [pl.BlockSpec((tm, LANES), lambda i, bb: (i, 0))] * 3
        args += list(tabs)
    tile = lambda wd: pl.BlockSpec((1, tm, wd), lambda i, bb: (bb, i, 0))
    out_shape, out_specs = [], []
    emit_x = emit_x and comb is not None
    if emit_x:
        out_shape.append(jax.ShapeDtypeStruct((b, L, d), F32))
        out_specs.append(tile(d))
    if full:
        out_shape += [jax.ShapeDtypeStruct((b, L, 256), BF16),
                      jax.ShapeDtypeStruct((b, L, 256), BF16),
                      jax.ShapeDtypeStruct((b, L, MIX_IN_W), F32)]
        out_specs += [tile(256), tile(256), tile(MIX_IN_W)]
    else:
        out_shape.append(jax.ShapeDtypeStruct((b, L, 256), BF16))
        out_specs.append(tile(256))
    return pl.pallas_call(
        functools.partial(_inproj_kernel, rope=rope, full=full, fused=comb is not None, emit_x=emit_x),
        grid=(nt, b),
        in_specs=in_specs, out_specs=tuple(out_specs), out_shape=tuple(out_shape),
        compiler_params=_cparams(("arbitrary", "arbitrary")),
        name="inproj_x" if rope else "inproj_ctx",
    )(*args)


def _attn_kernel(*refs, local, seq_len, q_blocks):
    if local:
        sink_ref, q_ref, kv_ref, kvc_ref, g_ref, bias_ref, o_ref = refs
    else:
        sink_ref, q_ref, kv_ref, kvc_ref, g_ref, o_ref = refs
    step = pl.program_id(1)
    lane = lax.broadcasted_iota(jnp.int32, (1, LANES), 1)
    row = lax.broadcasted_iota(jnp.int32, (2 * ATT_BLOCK, 1), 0)
    kvc = kvc_ref[0]
    kc, vc = kvc[:, :LANES], kvc[:, LANES:]
    lc = kvc.shape[0]
    nt = (((1,), (1,)), ((), ()))
    zero = jnp.zeros((), BF16)
    lm0 = lane < HEAD_DIM
    for sb in range(q_blocks):
        n = step * q_blocks + sb
        q = q_ref[0, sb * ATT_BLOCK:(sb + 1) * ATT_BLOCK, :]
        if local:
            blk0 = jnp.clip(n - 1, 0, seq_len // ATT_BLOCK - 3)
            kw = kv_ref[0, pl.ds(pl.multiple_of(blk0 * ATT_BLOCK, ATT_BLOCK), 3 * ATT_BLOCK), :]
            kl, vl = kw[:, :LANES], kw[:, LANES:]
            bias = bias_ref[n - blk0]
        outs = []
        for kh in range(N_KV_HEADS):
            lm = lm0 if kh == 0 else jnp.logical_not(lm0)
            lhs = jnp.concatenate([jnp.where(lm, q[:, :LANES], zero), jnp.where(lm, q[:, LANES:], zero)],
                                  axis=0)
            sk = jnp.where(row < ATT_BLOCK, sink_ref[2 * kh], sink_ref[2 * kh + 1])
            s = lax.dot_general(lhs, kc, nt, preferred_element_type=F32)
            if local:
                s_l = lax.dot_general(lhs, kl, nt, preferred_element_type=F32) + bias
                s = jnp.concatenate([s, s_l], axis=1)
            m = jnp.maximum(sk, jnp.max(s, axis=-1, keepdims=True))
            e = jnp.exp(s - m)
            den = jnp.exp(sk - m) + jnp.sum(e, axis=-1, keepdims=True)
            e = e.astype(BF16)
            o = jnp.dot(e[:, :lc], vc, preferred_element_type=F32)
            if local:
                o = o + jnp.dot(e[:, lc:], vl, preferred_element_type=F32)
            outs.append(o / den)
        o01, o23 = outs
        att = jnp.concatenate([jnp.where(lm0, o01[:ATT_BLOCK], o23[:ATT_BLOCK]),
                               jnp.where(lm0, o01[ATT_BLOCK:], o23[ATT_BLOCK:])], axis=1)
        ms = jnp.mean(att * att, axis=-1, keepdims=True)
        o_ref[0, sb * ATT_BLOCK:(sb + 1) * ATT_BLOCK, :] = (att * lax.rsqrt(ms + EPS) * g_ref[...]).astype(BF16)


def _band_bias():
    i = np.arange(2 * ATT_BLOCK)[None, :, None] % ATT_BLOCK
    j = np.arange(3 * ATT_BLOCK)[None, None, :]
    v = np.arange(3)[:, None, None]
    return jnp.asarray(np.where(np.abs(j - i - ATT_BLOCK * v) <= WINDOW, 0.0, NEG_INF), F32)


def _attention(q, kv, kvc, sink, g_att, *, local):
    b, L, _ = q.shape
    lc = kvc.shape[1]
    nb = L // ATT_BLOCK
    assert not local or nb >= 3
    q_blocks = max(q for q in (4, 2, 1) if nb % q == 0)
    tq = q_blocks * ATT_BLOCK
    in_specs = [pl.BlockSpec(memory_space=pltpu.SMEM),
                pl.BlockSpec((1, tq, 256), lambda bb, i: (bb, i, 0)),
                pl.BlockSpec((1, kv.shape[1], 256), lambda bb, i: (bb, 0, 0)),
                pl.BlockSpec((1, lc, 256), lambda bb, i: (bb, 0, 0)),
                pl.BlockSpec((1, 256), lambda bb, i: (0, 0))]
    args = [sink, q, kv, kvc, g_att.reshape(1, 256)]
    if local:
        in_specs.append(pl.BlockSpec((3, 2 * ATT_BLOCK, 3 * ATT_BLOCK), lambda bb, i: (0, 0, 0)))
        args.append(_band_bias())
    return pl.pallas_call(
        functools.partial(_attn_kernel, local=local, seq_len=L, q_blocks=q_blocks),
        grid=(b, nb // q_blocks),
        in_specs=in_specs,
        out_specs=pl.BlockSpec((1, tq, 256), lambda bb, i: (bb, i, 0)),
        out_shape=jax.ShapeDtypeStruct((b, L, 256), BF16),
        compiler_params=_cparams(("arbitrary", "arbitrary")),
        name="attn_window" if local else "attn_ctx",
    )(*args)


def _layernorm(x, g, b):
    mu = jnp.mean(x, axis=-1, keepdims=True)
    xc = x - mu
    return xc * lax.rsqrt(jnp.mean(xc * xc, axis=-1, keepdims=True) + EPS) * g + b


def _group_norm_store(y, g):
    ms = jnp.mean(y * y, axis=-1, keepdims=True)
    return (y * lax.rsqrt(ms + EPS) * g).astype(BF16)


def _mix_kernel(cur_ref, prev_ref, next_ref, dww_ref, dwb_ref, clg_ref, clb_ref, cpw_ref,
                glg_ref, glb_ref, gws_ref, gbs_ref, pw_ref, ps_ref, gn_ref, o_ref,
                hc_ref, sh_ref, hp_ref, c2_ref, c4_ref, c8_ref, *, ts, seq_len):
    i = pl.program_id(1)
    nt = pl.num_programs(1)
    has_prev = (i > 0).astype(F32)
    has_next = (i < nt - 1).astype(F32)

    def glu(blk):
        return blk[:, 0:256] * jax.nn.sigmoid(blk[:, 256:512])

    pv = prev_ref[0]
    nx = next_ref[0]
    hc_ref[0:HALO, :] = glu(pv) * has_prev
    hc_ref[HALO + ts:2 * HALO + ts, :] = glu(nx) * has_next
    hp_ref[0:HALO, :] = pv[:, 1024:1280] * has_prev
    hp_ref[HALO + ts:2 * HALO + ts, :] = nx[:, 1024:1280] * has_next
    hp_ref[2 * HALO + ts:3 * HALO + ts, :] = jnp.zeros((HALO, 256), F32)
    hc_ref[HALO:HALO + ts, :] = glu(cur_ref[0, :, 0:512])
    hp_ref[HALO:HALO + ts, :] = cur_ref[0, :, 1024:1280]

    for r in range(1, 8):
        sh_ref[r - 1] = hc_ref[r:r + ts + 24, :]
    c2_ref[8:ts + 40, :] = hp_ref[8:ts + 40, :] + hp_ref[9:ts + 41, :]
    c4_ref[8:ts + 32, :] = c2_ref[8:ts + 32, :] + c2_ref[10:ts + 34, :]
    c8_ref[8:ts + 24, :] = c4_ref[8:ts + 24, :] + c4_ref[12:ts + 28, :]

    lane = lax.broadcasted_iota(jnp.int32, (1, 256), 1)
    rc = min(128, ts)
    for c in range(ts // rc):
        r0 = c * rc
        acc = jnp.zeros((rc, 256), F32)
        for j in range(CONV_WIDTH):
            a, r = divmod(1 + j, 8)
            lo = r0 + 8 * a
            tap = hc_ref[lo:lo + rc, :] if r == 0 else sh_ref[r - 1, lo:lo + rc, :]
            acc = acc + tap * dww_ref[j:j + 1, :]
        hcv = _layernorm(acc + dwb_ref[...], clg_ref[...], clb_ref[...])
        hcv = hcv * jax.nn.sigmoid(hcv)
        conv = jnp.dot(hcv.astype(BF16), cpw_ref[...], preferred_element_type=F32)
        o_ref[0, r0:r0 + rc, 0:256] = _group_norm_store(conv, gn_ref[:, 0:256])

        s0 = r0 + HALO
        hcur = hp_ref[s0:s0 + rc, :]
        s2 = c2_ref[s0 - 1:s0 - 1 + rc, :]
        s4 = c4_ref[s0 - 2:s0 - 2 + rc, :]
        s8 = c8_ref[s0 - 4:s0 - 4 + rc, :]
        s16 = c8_ref[s0 - 8:s0 - 8 + rc, :] + c8_ref[s0:s0 + rc, :]
        wsum = jnp.where(lane < 64, s2, jnp.where(lane < 128, s4, jnp.where(lane < 192, s8, s16)))
        half = jnp.where(lane < 64, 1, jnp.where(lane < 128, 2, jnp.where(lane < 192, 4, 8)))
        t = i * ts + r0 + lax.broadcasted_iota(jnp.int32, (rc, 1), 0)
        cnt = jnp.clip(t + half, 0, seq_len) - jnp.clip(t - half, 0, seq_len)
        y = wsum / cnt.astype(F32) - hcur
        pool = jnp.dot(y.astype(BF16), pw_ref[...], preferred_element_type=F32) * ps_ref[...]
        o_ref[0, r0:r0 + rc, 512:768] = _group_norm_store(pool, gn_ref[:, 512:768])

    for c in range(ts // CHUNK):
        r0 = c * CHUNK
        u = cur_ref[0, r0:r0 + CHUNK, 512:768]
        v = _layernorm(cur_ref[0, r0:r0 + CHUNK, 768:1024], glg_ref[...], glb_ref[...])
        r = jnp.dot(gws_ref[...], v.astype(BF16), preferred_element_type=F32)
        mixed = jnp.where(lane < 64, r[0:CHUNK],
                          jnp.where(lane < 128, r[CHUNK:2 * CHUNK],
                                    jnp.where(lane < 192, r[2 * CHUNK:3 * CHUNK], r[3 * CHUNK:])))
        gm = u * (mixed + gbs_ref[...])
        o_ref[0, r0:r0 + CHUNK, 256:512] = _group_norm_store(gm, gn_ref[:, 256:512])


def _local_mixers(mix, lp):
    b, L, w = mix.shape
    ts = min(ROW_TILE, L)
    assert L % ts == 0 and ts % CHUNK == 0
    nt = L // ts
    hb = ts // HALO
    last_h = L // HALO - 1
    full = lambda shape: pl.BlockSpec(shape, lambda bb, i: (0,) * len(shape))
    params = [lp["dw_w"], lp["dw_b"], lp["cln_g"], lp["cln_b"], lp["cpw"], lp["gln_g"], lp["gln_b"],
              lp["gws"], lp["gbs"], lp["pw"], lp["ps"], lp["gn_local"]]
    return pl.pallas_call(
        functools.partial(_mix_kernel, ts=ts, seq_len=L),
        grid=(b, nt),
        in_specs=[pl.BlockSpec((1, ts, w), lambda bb, i: (bb, i, 0)),
                  pl.BlockSpec((1, HALO, w), lambda bb, i: (bb, jnp.maximum(i * hb - 1, 0), 0)),
                  pl.BlockSpec((1, HALO, w), lambda bb, i: (bb, jnp.minimum((i + 1) * hb, last_h), 0))]
                 + [full(p.shape) for p in params],
        out_specs=pl.BlockSpec((1, ts, LOCAL_W), lambda bb, i: (bb, i, 0)),
        out_shape=jax.ShapeDtypeStruct((b, L, LOCAL_W), BF16),
        scratch_shapes=[pltpu.VMEM((ts + 2 * HALO, 256), F32), pltpu.VMEM((7, ts + 24, 256), F32)]
                       + [pltpu.VMEM((ts + 3 * HALO, 256), F32)] * 4,
        compiler_params=_cparams(("arbitrary", "arbitrary")),
        name="local_mixers",
    )(mix, mix, mix, *params)


def _merge_kernel(*refs, n_alias):
    (att_ref, loc_ref, x_ref, g1_ref, sc_ref, sh_ref, n2g_ref, woa_ref, wol_ref,
     rw_ref, rb_ref, ltri_ref, cnt_in_ref) = refs[:13]
    xo_ref, h2_ref, gate_ref, ir_ref, cnt_ref, run_ref = refs[13 + n_alias:]

    @pl.when(jnp.logical_and(pl.program_id(0) == 0, pl.program_id(1) == 0))
    def _():
        run_ref[...] = cnt_in_ref[...]

    tm = x_ref.shape[1]
    sub = ltri_ref.shape[0]
    lane = lax.broadcasted_iota(jnp.int32, (sub, ROUTE_W), 1).astype(F32)
    for s in range(tm // sub):
        r0 = s * sub
        y = (jnp.dot(att_ref[0, r0:r0 + sub, :], woa_ref[...], preferred_element_type=F32)
             + jnp.dot(loc_ref[0, r0:r0 + sub, :], wol_ref[...], preferred_element_type=F32))
        xn = x_ref[0, r0:r0 + sub, :] + g1_ref[0] * y
        xo_ref[0, r0:r0 + sub, :] = xn
        h2 = _modulated_rmsnorm(xn, n2g_ref[...], sc_ref[0], sh_ref[0]).astype(BF16)
        h2_ref[r0:r0 + sub, :] = h2
        logits = jnp.dot(h2, rw_ref[...], preferred_element_type=F32) + rb_ref[...]
        vals = jnp.zeros(logits.shape, F32)
        idxs = jnp.zeros(logits.shape, F32)
        hits = jnp.zeros(logits.shape, F32)
        sels = []
        top = None
        for k in range(TOP_K):
            m = jnp.max(logits, axis=-1, keepdims=True)
            sel = jnp.min(jnp.where(logits == m, lane, float(ROUTE_W)), axis=-1, keepdims=True)
            if top is None:
                top = m
            chosen = lane == sel
            vals = jnp.where(lane == float(k), jnp.exp(m - top), vals)
            idxs = jnp.where(lane == float(k), sel, idxs)
            hits = jnp.where(chosen, 1.0, hits)
            logits = jnp.where(chosen, -jnp.inf, logits)
            sels.append(chosen)
        gate_ref[r0:r0 + sub, :] = vals / jnp.sum(vals, axis=-1, keepdims=True)
        base = run_ref[...] + jnp.dot(ltri_ref[...], hits.astype(BF16), preferred_element_type=F32)
        packed = idxs
        for k in range(TOP_K):
            rk = jnp.sum(jnp.where(sels[k], base, 0.0), axis=-1, keepdims=True)
            packed = jnp.where(lane == float(TOP_K + k), rk, packed)
        ir_ref[:, r0:r0 + sub] = packed.T[0:2 * TOP_K, :].astype(jnp.int32)
        run_ref[...] = run_ref[...] + jnp.sum(hits, axis=0, keepdims=True)
    cnt_ref[...] = run_ref[...]


def _merge(att, loc, x, xoff, g1, sc2, sh2, moff, n2g, woa, wol, rw, rb, cnt_in, bufs, row_off, t_total):
    b = att.shape[0]
    _, L, d = x.shape
    tm = min(ROW_TILE, L)
    nt = L // tm
    blk_off = row_off // tm
    bm = g1.shape[0]
    mod_map = (lambda bb, i: (moff + bb, 0, 0)) if bm > 1 else (lambda bb, i: (0, 0, 0))
    tile = lambda w: pl.BlockSpec((1, tm, w), lambda bb, i: (bb, i, 0))
    xtile = pl.BlockSpec((1, tm, d), lambda bb, i: (xoff + bb, i, 0))
    flat = lambda w: pl.BlockSpec((tm, w), lambda bb, i: (blk_off + bb * nt + i, 0))
    full = lambda shape: pl.BlockSpec(shape, lambda bb, i: (0,) * len(shape))
    sub = min(MERGE_SUB, tm)
    assert L % tm == 0 and tm % sub == 0 and row_off % tm == 0
    ltri = (np.arange(sub)[:, None] > np.arange(sub)[None, :]).astype(np.float32)
    ltri = jnp.asarray(ltri, BF16)
    n_alias = 0 if bufs is None else len(bufs)
    n_in = 13
    return pl.pallas_call(
        functools.partial(_merge_kernel, n_alias=n_alias),
        grid=(b, nt),
        in_specs=[tile(256), tile(LOCAL_W), xtile,
                  pl.BlockSpec((1, 1, d), mod_map), pl.BlockSpec((1, 1, d), mod_map),
                  pl.BlockSpec((1, 1, d), mod_map),
                  full((1, d)), full(woa.shape), full(wol.shape), full(rw.shape), full(rb.shape),
                  full((sub, sub)), full((1, ROUTE_W))]
                 + [pl.BlockSpec(memory_space=pl.ANY)] * n_alias,
        out_specs=(tile(d), flat(d), flat(ROUTE_W),
                   pl.BlockSpec((2 * TOP_K, tm), lambda bb, i: (0, blk_off + bb * nt + i)),
                   full((1, ROUTE_W))),
        out_shape=(jax.ShapeDtypeStruct((b, L, d), F32),
                   jax.ShapeDtypeStruct((t_total, d), BF16),
                   jax.ShapeDtypeStruct((t_total, ROUTE_W), F32),
                   jax.ShapeDtypeStruct((2 * TOP_K, t_total), jnp.int32),
                   jax.ShapeDtypeStruct((1, ROUTE_W), F32)),
        scratch_shapes=[pltpu.VMEM((1, ROUTE_W), F32)],
        input_output_aliases={n_in + j: 1 + j for j in range(n_alias)},
        compiler_params=_cparams(("arbitrary", "arbitrary")),
        name="merge_router",
    )(att, loc, x, g1, sc2, sh2, n2g.reshape(1, d), woa, wol, rw, rb, ltri, cnt_in,
      *(bufs if bufs is not None else ()))


def _moe_kernel(blk_e_ref, n_used_ref, nxt_e_ref, x_ref, w1_hbm, b1_ref, w2_hbm, b2_ref, o_ref,
                w1f_ref, w2f_ref, w1s_ref, w2s_ref, sem, *, layer):
    i = pl.program_id(0)
    active = i < n_used_ref[0]
    e = blk_e_ref[i]
    new_expert = jnp.logical_or(i == 0, e != blk_e_ref[jnp.maximum(i - 1, 0)])

    def weight_copies(expert):
        return (pltpu.make_async_copy(w1_hbm.at[layer, expert], w1f_ref, sem.at[0]),
                pltpu.make_async_copy(w2_hbm.at[layer, expert], w2f_ref, sem.at[1]))

    @pl.when(jnp.logical_and(i == 0, active))
    def _():
        for cp in weight_copies(e):
            cp.start()

    @pl.when(jnp.logical_and(active, new_expert))
    def _():
        for cp in weight_copies(e):
            cp.wait()
        rows = 128
        for r in range(D_MODEL // rows):
            w1s_ref[r * rows:(r + 1) * rows, :] = w1f_ref[r * rows:(r + 1) * rows, :].astype(BF16)
        for r in range(D_FF // rows):
            w2s_ref[r * rows:(r + 1) * rows, :] = w2f_ref[r * rows:(r + 1) * rows, :].astype(BF16)
        nxt = nxt_e_ref[i]

        @pl.when(nxt >= 0)
        def _():
            for cp in weight_copies(nxt):
                cp.start()

    @pl.when(active)
    def _():
        x = x_ref[...]
        acc = jnp.zeros((x.shape[0], D_MODEL), F32)
        for c in range(D_FF // FF_CHUNK):
            lo = c * FF_CHUNK
            glu = jnp.dot(x, w1s_ref[:, lo:lo + FF_CHUNK], preferred_element_type=F32) \
                + b1_ref[0, 0, :, lo:lo + FF_CHUNK]
            lin = jnp.dot(x, w1s_ref[:, D_FF + lo:D_FF + lo + FF_CHUNK], preferred_element_type=F32) \
                + b1_ref[0, 0, :, D_FF + lo:D_FF + lo + FF_CHUNK]
            glu = jnp.minimum(glu, SWIGLU_LIMIT)
            lin = jnp.clip(lin, -SWIGLU_LIMIT, SWIGLU_LIMIT)
            act = glu * jax.nn.sigmoid(SWIGLU_ALPHA * glu) * (lin + 1.0)
            acc = acc + jnp.dot(act.astype(BF16), w2s_ref[lo:lo + FF_CHUNK, :], preferred_element_type=F32)
        o_ref[...] = (acc + b2_ref[0, 0]).astype(o_ref.dtype)

    @pl.when(i >= n_used_ref[0])
    def _():
        o_ref[...] = jnp.zeros(o_ref.shape, o_ref.dtype)


def _moe_ffn(xp, blk_e, n_used, nxt_e, w1, b1, w2, b2, layer):
    n_rows, d = xp.shape
    n_blk = n_rows // MOE_TM
    depth, e = w1.shape[:2]

    def row_map(i, be, nu, ne):
        return (jnp.maximum(jnp.minimum(i, nu[0] - 1), 0), 0)

    def b_map(i, be, nu, ne):
        return (layer, be[i], 0, 0)

    return pl.pallas_call(
        functools.partial(_moe_kernel, layer=layer),
        grid_spec=pltpu.PrefetchScalarGridSpec(
            num_scalar_prefetch=3,
            grid=(n_blk,),
            in_specs=[pl.BlockSpec((MOE_TM, d), row_map),
                      pl.BlockSpec(memory_space=pl.ANY),
                      pl.BlockSpec((1, 1, 1, 2 * D_FF), b_map),
                      pl.BlockSpec(memory_space=pl.ANY),
                      pl.BlockSpec((1, 1, 1, d), b_map)],
            out_specs=pl.BlockSpec((MOE_TM, d), lambda i, be, nu, ne: (i, 0)),
            scratch_shapes=[pltpu.VMEM((d, 2 * D_FF), F32), pltpu.VMEM((D_FF, d), F32),
                            pltpu.VMEM((d, 2 * D_FF), BF16), pltpu.VMEM((D_FF, d), BF16),
                            pltpu.SemaphoreType.DMA((2,))]),
        out_shape=jax.ShapeDtypeStruct((n_rows, d), BF16),
        compiler_params=_cparams(("arbitrary",)),
        name="moe_ffn",
    )(blk_e, n_used, nxt_e, xp, w1, b1.reshape(depth, e, 1, 2 * D_FF), w2, b2.reshape(depth, e, 1, d))


def _combine_kernel(*refs):
    yg_ref, gate_ref, x_ref, g2_ref, fg_ref = refs[:5]
    o_ref = refs[-1]
    xo = x_ref[0] + g2_ref[0] * _weighted_expert_sum(yg_ref, gate_ref[...])
    ms = jnp.mean(xo * xo, axis=-1, keepdims=True)
    o_ref[0] = xo * lax.rsqrt(ms + EPS) * fg_ref[...]


def _combine(yg, gates, x, g2, moff, fg, row_off, *, out_buf=None, ooff=0, out_b=None):
    b, L, d = x.shape
    tm = min(COMBINE_TILE, L)
    assert L % tm == 0 and row_off % tm == 0
    nt = L // tm
    blk_off = row_off // tm
    bm = g2.shape[0]
    out_b = b if out_b is None else out_b
    mod_map = (lambda bb, i: (moff + bb, 0, 0)) if bm > 1 else (lambda bb, i: (0, 0, 0))
    extra = () if out_buf is None else (out_buf,)
    return pl.pallas_call(
        _combine_kernel,
        grid=(b, nt),
        in_specs=[pl.BlockSpec((TOP_K, 1, tm, d), lambda bb, i: (0, bb, i, 0)),
                  pl.BlockSpec((tm, ROUTE_W), lambda bb, i: (blk_off + bb * nt + i, 0)),
                  pl.BlockSpec((1, tm, d), lambda bb, i: (bb, i, 0)),
                  pl.BlockSpec((1, 1, d), mod_map),
                  pl.BlockSpec((1, d), lambda bb, i: (0, 0))]
                 + [pl.BlockSpec(memory_space=pl.ANY)] * len(extra),
        out_specs=pl.BlockSpec((1, tm, d), lambda bb, i: (ooff + bb, i, 0)),
        out_shape=jax.ShapeDtypeStruct((out_b, L, d), F32),
        input_output_aliases={5: 0} if extra else {},
        compiler_params=_cparams(("arbitrary", "arbitrary")),
        name="combine",
    )(yg, gates, x, g2, fg.reshape(1, d), *extra)


def _dispatch_plan(idx, rank, counts):
    n_tok = idx.shape[1]
    n_asg = n_tok * TOP_K
    padded = (counts + MOE_TM - 1) // MOE_TM * MOE_TM
    pad_end = jnp.cumsum(padded)
    pad_start = (pad_end - padded).astype(jnp.int32)
    pos = rank
    for e in range(N_EXPERTS):
        pos = pos + jnp.where(idx == e, pad_start[e], 0)
    n_rows = -(-n_asg // MOE_TM) * MOE_TM + N_EXPERTS * MOE_TM
    n_blk = n_rows // MOE_TM
    tok = jnp.broadcast_to(jnp.arange(n_tok, dtype=jnp.int32)[None, :], (TOP_K, n_tok))
    fill = jnp.arange(n_rows, dtype=jnp.int32) % n_tok
    row_tok = fill.at[pos.reshape(-1)].add((tok - pos % n_tok).reshape(-1), mode="promise_in_bounds")
    blk_start = jnp.arange(n_blk, dtype=jnp.int32) * MOE_TM
    blk_e = jnp.minimum(jnp.sum((pad_end[None, :] <= blk_start[:, None]).astype(jnp.int32), axis=1),
                        N_EXPERTS - 1)
    n_used = (pad_end[-1] // MOE_TM).astype(jnp.int32).reshape(1)
    ids = jnp.arange(N_EXPERTS, dtype=jnp.int32)
    first_used_from = lax.cummin(jnp.where(padded > 0, ids, N_EXPERTS), axis=0, reverse=True)
    nxt_of = jnp.concatenate([first_used_from[1:], jnp.full((1,), N_EXPERTS, jnp.int32)])
    nxt_of = jnp.where(nxt_of >= N_EXPERTS, -1, nxt_of)
    nxt_e = jnp.sum(jnp.where(blk_e[:, None] == ids[None, :], nxt_of[None, :], 0), axis=1).astype(jnp.int32)
    return row_tok, pos, blk_e, n_used, nxt_e


_Q_PERM = np.concatenate([np.arange(0, 64), np.arange(128, 192), np.arange(64, 128), np.arange(192, 256)])


def _rope_tables(L):
    t = jnp.arange(L)
    row = (t // GRID_W).astype(F32)
    col = (t % GRID_W).astype(F32)
    half = HEAD_DIM // 2
    inv = ROPE_BASE ** (-jnp.arange(0, half, 2, dtype=F32) / half)
    ar, ac = row[:, None] * inv[None, :], col[:, None] * inv[None, :]
    z = jnp.zeros_like(ar)
    cos = jnp.concatenate([jnp.cos(ar), jnp.cos(ar), jnp.cos(ac), jnp.cos(ac)], axis=1)
    s1 = jnp.concatenate([-jnp.sin(ar), z, -jnp.sin(ac), z], axis=1)
    s2 = jnp.concatenate([z, jnp.sin(ar), z, jnp.sin(ac)], axis=1)
    rep = lambda a: jnp.concatenate([a, a], axis=1)
    return rep(cos), rep(s1), rep(s2)


def _block_diag(pw):
    g, n, _ = pw.shape
    out = jnp.zeros((g * n, g * n), pw.dtype)
    for gi in range(g):
        out = out.at[gi * n:(gi + 1) * n, gi * n:(gi + 1) * n].set(pw[gi])
    return out


def kernel(x, c, ctx, c_ctx, ada_w, ada_b, norm1_g, norm2_g, w_in, attn_sink, conv_dw_w, conv_dw_b, conv_ln_g, conv_ln_b, conv_pw_w, gmlp_ln_g, gmlp_ln_b, gmlp_ws, gmlp_bs, pool_w, pool_scale, group_norm_g, w_out, router_w, router_b, exp_w1, exp_b1, exp_w2, exp_b2, final_norm_g):
    b, L, d = x.shape
    lc = ctx.shape[1]
    depth = ada_w.shape[0]
    tabs = _rope_tables(L)

    r = -(-(b + 1) // 8) * 8
    cond = jnp.zeros((r, d), F32).at[:b].set(c).at[b].set(c_ctx)
    mods = _adaln(cond, ada_w, ada_b).reshape(depth, r, 6, d)

    n_groups = BATCH_GROUPS if b % BATCH_GROUPS == 0 else 1
    bh = b // n_groups
    xs, cs = [x] * n_groups, [ctx] * n_groups
    pending = [None] * n_groups

    def finish(st, out_buf):
        return _combine(st["ygx"], st["gates"], st["xh"], st["g2"], st["moff"], final_norm_g, 0,
                        out_buf=out_buf, ooff=st["moff"], out_b=b)

    for l in range(depth):
        last = l == depth - 1
        mx = [mods[l, :b, i].reshape(b, 1, d) for i in range(6)]
        mc = [mods[l, b:b + 1, i].reshape(1, 1, d) for i in range(6)]
        sh1, sc1, g1, sh2, sc2, g2 = mx
        sh1c, sc1c, g1c, sh2c, sc2c, g2c = mc

        wl = w_in[l]
        wq = wl[:, :OFF_K][:, _Q_PERM] * (HEAD_DIM ** -0.5)
        w_full = jnp.concatenate([wq, wl[:, OFF_K:]], axis=1).astype(BF16)
        gn = group_norm_g[l]
        g_att = gn[:256][_Q_PERM]
        wo = w_out[l]
        woa = wo[:256][_Q_PERM].astype(BF16)
        wol = wo[256:].astype(BF16)
        lp = dict(
            dw_w=conv_dw_w[l], dw_b=conv_dw_b[l].reshape(1, -1),
            cln_g=conv_ln_g[l].reshape(1, -1), cln_b=conv_ln_b[l].reshape(1, -1),
            cpw=conv_pw_w[l].astype(BF16),
            gln_g=gmlp_ln_g[l].reshape(1, -1), gln_b=gmlp_ln_b[l].reshape(1, -1),
            gws=gmlp_ws[l].reshape(GMLP_HEADS * CHUNK, CHUNK).astype(BF16),
            gbs=jnp.repeat(gmlp_bs[l].T, GROUP_W // GMLP_HEADS, axis=1),
            pw=_block_diag(pool_w[l]).astype(BF16), ps=pool_scale[l].reshape(1, -1),
            gn_local=gn[256:].reshape(1, -1))
        rw = jnp.zeros((d, ROUTE_W), F32).at[:, :N_EXPERTS].set(router_w[l]).astype(BF16)
        rb = jnp.full((1, ROUTE_W), -jnp.inf, F32).at[0, :N_EXPERTS].set(router_b[l])

        w_kv = wl[:, OFF_K:OFF_CONV].astype(BF16)
        take = lambda a, i: a.at[i].get(mode="promise_in_bounds")
        staged = []
        for h in range(n_groups):
            moff = h * bh
            pd = pending[h]
            xoff = moff if l == 0 else 0

            if pd is None:
                xh, ch = xs[h], cs[h]
                q, kv, mix = _inproj(xh, xoff, bh, sc1, sh1, moff, norm1_g[l], w_full, tabs, full=True)
                if last:
                    (kvc,) = _inproj(ch, xoff, bh, sc1c, sh1c, 0, norm1_g[l], w_kv, None, full=False)
                else:
                    qc, kvc, mixc = _inproj(ch, xoff, bh, sc1c, sh1c, 0, norm1_g[l], w_full, None, full=True)
            else:
                if staged:
                    pd["ygx"], staged[-1]["xh"] = lax.optimization_barrier((pd["ygx"], staged[-1]["xh"]))
                combx = (pd["ygx"], pd["gates"], 0, pd["g2"], pd["moff"])
                combc = (pd["ygc"], pd["gates"], bh * L, pd["g2c"], 0)
                xh, q, kv, mix = _inproj(pd["xh"], 0, bh, sc1, sh1, moff, norm1_g[l], w_full, tabs,
                                         full=True, comb=combx)
                if last:
                    ch = None
                    (kvc,) = _inproj(pd["ch"], 0, bh, sc1c, sh1c, 0, norm1_g[l], w_kv, None,
                                     full=False, comb=combc, emit_x=False)
                else:
                    ch, qc, kvc, mixc = _inproj(pd["ch"], 0, bh, sc1c, sh1c, 0, norm1_g[l], w_full, None,
                                                full=True, comb=combc)
                xoff = 0
            att = _attention(q, kv, kvc, attn_sink[l], g_att, local=True)
            loc = _local_mixers(mix, lp)
            if staged:
                prev = staged[-1]
                prev["xp"], loc = lax.optimization_barrier((prev["xp"], loc))
            nxt = pending[h + 1] if h + 1 < n_groups else None
            if nxt is not None:
                nxt["ygx"], loc = lax.optimization_barrier((nxt["ygx"], loc))
            t_total = bh * L if last else bh * (L + lc)
            cnt0 = jnp.zeros((1, ROUTE_W), F32)
            xh, h2, gates, ir, cnt = _merge(att, loc, xh, xoff, g1, sc2, sh2, moff, norm2_g[l], woa, wol,
                                            rw, rb, cnt0, None, 0, t_total)
            if not last:
                attc = _attention(qc, kvc, kvc, attn_sink[l], g_att, local=False)
                locc = _local_mixers(mixc, lp)
                ch, h2, gates, ir, cnt = _merge(attc, locc, ch, xoff, g1c, sc2c, sh2c, 0, norm2_g[l],
                                                woa, wol, rw, rb, cnt, (h2, gates, ir), bh * L, t_total)
            counts = cnt[0, :N_EXPERTS].astype(jnp.int32)
            row_tok, pos, blk_e, n_used, nxt_e = _dispatch_plan(ir[:TOP_K], ir[TOP_K:], counts)
            staged.append(dict(xh=xh, ch=ch, gates=gates, pos=pos, blk_e=blk_e, n_used=n_used, nxt_e=nxt_e,
                               xp=take(h2, row_tok), moff=moff, g2=g2, g2c=g2c, last=last))

        yps = []
        for h, st in enumerate(staged):
            if yps:
                yps[-1], st["xp"] = lax.optimization_barrier((yps[-1], st["xp"]))
            yps.append(_moe_ffn(st["xp"], st["blk_e"], st["n_used"], st["nxt_e"], exp_w1, exp_b1, exp_w2, exp_b2, l))
        for h, st in enumerate(staged):
            if h == n_groups - 1 and h > 0:
                yps[h], staged[0]["ygx"] = lax.optimization_barrier((yps[h], staged[0]["ygx"]))
            pos = st["pos"]
            st["ygx"] = take(yps[h], pos[:, :bh * L]).reshape(TOP_K, bh, L, d)
            if not last:
                st["ygc"] = take(yps[h], pos[:, bh * L:]).reshape(TOP_K, bh, lc, d)
            pending[h] = st

    out = None
    for h in range(n_groups):
        out = finish(pending[h], out)
    return out
```

```python
import functools

import jax
import jax.numpy as jnp
import numpy as np
from jax import lax
from jax.experimental import pallas as pl
from jax.experimental.pallas import tpu as pltpu

F32 = jnp.float32
BF16 = jnp.bfloat16

D_MODEL = 1024
GRID_W = 64
GROUP_W = 256
HEAD_DIM = 64
N_Q_HEADS = 4
N_KV_HEADS = 2
WINDOW = 128
ATT_BLOCK = 128
ROPE_BASE = 10000.0
CONV_WIDTH = 31
CHUNK = 128
GMLP_HEADS = 4
POOL_WINDOWS = (2, 4, 8, 16)
N_EXPERTS = 32
TOP_K = 4
D_FF = 1024
SWIGLU_ALPHA = 1.702
SWIGLU_LIMIT = 7.0
EPS = 1e-6
NEG_INF = -1e30

OFF_K = 256
OFF_V = 384
OFF_CONV = 512
IN_W = 1792
MIX_IN_W = IN_W - OFF_CONV
LOCAL_W = 3 * GROUP_W

LANES = 128
HALO = 16
ROUTE_W = LANES
ROW_TILE = 512
COMBINE_TILE = 256
ADA_TN = 1536
MOE_TM = 512
FF_CHUNK = 512
BATCH_GROUPS = 2
VMEM_LIMIT = 56 * 1024 * 1024


def _cparams(sem):
    return pltpu.CompilerParams(dimension_semantics=sem, vmem_limit_bytes=VMEM_LIMIT)


def _ada_kernel(c_ref, w_ref, b_ref, o_ref):
    c = c_ref[...]
    s = c * jax.nn.sigmoid(c)
    o_ref[0] = jnp.dot(s.astype(BF16), w_ref[0].astype(BF16), preferred_element_type=F32) + b_ref[0]


def _adaln(cond, ada_w, ada_b):
    depth, d, n = ada_w.shape
    r = cond.shape[0]
    tn = ADA_TN
    assert n % tn == 0
    return pl.pallas_call(
        _ada_kernel,
        grid=(depth, n // tn),
        in_specs=[pl.BlockSpec((r, d), lambda l, j: (0, 0)),
                  pl.BlockSpec((1, d, tn), lambda l, j: (l, 0, j)),
                  pl.BlockSpec((1, 1, tn), lambda l, j: (l, 0, j))],
        out_specs=pl.BlockSpec((1, r, tn), lambda l, j: (l, 0, j)),
        out_shape=jax.ShapeDtypeStruct((depth, r, n), F32),
        compiler_params=_cparams(("arbitrary", "arbitrary")),
        name="adaln",
    )(cond, ada_w, ada_b.reshape(depth, 1, n))


def _modulated_rmsnorm(x, g, sc, sh):
    ms = jnp.mean(x * x, axis=-1, keepdims=True)
    return x * lax.rsqrt(ms + EPS) * (g * (1.0 + sc)) + sh


def _weighted_expert_sum(yg_ref, gates):
    ff = gates[:, 0:1] * yg_ref[0, 0].astype(F32)
    for k in range(1, TOP_K):
        ff = ff + gates[:, k:k + 1] * yg_ref[k, 0].astype(F32)
    return ff


def _inproj_kernel(*refs, rope, full, fused, emit_x):
    refs = list(refs)
    if fused:
        yg_ref, gate_ref, g2_ref = refs[:3]
        refs = refs[3:]
    if rope:
        x_ref, sc_ref, sh_ref, g_ref, w_ref, cos_ref, s1_ref, s2_ref = refs[:8]
        outs = refs[8:]
    else:
        x_ref, sc_ref, sh_ref, g_ref, w_ref = refs[:5]
        outs = refs[5:]
    x = x_ref[0]
    if fused:
        x = x + g2_ref[0] * _weighted_expert_sum(yg_ref, gate_ref[...])
        if emit_x:
            outs[0][0] = x
            outs = outs[1:]
    h = _modulated_rmsnorm(x, g_ref[...], sc_ref[0], sh_ref[0])
    p = jnp.dot(h.astype(BF16), w_ref[...], preferred_element_type=F32)

    def roped(xs):
        if not rope:
            return xs
        return (xs * cos_ref[...] + pltpu.roll(xs, LANES - 16, 1) * s1_ref[...]
                + pltpu.roll(xs, 16, 1) * s2_ref[...])

    if full:
        q_ref, kv_ref, mix_ref = outs
        q_ref[0] = jnp.concatenate([roped(p[:, 0:128]), roped(p[:, 128:256])], axis=1).astype(BF16)
        kv_ref[0] = jnp.concatenate([roped(p[:, 256:384]), p[:, 384:512]], axis=1).astype(BF16)
        mix_ref[0] = p[:, OFF_CONV:]
    else:
        (kv_ref,) = outs
        kv_ref[0] = p.astype(BF16)


def _inproj(x, xoff, b, sc, sh, moff, g, w, tabs, *, full, comb=None, emit_x=True):
    _, L, d = x.shape
    n = w.shape[1]
    tm = min(ROW_TILE, L)
    nt = L // tm
    assert L % tm == 0
    rope = tabs is not None
    bm = sc.shape[0]
    mod_map = (lambda i, bb: (moff + bb, 0, 0)) if bm > 1 else (lambda i, bb: (0, 0, 0))
    in_specs, args = [], []
    if comb is not None:
        yg, gates, row_off, g2, g2off = comb
        assert row_off % tm == 0
        blk_off = row_off // tm
        g2_map = (lambda i, bb: (g2off + bb, 0, 0)) if g2.shape[0] > 1 else (lambda i, bb: (0, 0, 0))
        in_specs += [pl.BlockSpec((TOP_K, 1, tm, d), lambda i, bb: (0, bb, i, 0)),
                     pl.BlockSpec((tm, ROUTE_W), lambda i, bb: (blk_off + bb * nt + i, 0)),
                     pl.BlockSpec((1, 1, d), g2_map)]
        args += [yg, gates, g2]
    in_specs += [pl.BlockSpec((1, tm, d), lambda i, bb: (xoff + bb, i, 0)),
                 pl.BlockSpec((1, 1, d), mod_map),
                 pl.BlockSpec((1, 1, d), mod_map),
                 pl.BlockSpec((1, d), lambda i, bb: (0, 0)),
                 pl.BlockSpec((d, n), lambda i, bb: (0, 0))]
    args += [x, sc, sh, g.reshape(1, d), w]
    if rope:
        in_specs += [pl.BlockSpec((tm, LANES), lambda i, bb: (i, 0))] * 3
        args += list(tabs)
    tile = lambda wd: pl.BlockSpec((1, tm, wd), lambda i, bb: (bb, i, 0))
    out_shape, out_specs = [], []
    emit_x = emit_x and comb is not None
    if emit_x:
        out_shape.append(jax.ShapeDtypeStruct((b, L, d), F32))
        out_specs.append(tile(d))
    if full:
        out_shape += [jax.ShapeDtypeStruct((b, L, 256), BF16),
                      jax.ShapeDtypeStruct((b, L, 256), BF16),
                      jax.ShapeDtypeStruct((b, L, MIX_IN_W), F32)]
        out_specs += [tile(256), tile(256), tile(MIX_IN_W)]
    else:
        out_shape.append(jax.ShapeDtypeStruct((b, L, 256), BF16))
        out_specs.append(tile(256))
    return pl.pallas_call(
        functools.partial(_inproj_kernel, rope=rope, full=full, fused=comb is not None, emit_x=emit_x),
        grid=(nt, b),
        in_specs=in_specs, out_specs=tuple(out_specs), out_shape=tuple(out_shape),
        compiler_params=_cparams(("arbitrary", "arbitrary")),
        name="inproj_x" if rope else "inproj_ctx",
    )(*args)


def _attn_kernel(*refs, local, seq_len, q_blocks):
    if local:
        sink_ref, q_ref, kv_ref, kvc_ref, g_ref, bias_ref, o_ref = refs
    else:
        sink_ref, q_ref, kv_ref, kvc_ref, g_ref, o_ref = refs
    step = pl.program_id(1)
    lane = lax.broadcasted_iota(jnp.int32, (1, LANES), 1)
    row = lax.broadcasted_iota(jnp.int32, (2 * ATT_BLOCK, 1), 0)
    kvc = kvc_ref[0]
    kc, vc = kvc[:, :LANES], kvc[:, LANES:]
    lc = kvc.shape[0]
    nt = (((1,), (1,)), ((), ()))
    zero = jnp.zeros((), BF16)
    lm0 = lane < HEAD_DIM
    for sb in range(q_blocks):
        n = step * q_blocks + sb
        q = q_ref[0, sb * ATT_BLOCK:(sb + 1) * ATT_BLOCK, :]
        if local:
            blk0 = jnp.clip(n - 1, 0, seq_len // ATT_BLOCK - 3)
            kw = kv_ref[0, pl.ds(pl.multiple_of(blk0 * ATT_BLOCK, ATT_BLOCK), 3 * ATT_BLOCK), :]
            kl, vl = kw[:, :LANES], kw[:, LANES:]
            bias = bias_ref[n - blk0]
        outs = []
        for kh in range(N_KV_HEADS):
            lm = lm0 if kh == 0 else jnp.logical_not(lm0)
            lhs = jnp.concatenate([jnp.where(lm, q[:, :LANES], zero), jnp.where(lm, q[:, LANES:], zero)],
                                  axis=0)
            sk = jnp.where(row < ATT_BLOCK, sink_ref[2 * kh], sink_ref[2 * kh + 1])
            s = lax.dot_general(lhs, kc, nt, preferred_element_type=F32)
            if local:
                s_l = lax.dot_general(lhs, kl, nt, preferred_element_type=F32) + bias
                s = jnp.concatenate([s, s_l], axis=1)
            m = jnp.maximum(sk, jnp.max(s, axis=-1, keepdims=True))
            e = jnp.exp(s - m)
            den = jnp.exp(sk - m) + jnp.sum(e, axis=-1, keepdims=True)
            e = e.astype(BF16)
            o = jnp.dot(e[:, :lc], vc, preferred_element_type=F32)
            if local:
                o = o + jnp.dot(e[:, lc:], vl, preferred_element_type=F32)
            outs.append(o / den)
        o01, o23 = outs
        att = jnp.concatenate([jnp.where(lm0, o01[:ATT_BLOCK], o23[:ATT_BLOCK]),
                               jnp.where(lm0, o01[ATT_BLOCK:], o23[ATT_BLOCK:])], axis=1)
        ms = jnp.mean(att * att, axis=-1, keepdims=True)
        o_ref[0, sb * ATT_BLOCK:(sb + 1) * ATT_BLOCK, :] = (att * lax.rsqrt(ms + EPS) * g_ref[...]).astype(BF16)


def _band_bias():
    i = np.arange(2 * ATT_BLOCK)[None, :, None] % ATT_BLOCK
    j = np.arange(3 * ATT_BLOCK)[None, None, :]
    v = np.arange(3)[:, None, None]
    return jnp.asarray(np.where(np.abs(j - i - ATT_BLOCK * v) <= WINDOW, 0.0, NEG_INF), F32)


def _attention(q, kv, kvc, sink, g_att, *, local):
    b, L, _ = q.shape
    lc = kvc.shape[1]
    nb = L // ATT_BLOCK
    assert not local or nb >= 3
    q_blocks = max(q for q in (4, 2, 1) if nb % q == 0)
    tq = q_blocks * ATT_BLOCK
    in_specs = [pl.BlockSpec(memory_space=pltpu.SMEM),
                pl.BlockSpec((1, tq, 256), lambda bb, i: (bb, i, 0)),
                pl.BlockSpec((1, kv.shape[1], 256), lambda bb, i: (bb, 0, 0)),
                pl.BlockSpec((1, lc, 256), lambda bb, i: (bb, 0, 0)),
                pl.BlockSpec((1, 256), lambda bb, i: (0, 0))]
    args = [sink, q, kv, kvc, g_att.reshape(1, 256)]
    if local:
        in_specs.append(pl.BlockSpec((3, 2 * ATT_BLOCK, 3 * ATT_BLOCK), lambda bb, i: (0, 0, 0)))
        args.append(_band_bias())
    return pl.pallas_call(
        functools.partial(_attn_kernel, local=local, seq_len=L, q_blocks=q_blocks),
        grid=(b, nb // q_blocks),
        in_specs=in_specs,
        out_specs=pl.BlockSpec((1, tq, 256), lambda bb, i: (bb, i, 0)),
        out_shape=jax.ShapeDtypeStruct((b, L, 256), BF16),
        compiler_params=_cparams(("arbitrary", "arbitrary")),
        name="attn_window" if local else "attn_ctx",
    )(*args)


def _layernorm(x, g, b):
    mu = jnp.mean(x, axis=-1, keepdims=True)
    xc = x - mu
    return xc * lax.rsqrt(jnp.mean(xc * xc, axis=-1, keepdims=True) + EPS) * g + b


def _group_norm_store(y, g):
    ms = jnp.mean(y * y, axis=-1, keepdims=True)
    return (y * lax.rsqrt(ms + EPS) * g).astype(BF16)


def _mix_kernel(cur_ref, prev_ref, next_ref, dww_ref, dwb_ref, clg_ref, clb_ref, cpw_ref,
                glg_ref, glb_ref, gws_ref, gbs_ref, pw_ref, ps_ref, gn_ref, o_ref,
                hc_ref, sh_ref, hp_ref, c2_ref, c4_ref, c8_ref, *, ts, seq_len):
    i = pl.program_id(1)
    nt = pl.num_programs(1)
    has_prev = (i > 0).astype(F32)
    has_next = (i < nt - 1).astype(F32)

    def glu(blk):
        return blk[:, 0:256] * jax.nn.sigmoid(blk[:, 256:512])

    pv = prev_ref[0]
    nx = next_ref[0]
    hc_ref[0:HALO, :] = glu(pv) * has_prev
    hc_ref[HALO + ts:2 * HALO + ts, :] = glu(nx) * has_next
    hp_ref[0:HALO, :] = pv[:, 1024:1280] * has_prev
    hp_ref[HALO + ts:2 * HALO + ts, :] = nx[:, 1024:1280] * has_next
    hp_ref[2 * HALO + ts:3 * HALO + ts, :] = jnp.zeros((HALO, 256), F32)
    hc_ref[HALO:HALO + ts, :] = glu(cur_ref[0, :, 0:512])
    hp_ref[HALO:HALO + ts, :] = cur_ref[0, :, 1024:1280]

    for r in range(1, 8):
        sh_ref[r - 1] = hc_ref[r:r + ts + 24, :]
    c2_ref[8:ts + 40, :] = hp_ref[8:ts + 40, :] + hp_ref[9:ts + 41, :]
    c4_ref[8:ts + 32, :] = c2_ref[8:ts + 32, :] + c2_ref[10:ts + 34, :]
    c8_ref[8:ts + 24, :] = c4_ref[8:ts + 24, :] + c4_ref[12:ts + 28, :]

    lane = lax.broadcasted_iota(jnp.int32, (1, 256), 1)
    rc = min(128, ts)
    for c in range(ts // rc):
        r0 = c * rc
        acc = jnp.zeros((rc, 256), F32)
        for j in range(CONV_WIDTH):
            a, r = divmod(1 + j, 8)
            lo = r0 + 8 * a
            tap = hc_ref[lo:lo + rc, :] if r == 0 else sh_ref[r - 1, lo:lo + rc, :]
            acc = acc + tap * dww_ref[j:j + 1, :]
        hcv = _layernorm(acc + dwb_ref[...], clg_ref[...], clb_ref[...])
        hcv = hcv * jax.nn.sigmoid(hcv)
        conv = jnp.dot(hcv.astype(BF16), cpw_ref[...], preferred_element_type=F32)
        o_ref[0, r0:r0 + rc, 0:256] = _group_norm_store(conv, gn_ref[:, 0:256])

        s0 = r0 + HALO
        hcur = hp_ref[s0:s0 + rc, :]
        s2 = c2_ref[s0 - 1:s0 - 1 + rc, :]
        s4 = c4_ref[s0 - 2:s0 - 2 + rc, :]
        s8 = c8_ref[s0 - 4:s0 - 4 + rc, :]
        s16 = c8_ref[s0 - 8:s0 - 8 + rc, :] + c8_ref[s0:s0 + rc, :]
        wsum = jnp.where(lane < 64, s2, jnp.where(lane < 128, s4, jnp.where(lane < 192, s8, s16)))
        half = jnp.where(lane < 64, 1, jnp.where(lane < 128, 2, jnp.where(lane < 192, 4, 8)))
        t = i * ts + r0 + lax.broadcasted_iota(jnp.int32, (rc, 1), 0)
        cnt = jnp.clip(t + half, 0, seq_len) - jnp.clip(t - half, 0, seq_len)
        y = wsum / cnt.astype(F32) - hcur
        pool = jnp.dot(y.astype(BF16), pw_ref[...], preferred_element_type=F32) * ps_ref[...]
        o_ref[0, r0:r0 + rc, 512:768] = _group_norm_store(pool, gn_ref[:, 512:768])

    for c in range(ts // CHUNK):
        r0 = c * CHUNK
        u = cur_ref[0, r0:r0 + CHUNK, 512:768]
        v = _layernorm(cur_ref[0, r0:r0 + CHUNK, 768:1024], glg_ref[...], glb_ref[...])
        r = jnp.dot(gws_ref[...], v.astype(BF16), preferred_element_type=F32)
        mixed = jnp.where(lane < 64, r[0:CHUNK],
                          jnp.where(lane < 128, r[CHUNK:2 * CHUNK],
                                    jnp.where(lane < 192, r[2 * CHUNK:3 * CHUNK], r[3 * CHUNK:])))
        gm = u * (mixed + gbs_ref[...])
        o_ref[0, r0:r0 + CHUNK, 256:512] = _group_norm_store(gm, gn_ref[:, 256:512])


def _local_mixers(mix, lp):
    b, L, w = mix.shape
    ts = min(ROW_TILE, L)
    assert L % ts == 0 and ts % CHUNK == 0
    nt = L // ts
    hb = ts // HALO
    last_h = L // HALO - 1
    full = lambda shape: pl.BlockSpec(shape, lambda bb, i: (0,) * len(shape))
    params = [lp["dw_w"], lp["dw_b"], lp["cln_g"], lp["cln_b"], lp["cpw"], lp["gln_g"], lp["gln_b"],
              lp["gws"], lp["gbs"], lp["pw"], lp["ps"], lp["gn_local"]]
    return pl.pallas_call(
        functools.partial(_mix_kernel, ts=ts, seq_len=L),
        grid=(b, nt),
        in_specs=[pl.BlockSpec((1, ts, w), lambda bb, i: (bb, i, 0)),
                  pl.BlockSpec((1, HALO, w), lambda bb, i: (bb, jnp.maximum(i * hb - 1, 0), 0)),
                  pl.BlockSpec((1, HALO, w), lambda bb, i: (bb, jnp.minimum((i + 1) * hb, last_h), 0))]
                 + [full(p.shape) for p in params],
        out_specs=pl.BlockSpec((1, ts, LOCAL_W), lambda bb, i: (bb, i, 0)),
        out_shape=jax.ShapeDtypeStruct((b, L, LOCAL_W), BF16),
        scratch_shapes=[pltpu.VMEM((ts + 2 * HALO, 256), F32), pltpu.VMEM((7, ts + 24, 256), F32)]
                       + [pltpu.VMEM((ts + 3 * HALO, 256), F32)] * 4,
        compiler_params=_cparams(("arbitrary", "arbitrary")),
        name="local_mixers",
    )(mix, mix, mix, *params)


def _merge_kernel(*refs, n_alias):
    (att_ref, loc_ref, x_ref, g1_ref, sc_ref, sh_ref, n2g_ref, woa_ref, wol_ref,
     rw_ref, rb_ref, utri_ref, cnt_in_ref) = refs[:13]
    xo_ref, h2_ref, gate_ref, ir_ref, cnt_ref, run_ref = refs[13 + n_alias:]

    @pl.when(jnp.logical_and(pl.program_id(0) == 0, pl.program_id(1) == 0))
    def _():
        run_ref[...] = cnt_in_ref[...]

    tm = x_ref.shape[1]
    y = (jnp.dot(att_ref[0], woa_ref[...], preferred_element_type=F32)
         + jnp.dot(loc_ref[0], wol_ref[...], preferred_element_type=F32))
    xn = x_ref[0] + g1_ref[0] * y
    xo_ref[0] = xn
    h2 = _modulated_rmsnorm(xn, n2g_ref[...], sc_ref[0], sh_ref[0]).astype(BF16)
    h2_ref[...] = h2
    nt_dims = (((1,), (1,)), ((), ()))
    logits = lax.dot_general(rw_ref[...], h2, nt_dims, preferred_element_type=F32) + rb_ref[...]
    eidx = lax.broadcasted_iota(jnp.int32, logits.shape, 0).astype(F32)
    hits = jnp.zeros(logits.shape, F32)
    sels, ids, vals = [], [], []
    top = None
    for k in range(TOP_K):
        m = jnp.max(logits, axis=0, keepdims=True)
        sel = jnp.min(jnp.where(logits == m, eidx, float(N_EXPERTS)), axis=0, keepdims=True)
        chosen = eidx == sel
        if top is None:
            top = m
        vals.append(jnp.exp(m - top))
        ids.append(sel)
        hits = jnp.where(chosen, 1.0, hits)
        logits = jnp.where(chosen, -jnp.inf, logits)
        sels.append(chosen)
    den = vals[0] + vals[1] + vals[2] + vals[3]
    gates_t = jnp.concatenate([v / den for v in vals] + [jnp.zeros((ROUTE_W - TOP_K, tm), F32)], axis=0)
    gate_ref[...] = gates_t.T
    base = run_ref[...] + jnp.dot(hits.astype(BF16), utri_ref[...], preferred_element_type=F32)
    ranks = [jnp.sum(jnp.where(sels[k], base, 0.0), axis=0, keepdims=True) for k in range(TOP_K)]
    ir_ref[...] = jnp.concatenate(ids + ranks, axis=0).astype(jnp.int32)
    run_ref[...] = run_ref[...] + jnp.sum(hits, axis=1, keepdims=True)
    cnt_ref[...] = run_ref[...]


def _merge(att, loc, x, xoff, g1, sc2, sh2, moff, n2g, woa, wol, rw, rb, cnt_in, bufs, row_off, t_total):
    b = att.shape[0]
    _, L, d = x.shape
    tm = min(ROW_TILE, L)
    nt = L // tm
    blk_off = row_off // tm
    bm = g1.shape[0]
    mod_map = (lambda bb, i: (moff + bb, 0, 0)) if bm > 1 else (lambda bb, i: (0, 0, 0))
    tile = lambda w: pl.BlockSpec((1, tm, w), lambda bb, i: (bb, i, 0))
    xtile = pl.BlockSpec((1, tm, d), lambda bb, i: (xoff + bb, i, 0))
    flat = lambda w: pl.BlockSpec((tm, w), lambda bb, i: (blk_off + bb * nt + i, 0))
    full = lambda shape: pl.BlockSpec(shape, lambda bb, i: (0,) * len(shape))
    assert L % tm == 0 and row_off % tm == 0
    utri = jnp.asarray((np.arange(tm)[:, None] < np.arange(tm)[None, :]).astype(np.float32), BF16)
    n_alias = 0 if bufs is None else len(bufs)
    n_in = 13
    return pl.pallas_call(
        functools.partial(_merge_kernel, n_alias=n_alias),
        grid=(b, nt),
        in_specs=[tile(256), tile(LOCAL_W), xtile,
                  pl.BlockSpec((1, 1, d), mod_map), pl.BlockSpec((1, 1, d), mod_map),
                  pl.BlockSpec((1, 1, d), mod_map),
                  full((1, d)), full(woa.shape), full(wol.shape), full(rw.shape), full(rb.shape),
                  full((tm, tm)), full((N_EXPERTS, 1))]
                 + [pl.BlockSpec(memory_space=pl.ANY)] * n_alias,
        out_specs=(tile(d), flat(d), flat(ROUTE_W),
                   pl.BlockSpec((2 * TOP_K, tm), lambda bb, i: (0, blk_off + bb * nt + i)),
                   full((N_EXPERTS, 1))),
        out_shape=(jax.ShapeDtypeStruct((b, L, d), F32),
                   jax.ShapeDtypeStruct((t_total, d), BF16),
                   jax.ShapeDtypeStruct((t_total, ROUTE_W), F32),
                   jax.ShapeDtypeStruct((2 * TOP_K, t_total), jnp.int32),
                   jax.ShapeDtypeStruct((N_EXPERTS, 1), F32)),
        scratch_shapes=[pltpu.VMEM((N_EXPERTS, 1), F32)],
        input_output_aliases={n_in + j: 1 + j for j in range(n_alias)},
        compiler_params=_cparams(("arbitrary", "arbitrary")),
        name="merge_router",
    )(att, loc, x, g1, sc2, sh2, n2g.reshape(1, d), woa, wol, rw, rb, utri, cnt_in,
      *(bufs if bufs is not None else ()))


def _moe_kernel(blk_e_ref, n_used_ref, nxt_e_ref, x_ref, w1_hbm, b1_ref, w2_hbm, b2_ref, o_ref,
                w1f_ref, w2f_ref, w1s_ref, w2s_ref, sem, *, layer):
    i = pl.program_id(0)
    active = i < n_used_ref[0]
    e = blk_e_ref[i]
    new_expert = jnp.logical_or(i == 0, e != blk_e_ref[jnp.maximum(i - 1, 0)])

    def weight_copies(expert):
        return (pltpu.make_async_copy(w1_hbm.at[layer, expert], w1f_ref, sem.at[0]),
                pltpu.make_async_copy(w2_hbm.at[layer, expert], w2f_ref, sem.at[1]))

    @pl.when(jnp.logical_and(i == 0, active))
    def _():
        for cp in weight_copies(e):
            cp.start()

    @pl.when(jnp.logical_and(active, new_expert))
    def _():
        for cp in weight_copies(e):
            cp.wait()
        rows = 128
        for r in range(D_MODEL // rows):
            w1s_ref[r * rows:(r + 1) * rows, :] = w1f_ref[r * rows:(r + 1) * rows, :].astype(BF16)
        for r in range(D_FF // rows):
            w2s_ref[r * rows:(r + 1) * rows, :] = w2f_ref[r * rows:(r + 1) * rows, :].astype(BF16)
        nxt = nxt_e_ref[i]

        @pl.when(nxt >= 0)
        def _():
            for cp in weight_copies(nxt):
                cp.start()

    @pl.when(active)
    def _():
        x = x_ref[...]
        acc = jnp.zeros((x.shape[0], D_MODEL), F32)
        for c in range(D_FF // FF_CHUNK):
            lo = c * FF_CHUNK
            glu = jnp.dot(x, w1s_ref[:, lo:lo + FF_CHUNK], preferred_element_type=F32) \
                + b1_ref[0, 0, :, lo:lo + FF_CHUNK]
            lin = jnp.dot(x, w1s_ref[:, D_FF + lo:D_FF + lo + FF_CHUNK], preferred_element_type=F32) \
                + b1_ref[0, 0, :, D_FF + lo:D_FF + lo + FF_CHUNK]
            glu = jnp.minimum(glu, SWIGLU_LIMIT)
            lin = jnp.clip(lin, -SWIGLU_LIMIT, SWIGLU_LIMIT)
            act = glu * jax.nn.sigmoid(SWIGLU_ALPHA * glu) * (lin + 1.0)
            acc = acc + jnp.dot(act.astype(BF16), w2s_ref[lo:lo + FF_CHUNK, :], preferred_element_type=F32)
        o_ref[...] = (acc + b2_ref[0, 0]).astype(o_ref.dtype)

    @pl.when(i >= n_used_ref[0])
    def _():
        o_ref[...] = jnp.zeros(o_ref.shape, o_ref.dtype)


def _moe_ffn(xp, blk_e, n_used, nxt_e, w1, b1, w2, b2, layer):
    n_rows, d = xp.shape
    n_blk = n_rows // MOE_TM
    depth, e = w1.shape[:2]

    def row_map(i, be, nu, ne):
        return (jnp.maximum(jnp.minimum(i, nu[0] - 1), 0), 0)

    def b_map(i, be, nu, ne):
        return (layer, be[i], 0, 0)

    return pl.pallas_call(
        functools.partial(_moe_kernel, layer=layer),
        grid_spec=pltpu.PrefetchScalarGridSpec(
            num_scalar_prefetch=3,
            grid=(n_blk,),
            in_specs=[pl.BlockSpec((MOE_TM, d), row_map),
                      pl.BlockSpec(memory_space=pl.ANY),
                      pl.BlockSpec((1, 1, 1, 2 * D_FF), b_map),
                      pl.BlockSpec(memory_space=pl.ANY),
                      pl.BlockSpec((1, 1, 1, d), b_map)],
            out_specs=pl.BlockSpec((MOE_TM, d), lambda i, be, nu, ne: (i, 0)),
            scratch_shapes=[pltpu.VMEM((d, 2 * D_FF), F32), pltpu.VMEM((D_FF, d), F32),
                            pltpu.VMEM((d, 2 * D_FF), BF16), pltpu.VMEM((D_FF, d), BF16),
                            pltpu.SemaphoreType.DMA((2,))]),
        out_shape=jax.ShapeDtypeStruct((n_rows, d), BF16),
        compiler_params=_cparams(("arbitrary",)),
        name="moe_ffn",
    )(blk_e, n_used, nxt_e, xp, w1, b1.reshape(depth, e, 1, 2 * D_FF), w2, b2.reshape(depth, e, 1, d))


def _combine_kernel(*refs):
    yg_ref, gate_ref, x_ref, g2_ref, fg_ref = refs[:5]
    o_ref = refs[-1]
    xo = x_ref[0] + g2_ref[0] * _weighted_expert_sum(yg_ref, gate_ref[...])
    ms = jnp.mean(xo * xo, axis=-1, keepdims=True)
    o_ref[0] = xo * lax.rsqrt(ms + EPS) * fg_ref[...]


def _combine(yg, gates, x, g2, moff, fg, row_off, *, out_buf=None, ooff=0, out_b=None):
    b, L, d = x.shape
    tm = min(COMBINE_TILE, L)
    assert L % tm == 0 and row_off % tm == 0
    nt = L // tm
    blk_off = row_off // tm
    bm = g2.shape[0]
    out_b = b if out_b is None else out_b
    mod_map = (lambda bb, i: (moff + bb, 0, 0)) if bm > 1 else (lambda bb, i: (0, 0, 0))
    extra = () if out_buf is None else (out_buf,)
    return pl.pallas_call(
        _combine_kernel,
        grid=(b, nt),
        in_specs=[pl.BlockSpec((TOP_K, 1, tm, d), lambda bb, i: (0, bb, i, 0)),
                  pl.BlockSpec((tm, ROUTE_W), lambda bb, i: (blk_off + bb * nt + i, 0)),
                  pl.BlockSpec((1, tm, d), lambda bb, i: (bb, i, 0)),
                  pl.BlockSpec((1, 1, d), mod_map),
                  pl.BlockSpec((1, d), lambda bb, i: (0, 0))]
                 + [pl.BlockSpec(memory_space=pl.ANY)] * len(extra),
        out_specs=pl.BlockSpec((1, tm, d), lambda bb, i: (ooff + bb, i, 0)),
        out_shape=jax.ShapeDtypeStruct((out_b, L, d), F32),
        input_output_aliases={5: 0} if extra else {},
        compiler_params=_cparams(("arbitrary", "arbitrary")),
        name="combine",
    )(yg, gates, x, g2, fg.reshape(1, d), *extra)


def _dispatch_plan(idx, rank, counts):
    n_tok = idx.shape[1]
    n_asg = n_tok * TOP_K
    padded = (counts + MOE_TM - 1) // MOE_TM * MOE_TM
    pad_end = jnp.cumsum(padded)
    pad_start = (pad_end - padded).astype(jnp.int32)
    pos = rank
    for e in range(N_EXPERTS):
        pos = pos + jnp.where(idx == e, pad_start[e], 0)
    n_rows = -(-n_asg // MOE_TM) * MOE_TM + N_EXPERTS * MOE_TM
    n_blk = n_rows // MOE_TM
    tok = jnp.broadcast_to(jnp.arange(n_tok, dtype=jnp.int32)[None, :], (TOP_K, n_tok))
    fill = jnp.arange(n_rows, dtype=jnp.int32) % n_tok
    row_tok = fill.at[pos.reshape(-1)].add((tok - pos % n_tok).reshape(-1), mode="promise_in_bounds")
    blk_start = jnp.arange(n_blk, dtype=jnp.int32) * MOE_TM
    blk_e = jnp.minimum(jnp.sum((pad_end[None, :] <= blk_start[:, None]).astype(jnp.int32), axis=1),
                        N_EXPERTS - 1)
    n_used = (pad_end[-1] // MOE_TM).astype(jnp.int32).reshape(1)
    ids = jnp.arange(N_EXPERTS, dtype=jnp.int32)
    first_used_from = lax.cummin(jnp.where(padded > 0, ids, N_EXPERTS), axis=0, reverse=True)
    nxt_of = jnp.concatenate([first_used_from[1:], jnp.full((1,), N_EXPERTS, jnp.int32)])
    nxt_of = jnp.where(nxt_of >= N_EXPERTS, -1, nxt_of)
    nxt_e = jnp.sum(jnp.where(blk_e[:, None] == ids[None, :], nxt_of[None, :], 0), axis=1).astype(jnp.int32)
    return row_tok, pos, blk_e, n_used, nxt_e


_Q_PERM = np.concatenate([np.arange(0, 64), np.arange(128, 192), np.arange(64, 128), np.arange(192, 256)])


def _rope_tables(L):
    t = jnp.arange(L)
    row = (t // GRID_W).astype(F32)
    col = (t % GRID_W).astype(F32)
    half = HEAD_DIM // 2
    inv = ROPE_BASE ** (-jnp.arange(0, half, 2, dtype=F32) / half)
    ar, ac = row[:, None] * inv[None, :], col[:, None] * inv[None, :]
    z = jnp.zeros_like(ar)
    cos = jnp.concatenate([jnp.cos(ar), jnp.cos(ar), jnp.cos(ac), jnp.cos(ac)], axis=1)
    s1 = jnp.concatenate([-jnp.sin(ar), z, -jnp.sin(ac), z], axis=1)
    s2 = jnp.concatenate([z, jnp.sin(ar), z, jnp.sin(ac)], axis=1)
    rep = lambda a: jnp.concatenate([a, a], axis=1)
    return rep(cos), rep(s1), rep(s2)


def _block_diag(pw):
    g, n, _ = pw.shape
    out = jnp.zeros((g * n, g * n), pw.dtype)
    for gi in range(g):
        out = out.at[gi * n:(gi + 1) * n, gi * n:(gi + 1) * n].set(pw[gi])
    return out


def kernel(x, c, ctx, c_ctx, ada_w, ada_b, norm1_g, norm2_g, w_in, attn_sink, conv_dw_w, conv_dw_b, conv_ln_g, conv_ln_b, conv_pw_w, gmlp_ln_g, gmlp_ln_b, gmlp_ws, gmlp_bs, pool_w, pool_scale, group_norm_g, w_out, router_w, router_b, exp_w1, exp_b1, exp_w2, exp_b2, final_norm_g):
    b, L, d = x.shape
    lc = ctx.shape[1]
    depth = ada_w.shape[0]
    tabs = _rope_tables(L)

    r = -(-(b + 1) // 8) * 8
    cond = jnp.zeros((r, d), F32).at[:b].set(c).at[b].set(c_ctx)
    mods = _adaln(cond, ada_w, ada_b).reshape(depth, r, 6, d)

    n_groups = BATCH_GROUPS if b % BATCH_GROUPS == 0 else 1
    bh = b // n_groups
    xs, cs = [x] * n_groups, [ctx] * n_groups
    pending = [None] * n_groups

    def finish(st, out_buf):
        return _combine(st["ygx"], st["gates"], st["xh"], st["g2"], st["moff"], final_norm_g, 0,
                        out_buf=out_buf, ooff=st["moff"], out_b=b)

    for l in range(depth):
        last = l == depth - 1
        mx = [mods[l, :b, i].reshape(b, 1, d) for i in range(6)]
        mc = [mods[l, b:b + 1, i].reshape(1, 1, d) for i in range(6)]
        sh1, sc1, g1, sh2, sc2, g2 = mx
        sh1c, sc1c, g1c, sh2c, sc2c, g2c = mc

        wl = w_in[l]
        wq = wl[:, :OFF_K][:, _Q_PERM] * (HEAD_DIM ** -0.5)
        w_full = jnp.concatenate([wq, wl[:, OFF_K:]], axis=1).astype(BF16)
        gn = group_norm_g[l]
        g_att = gn[:256][_Q_PERM]
        wo = w_out[l]
        woa = wo[:256][_Q_PERM].astype(BF16)
        wol = wo[256:].astype(BF16)
        lp = dict(
            dw_w=conv_dw_w[l], dw_b=conv_dw_b[l].reshape(1, -1),
            cln_g=conv_ln_g[l].reshape(1, -1), cln_b=conv_ln_b[l].reshape(1, -1),
            cpw=conv_pw_w[l].astype(BF16),
            gln_g=gmlp_ln_g[l].reshape(1, -1), gln_b=gmlp_ln_b[l].reshape(1, -1),
            gws=gmlp_ws[l].reshape(GMLP_HEADS * CHUNK, CHUNK).astype(BF16),
            gbs=jnp.repeat(gmlp_bs[l].T, GROUP_W // GMLP_HEADS, axis=1),
            pw=_block_diag(pool_w[l]).astype(BF16), ps=pool_scale[l].reshape(1, -1),
            gn_local=gn[256:].reshape(1, -1))
        rw = router_w[l].T.astype(BF16)
        rb = router_b[l].reshape(N_EXPERTS, 1)

        w_kv = wl[:, OFF_K:OFF_CONV].astype(BF16)
        take = lambda a, i: a.at[i].get(mode="promise_in_bounds")
        staged = []
        for h in range(n_groups):
            moff = h * bh
            pd = pending[h]
            xoff = moff if l == 0 else 0

            if pd is None:
                xh, ch = xs[h], cs[h]
                q, kv, mix = _inproj(xh, xoff, bh, sc1, sh1, moff, norm1_g[l], w_full, tabs, full=True)
                if last:
                    (kvc,) = _inproj(ch, xoff, bh, sc1c, sh1c, 0, norm1_g[l], w_kv, None, full=False)
                else:
                    qc, kvc, mixc = _inproj(ch, xoff, bh, sc1c, sh1c, 0, norm1_g[l], w_full, None, full=True)
            else:
                if staged:
                    pd["ygx"], staged[-1]["xh"] = lax.optimization_barrier((pd["ygx"], staged[-1]["xh"]))
                combx = (pd["ygx"], pd["gates"], 0, pd["g2"], pd["moff"])
                combc = (pd["ygc"], pd["gates"], bh * L, pd["g2c"], 0)
                xh, q, kv, mix = _inproj(pd["xh"], 0, bh, sc1, sh1, moff, norm1_g[l], w_full, tabs,
                                         full=True, comb=combx)
                if last:
                    ch = None
                    (kvc,) = _inproj(pd["ch"], 0, bh, sc1c, sh1c, 0, norm1_g[l], w_kv, None,
                                     full=False, comb=combc, emit_x=False)
                else:
                    ch, qc, kvc, mixc = _inproj(pd["ch"], 0, bh, sc1c, sh1c, 0, norm1_g[l], w_full, None,
                                                full=True, comb=combc)
                xoff = 0
            att = _attention(q, kv, kvc, attn_sink[l], g_att, local=True)
            loc = _local_mixers(mix, lp)
            if staged:
                prev = staged[-1]
                prev["xp"], loc = lax.optimization_barrier((prev["xp"], loc))
            nxt = pending[h + 1] if h + 1 < n_groups else None
            if nxt is not None:
                nxt["ygx"], loc = lax.optimization_barrier((nxt["ygx"], loc))
            t_total = bh * L if last else bh * (L + lc)
            cnt0 = jnp.zeros((N_EXPERTS, 1), F32)
            xh, h2, gates, ir, cnt = _merge(att, loc, xh, xoff, g1, sc2, sh2, moff, norm2_g[l], woa, wol,
                                            rw, rb, cnt0, None, 0, t_total)
            if not last:
                attc = _attention(qc, kvc, kvc, attn_sink[l], g_att, local=False)
                locc = _local_mixers(mixc, lp)
                ch, h2, gates, ir, cnt = _merge(attc, locc, ch, xoff, g1c, sc2c, sh2c, 0, norm2_g[l],
                                                woa, wol, rw, rb, cnt, (h2, gates, ir), bh * L, t_total)
            counts = cnt[:, 0].astype(jnp.int32)
            row_tok, pos, blk_e, n_used, nxt_e = _dispatch_plan(ir[:TOP_K], ir[TOP_K:], counts)
            staged.append(dict(xh=xh, ch=ch, gates=gates, pos=pos, blk_e=blk_e, n_used=n_used, nxt_e=nxt_e,
                               xp=take(h2, row_tok), moff=moff, g2=g2, g2c=g2c, last=last))

        yps = []
        for h, st in enumerate(staged):
            if yps:
                yps[-1], st["xp"] = lax.optimization_barrier((yps[-1], st["xp"]))
            yps.append(_moe_ffn(st["xp"], st["blk_e"], st["n_used"], st["nxt_e"], exp_w1, exp_b1, exp_w2, exp_b2, l))
        for h, st in enumerate(staged):
            if h == n_groups - 1 and h > 0:
                yps[h], staged[0]["ygx"] = lax.optimization_barrier((yps[h], staged[0]["ygx"]))
            pos = st["pos"]
            st["ygx"] = take(yps[h], pos[:, :bh * L]).reshape(TOP_K, bh, L, d)
            if not last:
                st["ygc"] = take(yps[h], pos[:, bh * L:]).reshape(TOP_K, bh, lc, d)
            pending[h] = st

    out = None
    for h in range(n_groups):
        out = finish(pending[h], out)
    return out
```

```python
import functools

import jax
import jax.numpy as jnp
import numpy as np
from jax import lax
from jax.experimental import pallas as pl
from jax.experimental.pallas import tpu as pltpu

F32 = jnp.float32
BF16 = jnp.bfloat16

D_MODEL = 1024
GRID_W = 64
GROUP_W = 256
HEAD_DIM = 64
N_Q_HEADS = 4
N_KV_HEADS = 2
WINDOW = 128
ATT_BLOCK = 128
ROPE_BASE = 10000.0
CONV_WIDTH = 31
CHUNK = 128
GMLP_HEADS = 4
POOL_WINDOWS = (2, 4, 8, 16)
N_EXPERTS = 32
TOP_K = 4
D_FF = 1024
SWIGLU_ALPHA = 1.702
SWIGLU_LIMIT = 7.0
EPS = 1e-6
NEG_INF = -1e30

OFF_K = 256
OFF_V = 384
OFF_CONV = 512
IN_W = 1792
MIX_IN_W = IN_W - OFF_CONV
LOCAL_W = 3 * GROUP_W

LANES = 128
HALO = 16
ROUTE_W = LANES
ROW_TILE = 512
COMBINE_TILE = 256
ADA_TN = 1536
MOE_TM = 1024
MOE_SUB = 512
FF_CHUNK = 512
BATCH_GROUPS = 2
VMEM_LIMIT = 56 * 1024 * 1024


def _cparams(sem):
    return pltpu.CompilerParams(dimension_semantics=sem, vmem_limit_bytes=VMEM_LIMIT)


def _ada_kernel(c_ref, w_ref, b_ref, o_ref):
    c = c_ref[...]
    s = c * jax.nn.sigmoid(c)
    o_ref[0] = jnp.dot(s.astype(BF16), w_ref[0].astype(BF16), preferred_element_type=F32) + b_ref[0]


def _adaln(cond, ada_w, ada_b):
    depth, d, n = ada_w.shape
    r = cond.shape[0]
    tn = ADA_TN
    assert n % tn == 0
    return pl.pallas_call(
        _ada_kernel,
        grid=(depth, n // tn),
        in_specs=[pl.BlockSpec((r, d), lambda l, j: (0, 0)),
                  pl.BlockSpec((1, d, tn), lambda l, j: (l, 0, j)),
                  pl.BlockSpec((1, 1, tn), lambda l, j: (l, 0, j))],
        out_specs=pl.BlockSpec((1, r, tn), lambda l, j: (l, 0, j)),
        out_shape=jax.ShapeDtypeStruct((depth, r, n), F32),
        compiler_params=_cparams(("arbitrary", "arbitrary")),
        name="adaln",
    )(cond, ada_w, ada_b.reshape(depth, 1, n))


def _modulated_rmsnorm(x, g, sc, sh):
    ms = jnp.mean(x * x, axis=-1, keepdims=True)
    return x * lax.rsqrt(ms + EPS) * (g * (1.0 + sc)) + sh


def _weighted_expert_sum(yg_ref, gates):
    ff = gates[:, 0:1] * yg_ref[0, 0].astype(F32)
    for k in range(1, TOP_K):
        ff = ff + gates[:, k:k + 1] * yg_ref[k, 0].astype(F32)
    return ff


def _inproj_kernel(*refs, rope, full, fused, emit_x):
    refs = list(refs)
    if fused:
        yg_ref, gate_ref, g2_ref = refs[:3]
        refs = refs[3:]
    if rope:
        x_ref, sc_ref, sh_ref, g_ref, w_ref, cos_ref, s1_ref, s2_ref = refs[:8]
        outs = refs[8:]
    else:
        x_ref, sc_ref, sh_ref, g_ref, w_ref = refs[:5]
        outs = refs[5:]
    x = x_ref[0]
    if fused:
        x = x + g2_ref[0] * _weighted_expert_sum(yg_ref, gate_ref[...])
        if emit_x:
            outs[0][0] = x
            outs = outs[1:]
    h = _modulated_rmsnorm(x, g_ref[...], sc_ref[0], sh_ref[0])
    p = jnp.dot(h.astype(BF16), w_ref[...], preferred_element_type=F32)

    def roped(xs):
        if not rope:
            return xs
        return (xs * cos_ref[...] + pltpu.roll(xs, LANES - 16, 1) * s1_ref[...]
                + pltpu.roll(xs, 16, 1) * s2_ref[...])

    if full:
        q_ref, kv_ref, mix_ref = outs
        q_ref[0] = jnp.concatenate([roped(p[:, 0:128]), roped(p[:, 128:256])], axis=1).astype(BF16)
        kv_ref[0] = jnp.concatenate([roped(p[:, 256:384]), p[:, 384:512]], axis=1).astype(BF16)
        mix_ref[0] = p[:, OFF_CONV:]
    else:
        (kv_ref,) = outs
        kv_ref[0] = p.astype(BF16)


def _inproj(x, xoff, b, sc, sh, moff, g, w, tabs, *, full, comb=None, emit_x=True):
    _, L, d = x.shape
    n = w.shape[1]
    tm = min(ROW_TILE, L)
    nt = L // tm
    assert L % tm == 0
    rope = tabs is not None
    bm = sc.shape[0]
    mod_map = (lambda i, bb: (moff + bb, 0, 0)) if bm > 1 else (lambda i, bb: (0, 0, 0))
    in_specs, args = [], []
    if comb is not None:
        yg, gates, row_off, g2, g2off = comb
        assert row_off % tm == 0
        blk_off = row_off // tm
        g2_map = (lambda i, bb: (g2off + bb, 0, 0)) if g2.shape[0] > 1 else (lambda i, bb: (0, 0, 0))
        in_specs += [pl.BlockSpec((TOP_K, 1, tm, d), lambda i, bb: (0, bb, i, 0)),
                     pl.BlockSpec((tm, ROUTE_W), lambda i, bb: (blk_off + bb * nt + i, 0)),
                     pl.BlockSpec((1, 1, d), g2_map)]
        args += [yg, gates, g2]
    in_specs += [pl.BlockSpec((1, tm, d), lambda i, bb: (xoff + bb, i, 0)),
                 pl.BlockSpec((1, 1, d), mod_map),
                 pl.BlockSpec((1, 1, d), mod_map),
                 pl.BlockSpec((1, d), lambda i, bb: (0, 0)),
                 pl.BlockSpec((d, n), lambda i, bb: (0, 0))]
    args += [x, sc, sh, g.reshape(1, d), w]
    if rope:
        in_specs += [pl.BlockSpec((tm, LANES), lambda i, bb: (i, 0))] * 3
        args += list(tabs)
    tile = lambda wd: pl.BlockSpec((1, tm, wd), lambda i, bb: (bb, i, 0))
    out_shape, out_specs = [], []
    emit_x = emit_x and comb is not None
    if emit_x:
        out_shape.append(jax.ShapeDtypeStruct((b, L, d), F32))
        out_specs.append(tile(d))
    if full:
        out_shape += [jax.ShapeDtypeStruct((b, L, 256), BF16),
                      jax.ShapeDtypeStruct((b, L, 256), BF16),
                      jax.ShapeDtypeStruct((b, L, MIX_IN_W), F32)]
        out_specs += [tile(256), tile(256), tile(MIX_IN_W)]
    else:
        out_shape.append(jax.ShapeDtypeStruct((b, L, 256), BF16))
        out_specs.append(tile(256))
    return pl.pallas_call(
        functools.partial(_inproj_kernel, rope=rope, full=full, fused=comb is not None, emit_x=emit_x),
        grid=(nt, b),
        in_specs=in_specs, out_specs=tuple(out_specs), out_shape=tuple(out_shape),
        compiler_params=_cparams(("arbitrary", "arbitrary")),
        name="inproj_x" if rope else "inproj_ctx",
    )(*args)


def _attn_kernel(*refs, local, seq_len, q_blocks):
    if local:
        sink_ref, q_ref, kv_ref, kvc_ref, g_ref, bias_ref, o_ref = refs
    else:
        sink_ref, q_ref, kv_ref, kvc_ref, g_ref, o_ref = refs
    step = pl.program_id(1)
    lane = lax.broadcasted_iota(jnp.int32, (1, LANES), 1)
    row = lax.broadcasted_iota(jnp.int32, (2 * ATT_BLOCK, 1), 0)
    kvc = kvc_ref[0]
    kc, vc = kvc[:, :LANES], kvc[:, LANES:]
    lc = kvc.shape[0]
    nt = (((1,), (1,)), ((), ()))
    zero = jnp.zeros((), BF16)
    lm0 = lane < HEAD_DIM
    for sb in range(q_blocks):
        n = step * q_blocks + sb
        q = q_ref[0, sb * ATT_BLOCK:(sb + 1) * ATT_BLOCK, :]
        if local:
            blk0 = jnp.clip(n - 1, 0, seq_len // ATT_BLOCK - 3)
            kw = kv_ref[0, pl.ds(pl.multiple_of(blk0 * ATT_BLOCK, ATT_BLOCK), 3 * ATT_BLOCK), :]
            kl, vl = kw[:, :LANES], kw[:, LANES:]
            bias = bias_ref[n - blk0]
        outs = []
        for kh in range(N_KV_HEADS):
            lm = lm0 if kh == 0 else jnp.logical_not(lm0)
            lhs = jnp.concatenate([jnp.where(lm, q[:, :LANES], zero), jnp.where(lm, q[:, LANES:], zero)],
                                  axis=0)
            sk = jnp.where(row < ATT_BLOCK, sink_ref[2 * kh], sink_ref[2 * kh + 1])
            s = lax.dot_general(lhs, kc, nt, preferred_element_type=F32)
            if local:
                s_l = lax.dot_general(lhs, kl, nt, preferred_element_type=F32) + bias
                s = jnp.concatenate([s, s_l], axis=1)
            m = jnp.maximum(sk, jnp.max(s, axis=-1, keepdims=True))
            e = jnp.exp(s - m)
            den = jnp.exp(sk - m) + jnp.sum(e, axis=-1, keepdims=True)
            e = e.astype(BF16)
            o = jnp.dot(e[:, :lc], vc, preferred_element_type=F32)
            if local:
                o = o + jnp.dot(e[:, lc:], vl, preferred_element_type=F32)
            outs.append(o / den)
        o01, o23 = outs
        att = jnp.concatenate([jnp.where(lm0, o01[:ATT_BLOCK], o23[:ATT_BLOCK]),
                               jnp.where(lm0, o01[ATT_BLOCK:], o23[ATT_BLOCK:])], axis=1)
        ms = jnp.mean(att * att, axis=-1, keepdims=True)
        o_ref[0, sb * ATT_BLOCK:(sb + 1) * ATT_BLOCK, :] = (att * lax.rsqrt(ms + EPS) * g_ref[...]).astype(BF16)


def _band_bias():
    i = np.arange(2 * ATT_BLOCK)[None, :, None] % ATT_BLOCK
    j = np.arange(3 * ATT_BLOCK)[None, None, :]
    v = np.arange(3)[:, None, None]
    return jnp.asarray(np.where(np.abs(j - i - ATT_BLOCK * v) <= WINDOW, 0.0, NEG_INF), F32)


def _attention(q, kv, kvc, sink, g_att, *, local):
    b, L, _ = q.shape
    lc = kvc.shape[1]
    nb = L // ATT_BLOCK
    assert not local or nb >= 3
    q_blocks = max(q for q in (4, 2, 1) if nb % q == 0)
    tq = q_blocks * ATT_BLOCK
    in_specs = [pl.BlockSpec(memory_space=pltpu.SMEM),
                pl.BlockSpec((1, tq, 256), lambda bb, i: (bb, i, 0)),
                pl.BlockSpec((1, kv.shape[1], 256), lambda bb, i: (bb, 0, 0)),
                pl.BlockSpec((1, lc, 256), lambda bb, i: (bb, 0, 0)),
                pl.BlockSpec((1, 256), lambda bb, i: (0, 0))]
    args = [sink, q, kv, kvc, g_att.reshape(1, 256)]
    if local:
        in_specs.append(pl.BlockSpec((3, 2 * ATT_BLOCK, 3 * ATT_BLOCK), lambda bb, i: (0, 0, 0)))
        args.append(_band_bias())
    return pl.pallas_call(
        functools.partial(_attn_kernel, local=local, seq_len=L, q_blocks=q_blocks),
        grid=(b, nb // q_blocks),
        in_specs=in_specs,
        out_specs=pl.BlockSpec((1, tq, 256), lambda bb, i: (bb, i, 0)),
        out_shape=jax.ShapeDtypeStruct((b, L, 256), BF16),
        compiler_params=_cparams(("arbitrary", "arbitrary")),
        name="attn_window" if local else "attn_ctx",
    )(*args)


def _layernorm(x, g, b):
    mu = jnp.mean(x, axis=-1, keepdims=True)
    xc = x - mu
    return xc * lax.rsqrt(jnp.mean(xc * xc, axis=-1, keepdims=True) + EPS) * g + b


def _group_norm_store(y, g):
    ms = jnp.mean(y * y, axis=-1, keepdims=True)
    return (y * lax.rsqrt(ms + EPS) * g).astype(BF16)


def _mix_kernel(cur_ref, prev_ref, next_ref, dww_ref, dwb_ref, clg_ref, clb_ref, cpw_ref,
                glg_ref, glb_ref, gws_ref, gbs_ref, pw_ref, ps_ref, gn_ref, o_ref,
                hc_ref, sh_ref, hp_ref, c2_ref, c4_ref, c8_ref, *, ts, seq_len):
    i = pl.program_id(1)
    nt = pl.num_programs(1)
    has_prev = (i > 0).astype(F32)
    has_next = (i < nt - 1).astype(F32)

    def glu(blk):
        return blk[:, 0:256] * jax.nn.sigmoid(blk[:, 256:512])

    pv = prev_ref[0]
    nx = next_ref[0]
    hc_ref[0:HALO, :] = glu(pv) * has_prev
    hc_ref[HALO + ts:2 * HALO + ts, :] = glu(nx) * has_next
    hp_ref[0:HALO, :] = pv[:, 1024:1280] * has_prev
    hp_ref[HALO + ts:2 * HALO + ts, :] = nx[:, 1024:1280] * has_next
    hp_ref[2 * HALO + ts:3 * HALO + ts, :] = jnp.zeros((HALO, 256), F32)
    hc_ref[HALO:HALO + ts, :] = glu(cur_ref[0, :, 0:512])
    hp_ref[HALO:HALO + ts, :] = cur_ref[0, :, 1024:1280]

    for r in range(1, 8):
        sh_ref[r - 1] = hc_ref[r:r + ts + 24, :]
    c2_ref[8:ts + 40, :] = hp_ref[8:ts + 40, :] + hp_ref[9:ts + 41, :]
    c4_ref[8:ts + 32, :] = c2_ref[8:ts + 32, :] + c2_ref[10:ts + 34, :]
    c8_ref[8:ts + 24, :] = c4_ref[8:ts + 24, :] + c4_ref[12:ts + 28, :]

    lane = lax.broadcasted_iota(jnp.int32, (1, 256), 1)
    rc = min(128, ts)
    for c in range(ts // rc):
        r0 = c * rc
        acc = jnp.zeros((rc, 256), F32)
        for j in range(CONV_WIDTH):
            a, r = divmod(1 + j, 8)
            lo = r0 + 8 * a
            tap = hc_ref[lo:lo + rc, :] if r == 0 else sh_ref[r - 1, lo:lo + rc, :]
            acc = acc + tap * dww_ref[j:j + 1, :]
        hcv = _layernorm(acc + dwb_ref[...], clg_ref[...], clb_ref[...])
        hcv = hcv * jax.nn.sigmoid(hcv)
        conv = jnp.dot(hcv.astype(BF16), cpw_ref[...], preferred_element_type=F32)
        o_ref[0, r0:r0 + rc, 0:256] = _group_norm_store(conv, gn_ref[:, 0:256])

        s0 = r0 + HALO
        hcur = hp_ref[s0:s0 + rc, :]
        s2 = c2_ref[s0 - 1:s0 - 1 + rc, :]
        s4 = c4_ref[s0 - 2:s0 - 2 + rc, :]
        s8 = c8_ref[s0 - 4:s0 - 4 + rc, :]
        s16 = c8_ref[s0 - 8:s0 - 8 + rc, :] + c8_ref[s0:s0 + rc, :]
        wsum = jnp.where(lane < 64, s2, jnp.where(lane < 128, s4, jnp.where(lane < 192, s8, s16)))
        half = jnp.where(lane < 64, 1, jnp.where(lane < 128, 2, jnp.where(lane < 192, 4, 8)))
        t = i * ts + r0 + lax.broadcasted_iota(jnp.int32, (rc, 1), 0)
        cnt = jnp.clip(t + half, 0, seq_len) - jnp.clip(t - half, 0, seq_len)
        y = wsum / cnt.astype(F32) - hcur
        pool = jnp.dot(y.astype(BF16), pw_ref[...], preferred_element_type=F32) * ps_ref[...]
        o_ref[0, r0:r0 + rc, 512:768] = _group_norm_store(pool, gn_ref[:, 512:768])

    for c in range(ts // CHUNK):
        r0 = c * CHUNK
        u = cur_ref[0, r0:r0 + CHUNK, 512:768]
        v = _layernorm(cur_ref[0, r0:r0 + CHUNK, 768:1024], glg_ref[...], glb_ref[...])
        r = jnp.dot(gws_ref[...], v.astype(BF16), preferred_element_type=F32)
        mixed = jnp.where(lane < 64, r[0:CHUNK],
                          jnp.where(lane < 128, r[CHUNK:2 * CHUNK],
                                    jnp.where(lane < 192, r[2 * CHUNK:3 * CHUNK], r[3 * CHUNK:])))
        gm = u * (mixed + gbs_ref[...])
        o_ref[0, r0:r0 + CHUNK, 256:512] = _group_norm_store(gm, gn_ref[:, 256:512])


def _local_mixers(mix, lp):
    b, L, w = mix.shape
    ts = min(ROW_TILE, L)
    assert L % ts == 0 and ts % CHUNK == 0
    nt = L // ts
    hb = ts // HALO
    last_h = L // HALO - 1
    full = lambda shape: pl.BlockSpec(shape, lambda bb, i: (0,) * len(shape))
    params = [lp["dw_w"], lp["dw_b"], lp["cln_g"], lp["cln_b"], lp["cpw"], lp["gln_g"], lp["gln_b"],
              lp["gws"], lp["gbs"], lp["pw"], lp["ps"], lp["gn_local"]]
    return pl.pallas_call(
        functools.partial(_mix_kernel, ts=ts, seq_len=L),
        grid=(b, nt),
        in_specs=[pl.BlockSpec((1, ts, w), lambda bb, i: (bb, i, 0)),
                  pl.BlockSpec((1, HALO, w), lambda bb, i: (bb, jnp.maximum(i * hb - 1, 0), 0)),
                  pl.BlockSpec((1, HALO, w), lambda bb, i: (bb, jnp.minimum((i + 1) * hb, last_h), 0))]
                 + [full(p.shape) for p in params],
        out_specs=pl.BlockSpec((1, ts, LOCAL_W), lambda bb, i: (bb, i, 0)),
        out_shape=jax.ShapeDtypeStruct((b, L, LOCAL_W), BF16),
        scratch_shapes=[pltpu.VMEM((ts + 2 * HALO, 256), F32), pltpu.VMEM((7, ts + 24, 256), F32)]
                       + [pltpu.VMEM((ts + 3 * HALO, 256), F32)] * 4,
        compiler_params=_cparams(("arbitrary", "arbitrary")),
        name="local_mixers",
    )(mix, mix, mix, *params)


def _merge_kernel(*refs, n_alias):
    (att_ref, loc_ref, x_ref, g1_ref, sc_ref, sh_ref, n2g_ref, woa_ref, wol_ref,
     rw_ref, rb_ref, utri_ref, cnt_in_ref) = refs[:13]
    xo_ref, h2_ref, gate_ref, ir_ref, cnt_ref, run_ref = refs[13 + n_alias:]

    @pl.when(jnp.logical_and(pl.program_id(0) == 0, pl.program_id(1) == 0))
    def _():
        run_ref[...] = cnt_in_ref[...]

    tm = x_ref.shape[1]
    y = (jnp.dot(att_ref[0], woa_ref[...], preferred_element_type=F32)
         + jnp.dot(loc_ref[0], wol_ref[...], preferred_element_type=F32))
    xn = x_ref[0] + g1_ref[0] * y
    xo_ref[0] = xn
    h2 = _modulated_rmsnorm(xn, n2g_ref[...], sc_ref[0], sh_ref[0]).astype(BF16)
    h2_ref[...] = h2
    nt_dims = (((1,), (1,)), ((), ()))
    logits = lax.dot_general(rw_ref[...], h2, nt_dims, preferred_element_type=F32) + rb_ref[...]
    eidx = lax.broadcasted_iota(jnp.int32, logits.shape, 0).astype(F32)
    hits = jnp.zeros(logits.shape, F32)
    sels, ids, vals = [], [], []
    top = None
    for k in range(TOP_K):
        m = jnp.max(logits, axis=0, keepdims=True)
        sel = jnp.min(jnp.where(logits == m, eidx, float(N_EXPERTS)), axis=0, keepdims=True)
        chosen = eidx == sel
        if top is None:
            top = m
        vals.append(jnp.exp(m - top))
        ids.append(sel)
        hits = jnp.where(chosen, 1.0, hits)
        logits = jnp.where(chosen, -jnp.inf, logits)
        sels.append(chosen)
    den = vals[0] + vals[1] + vals[2] + vals[3]
    gates_t = jnp.concatenate([v / den for v in vals] + [jnp.zeros((ROUTE_W - TOP_K, tm), F32)], axis=0)
    gate_ref[...] = gates_t.T
    base = run_ref[...] + jnp.dot(hits.astype(BF16), utri_ref[...], preferred_element_type=F32)
    ranks = [jnp.sum(jnp.where(sels[k], base, 0.0), axis=0, keepdims=True) for k in range(TOP_K)]
    ir_ref[...] = jnp.concatenate(ids + ranks, axis=0).astype(jnp.int32)
    run_ref[...] = run_ref[...] + jnp.sum(hits, axis=1, keepdims=True)
    cnt_ref[...] = run_ref[...]


def _merge(att, loc, x, xoff, g1, sc2, sh2, moff, n2g, woa, wol, rw, rb, cnt_in, bufs, row_off, t_total):
    b = att.shape[0]
    _, L, d = x.shape
    tm = min(ROW_TILE, L)
    nt = L // tm
    blk_off = row_off // tm
    bm = g1.shape[0]
    mod_map = (lambda bb, i: (moff + bb, 0, 0)) if bm > 1 else (lambda bb, i: (0, 0, 0))
    tile = lambda w: pl.BlockSpec((1, tm, w), lambda bb, i: (bb, i, 0))
    xtile = pl.BlockSpec((1, tm, d), lambda bb, i: (xoff + bb, i, 0))
    flat = lambda w: pl.BlockSpec((tm, w), lambda bb, i: (blk_off + bb * nt + i, 0))
    full = lambda shape: pl.BlockSpec(shape, lambda bb, i: (0,) * len(shape))
    assert L % tm == 0 and row_off % tm == 0
    utri = jnp.asarray((np.arange(tm)[:, None] < np.arange(tm)[None, :]).astype(np.float32), BF16)
    n_alias = 0 if bufs is None else len(bufs)
    n_in = 13
    return pl.pallas_call(
        functools.partial(_merge_kernel, n_alias=n_alias),
        grid=(b, nt),
        in_specs=[tile(256), tile(LOCAL_W), xtile,
                  pl.BlockSpec((1, 1, d), mod_map), pl.BlockSpec((1, 1, d), mod_map),
                  pl.BlockSpec((1, 1, d), mod_map),
                  full((1, d)), full(woa.shape), full(wol.shape), full(rw.shape), full(rb.shape),
                  full((tm, tm)), full((N_EXPERTS, 1))]
                 + [pl.BlockSpec(memory_space=pl.ANY)] * n_alias,
        out_specs=(tile(d), flat(d), flat(ROUTE_W),
                   pl.BlockSpec((2 * TOP_K, tm), lambda bb, i: (0, blk_off + bb * nt + i)),
                   full((N_EXPERTS, 1))),
        out_shape=(jax.ShapeDtypeStruct((b, L, d), F32),
                   jax.ShapeDtypeStruct((t_total, d), BF16),
                   jax.ShapeDtypeStruct((t_total, ROUTE_W), F32),
                   jax.ShapeDtypeStruct((2 * TOP_K, t_total), jnp.int32),
                   jax.ShapeDtypeStruct((N_EXPERTS, 1), F32)),
        scratch_shapes=[pltpu.VMEM((N_EXPERTS, 1), F32)],
        input_output_aliases={n_in + j: 1 + j for j in range(n_alias)},
        compiler_params=_cparams(("arbitrary", "arbitrary")),
        name="merge_router",
    )(att, loc, x, g1, sc2, sh2, n2g.reshape(1, d), woa, wol, rw, rb, utri, cnt_in,
      *(bufs if bufs is not None else ()))


def _moe_kernel(blk_e_ref, n_used_ref, nxt_e_ref, blk_valid_ref, x_ref, w1_hbm, b1_ref, w2_hbm, b2_ref, o_ref,
                w1f_ref, w2f_ref, w1s_ref, w2s_ref, sem, *, layer):
    i = pl.program_id(0)
    active = i < n_used_ref[0]
    e = blk_e_ref[i]
    new_expert = jnp.logical_or(i == 0, e != blk_e_ref[jnp.maximum(i - 1, 0)])

    def weight_copies(expert):
        return (pltpu.make_async_copy(w1_hbm.at[layer, expert], w1f_ref, sem.at[0]),
                pltpu.make_async_copy(w2_hbm.at[layer, expert], w2f_ref, sem.at[1]))

    @pl.when(jnp.logical_and(i == 0, active))
    def _():
        for cp in weight_copies(e):
            cp.start()

    @pl.when(jnp.logical_and(active, new_expert))
    def _():
        for cp in weight_copies(e):
            cp.wait()
        rows = 128
        for r in range(D_MODEL // rows):
            w1s_ref[r * rows:(r + 1) * rows, :] = w1f_ref[r * rows:(r + 1) * rows, :].astype(BF16)
        for r in range(D_FF // rows):
            w2s_ref[r * rows:(r + 1) * rows, :] = w2f_ref[r * rows:(r + 1) * rows, :].astype(BF16)
        nxt = nxt_e_ref[i]

        @pl.when(nxt >= 0)
        def _():
            for cp in weight_copies(nxt):
                cp.start()

    valid = blk_valid_ref[i]
    for part in range(MOE_TM // MOE_SUB):
        r0 = part * MOE_SUB
        live = jnp.logical_and(active, valid > r0)

        @pl.when(live)
        def _():
            x = x_ref[r0:r0 + MOE_SUB, :]
            acc = jnp.zeros((MOE_SUB, D_MODEL), F32)
            for c in range(D_FF // FF_CHUNK):
                lo = c * FF_CHUNK
                glu = jnp.dot(x, w1s_ref[:, lo:lo + FF_CHUNK], preferred_element_type=F32) \
                    + b1_ref[0, 0, :, lo:lo + FF_CHUNK]
                lin = jnp.dot(x, w1s_ref[:, D_FF + lo:D_FF + lo + FF_CHUNK], preferred_element_type=F32) \
                    + b1_ref[0, 0, :, D_FF + lo:D_FF + lo + FF_CHUNK]
                glu = jnp.minimum(glu, SWIGLU_LIMIT)
                lin = jnp.clip(lin, -SWIGLU_LIMIT, SWIGLU_LIMIT)
                act = glu * jax.nn.sigmoid(SWIGLU_ALPHA * glu) * (lin + 1.0)
                acc = acc + jnp.dot(act.astype(BF16), w2s_ref[lo:lo + FF_CHUNK, :],
                                    preferred_element_type=F32)
            o_ref[r0:r0 + MOE_SUB, :] = (acc + b2_ref[0, 0]).astype(o_ref.dtype)

        @pl.when(jnp.logical_not(live))
        def _():
            o_ref[r0:r0 + MOE_SUB, :] = jnp.zeros((MOE_SUB, D_MODEL), o_ref.dtype)


def _moe_ffn(xp, blk_e, n_used, nxt_e, blk_valid, w1, b1, w2, b2, layer):
    n_rows, d = xp.shape
    n_blk = n_rows // MOE_TM
    depth, e = w1.shape[:2]

    def row_map(i, be, nu, ne, bv):
        return (jnp.maximum(jnp.minimum(i, nu[0] - 1), 0), 0)

    def b_map(i, be, nu, ne, bv):
        return (layer, be[i], 0, 0)

    return pl.pallas_call(
        functools.partial(_moe_kernel, layer=layer),
        grid_spec=pltpu.PrefetchScalarGridSpec(
            num_scalar_prefetch=4,
            grid=(n_blk,),
            in_specs=[pl.BlockSpec((MOE_TM, d), row_map),
                      pl.BlockSpec(memory_space=pl.ANY),
                      pl.BlockSpec((1, 1, 1, 2 * D_FF), b_map),
                      pl.BlockSpec(memory_space=pl.ANY),
                      pl.BlockSpec((1, 1, 1, d), b_map)],
            out_specs=pl.BlockSpec((MOE_TM, d), lambda i, be, nu, ne, bv: (i, 0)),
            scratch_shapes=[pltpu.VMEM((d, 2 * D_FF), F32), pltpu.VMEM((D_FF, d), F32),
                            pltpu.VMEM((d, 2 * D_FF), BF16), pltpu.VMEM((D_FF, d), BF16),
                            pltpu.SemaphoreType.DMA((2,))]),
        out_shape=jax.ShapeDtypeStruct((n_rows, d), BF16),
        compiler_params=_cparams(("arbitrary",)),
        name="moe_ffn",
    )(blk_e, n_used, nxt_e, blk_valid, xp, w1, b1.reshape(depth, e, 1, 2 * D_FF), w2, b2.reshape(depth, e, 1, d))


def _combine_kernel(*refs):
    yg_ref, gate_ref, x_ref, g2_ref, fg_ref = refs[:5]
    o_ref = refs[-1]
    xo = x_ref[0] + g2_ref[0] * _weighted_expert_sum(yg_ref, gate_ref[...])
    ms = jnp.mean(xo * xo, axis=-1, keepdims=True)
    o_ref[0] = xo * lax.rsqrt(ms + EPS) * fg_ref[...]


def _combine(yg, gates, x, g2, moff, fg, row_off, *, out_buf=None, ooff=0, out_b=None):
    b, L, d = x.shape
    tm = min(COMBINE_TILE, L)
    assert L % tm == 0 and row_off % tm == 0
    nt = L // tm
    blk_off = row_off // tm
    bm = g2.shape[0]
    out_b = b if out_b is None else out_b
    mod_map = (lambda bb, i: (moff + bb, 0, 0)) if bm > 1 else (lambda bb, i: (0, 0, 0))
    extra = () if out_buf is None else (out_buf,)
    return pl.pallas_call(
        _combine_kernel,
        grid=(b, nt),
        in_specs=[pl.BlockSpec((TOP_K, 1, tm, d), lambda bb, i: (0, bb, i, 0)),
                  pl.BlockSpec((tm, ROUTE_W), lambda bb, i: (blk_off + bb * nt + i, 0)),
                  pl.BlockSpec((1, tm, d), lambda bb, i: (bb, i, 0)),
                  pl.BlockSpec((1, 1, d), mod_map),
                  pl.BlockSpec((1, d), lambda bb, i: (0, 0))]
                 + [pl.BlockSpec(memory_space=pl.ANY)] * len(extra),
        out_specs=pl.BlockSpec((1, tm, d), lambda bb, i: (ooff + bb, i, 0)),
        out_shape=jax.ShapeDtypeStruct((out_b, L, d), F32),
        input_output_aliases={5: 0} if extra else {},
        compiler_params=_cparams(("arbitrary", "arbitrary")),
        name="combine",
    )(yg, gates, x, g2, fg.reshape(1, d), *extra)


def _dispatch_plan(idx, rank, counts):
    n_tok = idx.shape[1]
    n_asg = n_tok * TOP_K
    padded = (counts + MOE_TM - 1) // MOE_TM * MOE_TM
    pad_end = jnp.cumsum(padded)
    pad_start = (pad_end - padded).astype(jnp.int32)
    pos = rank
    for e in range(N_EXPERTS):
        pos = pos + jnp.where(idx == e, pad_start[e], 0)
    n_rows = -(-n_asg // MOE_TM) * MOE_TM + N_EXPERTS * MOE_TM
    n_blk = n_rows // MOE_TM
    tok = jnp.broadcast_to(jnp.arange(n_tok, dtype=jnp.int32)[None, :], (TOP_K, n_tok))
    fill = jnp.arange(n_rows, dtype=jnp.int32) % n_tok
    row_tok = fill.at[pos.reshape(-1)].add((tok - pos % n_tok).reshape(-1), mode="promise_in_bounds")
    blk_start = jnp.arange(n_blk, dtype=jnp.int32) * MOE_TM
    blk_e = jnp.minimum(jnp.sum((pad_end[None, :] <= blk_start[:, None]).astype(jnp.int32), axis=1),
                        N_EXPERTS - 1)
    n_used = (pad_end[-1] // MOE_TM).astype(jnp.int32).reshape(1)
    ids = jnp.arange(N_EXPERTS, dtype=jnp.int32)
    first_used_from = lax.cummin(jnp.where(padded > 0, ids, N_EXPERTS), axis=0, reverse=True)
    nxt_of = jnp.concatenate([first_used_from[1:], jnp.full((1,), N_EXPERTS, jnp.int32)])
    nxt_of = jnp.where(nxt_of >= N_EXPERTS, -1, nxt_of)
    own = blk_e[:, None] == ids[None, :]
    nxt_e = jnp.sum(jnp.where(own, nxt_of[None, :], 0), axis=1).astype(jnp.int32)
    seg_end = jnp.sum(jnp.where(own, (pad_start + counts)[None, :], 0), axis=1)
    blk_valid = jnp.clip(seg_end - blk_start, 0, MOE_TM).astype(jnp.int32)
    return row_tok, pos, blk_e, n_used, nxt_e, blk_valid


_Q_PERM = np.concatenate([np.arange(0, 64), np.arange(128, 192), np.arange(64, 128), np.arange(192, 256)])


def _rope_tables(L):
    t = jnp.arange(L)
    row = (t // GRID_W).astype(F32)
    col = (t % GRID_W).astype(F32)
    half = HEAD_DIM // 2
    inv = ROPE_BASE ** (-jnp.arange(0, half, 2, dtype=F32) / half)
    ar, ac = row[:, None] * inv[None, :], col[:, None] * inv[None, :]
    z = jnp.zeros_like(ar)
    cos = jnp.concatenate([jnp.cos(ar), jnp.cos(ar), jnp.cos(ac), jnp.cos(ac)], axis=1)
    s1 = jnp.concatenate([-jnp.sin(ar), z, -jnp.sin(ac), z], axis=1)
    s2 = jnp.concatenate([z, jnp.sin(ar), z, jnp.sin(ac)], axis=1)
    rep = lambda a: jnp.concatenate([a, a], axis=1)
    return rep(cos), rep(s1), rep(s2)


def _block_diag(pw):
    g, n, _ = pw.shape
    out = jnp.zeros((g * n, g * n), pw.dtype)
    for gi in range(g):
        out = out.at[gi * n:(gi + 1) * n, gi * n:(gi + 1) * n].set(pw[gi])
    return out


def kernel(x, c, ctx, c_ctx, ada_w, ada_b, norm1_g, norm2_g, w_in, attn_sink, conv_dw_w, conv_dw_b, conv_ln_g, conv_ln_b, conv_pw_w, gmlp_ln_g, gmlp_ln_b, gmlp_ws, gmlp_bs, pool_w, pool_scale, group_norm_g, w_out, router_w, router_b, exp_w1, exp_b1, exp_w2, exp_b2, final_norm_g):
    b, L, d = x.shape
    lc = ctx.shape[1]
    depth = ada_w.shape[0]
    tabs = _rope_tables(L)

    r = -(-(b + 1) // 8) * 8
    cond = jnp.zeros((r, d), F32).at[:b].set(c).at[b].set(c_ctx)
    mods = _adaln(cond, ada_w, ada_b).reshape(depth, r, 6, d)

    n_groups = BATCH_GROUPS if b % BATCH_GROUPS == 0 else 1
    bh = b // n_groups
    xs, cs = [x] * n_groups, [ctx] * n_groups
    pending = [None] * n_groups

    def finish(st, out_buf):
        return _combine(st["ygx"], st["gates"], st["xh"], st["g2"], st["moff"], final_norm_g, 0,
                        out_buf=out_buf, ooff=st["moff"], out_b=b)

    for l in range(depth):
        last = l == depth - 1
        mx = [mods[l, :b, i].reshape(b, 1, d) for i in range(6)]
        mc = [mods[l, b:b + 1, i].reshape(1, 1, d) for i in range(6)]
        sh1, sc1, g1, sh2, sc2, g2 = mx
        sh1c, sc1c, g1c, sh2c, sc2c, g2c = mc

        wl = w_in[l]
        wq = wl[:, :OFF_K][:, _Q_PERM] * (HEAD_DIM ** -0.5)
        w_full = jnp.concatenate([wq, wl[:, OFF_K:]], axis=1).astype(BF16)
        gn = group_norm_g[l]
        g_att = gn[:256][_Q_PERM]
        wo = w_out[l]
        woa = wo[:256][_Q_PERM].astype(BF16)
        wol = wo[256:].astype(BF16)
        lp = dict(
            dw_w=conv_dw_w[l], dw_b=conv_dw_b[l].reshape(1, -1),
            cln_g=conv_ln_g[l].reshape(1, -1), cln_b=conv_ln_b[l].reshape(1, -1),
            cpw=conv_pw_w[l].astype(BF16),
            gln_g=gmlp_ln_g[l].reshape(1, -1), gln_b=gmlp_ln_b[l].reshape(1, -1),
            gws=gmlp_ws[l].reshape(GMLP_HEADS * CHUNK, CHUNK).astype(BF16),
            gbs=jnp.repeat(gmlp_bs[l].T, GROUP_W // GMLP_HEADS, axis=1),
            pw=_block_diag(pool_w[l]).astype(BF16), ps=pool_scale[l].reshape(1, -1),
            gn_local=gn[256:].reshape(1, -1))
        rw = router_w[l].T.astype(BF16)
        rb = router_b[l].reshape(N_EXPERTS, 1)

        w_kv = wl[:, OFF_K:OFF_CONV].astype(BF16)
        take = lambda a, i: a.at[i].get(mode="promise_in_bounds")
        staged = []
        for h in range(n_groups):
            moff = h * bh
            pd = pending[h]
            xoff = moff if l == 0 else 0

            if pd is None:
                xh, ch = xs[h], cs[h]
                q, kv, mix = _inproj(xh, xoff, bh, sc1, sh1, moff, norm1_g[l], w_full, tabs, full=True)
                if last:
                    (kvc,) = _inproj(ch, xoff, bh, sc1c, sh1c, 0, norm1_g[l], w_kv, None, full=False)
                else:
                    qc, kvc, mixc = _inproj(ch, xoff, bh, sc1c, sh1c, 0, norm1_g[l], w_full, None, full=True)
            else:
                if staged:
                    pd["ygx"], staged[-1]["xh"] = lax.optimization_barrier((pd["ygx"], staged[-1]["xh"]))
                combx = (pd["ygx"], pd["gates"], 0, pd["g2"], pd["moff"])
                combc = (pd["ygc"], pd["gates"], bh * L, pd["g2c"], 0)
                xh, q, kv, mix = _inproj(pd["xh"], 0, bh, sc1, sh1, moff, norm1_g[l], w_full, tabs,
                                         full=True, comb=combx)
                if last:
                    ch = None
                    (kvc,) = _inproj(pd["ch"], 0, bh, sc1c, sh1c, 0, norm1_g[l], w_kv, None,
                                     full=False, comb=combc, emit_x=False)
                else:
                    ch, qc, kvc, mixc = _inproj(pd["ch"], 0, bh, sc1c, sh1c, 0, norm1_g[l], w_full, None,
                                                full=True, comb=combc)
                xoff = 0
            att = _attention(q, kv, kvc, attn_sink[l], g_att, local=True)
            loc = _local_mixers(mix, lp)
            if staged:
                prev = staged[-1]
                prev["xp"], loc = lax.optimization_barrier((prev["xp"], loc))
            nxt = pending[h + 1] if h + 1 < n_groups else None
            if nxt is not None:
                nxt["ygx"], loc = lax.optimization_barrier((nxt["ygx"], loc))
            t_total = bh * L if last else bh * (L + lc)
            cnt0 = jnp.zeros((N_EXPERTS, 1), F32)
            xh, h2, gates, ir, cnt = _merge(att, loc, xh, xoff, g1, sc2, sh2, moff, norm2_g[l], woa, wol,
                                            rw, rb, cnt0, None, 0, t_total)
            if not last:
                attc = _attention(qc, kvc, kvc, attn_sink[l], g_att, local=False)
                locc = _local_mixers(mixc, lp)
                ch, h2, gates, ir, cnt = _merge(attc, locc, ch, xoff, g1c, sc2c, sh2c, 0, norm2_g[l],
                                                woa, wol, rw, rb, cnt, (h2, gates, ir), bh * L, t_total)
            counts = cnt[:, 0].astype(jnp.int32)
            row_tok, pos, blk_e, n_used, nxt_e, blk_valid = _dispatch_plan(ir[:TOP_K], ir[TOP_K:], counts)
            staged.append(dict(xh=xh, ch=ch, gates=gates, pos=pos, blk_e=blk_e, n_used=n_used, nxt_e=nxt_e, blk_valid=blk_valid,
                               xp=take(h2, row_tok), moff=moff, g2=g2, g2c=g2c, last=last))

        yps = []
        for h, st in enumerate(staged):
            if yps:
                yps[-1], st["xp"] = lax.optimization_barrier((yps[-1], st["xp"]))
            yps.append(_moe_ffn(st["xp"], st["blk_e"], st["n_used"], st["nxt_e"], st["blk_valid"], exp_w1, exp_b1, exp_w2, exp_b2, l))
        for h, st in enumerate(staged):
            if h == n_groups - 1 and h > 0:
                yps[h], staged[0]["ygx"] = lax.optimization_barrier((yps[h], staged[0]["ygx"]))
            pos = st["pos"]
            st["ygx"] = take(yps[h], pos[:, :bh * L]).reshape(TOP_K, bh, L, d)
            if not last:
                st["ygc"] = take(yps[h], pos[:, bh * L:]).reshape(TOP_K, bh, lc, d)
            pending[h] = st

    out = None
    for h in range(n_groups):
        out = finish(pending[h], out)
    return out
```

```python
import functools

import jax
import jax.numpy as jnp
import numpy as np
from jax import lax
from jax.experimental import pallas as pl
from jax.experimental.pallas import tpu as pltpu

F32 = jnp.float32
BF16 = jnp.bfloat16

D_MODEL = 1024
GRID_W = 64
GROUP_W = 256
HEAD_DIM = 64
N_Q_HEADS = 4
N_KV_HEADS = 2
WINDOW = 128
ATT_BLOCK = 128
ROPE_BASE = 10000.0
CONV_WIDTH = 31
CHUNK = 128
GMLP_HEADS = 4
POOL_WINDOWS = (2, 4, 8, 16)
N_EXPERTS = 32
TOP_K = 4
D_FF = 1024
SWIGLU_ALPHA = 1.702
SWIGLU_LIMIT = 7.0
EPS = 1e-6
NEG_INF = -1e30

OFF_K = 256
OFF_V = 384
OFF_CONV = 512
IN_W = 1792
MIX_IN_W = IN_W - OFF_CONV
LOCAL_W = 3 * GROUP_W

LANES = 128
HALO = 16
ROUTE_W = LANES
ROW_TILE = 512
COMBINE_TILE = 256
ADA_TN = 1536
MOE_TM = 1024
MOE_SUB = 512
FF_CHUNK = 512
X_RING = 3
BATCH_GROUPS = 2
VMEM_LIMIT = 56 * 1024 * 1024


def _cparams(sem):
    return pltpu.CompilerParams(dimension_semantics=sem, vmem_limit_bytes=VMEM_LIMIT)


def _ada_kernel(c_ref, w_ref, b_ref, o_ref):
    c = c_ref[...]
    s = c * jax.nn.sigmoid(c)
    o_ref[0] = jnp.dot(s.astype(BF16), w_ref[0].astype(BF16), preferred_element_type=F32) + b_ref[0]


def _adaln(cond, ada_w, ada_b):
    depth, d, n = ada_w.shape
    r = cond.shape[0]
    tn = ADA_TN
    assert n % tn == 0
    return pl.pallas_call(
        _ada_kernel,
        grid=(depth, n // tn),
        in_specs=[pl.BlockSpec((r, d), lambda l, j: (0, 0)),
                  pl.BlockSpec((1, d, tn), lambda l, j: (l, 0, j)),
                  pl.BlockSpec((1, 1, tn), lambda l, j: (l, 0, j))],
        out_specs=pl.BlockSpec((1, r, tn), lambda l, j: (l, 0, j)),
        out_shape=jax.ShapeDtypeStruct((depth, r, n), F32),
        compiler_params=_cparams(("arbitrary", "arbitrary")),
        name="adaln",
    )(cond, ada_w, ada_b.reshape(depth, 1, n))


def _modulated_rmsnorm(x, g, sc, sh):
    ms = jnp.mean(x * x, axis=-1, keepdims=True)
    return x * lax.rsqrt(ms + EPS) * (g * (1.0 + sc)) + sh


def _weighted_expert_sum(yg_ref, gates):
    ff = gates[:, 0:1] * yg_ref[0, 0].astype(F32)
    for k in range(1, TOP_K):
        ff = ff + gates[:, k:k + 1] * yg_ref[k, 0].astype(F32)
    return ff


def _inproj_kernel(*refs, rope, full, fused, emit_x):
    refs = list(refs)
    if fused:
        yg_ref, gate_ref, g2_ref = refs[:3]
        refs = refs[3:]
    if rope:
        x_ref, sc_ref, sh_ref, g_ref, w_ref, cos_ref, s1_ref, s2_ref = refs[:8]
        outs = refs[8:]
    else:
        x_ref, sc_ref, sh_ref, g_ref, w_ref = refs[:5]
        outs = refs[5:]
    x = x_ref[0]
    if fused:
        x = x + g2_ref[0] * _weighted_expert_sum(yg_ref, gate_ref[...])
        if emit_x:
            outs[0][0] = x
            outs = outs[1:]
    h = _modulated_rmsnorm(x, g_ref[...], sc_ref[0], sh_ref[0])
    p = jnp.dot(h.astype(BF16), w_ref[...], preferred_element_type=F32)

    def roped(xs):
        if not rope:
            return xs
        return (xs * cos_ref[...] + pltpu.roll(xs, LANES - 16, 1) * s1_ref[...]
                + pltpu.roll(xs, 16, 1) * s2_ref[...])

    if full:
        q_ref, kv_ref, mix_ref = outs
        q_ref[0] = jnp.concatenate([roped(p[:, 0:128]), roped(p[:, 128:256])], axis=1).astype(BF16)
        kv_ref[0] = jnp.concatenate([roped(p[:, 256:384]), p[:, 384:512]], axis=1).astype(BF16)
        mix_ref[0] = p[:, OFF_CONV:]
    else:
        (kv_ref,) = outs
        kv_ref[0] = p.astype(BF16)


def _inproj(x, xoff, b, sc, sh, moff, g, w, tabs, *, full, comb=None, emit_x=True):
    _, L, d = x.shape
    n = w.shape[1]
    tm = min(ROW_TILE, L)
    nt = L // tm
    assert L % tm == 0
    rope = tabs is not None
    bm = sc.shape[0]
    mod_map = (lambda i, bb: (moff + bb, 0, 0)) if bm > 1 else (lambda i, bb: (0, 0, 0))
    in_specs, args = [], []
    if comb is not None:
        yg, gates, row_off, g2, g2off = comb
        assert row_off % tm == 0
        blk_off = row_off // tm
        g2_map = (lambda i, bb: (g2off + bb, 0, 0)) if g2.shape[0] > 1 else (lambda i, bb: (0, 0, 0))
        in_specs += [pl.BlockSpec((TOP_K, 1, tm, d), lambda i, bb: (0, bb, i, 0)),
                     pl.BlockSpec((tm, ROUTE_W), lambda i, bb: (blk_off + bb * nt + i, 0)),
                     pl.BlockSpec((1, 1, d), g2_map)]
        args += [yg, gates, g2]
    in_specs += [pl.BlockSpec((1, tm, d), lambda i, bb: (xoff + bb, i, 0)),
                 pl.BlockSpec((1, 1, d), mod_map),
                 pl.BlockSpec((1, 1, d), mod_map),
                 pl.BlockSpec((1, d), lambda i, bb: (0, 0)),
                 pl.BlockSpec((d, n), lambda i, bb: (0, 0))]
    args += [x, sc, sh, g.reshape(1, d), w]
    if rope:
        in_specs += [pl.BlockSpec((tm, LANES), lambda i, bb: (i, 0))] * 3
        args += list(tabs)
    tile = lambda wd: pl.BlockSpec((1, tm, wd), lambda i, bb: (bb, i, 0))
    out_shape, out_specs = [], []
    emit_x = emit_x and comb is not None
    if emit_x:
        out_shape.append(jax.ShapeDtypeStruct((b, L, d), F32))
        out_specs.append(tile(d))
    if full:
        out_shape += [jax.ShapeDtypeStruct((b, L, 256), BF16),
                      jax.ShapeDtypeStruct((b, L, 256), BF16),
                      jax.ShapeDtypeStruct((b, L, MIX_IN_W), F32)]
        out_specs += [tile(256), tile(256), tile(MIX_IN_W)]
    else:
        out_shape.append(jax.ShapeDtypeStruct((b, L, 256), BF16))
        out_specs.append(tile(256))
    return pl.pallas_call(
        functools.partial(_inproj_kernel, rope=rope, full=full, fused=comb is not None, emit_x=emit_x),
        grid=(nt, b),
        in_specs=in_specs, out_specs=tuple(out_specs), out_shape=tuple(out_shape),
        compiler_params=_cparams(("arbitrary", "arbitrary")),
        name="inproj_x" if rope else "inproj_ctx",
    )(*args)


def _attn_kernel(*refs, local, seq_len, q_blocks):
    if local:
        sink_ref, q_ref, kv_ref, kvc_ref, g_ref, bias_ref, o_ref = refs
    else:
        sink_ref, q_ref, kv_ref, kvc_ref, g_ref, o_ref = refs
    step = pl.program_id(1)
    lane = lax.broadcasted_iota(jnp.int32, (1, LANES), 1)
    row = lax.broadcasted_iota(jnp.int32, (2 * ATT_BLOCK, 1), 0)
    kvc = kvc_ref[0]
    kc, vc = kvc[:, :LANES], kvc[:, LANES:]
    lc = kvc.shape[0]
    nt = (((1,), (1,)), ((), ()))
    zero = jnp.zeros((), BF16)
    lm0 = lane < HEAD_DIM
    for sb in range(q_blocks):
        n = step * q_blocks + sb
        q = q_ref[0, sb * ATT_BLOCK:(sb + 1) * ATT_BLOCK, :]
        if local:
            blk0 = jnp.clip(n - 1, 0, seq_len // ATT_BLOCK - 3)
            kw = kv_ref[0, pl.ds(pl.multiple_of(blk0 * ATT_BLOCK, ATT_BLOCK), 3 * ATT_BLOCK), :]
            kl, vl = kw[:, :LANES], kw[:, LANES:]
            bias = bias_ref[n - blk0]
        outs = []
        for kh in range(N_KV_HEADS):
            lm = lm0 if kh == 0 else jnp.logical_not(lm0)
            lhs = jnp.concatenate([jnp.where(lm, q[:, :LANES], zero), jnp.where(lm, q[:, LANES:], zero)],
                                  axis=0)
            sk = jnp.where(row < ATT_BLOCK, sink_ref[2 * kh], sink_ref[2 * kh + 1])
            s = lax.dot_general(lhs, kc, nt, preferred_element_type=F32)
            if local:
                s_l = lax.dot_general(lhs, kl, nt, preferred_element_type=F32) + bias
                s = jnp.concatenate([s, s_l], axis=1)
            m = jnp.maximum(sk, jnp.max(s, axis=-1, keepdims=True))
            e = jnp.exp(s - m)
            den = jnp.exp(sk - m) + jnp.sum(e, axis=-1, keepdims=True)
            e = e.astype(BF16)
            o = jnp.dot(e[:, :lc], vc, preferred_element_type=F32)
            if local:
                o = o + jnp.dot(e[:, lc:], vl, preferred_element_type=F32)
            outs.append(o / den)
        o01, o23 = outs
        att = jnp.concatenate([jnp.where(lm0, o01[:ATT_BLOCK], o23[:ATT_BLOCK]),
                               jnp.where(lm0, o01[ATT_BLOCK:], o23[ATT_BLOCK:])], axis=1)
        ms = jnp.mean(att * att, axis=-1, keepdims=True)
        o_ref[0, sb * ATT_BLOCK:(sb + 1) * ATT_BLOCK, :] = (att * lax.rsqrt(ms + EPS) * g_ref[...]).astype(BF16)


def _band_bias():
    i = np.arange(2 * ATT_BLOCK)[None, :, None] % ATT_BLOCK
    j = np.arange(3 * ATT_BLOCK)[None, None, :]
    v = np.arange(3)[:, None, None]
    return jnp.asarray(np.where(np.abs(j - i - ATT_BLOCK * v) <= WINDOW, 0.0, NEG_INF), F32)


def _attention(q, kv, kvc, sink, g_att, *, local):
    b, L, _ = q.shape
    lc = kvc.shape[1]
    nb = L // ATT_BLOCK
    assert not local or nb >= 3
    q_blocks = max(q for q in (4, 2, 1) if nb % q == 0)
    tq = q_blocks * ATT_BLOCK
    in_specs = [pl.BlockSpec(memory_space=pltpu.SMEM),
                pl.BlockSpec((1, tq, 256), lambda bb, i: (bb, i, 0)),
                pl.BlockSpec((1, kv.shape[1], 256), lambda bb, i: (bb, 0, 0)),
                pl.BlockSpec((1, lc, 256), lambda bb, i: (bb, 0, 0)),
                pl.BlockSpec((1, 256), lambda bb, i: (0, 0))]
    args = [sink, q, kv, kvc, g_att.reshape(1, 256)]
    if local:
        in_specs.append(pl.BlockSpec((3, 2 * ATT_BLOCK, 3 * ATT_BLOCK), lambda bb, i: (0, 0, 0)))
        args.append(_band_bias())
    return pl.pallas_call(
        functools.partial(_attn_kernel, local=local, seq_len=L, q_blocks=q_blocks),
        grid=(b, nb // q_blocks),
        in_specs=in_specs,
        out_specs=pl.BlockSpec((1, tq, 256), lambda bb, i: (bb, i, 0)),
        out_shape=jax.ShapeDtypeStruct((b, L, 256), BF16),
        compiler_params=_cparams(("arbitrary", "arbitrary")),
        name="attn_window" if local else "attn_ctx",
    )(*args)


def _layernorm(x, g, b):
    mu = jnp.mean(x, axis=-1, keepdims=True)
    xc = x - mu
    return xc * lax.rsqrt(jnp.mean(xc * xc, axis=-1, keepdims=True) + EPS) * g + b


def _group_norm_store(y, g):
    ms = jnp.mean(y * y, axis=-1, keepdims=True)
    return (y * lax.rsqrt(ms + EPS) * g).astype(BF16)


def _mix_kernel(cur_ref, prev_ref, next_ref, dww_ref, dwb_ref, clg_ref, clb_ref, cpw_ref,
                glg_ref, glb_ref, gws_ref, gbs_ref, pw_ref, ps_ref, gn_ref, o_ref,
                hc_ref, sh_ref, hp_ref, c2_ref, c4_ref, c8_ref, *, ts, seq_len):
    i = pl.program_id(1)
    nt = pl.num_programs(1)
    has_prev = (i > 0).astype(F32)
    has_next = (i < nt - 1).astype(F32)

    def glu(blk):
        return blk[:, 0:256] * jax.nn.sigmoid(blk[:, 256:512])

    pv = prev_ref[0]
    nx = next_ref[0]
    hc_ref[0:HALO, :] = glu(pv) * has_prev
    hc_ref[HALO + ts:2 * HALO + ts, :] = glu(nx) * has_next
    hp_ref[0:HALO, :] = pv[:, 1024:1280] * has_prev
    hp_ref[HALO + ts:2 * HALO + ts, :] = nx[:, 1024:1280] * has_next
    hp_ref[2 * HALO + ts:3 * HALO + ts, :] = jnp.zeros((HALO, 256), F32)
    hc_ref[HALO:HALO + ts, :] = glu(cur_ref[0, :, 0:512])
    hp_ref[HALO:HALO + ts, :] = cur_ref[0, :, 1024:1280]

    for r in range(1, 8):
        sh_ref[r - 1] = hc_ref[r:r + ts + 24, :]
    c2_ref[8:ts + 40, :] = hp_ref[8:ts + 40, :] + hp_ref[9:ts + 41, :]
    c4_ref[8:ts + 32, :] = c2_ref[8:ts + 32, :] + c2_ref[10:ts + 34, :]
    c8_ref[8:ts + 24, :] = c4_ref[8:ts + 24, :] + c4_ref[12:ts + 28, :]

    lane = lax.broadcasted_iota(jnp.int32, (1, 256), 1)
    rc = min(128, ts)
    for c in range(ts // rc):
        r0 = c * rc
        acc = jnp.zeros((rc, 256), F32)
        for j in range(CONV_WIDTH):
            a, r = divmod(1 + j, 8)
            lo = r0 + 8 * a
            tap = hc_ref[lo:lo + rc, :] if r == 0 else sh_ref[r - 1, lo:lo + rc, :]
            acc = acc + tap * dww_ref[j:j + 1, :]
        hcv = _layernorm(acc + dwb_ref[...], clg_ref[...], clb_ref[...])
        hcv = hcv * jax.nn.sigmoid(hcv)
        conv = jnp.dot(hcv.astype(BF16), cpw_ref[...], preferred_element_type=F32)
        o_ref[0, r0:r0 + rc, 0:256] = _group_norm_store(conv, gn_ref[:, 0:256])

        s0 = r0 + HALO
        hcur = hp_ref[s0:s0 + rc, :]
        s2 = c2_ref[s0 - 1:s0 - 1 + rc, :]
        s4 = c4_ref[s0 - 2:s0 - 2 + rc, :]
        s8 = c8_ref[s0 - 4:s0 - 4 + rc, :]
        s16 = c8_ref[s0 - 8:s0 - 8 + rc, :] + c8_ref[s0:s0 + rc, :]
        wsum = jnp.where(lane < 64, s2, jnp.where(lane < 128, s4, jnp.where(lane < 192, s8, s16)))
        half = jnp.where(lane < 64, 1, jnp.where(lane < 128, 2, jnp.where(lane < 192, 4, 8)))
        t = i * ts + r0 + lax.broadcasted_iota(jnp.int32, (rc, 1), 0)
        cnt = jnp.clip(t + half, 0, seq_len) - jnp.clip(t - half, 0, seq_len)
        y = wsum / cnt.astype(F32) - hcur
        pool = jnp.dot(y.astype(BF16), pw_ref[...], preferred_element_type=F32) * ps_ref[...]
        o_ref[0, r0:r0 + rc, 512:768] = _group_norm_store(pool, gn_ref[:, 512:768])

    for c in range(ts // CHUNK):
        r0 = c * CHUNK
        u = cur_ref[0, r0:r0 + CHUNK, 512:768]
        v = _layernorm(cur_ref[0, r0:r0 + CHUNK, 768:1024], glg_ref[...], glb_ref[...])
        r = jnp.dot(gws_ref[...], v.astype(BF16), preferred_element_type=F32)
        mixed = jnp.where(lane < 64, r[0:CHUNK],
                          jnp.where(lane < 128, r[CHUNK:2 * CHUNK],
                                    jnp.where(lane < 192, r[2 * CHUNK:3 * CHUNK], r[3 * CHUNK:])))
        gm = u * (mixed + gbs_ref[...])
        o_ref[0, r0:r0 + CHUNK, 256:512] = _group_norm_store(gm, gn_ref[:, 256:512])


def _local_mixers(mix, lp):
    b, L, w = mix.shape
    ts = min(ROW_TILE, L)
    assert L % ts == 0 and ts % CHUNK == 0
    nt = L // ts
    hb = ts // HALO
    last_h = L // HALO - 1
    full = lambda shape: pl.BlockSpec(shape, lambda bb, i: (0,) * len(shape))
    params = [lp["dw_w"], lp["dw_b"], lp["cln_g"], lp["cln_b"], lp["cpw"], lp["gln_g"], lp["gln_b"],
              lp["gws"], lp["gbs"], lp["pw"], lp["ps"], lp["gn_local"]]
    return pl.pallas_call(
        functools.partial(_mix_kernel, ts=ts, seq_len=L),
        grid=(b, nt),
        in_specs=[pl.BlockSpec((1, ts, w), lambda bb, i: (bb, i, 0)),
                  pl.BlockSpec((1, HALO, w), lambda bb, i: (bb, jnp.maximum(i * hb - 1, 0), 0)),
                  pl.BlockSpec((1, HALO, w), lambda bb, i: (bb, jnp.minimum((i + 1) * hb, last_h), 0))]
                 + [full(p.shape) for p in params],
        out_specs=pl.BlockSpec((1, ts, LOCAL_W), lambda bb, i: (bb, i, 0)),
        out_shape=jax.ShapeDtypeStruct((b, L, LOCAL_W), BF16),
        scratch_shapes=[pltpu.VMEM((ts + 2 * HALO, 256), F32), pltpu.VMEM((7, ts + 24, 256), F32)]
                       + [pltpu.VMEM((ts + 3 * HALO, 256), F32)] * 4,
        compiler_params=_cparams(("arbitrary", "arbitrary")),
        name="local_mixers",
    )(mix, mix, mix, *params)


def _merge_kernel(*refs, n_alias, xoff):
    (att_ref, loc_ref, x_hbm, g1_ref, sc_ref, sh_ref, n2g_ref, woa_ref, wol_ref,
     rw_ref, rb_ref, utri_ref, cnt_in_ref) = refs[:13]
    xo_ref, h2_ref, gate_ref, ir_ref, cnt_ref, run_ref, xbuf_ref, xsem = refs[13 + n_alias:]

    @pl.when(jnp.logical_and(pl.program_id(0) == 0, pl.program_id(1) == 0))
    def _():
        run_ref[...] = cnt_in_ref[...]

    tm = xbuf_ref.shape[1]
    nt = pl.num_programs(1)
    total = pl.num_programs(0) * nt
    step = pl.program_id(0) * nt + pl.program_id(1)

    def x_copy(s):
        slot = lax.rem(s, X_RING)
        return pltpu.make_async_copy(x_hbm.at[xoff + s // nt, pl.ds(lax.rem(s, nt) * tm, tm)],
                                     xbuf_ref.at[slot], xsem.at[slot])

    @pl.when(step == 0)
    def _():
        for s in range(X_RING - 1):
            @pl.when(s < total)
            def _():
                x_copy(s).start()

    @pl.when(step + X_RING - 1 < total)
    def _():
        x_copy(step + X_RING - 1).start()

    x_copy(step).wait()
    y = (jnp.dot(att_ref[0], woa_ref[...], preferred_element_type=F32)
         + jnp.dot(loc_ref[0], wol_ref[...], preferred_element_type=F32))
    xn = xbuf_ref[lax.rem(step, X_RING)] + g1_ref[0] * y
    xo_ref[0] = xn
    h2 = _modulated_rmsnorm(xn, n2g_ref[...], sc_ref[0], sh_ref[0]).astype(BF16)
    h2_ref[...] = h2
    nt_dims = (((1,), (1,)), ((), ()))
    logits = lax.dot_general(rw_ref[...], h2, nt_dims, preferred_element_type=F32) + rb_ref[...]
    eidx = lax.broadcasted_iota(jnp.int32, logits.shape, 0).astype(F32)
    hits = jnp.zeros(logits.shape, F32)
    sels, ids, vals = [], [], []
    top = None
    for k in range(TOP_K):
        m = jnp.max(logits, axis=0, keepdims=True)
        sel = jnp.min(jnp.where(logits == m, eidx, float(N_EXPERTS)), axis=0, keepdims=True)
        chosen = eidx == sel
        if top is None:
            top = m
        vals.append(jnp.exp(m - top))
        ids.append(sel)
        hits = jnp.where(chosen, 1.0, hits)
        logits = jnp.where(chosen, -jnp.inf, logits)
        sels.append(chosen)
    den = vals[0] + vals[1] + vals[2] + vals[3]
    gates_t = jnp.concatenate([v / den for v in vals] + [jnp.zeros((ROUTE_W - TOP_K, tm), F32)], axis=0)
    gate_ref[...] = gates_t.T
    base = run_ref[...] + jnp.dot(hits.astype(BF16), utri_ref[...], preferred_element_type=F32)
    ranks = [jnp.sum(jnp.where(sels[k], base, 0.0), axis=0, keepdims=True) for k in range(TOP_K)]
    ir_ref[...] = jnp.concatenate(ids + ranks, axis=0).astype(jnp.int32)
    run_ref[...] = run_ref[...] + jnp.sum(hits, axis=1, keepdims=True)
    cnt_ref[...] = run_ref[...]


def _merge(att, loc, x, xoff, g1, sc2, sh2, moff, n2g, woa, wol, rw, rb, cnt_in, bufs, row_off, t_total):
    b = att.shape[0]
    _, L, d = x.shape
    tm = min(ROW_TILE, L)
    nt = L // tm
    blk_off = row_off // tm
    bm = g1.shape[0]
    mod_map = (lambda bb, i: (moff + bb, 0, 0)) if bm > 1 else (lambda bb, i: (0, 0, 0))
    tile = lambda w: pl.BlockSpec((1, tm, w), lambda bb, i: (bb, i, 0))
    flat = lambda w: pl.BlockSpec((tm, w), lambda bb, i: (blk_off + bb * nt + i, 0))
    full = lambda shape: pl.BlockSpec(shape, lambda bb, i: (0,) * len(shape))
    assert L % tm == 0 and row_off % tm == 0
    utri = jnp.asarray((np.arange(tm)[:, None] < np.arange(tm)[None, :]).astype(np.float32), BF16)
    n_alias = 0 if bufs is None else len(bufs)
    n_in = 13
    return pl.pallas_call(
        functools.partial(_merge_kernel, n_alias=n_alias, xoff=xoff),
        grid=(b, nt),
        in_specs=[tile(256), tile(LOCAL_W), pl.BlockSpec(memory_space=pl.ANY),
                  pl.BlockSpec((1, 1, d), mod_map), pl.BlockSpec((1, 1, d), mod_map),
                  pl.BlockSpec((1, 1, d), mod_map),
                  full((1, d)), full(woa.shape), full(wol.shape), full(rw.shape), full(rb.shape),
                  full((tm, tm)), full((N_EXPERTS, 1))]
                 + [pl.BlockSpec(memory_space=pl.ANY)] * n_alias,
        out_specs=(tile(d), flat(d), flat(ROUTE_W),
                   pl.BlockSpec((2 * TOP_K, tm), lambda bb, i: (0, blk_off + bb * nt + i)),
                   full((N_EXPERTS, 1))),
        out_shape=(jax.ShapeDtypeStruct((b, L, d), F32),
                   jax.ShapeDtypeStruct((t_total, d), BF16),
                   jax.ShapeDtypeStruct((t_total, ROUTE_W), F32),
                   jax.ShapeDtypeStruct((2 * TOP_K, t_total), jnp.int32),
                   jax.ShapeDtypeStruct((N_EXPERTS, 1), F32)),
        scratch_shapes=[pltpu.VMEM((N_EXPERTS, 1), F32), pltpu.VMEM((X_RING, tm, d), F32),
                        pltpu.SemaphoreType.DMA((X_RING,))],
        input_output_aliases={n_in + j: 1 + j for j in range(n_alias)},
        compiler_params=_cparams(("arbitrary", "arbitrary")),
        name="merge_router",
    )(att, loc, x, g1, sc2, sh2, n2g.reshape(1, d), woa, wol, rw, rb, utri, cnt_in,
      *(bufs if bufs is not None else ()))


def _moe_kernel(blk_e_ref, n_used_ref, nxt_e_ref, blk_valid_ref, x_ref, w1_hbm, b1_ref, w2_hbm, b2_ref, o_ref,
                w1f_ref, w2f_ref, w1s_ref, w2s_ref, sem, *, layer):
    i = pl.program_id(0)
    active = i < n_used_ref[0]
    e = blk_e_ref[i]
    new_expert = jnp.logical_or(i == 0, e != blk_e_ref[jnp.maximum(i - 1, 0)])

    def weight_copies(expert):
        return (pltpu.make_async_copy(w1_hbm.at[layer, expert], w1f_ref, sem.at[0]),
                pltpu.make_async_copy(w2_hbm.at[layer, expert], w2f_ref, sem.at[1]))

    @pl.when(jnp.logical_and(i == 0, active))
    def _():
        for cp in weight_copies(e):
            cp.start()

    @pl.when(jnp.logical_and(active, new_expert))
    def _():
        for cp in weight_copies(e):
            cp.wait()
        rows = 128
        for r in range(D_MODEL // rows):
            w1s_ref[r * rows:(r + 1) * rows, :] = w1f_ref[r * rows:(r + 1) * rows, :].astype(BF16)
        for r in range(D_FF // rows):
            w2s_ref[r * rows:(r + 1) * rows, :] = w2f_ref[r * rows:(r + 1) * rows, :].astype(BF16)
        nxt = nxt_e_ref[i]

        @pl.when(nxt >= 0)
        def _():
            for cp in weight_copies(nxt):
                cp.start()

    valid = blk_valid_ref[i]
    for part in range(MOE_TM // MOE_SUB):
        r0 = part * MOE_SUB
        live = jnp.logical_and(active, valid > r0)

        @pl.when(live)
        def _():
            x = x_ref[r0:r0 + MOE_SUB, :]
            acc = jnp.zeros((MOE_SUB, D_MODEL), F32)
            for c in range(D_FF // FF_CHUNK):
                lo = c * FF_CHUNK
                glu = jnp.dot(x, w1s_ref[:, lo:lo + FF_CHUNK], preferred_element_type=F32) \
                    + b1_ref[0, 0, :, lo:lo + FF_CHUNK]
                lin = jnp.dot(x, w1s_ref[:, D_FF + lo:D_FF + lo + FF_CHUNK], preferred_element_type=F32) \
                    + b1_ref[0, 0, :, D_FF + lo:D_FF + lo + FF_CHUNK]
                glu = jnp.minimum(glu, SWIGLU_LIMIT)
                lin = jnp.clip(lin, -SWIGLU_LIMIT, SWIGLU_LIMIT)
                act = glu * jax.nn.sigmoid(SWIGLU_ALPHA * glu) * (lin + 1.0)
                acc = acc + jnp.dot(act.astype(BF16), w2s_ref[lo:lo + FF_CHUNK, :],
                                    preferred_element_type=F32)
            o_ref[r0:r0 + MOE_SUB, :] = (acc + b2_ref[0, 0]).astype(o_ref.dtype)

        @pl.when(jnp.logical_not(live))
        def _():
            o_ref[r0:r0 + MOE_SUB, :] = jnp.zeros((MOE_SUB, D_MODEL), o_ref.dtype)


def _moe_ffn(xp, blk_e, n_used, nxt_e, blk_valid, w1, b1, w2, b2, layer):
    n_rows, d = xp.shape
    n_blk = n_rows // MOE_TM
    depth, e = w1.shape[:2]

    def row_map(i, be, nu, ne, bv):
        return (jnp.maximum(jnp.minimum(i, nu[0] - 1), 0), 0)

    def b_map(i, be, nu, ne, bv):
        return (layer, be[i], 0, 0)

    return pl.pallas_call(
        functools.partial(_moe_kernel, layer=layer),
        grid_spec=pltpu.PrefetchScalarGridSpec(
            num_scalar_prefetch=4,
            grid=(n_blk,),
            in_specs=[pl.BlockSpec((MOE_TM, d), row_map),
                      pl.BlockSpec(memory_space=pl.ANY),
                      pl.BlockSpec((1, 1, 1, 2 * D_FF), b_map),
                      pl.BlockSpec(memory_space=pl.ANY),
                      pl.BlockSpec((1, 1, 1, d), b_map)],
            out_specs=pl.BlockSpec((MOE_TM, d), lambda i, be, nu, ne, bv: (i, 0)),
            scratch_shapes=[pltpu.VMEM((d, 2 * D_FF), F32), pltpu.VMEM((D_FF, d), F32),
                            pltpu.VMEM((d, 2 * D_FF), BF16), pltpu.VMEM((D_FF, d), BF16),
                            pltpu.SemaphoreType.DMA((2,))]),
        out_shape=jax.ShapeDtypeStruct((n_rows, d), BF16),
        compiler_params=_cparams(("arbitrary",)),
        name="moe_ffn",
    )(blk_e, n_used, nxt_e, blk_valid, xp, w1, b1.reshape(depth, e, 1, 2 * D_FF), w2, b2.reshape(depth, e, 1, d))


def _combine_kernel(*refs):
    yg_ref, gate_ref, x_ref, g2_ref, fg_ref = refs[:5]
    o_ref = refs[-1]
    xo = x_ref[0] + g2_ref[0] * _weighted_expert_sum(yg_ref, gate_ref[...])
    ms = jnp.mean(xo * xo, axis=-1, keepdims=True)
    o_ref[0] = xo * lax.rsqrt(ms + EPS) * fg_ref[...]


def _combine(yg, gates, x, g2, moff, fg, row_off, *, out_buf=None, ooff=0, out_b=None):
    b, L, d = x.shape
    tm = min(COMBINE_TILE, L)
    assert L % tm == 0 and row_off % tm == 0
    nt = L // tm
    blk_off = row_off // tm
    bm = g2.shape[0]
    out_b = b if out_b is None else out_b
    mod_map = (lambda bb, i: (moff + bb, 0, 0)) if bm > 1 else (lambda bb, i: (0, 0, 0))
    extra = () if out_buf is None else (out_buf,)
    return pl.pallas_call(
        _combine_kernel,
        grid=(b, nt),
        in_specs=[pl.BlockSpec((TOP_K, 1, tm, d), lambda bb, i: (0, bb, i, 0)),
                  pl.BlockSpec((tm, ROUTE_W), lambda bb, i: (blk_off + bb * nt + i, 0)),
                  pl.BlockSpec((1, tm, d), lambda bb, i: (bb, i, 0)),
                  pl.BlockSpec((1, 1, d), mod_map),
                  pl.BlockSpec((1, d), lambda bb, i: (0, 0))]
                 + [pl.BlockSpec(memory_space=pl.ANY)] * len(extra),
        out_specs=pl.BlockSpec((1, tm, d), lambda bb, i: (ooff + bb, i, 0)),
        out_shape=jax.ShapeDtypeStruct((out_b, L, d), F32),
        input_output_aliases={5: 0} if extra else {},
        compiler_params=_cparams(("arbitrary", "arbitrary")),
        name="combine",
    )(yg, gates, x, g2, fg.reshape(1, d), *extra)


def _dispatch_plan(idx, rank, counts):
    n_tok = idx.shape[1]
    n_asg = n_tok * TOP_K
    padded = (counts + MOE_TM - 1) // MOE_TM * MOE_TM
    pad_end = jnp.cumsum(padded)
    pad_start = (pad_end - padded).astype(jnp.int32)
    pos = rank
    for e in range(N_EXPERTS):
        pos = pos + jnp.where(idx == e, pad_start[e], 0)
    n_rows = -(-n_asg // MOE_TM) * MOE_TM + N_EXPERTS * MOE_TM
    n_blk = n_rows // MOE_TM
    tok = jnp.broadcast_to(jnp.arange(n_tok, dtype=jnp.int32)[None, :], (TOP_K, n_tok))
    fill = jnp.arange(n_rows, dtype=jnp.int32) % n_tok
    row_tok = fill.at[pos.reshape(-1)].add((tok - pos % n_tok).reshape(-1), mode="promise_in_bounds")
    blk_start = jnp.arange(n_blk, dtype=jnp.int32) * MOE_TM
    blk_e = jnp.minimum(jnp.sum((pad_end[None, :] <= blk_start[:, None]).astype(jnp.int32), axis=1),
                        N_EXPERTS - 1)
    n_used = (pad_end[-1] // MOE_TM).astype(jnp.int32).reshape(1)
    ids = jnp.arange(N_EXPERTS, dtype=jnp.int32)
    first_used_from = lax.cummin(jnp.where(padded > 0, ids, N_EXPERTS), axis=0, reverse=True)
    nxt_of = jnp.concatenate([first_used_from[1:], jnp.full((1,), N_EXPERTS, jnp.int32)])
    nxt_of = jnp.where(nxt_of >= N_EXPERTS, -1, nxt_of)
    own = blk_e[:, None] == ids[None, :]
    nxt_e = jnp.sum(jnp.where(own, nxt_of[None, :], 0), axis=1).astype(jnp.int32)
    seg_end = jnp.sum(jnp.where(own, (pad_start + counts)[None, :], 0), axis=1)
    blk_valid = jnp.clip(seg_end - blk_start, 0, MOE_TM).astype(jnp.int32)
    return row_tok, pos, blk_e, n_used, nxt_e, blk_valid


_Q_PERM = np.concatenate([np.arange(0, 64), np.arange(128, 192), np.arange(64, 128), np.arange(192, 256)])


def _rope_tables(L):
    t = jnp.arange(L)
    row = (t // GRID_W).astype(F32)
    col = (t % GRID_W).astype(F32)
    half = HEAD_DIM // 2
    inv = ROPE_BASE ** (-jnp.arange(0, half, 2, dtype=F32) / half)
    ar, ac = row[:, None] * inv[None, :], col[:, None] * inv[None, :]
    z = jnp.zeros_like(ar)
    cos = jnp.concatenate([jnp.cos(ar), jnp.cos(ar), jnp.cos(ac), jnp.cos(ac)], axis=1)
    s1 = jnp.concatenate([-jnp.sin(ar), z, -jnp.sin(ac), z], axis=1)
    s2 = jnp.concatenate([z, jnp.sin(ar), z, jnp.sin(ac)], axis=1)
    rep = lambda a: jnp.concatenate([a, a], axis=1)
    return rep(cos), rep(s1), rep(s2)


def _block_diag(pw):
    g, n, _ = pw.shape
    out = jnp.zeros((g * n, g * n), pw.dtype)
    for gi in range(g):
        out = out.at[gi * n:(gi + 1) * n, gi * n:(gi + 1) * n].set(pw[gi])
    return out


def kernel(x, c, ctx, c_ctx, ada_w, ada_b, norm1_g, norm2_g, w_in, attn_sink, conv_dw_w, conv_dw_b, conv_ln_g, conv_ln_b, conv_pw_w, gmlp_ln_g, gmlp_ln_b, gmlp_ws, gmlp_bs, pool_w, pool_scale, group_norm_g, w_out, router_w, router_b, exp_w1, exp_b1, exp_w2, exp_b2, final_norm_g):
    b, L, d = x.shape
    lc = ctx.shape[1]
    depth = ada_w.shape[0]
    tabs = _rope_tables(L)

    r = -(-(b + 1) // 8) * 8
    cond = jnp.zeros((r, d), F32).at[:b].set(c).at[b].set(c_ctx)
    mods = _adaln(cond, ada_w, ada_b).reshape(depth, r, 6, d)

    n_groups = BATCH_GROUPS if b % BATCH_GROUPS == 0 else 1
    bh = b // n_groups
    xs, cs = [x] * n_groups, [ctx] * n_groups
    pending = [None] * n_groups

    def finish(st, out_buf):
        return _combine(st["ygx"], st["gates"], st["xh"], st["g2"], st["moff"], final_norm_g, 0,
                        out_buf=out_buf, ooff=st["moff"], out_b=b)

    for l in range(depth):
        last = l == depth - 1
        mx = [mods[l, :b, i].reshape(b, 1, d) for i in range(6)]
        mc = [mods[l, b:b + 1, i].reshape(1, 1, d) for i in range(6)]
        sh1, sc1, g1, sh2, sc2, g2 = mx
        sh1c, sc1c, g1c, sh2c, sc2c, g2c = mc

        wl = w_in[l]
        wq = wl[:, :OFF_K][:, _Q_PERM] * (HEAD_DIM ** -0.5)
        w_full = jnp.concatenate([wq, wl[:, OFF_K:]], axis=1).astype(BF16)
        gn = group_norm_g[l]
        g_att = gn[:256][_Q_PERM]
        wo = w_out[l]
        woa = wo[:256][_Q_PERM].astype(BF16)
        wol = wo[256:].astype(BF16)
        lp = dict(
            dw_w=conv_dw_w[l], dw_b=conv_dw_b[l].reshape(1, -1),
            cln_g=conv_ln_g[l].reshape(1, -1), cln_b=conv_ln_b[l].reshape(1, -1),
            cpw=conv_pw_w[l].astype(BF16),
            gln_g=gmlp_ln_g[l].reshape(1, -1), gln_b=gmlp_ln_b[l].reshape(1, -1),
            gws=gmlp_ws[l].reshape(GMLP_HEADS * CHUNK, CHUNK).astype(BF16),
            gbs=jnp.repeat(gmlp_bs[l].T, GROUP_W // GMLP_HEADS, axis=1),
            pw=_block_diag(pool_w[l]).astype(BF16), ps=pool_scale[l].reshape(1, -1),
            gn_local=gn[256:].reshape(1, -1))
        rw = router_w[l].T.astype(BF16)
        rb = router_b[l].reshape(N_EXPERTS, 1)

        w_kv = wl[:, OFF_K:OFF_CONV].astype(BF16)
        take = lambda a, i: a.at[i].get(mode="promise_in_bounds")
        staged = []
        for h in range(n_groups):
            moff = h * bh
            pd = pending[h]
            xoff = moff if l == 0 else 0

            if pd is None:
                xh, ch = xs[h], cs[h]
                q, kv, mix = _inproj(xh, xoff, bh, sc1, sh1, moff, norm1_g[l], w_full, tabs, full=True)
                if last:
                    (kvc,) = _inproj(ch, xoff, bh, sc1c, sh1c, 0, norm1_g[l], w_kv, None, full=False)
                else:
                    qc, kvc, mixc = _inproj(ch, xoff, bh, sc1c, sh1c, 0, norm1_g[l], w_full, None, full=True)
            else:
                if staged:
                    pd["ygx"], staged[-1]["xh"] = lax.optimization_barrier((pd["ygx"], staged[-1]["xh"]))
                combx = (pd["ygx"], pd["gates"], 0, pd["g2"], pd["moff"])
                combc = (pd["ygc"], pd["gates"], bh * L, pd["g2c"], 0)
                xh, q, kv, mix = _inproj(pd["xh"], 0, bh, sc1, sh1, moff, norm1_g[l], w_full, tabs,
                                         full=True, comb=combx)
                if last:
                    ch = None
                    (kvc,) = _inproj(pd["ch"], 0, bh, sc1c, sh1c, 0, norm1_g[l], w_kv, None,
                                     full=False, comb=combc, emit_x=False)
                else:
                    ch, qc, kvc, mixc = _inproj(pd["ch"], 0, bh, sc1c, sh1c, 0, norm1_g[l], w_full, None,
                                                full=True, comb=combc)
                xoff = 0
            att = _attention(q, kv, kvc, attn_sink[l], g_att, local=True)
            loc = _local_mixers(mix, lp)
            if staged:
                prev = staged[-1]
                prev["xp"], loc = lax.optimization_barrier((prev["xp"], loc))
            nxt = pending[h + 1] if h + 1 < n_groups else None
            if nxt is not None:
                nxt["ygx"], loc = lax.optimization_barrier((nxt["ygx"], loc))
            t_total = bh * L if last else bh * (L + lc)
            cnt0 = jnp.zeros((N_EXPERTS, 1), F32)
            xh, h2, gates, ir, cnt = _merge(att, loc, xh, xoff, g1, sc2, sh2, moff, norm2_g[l], woa, wol,
                                            rw, rb, cnt0, None, 0, t_total)
            if not last:
                attc = _attention(qc, kvc, kvc, attn_sink[l], g_att, local=False)
                locc = _local_mixers(mixc, lp)
                ch, h2, gates, ir, cnt = _merge(attc, locc, ch, xoff, g1c, sc2c, sh2c, 0, norm2_g[l],
                                                woa, wol, rw, rb, cnt, (h2, gates, ir), bh * L, t_total)
            counts = cnt[:, 0].astype(jnp.int32)
            row_tok, pos, blk_e, n_used, nxt_e, blk_valid = _dispatch_plan(ir[:TOP_K], ir[TOP_K:], counts)
            staged.append(dict(xh=xh, ch=ch, gates=gates, pos=pos, blk_e=blk_e, n_used=n_used, nxt_e=nxt_e, blk_valid=blk_valid,
                               xp=take(h2, row_tok), moff=moff, g2=g2, g2c=g2c, last=last))

        yps = []
        for h, st in enumerate(staged):
            if yps:
                yps[-1], st["xp"] = lax.optimization_barrier((yps[-1], st["xp"]))
            yps.append(_moe_ffn(st["xp"], st["blk_e"], st["n_used"], st["nxt_e"], st["blk_valid"], exp_w1, exp_b1, exp_w2, exp_b2, l))
        for h, st in enumerate(staged):
            if h == n_groups - 1 and h > 0:
                yps[h], staged[0]["ygx"] = lax.optimization_barrier((yps[h], staged[0]["ygx"]))
            pos = st["pos"]
            st["ygx"] = take(yps[h], pos[:, :bh * L]).reshape(TOP_K, bh, L, d)
            if not last:
                st["ygc"] = take(yps[h], pos[:, bh * L:]).reshape(TOP_K, bh, lc, d)
            pending[h] = st

    out = None
    for h in range(n_groups):
        out = finish(pending[h], out)
    return out
```
